```python
import math
import jax, jax.numpy as jnp
from jax import lax
import numpy as np

D_MODEL = 1024
BATCH = 8
SEQ = 4096
DEPTH = 1

RMS_EPS = 1e-6
ROPE_THETA = 500000.0
ROPE_FRACTION = 4
DA_HEADS = 4
DA_HEAD_DIM = 64
DA_V_DIM = 2 * DA_HEAD_DIM
DA_WIDTH = DA_HEADS * DA_V_DIM
DA_QBLOCK = 128
NSA_HEADS = 8
NSA_GROUPS = 2
NSA_HEAD_DIM = 64
NSA_WIDTH = NSA_HEADS * NSA_HEAD_DIM
CMP_BLOCK = 32
CMP_STRIDE = 16
SEL_BLOCK = 64
SEL_TOPK = 16
WINDOW = 512
NSA_QBLOCK = 64
FORCE_BONUS = 1e4
NEG_LARGE = -1e30
PEER_HEADS = 8
PEER_NKEYS = 128
PEER_TOPK = 16
PEER_DQ = 256
N_EXPERTS = PEER_NKEYS * PEER_NKEYS
PEER_TBLOCK = 128
IN_SPLIT_SIZES = (DA_HEADS * 2 * DA_HEAD_DIM, DA_HEADS * 2 * DA_HEAD_DIM, DA_WIDTH, NSA_WIDTH) + (NSA_GROUPS * NSA_HEAD_DIM,) * 6 + (NSA_HEADS * 3, 2 * D_MODEL)
IN_COLS = sum(IN_SPLIT_SIZES)

kernel_name = 'hybrid_diffattn_nsa_peer_block'


def rmsnorm(x, g):
    xf = x.astype(jnp.float32)
    y = xf * lax.rsqrt(jnp.mean(xf * xf, axis=-1, keepdims=True) + RMS_EPS)
    return (y * g.astype(jnp.float32)).astype(x.dtype)


def partial_rope(x, pos):
    d = x.shape[-1]
    rot = d // ROPE_FRACTION
    half = rot // 2
    inv = jnp.power(ROPE_THETA, -jnp.arange(half, dtype=jnp.float32) * 2.0 / rot)
    ang = pos.astype(jnp.float32)[:, None] * inv[None, :]
    cos = jnp.cos(ang)[None, :, None, :].astype(x.dtype)
    sin = jnp.sin(ang)[None, :, None, :].astype(x.dtype)
    x1 = x[..., :half]
    x2 = x[..., half:rot]
    return jnp.concatenate([x1 * cos - x2 * sin, x2 * cos + x1 * sin, x[..., rot:]], axis=-1)


def diff_attention(q, k, v, lq1, lk1, lq2, lk2, subln, lambda_init):
    B, S, H, _, d = q.shape
    scale = d ** -0.5
    lam = (jnp.exp(jnp.sum(lq1.astype(jnp.float32) * lk1.astype(jnp.float32)))
           - jnp.exp(jnp.sum(lq2.astype(jnp.float32) * lk2.astype(jnp.float32))) + lambda_init)
    kh = k.transpose(0, 2, 3, 1, 4)
    vh = v.transpose(0, 2, 1, 3)
    n_blk = S // DA_QBLOCK
    qb = q.transpose(0, 2, 3, 1, 4).reshape(B, H, 2, n_blk, DA_QBLOCK, d).transpose(3, 0, 1, 2, 4, 5)
    kpos = jnp.arange(S)

    def one_block(args):
        qblk, i = args
        qpos = i * DA_QBLOCK + jnp.arange(DA_QBLOCK)
        s = jnp.einsum('bhcqd,bhckd->bhcqk', qblk, kh).astype(jnp.float32) * scale
        mask = kpos[None, :] <= qpos[:, None]
        p = jax.nn.softmax(jnp.where(mask, s, -jnp.inf), axis=-1)
        attn = p[:, :, 0] - lam * p[:, :, 1]
        return jnp.einsum('bhqk,bhkd->bhqd', attn.astype(vh.dtype), vh)

    o = lax.map(one_block, (qb, jnp.arange(n_blk)))
    o = o.transpose(1, 0, 3, 2, 4).reshape(B, S, H, 2 * d)
    o = rmsnorm(o, subln) * (1.0 - lambda_init)
    return o.reshape(B, S, H * 2 * d)


def compress(kv, pe, w1, w2):
    B, S, G, d = kv.shape
    n_cmp = (S - CMP_BLOCK) // CMP_STRIDE + 1
    idx = jnp.arange(n_cmp)[:, None] * CMP_STRIDE + jnp.arange(CMP_BLOCK)[None, :]
    blocks = kv[:, idx] + pe[None, None, :, None, :]
    flat = blocks.transpose(0, 3, 1, 2, 4).reshape(B, G, n_cmp, CMP_BLOCK * d)
    return jax.nn.gelu(flat @ w1, approximate=False) @ w2


def nsa_attention(q, kc_raw, vc_raw, ks, vs, kw, vw, gates, pe_k, pe_v, w1k, w1v, w2k, w2v):
    B, S, Hq, d = q.shape
    G = NSA_GROUPS
    Hg = Hq // G
    QB = NSA_QBLOCK
    scale = d ** -0.5
    kc = compress(kc_raw, pe_k, w1k, w2k)
    vc = compress(vc_raw, pe_v, w1v, w2v)
    n_cmp = kc.shape[2]
    cmp_end = jnp.arange(n_cmp) * CMP_STRIDE + CMP_BLOCK - 1
    n_sel = S // SEL_BLOCK
    k_sel = min(SEL_TOPK, n_sel)
    ci = jnp.arange(n_cmp)[:, None] * CMP_STRIDE
    sj = jnp.arange(n_sel)[None, :] * SEL_BLOCK
    overlap = ((ci < sj + SEL_BLOCK) & (ci + CMP_BLOCK > sj)).astype(jnp.float32)
    ksh = ks.transpose(0, 2, 1, 3)
    vsh = vs.transpose(0, 2, 1, 3)
    pad = ((0, 0), (0, 0), (WINDOW, 0), (0, 0))
    kwp = jnp.pad(kw.transpose(0, 2, 1, 3), pad)
    vwp = jnp.pad(vw.transpose(0, 2, 1, 3), pad)
    n_blk = S // QB
    qb = q.reshape(B, n_blk, QB, G, Hg, d).transpose(1, 0, 3, 4, 2, 5)
    gb = gates.reshape(B, n_blk, QB, G, Hg, 3).transpose(1, 0, 3, 4, 2, 5)
    b_ix = jnp.arange(B)[:, None, None, None]
    g_ix = jnp.arange(G)[None, :, None, None]
    blk_id = jnp.arange(n_sel)

    def one_block(args):
        qblk, gblk, i = args
        qs = i * QB
        qpos = qs + jnp.arange(QB)
        s = jnp.einsum('bghqd,bgnd->bghqn', qblk, kc).astype(jnp.float32) * scale
        cmask = cmp_end[None, :] <= qpos[:, None]
        p_cmp = jax.nn.softmax(jnp.where(cmask, s, NEG_LARGE), axis=-1) * cmask
        o_cmp = jnp.einsum('bghqn,bgnd->bghqd', p_cmp.astype(vc.dtype), vc)
        imp = jnp.einsum('bghqn,ns->bgqs', p_cmp, overlap)
        cur = qpos // SEL_BLOCK
        valid = blk_id[None, :] <= cur[:, None]
        forced = (blk_id[None, :] == 0) | (blk_id[None, :] == cur[:, None]) | (blk_id[None, :] == cur[:, None] - 1)
        score = jnp.where(valid, imp + jnp.where(forced, FORCE_BONUS, 0.0), -jnp.inf)
        _, sel = lax.top_k(score, k_sel)
        tok = (sel[..., None] * SEL_BLOCK + jnp.arange(SEL_BLOCK)).reshape(B, G, QB, k_sel * SEL_BLOCK)
        kg = ksh[b_ix, g_ix, tok]
        vg = vsh[b_ix, g_ix, tok]
        s = jnp.einsum('bghqd,bgqtd->bghqt', qblk, kg).astype(jnp.float32) * scale
        smask = (tok <= qpos[None, None, :, None])[:, :, None]
        p = jax.nn.softmax(jnp.where(smask, s, -jnp.inf), axis=-1)
        o_sel = jnp.einsum('bghqt,bgqtd->bghqd', p.astype(vg.dtype), vg)
        kwin = lax.dynamic_slice_in_dim(kwp, qs, WINDOW + QB, axis=2)
        vwin = lax.dynamic_slice_in_dim(vwp, qs, WINDOW + QB, axis=2)
        kpos = qs - WINDOW + jnp.arange(WINDOW + QB)
        wmask = (kpos[None, :] <= qpos[:, None]) & (kpos[None, :] > qpos[:, None] - WINDOW) & (kpos[None, :] >= 0)
        s = jnp.einsum('bghqd,bgkd->bghqk', qblk, kwin).astype(jnp.float32) * scale
        p = jax.nn.softmax(jnp.where(wmask, s, -jnp.inf), axis=-1)
        o_win = jnp.einsum('bghqk,bgkd->bghqd', p.astype(vwin.dtype), vwin)
        return gblk[..., 0:1] * o_cmp + gblk[..., 1:2] * o_sel + gblk[..., 2:3] * o_win

    o = lax.map(one_block, (qb, gb, jnp.arange(n_blk)))
    return o.transpose(1, 0, 4, 2, 3, 5).reshape(B, S, Hq * d)


def hybrid_mixer(a, w_in, lq1, lk1, lq2, lk2, subln, lambda_init, pe_k, pe_v, w1k, w1v, w2k, w2v, p_da, p_nsa, w_o):
    B, S, _ = a.shape
    pos = jnp.arange(S)
    z = a @ w_in
    splits = np.cumsum(IN_SPLIT_SIZES)[:-1].tolist()
    q_da, k_da, v_da, q_n, kc, vc, ks, vs, kw, vw, g_n, g_m = jnp.split(z, splits, axis=-1)
    q_da = partial_rope(q_da.reshape(B, S, 2 * DA_HEADS, DA_HEAD_DIM), pos).reshape(B, S, DA_HEADS, 2, DA_HEAD_DIM)
    k_da = partial_rope(k_da.reshape(B, S, 2 * DA_HEADS, DA_HEAD_DIM), pos).reshape(B, S, DA_HEADS, 2, DA_HEAD_DIM)
    v_da = v_da.reshape(B, S, DA_HEADS, DA_V_DIM)
    y_da = diff_attention(q_da, k_da, v_da, lq1, lk1, lq2, lk2, subln, lambda_init)
    kv_shape = (B, S, NSA_GROUPS, NSA_HEAD_DIM)
    q_n = partial_rope(q_n.reshape(B, S, NSA_HEADS, NSA_HEAD_DIM), pos)
    kc = partial_rope(kc.reshape(kv_shape), pos)
    ks = partial_rope(ks.reshape(kv_shape), pos)
    kw = partial_rope(kw.reshape(kv_shape), pos)
    gates = jax.nn.sigmoid(g_n.reshape(B, S, NSA_HEADS, 3))
    y_n = nsa_attention(q_n, kc, vc.reshape(kv_shape), ks, vs.reshape(kv_shape), kw, vw.reshape(kv_shape),
                        gates, pe_k, pe_v, w1k, w1v, w2k, w2v)
    g_a, g_b = jnp.split(jax.nn.sigmoid(g_m), 2, axis=-1)
    merged = g_a * (y_da @ p_da) + g_b * (y_n @ p_nsa)
    return merged @ w_o


def peer(c, w_q, sub_k1, sub_k2, u_tab, v_tab):
    B, S, D = c.shape
    T = B * S
    cf = c.reshape(T, D)
    q = (cf @ w_q).reshape(T, PEER_HEADS, 2, PEER_DQ // 2)
    s1 = jnp.einsum('thd,kd->thk', q[:, :, 0], sub_k1).astype(jnp.float32)
    s2 = jnp.einsum('thd,kd->thk', q[:, :, 1], sub_k2).astype(jnp.float32)
    v1, i1 = lax.top_k(s1, PEER_TOPK)
    v2, i2 = lax.top_k(s2, PEER_TOPK)
    cand = (v1[..., :, None] + v2[..., None, :]).reshape(T, PEER_HEADS, PEER_TOPK * PEER_TOPK)
    cidx = (i1[..., :, None] * PEER_NKEYS + i2[..., None, :]).reshape(T, PEER_HEADS, PEER_TOPK * PEER_TOPK)
    sc, pick = lax.top_k(cand, PEER_TOPK)
    eidx = jnp.take_along_axis(cidx, pick, axis=-1)
    gate = jax.nn.softmax(sc, axis=-1)
    E = PEER_HEADS * PEER_TOPK
    n_blk = T // PEER_TBLOCK
    xb = cf.reshape(n_blk, PEER_TBLOCK, D)
    ib = eidx.reshape(n_blk, PEER_TBLOCK, E)
    gb = gate.reshape(n_blk, PEER_TBLOCK, E).astype(c.dtype)

    def one_block(args):
        xblk, iblk, gblk = args
        u = u_tab[iblk]
        h = jax.nn.gelu(jnp.einsum('td,ted->te', xblk, u), approximate=False)
        return jnp.einsum('te,ted->td', h * gblk, v_tab[iblk])

    return lax.map(one_block, (xb, ib, gb)).reshape(B, S, D)


def setup_inputs(seed: int = 0) -> dict:
    key = jax.random.key(seed)
    ks = jax.random.split(key, 24)

    def nrm(k, shape, scale):
        return jax.random.normal(k, shape, jnp.float32) * scale

    d = NSA_HEAD_DIM
    return {
        'x': nrm(ks[0], (BATCH, SEQ, D_MODEL), 1.0),
        'attn_norm': 1.0 + nrm(ks[1], (DEPTH, D_MODEL), 0.01),
        'w_in': nrm(ks[2], (DEPTH, D_MODEL, IN_COLS), D_MODEL ** -0.5),
        'da_lambda_q1': nrm(ks[3], (DEPTH, DA_HEAD_DIM), 0.1),
        'da_lambda_k1': nrm(ks[4], (DEPTH, DA_HEAD_DIM), 0.1),
        'da_lambda_q2': nrm(ks[5], (DEPTH, DA_HEAD_DIM), 0.1),
        'da_lambda_k2': nrm(ks[6], (DEPTH, DA_HEAD_DIM), 0.1),
        'da_subln': 1.0 + nrm(ks[7], (DEPTH, DA_V_DIM), 0.01),
        'cmp_pe_k': nrm(ks[8], (DEPTH, CMP_BLOCK, d), 0.1),
        'cmp_pe_v': nrm(ks[9], (DEPTH, CMP_BLOCK, d), 0.1),
        'cmp_w1_k': nrm(ks[10], (DEPTH, CMP_BLOCK * d, d), (CMP_BLOCK * d) ** -0.5),
        'cmp_w1_v': nrm(ks[11], (DEPTH, CMP_BLOCK * d, d), (CMP_BLOCK * d) ** -0.5),
        'cmp_w2_k': nrm(ks[12], (DEPTH, d, d), d ** -0.5),
        'cmp_w2_v': nrm(ks[13], (DEPTH, d, d), d ** -0.5),
        'p_da': nrm(ks[14], (DEPTH, DA_WIDTH, D_MODEL), DA_WIDTH ** -0.5),
        'p_nsa': nrm(ks[15], (DEPTH, NSA_WIDTH, D_MODEL), NSA_WIDTH ** -0.5),
        'w_o': nrm(ks[16], (DEPTH, D_MODEL, D_MODEL), D_MODEL ** -0.5),
        'ffn_norm': 1.0 + nrm(ks[17], (DEPTH, D_MODEL), 0.01),
        'peer_wq': nrm(ks[18], (DEPTH, D_MODEL, PEER_HEADS * PEER_DQ), D_MODEL ** -0.5),
        'peer_k1': nrm(ks[19], (DEPTH, PEER_NKEYS, PEER_DQ // 2), (PEER_DQ // 2) ** -0.5),
        'peer_k2': nrm(ks[20], (DEPTH, PEER_NKEYS, PEER_DQ // 2), (PEER_DQ // 2) ** -0.5),
        'peer_u': nrm(ks[21], (DEPTH, N_EXPERTS, D_MODEL), D_MODEL ** -0.5),
        'peer_v': nrm(ks[22], (DEPTH, N_EXPERTS, D_MODEL), PEER_HEADS ** -0.5),
        'final_norm': 1.0 + nrm(ks[23], (D_MODEL,), 0.01),
    }


def reference(x, attn_norm, w_in, da_lambda_q1, da_lambda_k1, da_lambda_q2, da_lambda_k2, da_subln,
              cmp_pe_k, cmp_pe_v, cmp_w1_k, cmp_w1_v, cmp_w2_k, cmp_w2_v, p_da, p_nsa, w_o,
              ffn_norm, peer_wq, peer_k1, peer_k2, peer_u, peer_v, final_norm):
    h = x
    for l in range(DEPTH):
        lambda_init = 0.8 - 0.6 * math.exp(-0.3 * l)
        a = rmsnorm(h, attn_norm[l])
        h = h + hybrid_mixer(a, w_in[l], da_lambda_q1[l], da_lambda_k1[l], da_lambda_q2[l], da_lambda_k2[l],
                             da_subln[l], lambda_init, cmp_pe_k[l], cmp_pe_v[l], cmp_w1_k[l], cmp_w1_v[l],
                             cmp_w2_k[l], cmp_w2_v[l], p_da[l], p_nsa[l], w_o[l])
        c = rmsnorm(h, ffn_norm[l])
        h = h + peer(c, peer_wq[l], peer_k1[l], peer_k2[l], peer_u[l], peer_v[l])
    return rmsnorm(h, final_norm)
```

```python
import functools
import math

import jax
import jax.numpy as jnp
from jax import lax
from jax.experimental import pallas as pl
from jax.experimental.pallas import tpu as pltpu

F32 = jnp.float32
BF16 = jnp.bfloat16
I32 = jnp.int32

RMS_EPS = 1e-6
ROPE_THETA = 500000.0
ROPE_HALF = 8
HEAD = 64
DA_HEADS = 4
NSA_GROUPS = 2
NSA_HG = 4
CMP_STRIDE = 16
CMP_BLOCK = 32
SEL_BLOCK = 64
SEL_TOPK = 16
WINDOW = 512
FORCE_BONUS = 1e4
NEG_BIG = -1e30
SEL_MASK_BIAS = -32768.0
PEER_HEADS = 8
PEER_NKEYS = 128
PEER_TOPK = 16
LANES = 128
SUBLANES = 8
VMEM_LIMIT = 56 * 1024 * 1024

NT_DIMS = (((1,), (1,)), ((), ()))


def _rmsnorm(x, g):
    return x * lax.rsqrt(jnp.mean(x * x, axis=-1, keepdims=True) + RMS_EPS) * g


def _sigmoid(z):
    return 1.0 / (1.0 + jnp.exp(-z))


def _gelu(z):
    return 0.5 * z * (1.0 + lax.erf(z * (2.0 ** -0.5)))


def _params(*sem):
    return pltpu.CompilerParams(dimension_semantics=sem, vmem_limit_bytes=VMEM_LIMIT)


_QDA0, _KDA0, _VDA0, _QN0, _KVN0, _GM0, _GN0, _WCOLS = 0, 512, 1024, 1536, 2048, 2816, 4864, 5120


def _inproj_body(x_ref, g_ref, w_ref, rc_ref, rs1_ref, rs2_ref,
                 qda_ref, kda_ref, vda_ref, qn_ref, kvn_ref, gm_ref, gn_ref):
    a = _rmsnorm(x_ref[...], g_ref[...]).astype(BF16)
    rc, rs1, rs2 = rc_ref[...], rs1_ref[...], rs2_ref[...]

    def rope(z):
        return (z * rc + pltpu.roll(z, ROPE_HALF, 1) * rs1
                + pltpu.roll(z, LANES - ROPE_HALF, 1) * rs2)

    def proj(c0):
        return jnp.dot(a, w_ref[:, c0:c0 + 256], preferred_element_type=F32)

    def rope2(z):
        return jnp.concatenate([rope(z[:, :LANES]), rope(z[:, LANES:])], axis=1)

    for c in range(2):
        qda_ref[:, c * 256:(c + 1) * 256] = (rope2(proj(_QDA0 + c * 256)) * 0.125).astype(BF16)
        kda_ref[:, c * 256:(c + 1) * 256] = rope2(proj(_KDA0 + c * 256)).astype(BF16)
        vda_ref[:, c * 256:(c + 1) * 256] = proj(_VDA0 + c * 256).astype(BF16)
        qn_ref[:, c * 256:(c + 1) * 256] = (rope2(proj(_QN0 + c * 256)) * 0.125).astype(BF16)
    for c in range(3):
        z = proj(_KVN0 + c * 256)
        kvn_ref[:, c * 256:c * 256 + LANES] = rope(z[:, :LANES]).astype(BF16)
        kvn_ref[:, c * 256 + LANES:(c + 1) * 256] = z[:, LANES:].astype(BF16)
    for c in range(8):
        gm_ref[:, c * 256:(c + 1) * 256] = _sigmoid(proj(_GM0 + c * 256)).astype(BF16)
    gn_ref[...] = _sigmoid(proj(_GN0))


def _pack_w_in(w):
    d = w.shape[0]
    gn = w[:, 2816:2840]
    pad = jnp.zeros((d, LANES - 12), w.dtype)
    return jnp.concatenate([w[:, :2816], w[:, 2840:], gn[:, :12], pad, gn[:, 12:], pad],
                           axis=1).astype(BF16)


def _rope_tables(seq):
    inv = jnp.power(ROPE_THETA, -jnp.arange(ROPE_HALF, dtype=F32) * 2.0 / (2 * ROPE_HALF))
    ang = jnp.arange(seq, dtype=F32)[:, None] * inv[None, :]
    cos, sin = jnp.cos(ang), jnp.sin(ang)
    one = jnp.ones((seq, HEAD - 2 * ROPE_HALF), F32)
    zero8 = jnp.zeros((seq, ROPE_HALF), F32)
    zero48 = jnp.zeros_like(one)
    rc = jnp.concatenate([cos, cos, one], axis=1)
    rs1 = jnp.concatenate([zero8, sin, zero48], axis=1)
    rs2 = jnp.concatenate([-sin, zero8, zero48], axis=1)
    return tuple(jnp.concatenate([t, t], axis=1) for t in (rc, rs1, rs2))


def _in_proj(x2, norm_g, w_packed, rope_tabs, seq):
    t, d = x2.shape
    tm = 512
    nseq = seq // tm
    row = lambda i: (i, 0)
    fixed = lambda i: (0, 0)
    out_shapes = [
        jax.ShapeDtypeStruct((t, 512), BF16), jax.ShapeDtypeStruct((t, 512), BF16),
        jax.ShapeDtypeStruct((t, 512), BF16), jax.ShapeDtypeStruct((t, 512), BF16),
        jax.ShapeDtypeStruct((t, 768), BF16), jax.ShapeDtypeStruct((t, 2048), BF16),
        jax.ShapeDtypeStruct((t, 256), F32)]
    rope_spec = pl.BlockSpec((tm, LANES), lambda i: (i % nseq, 0))
    return pl.pallas_call(
        _inproj_body,
        grid=(t // tm,),
        in_specs=[pl.BlockSpec((tm, d), row), pl.BlockSpec((1, d), fixed),
                  pl.BlockSpec((d, _WCOLS), fixed), rope_spec, rope_spec, rope_spec],
        out_specs=[pl.BlockSpec((tm, s.shape[1]), row) for s in out_shapes],
        out_shape=out_shapes,
        compiler_params=_params("parallel"),
        name="in_proj",
    )(x2, norm_g, w_packed, *rope_tabs)


def _softmax_step(s, v, m_ref, l_ref, acc_ref):
    m_prev = m_ref[...]
    m_new = jnp.maximum(m_prev, jnp.max(s, axis=1, keepdims=True))
    alpha = jnp.exp(m_prev - m_new)
    p = jnp.exp(s - m_new)
    l_ref[...] = alpha * l_ref[...] + jnp.sum(p, axis=1, keepdims=True)
    acc_ref[...] = alpha * acc_ref[...] + jnp.dot(p.astype(BF16), v, preferred_element_type=F32)
    m_ref[...] = m_new


def _softmax_reset(m_ref, l_ref, acc_ref):
    m_ref[...] = jnp.full(m_ref.shape, NEG_BIG, F32)
    l_ref[...] = jnp.zeros(l_ref.shape, F32)
    acc_ref[...] = jnp.zeros(acc_ref.shape, F32)


def _diffattn_body(lambda_init, bq, q_ref, k_ref, v_ref, lq1_ref, lk1_ref, lq2_ref, lk2_ref,
                   sub_ref, y_ref, qbd_ref, m_ref, l_ref, acc_ref):
    qi = pl.program_id(2)
    q = q_ref[0]
    lane = lax.broadcasted_iota(I32, q.shape, 1)
    zero = jnp.zeros_like(q)
    qbd_ref[0:bq, :] = jnp.where(lane < HEAD, q, zero)
    qbd_ref[bq:2 * bq, :] = jnp.where(lane >= HEAD, q, zero)
    _softmax_reset(m_ref, l_ref, acc_ref)

    def scores(kt):
        k = k_ref[0, pl.ds(pl.multiple_of(kt * bq, bq), bq), :]
        return lax.dot_general(qbd_ref[...], k, NT_DIMS, preferred_element_type=F32)

    def values(kt):
        return v_ref[0, pl.ds(pl.multiple_of(kt * bq, bq), bq), :]

    def body(kt, carry):
        _softmax_step(scores(kt), values(kt), m_ref, l_ref, acc_ref)
        return carry

    lax.fori_loop(0, qi, body, 0)
    r = lax.broadcasted_iota(I32, (2 * bq, bq), 0)
    c = lax.broadcasted_iota(I32, (2 * bq, bq), 1)
    causal = c <= jnp.where(r >= bq, r - bq, r)
    _softmax_step(jnp.where(causal, scores(qi), NEG_BIG), values(qi), m_ref, l_ref, acc_ref)

    o = acc_ref[...] / l_ref[...]
    lam = (jnp.exp(jnp.sum(lq1_ref[...] * lk1_ref[...], axis=1, keepdims=True))
           - jnp.exp(jnp.sum(lq2_ref[...] * lk2_ref[...], axis=1, keepdims=True)) + lambda_init)
    d = o[0:bq] - lam * o[bq:2 * bq]
    y_ref[0] = (_rmsnorm(d, sub_ref[...]) * (1.0 - lambda_init)).astype(BF16)


def _diff_attn(qda, kda, vda, lq1, lk1, lq2, lk2, subln, lambda_init):
    b, s, _ = qda.shape
    bq = 256
    vec = lambda n: pl.BlockSpec((1, n), lambda bi, h, qi: (0, 0))
    full = pl.BlockSpec((1, s, LANES), lambda bi, h, qi: (bi, 0, h))
    tile = pl.BlockSpec((1, bq, LANES), lambda bi, h, qi: (bi, qi, h))
    return pl.pallas_call(
        functools.partial(_diffattn_body, lambda_init, bq),
        grid=(b, DA_HEADS, s // bq),
        in_specs=[tile, full, full, vec(HEAD), vec(HEAD), vec(HEAD), vec(HEAD), vec(2 * HEAD)],
        out_specs=tile,
        out_shape=jax.ShapeDtypeStruct((b, s, DA_HEADS * 2 * HEAD), BF16),
        scratch_shapes=[pltpu.VMEM((2 * bq, LANES), BF16), pltpu.VMEM((2 * bq, 1), F32),
                        pltpu.VMEM((2 * bq, 1), F32), pltpu.VMEM((2 * bq, LANES), F32)],
        compiler_params=_params("parallel", "parallel", "parallel"),
        name="diff_attn",
    )(qda, kda, vda, lq1, lk1, lq2, lk2, subln)


def _compress_body(xk_ref, xv_ref, pek_ref, pev_ref, w1k_ref, w1v_ref, w2k_ref, w2v_ref,
                   kc_ref, vc_ref):
    def mlp(x_ref, pe_ref, w1_ref, w2_ref):
        blocks = (x_ref[0].astype(F32) + pe_ref[...]).astype(BF16)
        hid = _gelu(jnp.dot(blocks, w1_ref[...], preferred_element_type=F32))
        return jnp.dot(hid.astype(BF16), w2_ref[...], preferred_element_type=F32)

    kc = mlp(xk_ref, pek_ref, w1k_ref, w2k_ref)
    kc_ref[0] = jnp.concatenate([kc, jnp.zeros_like(kc)], axis=1).astype(BF16)
    vc_ref[0] = mlp(xv_ref, pev_ref, w1v_ref, w2v_ref).astype(BF16)


def _compress(xk, xv, pe_k, pe_v, w1k, w1v, w2k, w2v):
    n, ncp, width = xk.shape
    blk = pl.BlockSpec((1, ncp, width), lambda i: (i, 0, 0))
    fixed = lambda shape: pl.BlockSpec(shape, lambda i: (0, 0))
    return pl.pallas_call(
        _compress_body,
        grid=(n,),
        in_specs=[blk, blk, fixed((1, width)), fixed((1, width)), fixed((width, HEAD)),
                  fixed((width, HEAD)), fixed((HEAD, HEAD)), fixed((HEAD, HEAD))],
        out_specs=[pl.BlockSpec((1, ncp, LANES), lambda i: (i, 0, 0)),
                   pl.BlockSpec((1, ncp, HEAD), lambda i: (i, 0, 0))],
        out_shape=[jax.ShapeDtypeStruct((n, ncp, LANES), BF16),
                   jax.ShapeDtypeStruct((n, ncp, HEAD), BF16)],
        compiler_params=_params("parallel"),
        name="compress",
    )(xk, xv, pe_k, pe_v, w1k, w1v, w2k, w2v)


def _cmpsel_body(bq, q_ref, kc_ref, vc_ref, ovl_ref, ocmp_ref, bias_ref):
    qi = pl.program_id(2)
    ncp = kc_ref.shape[2]
    rows = NSA_HG * bq
    q = q_ref[0, 0].reshape(rows, LANES)
    s = lax.dot_general(q, kc_ref[0, 0], NT_DIMS, preferred_element_type=F32)
    r = lax.broadcasted_iota(I32, (rows, ncp), 0)
    n = lax.broadcasted_iota(I32, (rows, ncp), 1)
    qpos = qi * bq + (r & (bq - 1))
    cmask = n * CMP_STRIDE + (CMP_BLOCK - 1) <= qpos
    s = jnp.where(cmask, s, NEG_BIG)
    e = jnp.exp(s - jnp.max(s, axis=1, keepdims=True))
    p = jnp.where(cmask, e / jnp.sum(e, axis=1, keepdims=True), 0.0)
    o = jnp.dot(p.astype(BF16), vc_ref[0, 0], preferred_element_type=F32)
    for h in range(NSA_HG):
        ocmp_ref[0, 0, h] = o[h * bq:(h + 1) * bq]

    psum = p[0:bq] + p[bq:2 * bq] + p[2 * bq:3 * bq] + p[3 * bq:4 * bq]
    imp = jnp.dot(psum, ovl_ref[...], preferred_element_type=F32)
    lane = lax.broadcasted_iota(I32, (bq, LANES), 1)
    blk = lane - HEAD
    pos = qi * bq + lax.broadcasted_iota(I32, (bq, LANES), 0)
    cur = lax.shift_right_logical(pos, 6)
    valid = (blk >= 0) & (blk <= cur)
    forced = (blk == 0) | (blk == cur) | (blk == cur - 1)
    score = jnp.where(valid, imp + jnp.where(forced, FORCE_BONUS, 0.0), -jnp.inf)
    rank = jnp.zeros((bq, LANES), I32)
    for sh in range(1, SEL_BLOCK):
        lower = pltpu.roll(score, sh, 1)
        upper = pltpu.roll(score, LANES - sh, 1)
        rank = rank + jnp.where(lower >= score, 1, 0) + jnp.where(upper > score, 1, 0)
    keep = valid & (rank < SEL_TOPK)
    bias_ref[0, 0] = jnp.where(keep | (blk < 0), 0.0, SEL_MASK_BIAS).astype(BF16)


def _cmp_select(qz, kca, vc, overlap):
    b, g, hg, s, _ = qz.shape
    ncp = kca.shape[2]
    bq = 128
    return pl.pallas_call(
        functools.partial(_cmpsel_body, bq),
        grid=(b, g, s // bq),
        in_specs=[pl.BlockSpec((1, 1, hg, bq, LANES), lambda bi, gi, qi: (bi, gi, 0, qi, 0)),
                  pl.BlockSpec((1, 1, ncp, LANES), lambda bi, gi, qi: (bi, gi, 0, 0)),
                  pl.BlockSpec((1, 1, ncp, HEAD), lambda bi, gi, qi: (bi, gi, 0, 0)),
                  pl.BlockSpec((ncp, LANES), lambda bi, gi, qi: (0, 0))],
        out_specs=[pl.BlockSpec((1, 1, hg, bq, HEAD), lambda bi, gi, qi: (bi, gi, 0, qi, 0)),
                   pl.BlockSpec((1, 1, bq, LANES), lambda bi, gi, qi: (bi, gi, qi, 0))],
        out_shape=[jax.ShapeDtypeStruct((b, g, hg, s, HEAD), F32),
                   jax.ShapeDtypeStruct((b, g, s, LANES), BF16)],
        compiler_params=_params("parallel", "parallel", "parallel"),
        name="cmp_select",
    )(qz, kca, vc, overlap)


def _selwin_body(bq, q_ref, bias_ref, ks_ref, vs_ref, kw_ref, vw_ref, ocmp_ref, gate_ref,
                 y_ref, m_ref, l_ref, acc_ref):
    qi = pl.program_id(2)
    rows = NSA_HG * bq
    q = q_ref[0, 0].reshape(rows, LANES)
    bias = bias_ref[0, 0]
    qa = q + jnp.concatenate([bias] * NSA_HG, axis=0)
    r = lax.broadcasted_iota(I32, (rows, bq), 0)
    c = lax.broadcasted_iota(I32, (rows, bq), 1)
    qpos = qi * bq + (r & (bq - 1))

    def tile(ref, kt):
        return ref[0, 0, pl.ds(pl.multiple_of(kt * bq, bq), bq), :]

    _softmax_reset(m_ref, l_ref, acc_ref)

    def sel_step(kt, carry):
        s = lax.dot_general(qa, tile(ks_ref, kt), NT_DIMS, preferred_element_type=F32)
        _softmax_step(s, tile(vs_ref, kt), m_ref, l_ref, acc_ref)
        return carry

    lax.fori_loop(0, qi, sel_step, 0)
    s = lax.dot_general(qa, tile(ks_ref, qi), NT_DIMS, preferred_element_type=F32)
    s = jnp.where(qi * bq + c <= qpos, s, NEG_BIG)
    _softmax_step(s, tile(vs_ref, qi), m_ref, l_ref, acc_ref)
    o_sel = acc_ref[...] / l_ref[...]

    _softmax_reset(m_ref, l_ref, acc_ref)
    for back in range(WINDOW // bq, -1, -1):
        kt = qi - back

        @pl.when(kt >= 0)
        def _():
            kpos = kt * bq + c
            s = lax.dot_general(q, tile(kw_ref, jnp.maximum(kt, 0)), NT_DIMS,
                                preferred_element_type=F32)
            s = jnp.where((kpos <= qpos) & (kpos > qpos - WINDOW), s, NEG_BIG)
            _softmax_step(s, tile(vw_ref, jnp.maximum(kt, 0)), m_ref, l_ref, acc_ref)

    o_win = acc_ref[...] / l_ref[...]
    gate = gate_ref[0]
    for h in range(NSA_HG):
        sl = slice(h * bq, (h + 1) * bq)
        y = (gate[:, 3 * h:3 * h + 1] * ocmp_ref[0, 0, h]
             + gate[:, 3 * h + 1:3 * h + 2] * o_sel[sl]
             + gate[:, 3 * h + 2:3 * h + 3] * o_win[sl])
        y_ref[0, 0, h] = y.astype(BF16)


def _sel_win(qz, bias, ksa, vs, kwa, vw, ocmp, gn3):
    b, g, hg, s, _ = qz.shape
    bq = 256
    qspec = lambda w: pl.BlockSpec((1, 1, hg, bq, w), lambda bi, gi, qi: (bi, gi, 0, qi, 0))
    full = lambda w: pl.BlockSpec((1, 1, s, w), lambda bi, gi, qi: (bi, gi, 0, 0))
    rows = hg * bq
    return pl.pallas_call(
        functools.partial(_selwin_body, bq),
        grid=(b, g, s // bq),
        in_specs=[qspec(LANES),
                  pl.BlockSpec((1, 1, bq, LANES), lambda bi, gi, qi: (bi, gi, qi, 0)),
                  full(LANES), full(HEAD), full(LANES), full(HEAD), qspec(HEAD),
                  pl.BlockSpec((1, bq, LANES), lambda bi, gi, qi: (bi, qi, gi))],
        out_specs=qspec(HEAD),
        out_shape=jax.ShapeDtypeStruct((b, g, hg, s, HEAD), BF16),
        scratch_shapes=[pltpu.VMEM((rows, 1), F32), pltpu.VMEM((rows, 1), F32),
                        pltpu.VMEM((rows, HEAD), F32)],
        compiler_params=_params("parallel", "parallel", "parallel"),
        name="sel_win",
    )(qz, bias, ksa, vs, kwa, vw, ocmp, gn3)


def _outproj_body(yda_ref, yn_ref, gm_ref, x_ref, pda_ref, pnsa_ref, wo_ref, fg_ref, wq_ref,
                  k1_ref, k2_ref, h1_ref, c_ref, s1_ref, s2_ref):
    d = x_ref.shape[1]
    a = jnp.dot(yda_ref[...], pda_ref[...], preferred_element_type=F32)
    bn = jnp.dot(yn_ref[...], pnsa_ref[...], preferred_element_type=F32)
    merged = gm_ref[:, :d].astype(F32) * a + gm_ref[:, d:].astype(F32) * bn
    h1 = x_ref[...] + jnp.dot(merged.astype(BF16), wo_ref[...], preferred_element_type=F32)
    h1_ref[...] = h1
    c = _rmsnorm(h1, fg_ref[...])
    c_ref[...] = c
    cb = c.astype(BF16)
    for h in range(PEER_HEADS):
        qh = jnp.dot(cb, wq_ref[:, h * 256:(h + 1) * 256], preferred_element_type=F32).astype(BF16)
        s1_ref[h] = lax.dot_general(k1_ref[...], qh[:, :LANES], NT_DIMS, preferred_element_type=F32)
        s2_ref[h] = lax.dot_general(k2_ref[...], qh[:, LANES:], NT_DIMS, preferred_element_type=F32)


def _out_proj(yda, yn, gm, x2, pda, pnsa, wo, ffn_g, wq, k1, k2):
    t, d = x2.shape
    tm = 256
    row = lambda w: pl.BlockSpec((tm, w), lambda i: (i, 0))
    fixed = lambda a: pl.BlockSpec(a.shape, lambda i: (0, 0))
    sspec = pl.BlockSpec((PEER_HEADS, PEER_NKEYS, tm), lambda i: (0, 0, i))
    return pl.pallas_call(
        _outproj_body,
        grid=(t // tm,),
        in_specs=[row(512), row(512), row(2 * d), row(d), fixed(pda), fixed(pnsa), fixed(wo),
                  fixed(ffn_g), fixed(wq), fixed(k1), fixed(k2)],
        out_specs=[row(d), row(d), sspec, sspec],
        out_shape=[jax.ShapeDtypeStruct((t, d), F32), jax.ShapeDtypeStruct((t, d), F32),
                   jax.ShapeDtypeStruct((PEER_HEADS, PEER_NKEYS, t), F32),
                   jax.ShapeDtypeStruct((PEER_HEADS, PEER_NKEYS, t), F32)],
        compiler_params=_params("parallel"),
        name="out_proj",
    )(yda, yn, gm, x2, pda, pnsa, wo, ffn_g, wq, k1, k2)


def _batcher_pairs(n):
    pairs = []

    def merge(lo, hi, r):
        step = r * 2
        if step < hi - lo:
            merge(lo, hi, step)
            merge(lo + r, hi, step)
            pairs.extend((i, i + r) for i in range(lo + r, hi - r, step))
        else:
            pairs.append((lo, lo + r))

    def sort(lo, hi):
        if hi - lo >= 1:
            mid = lo + (hi - lo) // 2
            sort(lo, mid)
            sort(mid + 1, hi)
            merge(lo, hi, 1)

    sort(0, n - 1)
    return pairs


_NET16 = _batcher_pairs(PEER_TOPK)


def _cmpx(a, b):
    c = a[0] >= b[0]
    return ((jnp.where(c, a[0], b[0]), jnp.where(c, a[1], b[1])),
            (jnp.where(c, b[0], a[0]), jnp.where(c, b[1], a[1])))


def _sort_lists(lists, n_real):
    lists = list(lists)
    for i, j in _NET16:
        if j < n_real:
            lists[i], lists[j] = _cmpx(lists[i], lists[j])
    return lists


def _merge_sublanes(lists):
    k = PEER_TOPK
    for dist in (4, 2, 1):
        other = [(pltpu.roll(v, dist, 0), pltpu.roll(ix, dist, 0)) for v, ix in lists]
        lists = [_cmpx(lists[i], other[k - 1 - i])[0] for i in range(k)]
        step = k // 2
        while step >= 1:
            for i in range(k):
                if i & step == 0:
                    lists[i], lists[i + step] = _cmpx(lists[i], lists[i + step])
            step //= 2
    return lists


def _spread(lists, off, sub):
    v, ix = lists[off]
    for r in range(1, SUBLANES):
        sel = sub == r
        v = jnp.where(sel, lists[off + r][0], v)
        ix = jnp.where(sel, lists[off + r][1], ix)
    return v, ix


def _peertopk_body(tt, s1_ref, s2_ref, idx_ref, gate_ref):
    sub = lax.broadcasted_iota(I32, (SUBLANES, LANES), 0)
    groups = PEER_NKEYS // SUBLANES

    def top16(ref, h, lanes):
        lists = [(ref[h, v * SUBLANES:(v + 1) * SUBLANES, lanes], sub + v * SUBLANES)
                 for v in range(groups)]
        return _merge_sublanes(_sort_lists(lists, groups))

    def unit(u, carry):
        h = u // (tt // LANES)
        lanes = pl.ds(pl.multiple_of((u % (tt // LANES)) * LANES, LANES), LANES)
        l1 = top16(s1_ref, h, lanes)
        l2 = top16(s2_ref, h, lanes)
        v2lo, v2hi, v1hi = _spread(l2, 0, sub), _spread(l2, SUBLANES, sub), _spread(l1, SUBLANES, sub)
        cands = [(l1[a][0] + v2lo[0], l1[a][1] * PEER_NKEYS + v2lo[1]) for a in range(SUBLANES)]
        cands.append((l1[0][0] + v2hi[0], l1[0][1] * PEER_NKEYS + v2hi[1]))
        cands.append((v1hi[0] + l2[0][0], v1hi[1] * PEER_NKEYS + l2[0][1]))
        n_real = len(cands)
        filler = (jnp.full((SUBLANES, LANES), -jnp.inf, F32), jnp.zeros((SUBLANES, LANES), I32))
        cands += [filler] * (PEER_TOPK - n_real)
        best = _merge_sublanes(_sort_lists(cands, n_real))
        ex = [jnp.exp(v - best[0][0]) for v, _ in best]
        z = ex[0]
        for e in ex[1:]:
            z = z + e
        gl = [(e / z, ix) for e, (_, ix) in zip(ex, best)]
        lo, hi = _spread(gl, 0, sub), _spread(gl, SUBLANES, sub)
        gate_ref[h, :, lanes] = jnp.concatenate([lo[0], hi[0]], axis=0)
        idx_ref[h, :, lanes] = jnp.concatenate([lo[1], hi[1]], axis=0)
        return carry

    lax.fori_loop(0, PEER_HEADS * (tt // LANES), unit, 0)


def _peer_topk(s1t, s2t):
    _, _, t = s1t.shape
    tt = 512
    spec_in = pl.BlockSpec((PEER_HEADS, PEER_NKEYS, tt), lambda i: (0, 0, i))
    spec_out = pl.BlockSpec((PEER_HEADS, PEER_TOPK, tt), lambda i: (0, 0, i))
    return pl.pallas_call(
        functools.partial(_peertopk_body, tt),
        grid=(t // tt,),
        in_specs=[spec_in, spec_in],
        out_specs=[spec_out, spec_out],
        out_shape=[jax.ShapeDtypeStruct((PEER_HEADS, PEER_TOPK, t), I32),
                   jax.ShapeDtypeStruct((PEER_HEADS, PEER_TOPK, t), F32)],
        compiler_params=_params("parallel"),
        name="peer_topk",
    )(s1t, s2t)


PEER_E = PEER_HEADS * PEER_TOPK


def _peerffn_body(tb, idx_ref, gate_ref, c_ref, h1_ref, fg_ref, uv_ref, out_ref, buf_ref,
                  acc_ref, sem_ref):
    d = c_ref.shape[1]

    def row_copy(t, j, slot):
        e = idx_ref[t, j]
        return pltpu.make_async_copy(uv_ref.at[pl.ds(e, 1), :],
                                     buf_ref.at[slot, pl.ds(j, 1), :], sem_ref.at[slot])

    def issue(t, slot):
        for j in range(PEER_E):
            row_copy(t, j, slot).start()

    def wait_all(slot):
        pltpu.make_async_copy(uv_ref.at[pl.ds(0, PEER_E), :], buf_ref.at[slot],
                              sem_ref.at[slot]).wait()

    issue(0, 0)

    def token(t, carry):
        slot = t & 1

        @pl.when(t + 1 < tb)
        def _():
            issue(t + 1, 1 - slot)

        wait_all(slot)
        rows = buf_ref[slot]
        x = jnp.broadcast_to(c_ref[pl.ds(t, 1), :], (SUBLANES, d)).astype(BF16)
        hid = lax.dot_general(x, rows[:, :d].astype(BF16), NT_DIMS, preferred_element_type=F32)
        w = _gelu(hid) * gate_ref[pl.ds(t, 1), :]
        o = jnp.dot(w.astype(BF16), rows[:, d:].astype(BF16), preferred_element_type=F32)
        acc_ref[pl.ds(t, 1), :] = o[0:1]
        return carry

    lax.fori_loop(0, tb, token, 0)
    out_ref[...] = _rmsnorm(h1_ref[...] + acc_ref[...], fg_ref[...])


def _peer_ffn(idx, gate, cb, h1, final_g, uv):
    t, d = h1.shape
    tb = 64
    row = lambda w: pl.BlockSpec((tb, w), lambda i: (i, 0))
    return pl.pallas_call(
        functools.partial(_peerffn_body, tb),
        grid=(t // tb,),
        in_specs=[pl.BlockSpec((tb, PEER_E), lambda i: (i, 0), memory_space=pltpu.SMEM),
                  row(PEER_E), row(d), row(d), pl.BlockSpec((1, d), lambda i: (0, 0)),
                  pl.BlockSpec(memory_space=pl.ANY)],
        out_specs=row(d),
        out_shape=jax.ShapeDtypeStruct((t, d), F32),
        scratch_shapes=[pltpu.VMEM((2, PEER_E, 2 * d), F32), pltpu.VMEM((tb, d), F32),
                        pltpu.SemaphoreType.DMA((2,))],
        compiler_params=_params("arbitrary"),
        name="peer_ffn",
    )(idx, gate, cb, h1, final_g, uv)


def _overlap_table(seq):
    ncp = seq // CMP_STRIDE
    ci = jnp.arange(ncp)[:, None] * CMP_STRIDE
    sj = (jnp.arange(LANES)[None, :] - HEAD) * SEL_BLOCK
    hit = (ci < sj + SEL_BLOCK) & (ci + CMP_BLOCK > sj) & (jnp.arange(LANES)[None, :] >= HEAD)
    return hit.astype(F32)


def _cmp_blocks(kv):
    b, g, s, dh = kv.shape
    r = kv.reshape(b * g, s // CMP_STRIDE, CMP_STRIDE * dh)
    return jnp.concatenate([r, jnp.roll(r, -1, axis=1)], axis=-1)


def _layer(h, lidx, attn_norm, w_in, lq1, lk1, lq2, lk2, subln, pe_k, pe_v, w1k, w1v, w2k, w2v,
           p_da, p_nsa, w_o, ffn_norm, wq, k1, k2, pu, pv, out_norm):
    b, s, d = h.shape
    t = b * s
    g, hg = NSA_GROUPS, NSA_HG
    lambda_init = 0.8 - 0.6 * math.exp(-0.3 * lidx)
    x2 = h.reshape(t, d)

    qda, kda, vda, qn, kvn, gm, gn = _in_proj(x2, attn_norm.reshape(1, d), _pack_w_in(w_in),
                                              _rope_tables(s), s)
    yda = _diff_attn(qda.reshape(b, s, -1), kda.reshape(b, s, -1), vda.reshape(b, s, -1),
                     lq1.reshape(1, -1), lk1.reshape(1, -1), lq2.reshape(1, -1),
                     lk2.reshape(1, -1), subln.reshape(1, -1), lambda_init)

    kv6 = kvn.reshape(b, s, 6, g, HEAD).transpose(2, 0, 3, 1, 4)
    kc_raw, vc_raw, ks, vs, kw, vw = (kv6[i] for i in range(6))
    kca, vc = _compress(_cmp_blocks(kc_raw), _cmp_blocks(vc_raw),
                        pe_k.reshape(1, -1), pe_v.reshape(1, -1),
                        w1k.astype(BF16), w1v.astype(BF16), w2k.astype(BF16), w2v.astype(BF16))
    ncp = s // CMP_STRIDE
    kca = kca.reshape(b, g, ncp, LANES)
    vc = vc.reshape(b, g, ncp, HEAD)

    q5 = qn.reshape(b, s, g, hg, HEAD).transpose(0, 2, 3, 1, 4)
    qz = jnp.concatenate([q5, jnp.zeros_like(q5)], axis=-1)
    ocmp, bias = _cmp_select(qz, kca, vc, _overlap_table(s))
    onehot = (jnp.arange(s)[:, None] // SEL_BLOCK == jnp.arange(HEAD)[None, :]).astype(BF16)
    ksa = jnp.concatenate([ks, jnp.broadcast_to(onehot, ks.shape)], axis=-1)
    kwa = jnp.concatenate([kw, jnp.zeros_like(kw)], axis=-1)
    yn5 = _sel_win(qz, bias, ksa, vs, kwa, vw, ocmp, gn.reshape(b, s, 2 * LANES))
    yn = yn5.transpose(0, 3, 1, 2, 4).reshape(t, g * hg * HEAD)

    h1, cb, s1t, s2t = _out_proj(yda.reshape(t, -1), yn, gm, x2, p_da.astype(BF16),
                                 p_nsa.astype(BF16), w_o.astype(BF16), ffn_norm.reshape(1, d),
                                 wq.astype(BF16), k1.astype(BF16), k2.astype(BF16))
    idx_t, gate_t = _peer_topk(s1t, s2t)
    idx = idx_t.reshape(PEER_E, t).T
    gate = gate_t.reshape(PEER_E, t).T
    uv = jnp.concatenate([pu, pv], axis=1)
    out = _peer_ffn(idx, gate, cb, h1, out_norm.reshape(1, d), uv)
    return out.reshape(b, s, d)


def kernel(x, attn_norm, w_in, da_lambda_q1, da_lambda_k1, da_lambda_q2, da_lambda_k2, da_subln,
           cmp_pe_k, cmp_pe_v, cmp_w1_k, cmp_w1_v, cmp_w2_k, cmp_w2_v, p_da, p_nsa, w_o,
           ffn_norm, peer_wq, peer_k1, peer_k2, peer_u, peer_v, final_norm):
    depth = attn_norm.shape[0]
    assert depth == 1, "the final norm is fused into the last layer's PEER kernel"
    h = x
    for l in range(depth):
        h = _layer(h, l, attn_norm[l], w_in[l], da_lambda_q1[l], da_lambda_k1[l], da_lambda_q2[l],
                   da_lambda_k2[l], da_subln[l], cmp_pe_k[l], cmp_pe_v[l], cmp_w1_k[l],
                   cmp_w1_v[l], cmp_w2_k[l], cmp_w2_v[l], p_da[l], p_nsa[l], w_o[l], ffn_norm[l],
                   peer_wq[l], peer_k1[l], peer_k2[l], peer_u[l], peer_v[l], final_norm)
    return h
```

```python
import functools
import math

import jax
import jax.numpy as jnp
from jax import lax
from jax.experimental import pallas as pl
from jax.experimental.pallas import tpu as pltpu

F32 = jnp.float32
BF16 = jnp.bfloat16
I32 = jnp.int32

RMS_EPS = 1e-6
ROPE_THETA = 500000.0
ROPE_HALF = 8
HEAD = 64
DA_HEADS = 4
NSA_GROUPS = 2
NSA_HG = 4
CMP_STRIDE = 16
CMP_BLOCK = 32
SEL_BLOCK = 64
SEL_TOPK = 16
WINDOW = 512
FORCE_BONUS = 1e4
NEG_BIG = -1e30
SEL_MASK_BIAS = -32768.0
PEER_HEADS = 8
PEER_NKEYS = 128
PEER_TOPK = 16
LANES = 128
SUBLANES = 8
VMEM_LIMIT = 56 * 1024 * 1024

NT_DIMS = (((1,), (1,)), ((), ()))


def _rmsnorm(x, g):
    return x * lax.rsqrt(jnp.mean(x * x, axis=-1, keepdims=True) + RMS_EPS) * g


def _sigmoid(z):
    return 1.0 / (1.0 + jnp.exp(-z))


def _gelu(z):
    return 0.5 * z * (1.0 + lax.erf(z * (2.0 ** -0.5)))


def _params(*sem):
    return pltpu.CompilerParams(dimension_semantics=sem, vmem_limit_bytes=VMEM_LIMIT)


_QDA0, _KDA0, _VDA0, _QN0, _KVN0, _GM0, _GN0, _WCOLS = 0, 512, 1024, 1536, 2048, 2816, 4864, 5120


def _inproj_body(x_ref, g_ref, w_ref, rc_ref, rs1_ref, rs2_ref,
                 qda_ref, kda_ref, vda_ref, qn_ref, kvn_ref, gm_ref, gn_ref):
    a = _rmsnorm(x_ref[...], g_ref[...]).astype(BF16)
    rc, rs1, rs2 = rc_ref[...], rs1_ref[...], rs2_ref[...]

    def rope(z):
        return (z * rc + pltpu.roll(z, ROPE_HALF, 1) * rs1
                + pltpu.roll(z, LANES - ROPE_HALF, 1) * rs2)

    def proj(c0):
        return jnp.dot(a, w_ref[:, c0:c0 + 256], preferred_element_type=F32)

    def rope2(z):
        return jnp.concatenate([rope(z[:, :LANES]), rope(z[:, LANES:])], axis=1)

    for c in range(2):
        qda_ref[:, c * 256:(c + 1) * 256] = (rope2(proj(_QDA0 + c * 256)) * 0.125).astype(BF16)
        kda_ref[:, c * 256:(c + 1) * 256] = rope2(proj(_KDA0 + c * 256)).astype(BF16)
        vda_ref[:, c * 256:(c + 1) * 256] = proj(_VDA0 + c * 256).astype(BF16)
        qn_ref[:, c * 256:(c + 1) * 256] = (rope2(proj(_QN0 + c * 256)) * 0.125).astype(BF16)
    for c in range(3):
        z = proj(_KVN0 + c * 256)
        kvn_ref[:, c * 256:c * 256 + LANES] = rope(z[:, :LANES]).astype(BF16)
        kvn_ref[:, c * 256 + LANES:(c + 1) * 256] = z[:, LANES:].astype(BF16)
    for c in range(8):
        gm_ref[:, c * 256:(c + 1) * 256] = _sigmoid(proj(_GM0 + c * 256)).astype(BF16)
    gn_ref[...] = _sigmoid(proj(_GN0))


def _pack_w_in(w):
    d = w.shape[0]
    gn = w[:, 2816:2840]
    pad = jnp.zeros((d, LANES - 12), w.dtype)
    return jnp.concatenate([w[:, :2816], w[:, 2840:], gn[:, :12], pad, gn[:, 12:], pad],
                           axis=1).astype(BF16)


def _rope_tables(seq):
    inv = jnp.power(ROPE_THETA, -jnp.arange(ROPE_HALF, dtype=F32) * 2.0 / (2 * ROPE_HALF))
    ang = jnp.arange(seq, dtype=F32)[:, None] * inv[None, :]
    cos, sin = jnp.cos(ang), jnp.sin(ang)
    one = jnp.ones((seq, HEAD - 2 * ROPE_HALF), F32)
    zero8 = jnp.zeros((seq, ROPE_HALF), F32)
    zero48 = jnp.zeros_like(one)
    rc = jnp.concatenate([cos, cos, one], axis=1)
    rs1 = jnp.concatenate([zero8, sin, zero48], axis=1)
    rs2 = jnp.concatenate([-sin, zero8, zero48], axis=1)
    return tuple(jnp.concatenate([t, t], axis=1) for t in (rc, rs1, rs2))


def _in_proj(x2, norm_g, w_packed, rope_tabs, seq):
    t, d = x2.shape
    tm = 512
    nseq = seq // tm
    row = lambda i: (i, 0)
    fixed = lambda i: (0, 0)
    out_shapes = [
        jax.ShapeDtypeStruct((t, 512), BF16), jax.ShapeDtypeStruct((t, 512), BF16),
        jax.ShapeDtypeStruct((t, 512), BF16), jax.ShapeDtypeStruct((t, 512), BF16),
        jax.ShapeDtypeStruct((t, 768), BF16), jax.ShapeDtypeStruct((t, 2048), BF16),
        jax.ShapeDtypeStruct((t, 256), F32)]
    rope_spec = pl.BlockSpec((tm, LANES), lambda i: (i % nseq, 0))
    return pl.pallas_call(
        _inproj_body,
        grid=(t // tm,),
        in_specs=[pl.BlockSpec((tm, d), row), pl.BlockSpec((1, d), fixed),
                  pl.BlockSpec((d, _WCOLS), fixed), rope_spec, rope_spec, rope_spec],
        out_specs=[pl.BlockSpec((tm, s.shape[1]), row) for s in out_shapes],
        out_shape=out_shapes,
        compiler_params=_params("parallel"),
        name="in_proj",
    )(x2, norm_g, w_packed, *rope_tabs)


def _softmax_step(s, vt, m_ref, l_ref, acc_ref):
    m_prev = m_ref[...]
    m_new = jnp.maximum(m_prev, jnp.max(s, axis=0, keepdims=True))
    alpha = jnp.exp(m_prev - m_new)
    p = jnp.exp(s - m_new)
    l_ref[...] = alpha * l_ref[...] + jnp.sum(p, axis=0, keepdims=True)
    acc_ref[...] = alpha * acc_ref[...] + jnp.dot(vt, p.astype(BF16), preferred_element_type=F32)
    m_ref[...] = m_new


def _softmax_reset(m_ref, l_ref, acc_ref):
    m_ref[...] = jnp.full(m_ref.shape, NEG_BIG, F32)
    l_ref[...] = jnp.zeros(l_ref.shape, F32)
    acc_ref[...] = jnp.zeros(acc_ref.shape, F32)


def _key_tile(ref, kt, bk):
    return ref[(0,) * (len(ref.shape) - 2) + (pl.ds(pl.multiple_of(kt * bk, bk), bk), slice(None))]


def _value_tile(ref, kt, bk):
    return ref[(0,) * (len(ref.shape) - 2) + (slice(None), pl.ds(pl.multiple_of(kt * bk, bk), bk))]


def _diffattn_body(lambda_init, bq, bk, q_ref, k_ref, v_ref, lq1_ref, lk1_ref, lq2_ref, lk2_ref,
                   sub_ref, y_ref, qbd_ref, m_ref, l_ref, acc_ref):
    qi = pl.program_id(2)
    qt = q_ref[0]
    sub = lax.broadcasted_iota(I32, qt.shape, 0)
    zero = jnp.zeros_like(qt)
    qbd_ref[:, 0:bq] = jnp.where(sub < HEAD, qt, zero)
    qbd_ref[:, bq:2 * bq] = jnp.where(sub >= HEAD, qt, zero)
    _softmax_reset(m_ref, l_ref, acc_ref)

    def scores(kt):
        return jnp.dot(_key_tile(k_ref, kt, bk), qbd_ref[...], preferred_element_type=F32)

    def body(kt, carry):
        _softmax_step(scores(kt), _value_tile(v_ref, kt, bk), m_ref, l_ref, acc_ref)
        return carry

    first_diag = qi * (bq // bk)
    lax.fori_loop(0, first_diag, body, 0)
    r = lax.broadcasted_iota(I32, (bk, 2 * bq), 0)
    c = lax.broadcasted_iota(I32, (bk, 2 * bq), 1) & (bq - 1)
    for j in range(bq // bk):
        kt = first_diag + j
        causal = j * bk + r <= c
        _softmax_step(jnp.where(causal, scores(kt), NEG_BIG), _value_tile(v_ref, kt, bk),
                      m_ref, l_ref, acc_ref)

    o = acc_ref[...] / l_ref[...]
    lam = (jnp.exp(jnp.sum(lq1_ref[...] * lk1_ref[...], axis=1, keepdims=True))
           - jnp.exp(jnp.sum(lq2_ref[...] * lk2_ref[...], axis=1, keepdims=True)) + lambda_init)
    d = o[:, 0:bq] - lam * o[:, bq:2 * bq]
    ms = jnp.mean(d * d, axis=0, keepdims=True)
    y = d * lax.rsqrt(ms + RMS_EPS) * sub_ref[...] * (1.0 - lambda_init)
    y_ref[0] = y.astype(BF16)


def _diff_attn(qdat, kda, vdat, lq1, lk1, lq2, lk2, subln_col, lambda_init):
    b, s, _ = kda.shape
    bq, bk = 256, 256
    vec = lambda n: pl.BlockSpec((1, n), lambda bi, h, qi: (0, 0))
    dv = 2 * HEAD
    qtile = pl.BlockSpec((1, dv, bq), lambda bi, h, qi: (bi, h, qi))
    return pl.pallas_call(
        functools.partial(_diffattn_body, lambda_init, bq, bk),
        grid=(b, DA_HEADS, s // bq),
        in_specs=[qtile,
                  pl.BlockSpec((1, s, LANES), lambda bi, h, qi: (bi, 0, h)),
                  pl.BlockSpec((1, dv, s), lambda bi, h, qi: (bi, h, 0)),
                  vec(HEAD), vec(HEAD), vec(HEAD), vec(HEAD),
                  pl.BlockSpec((dv, 1), lambda bi, h, qi: (0, 0))],
        out_specs=qtile,
        out_shape=jax.ShapeDtypeStruct((b, DA_HEADS * dv, s), BF16),
        scratch_shapes=[pltpu.VMEM((LANES, 2 * bq), BF16), pltpu.VMEM((1, 2 * bq), F32),
                        pltpu.VMEM((1, 2 * bq), F32), pltpu.VMEM((dv, 2 * bq), F32)],
        compiler_params=_params("parallel", "parallel", "parallel"),
        name="diff_attn",
    )(qdat, kda, vdat, lq1, lk1, lq2, lk2, subln_col)


def _compress_body(xk_ref, xv_ref, pek_ref, pev_ref, w1k_ref, w1v_ref, w2k_ref, w2v_ref,
                   kc_ref, vc_ref):
    def mlp(x_ref, pe_ref, w1_ref, w2_ref):
        blocks = (x_ref[0].astype(F32) + pe_ref[...]).astype(BF16)
        hid = _gelu(jnp.dot(blocks, w1_ref[...], preferred_element_type=F32))
        return jnp.dot(hid.astype(BF16), w2_ref[...], preferred_element_type=F32)

    kc = mlp(xk_ref, pek_ref, w1k_ref, w2k_ref)
    kc_ref[0] = jnp.concatenate([kc, jnp.zeros_like(kc)], axis=1).astype(BF16)
    vc_ref[0] = mlp(xv_ref, pev_ref, w1v_ref, w2v_ref).astype(BF16)


def _compress(xk, xv, pe_k, pe_v, w1k, w1v, w2k, w2v):
    n, ncp, width = xk.shape
    blk = pl.BlockSpec((1, ncp, width), lambda i: (i, 0, 0))
    fixed = lambda shape: pl.BlockSpec(shape, lambda i: (0, 0))
    return pl.pallas_call(
        _compress_body,
        grid=(n,),
        in_specs=[blk, blk, fixed((1, width)), fixed((1, width)), fixed((width, HEAD)),
                  fixed((width, HEAD)), fixed((HEAD, HEAD)), fixed((HEAD, HEAD))],
        out_specs=[pl.BlockSpec((1, ncp, LANES), lambda i: (i, 0, 0)),
                   pl.BlockSpec((1, ncp, HEAD), lambda i: (i, 0, 0))],
        out_shape=[jax.ShapeDtypeStruct((n, ncp, LANES), BF16),
                   jax.ShapeDtypeStruct((n, ncp, HEAD), BF16)],
        compiler_params=_params("parallel"),
        name="compress",
    )(xk, xv, pe_k, pe_v, w1k, w1v, w2k, w2v)


def _heads_on_lanes(qt, bq):
    return jnp.concatenate([qt[h * HEAD:(h + 1) * HEAD, :] for h in range(NSA_HG)], axis=1)


def _cmpsel_body(bq, q_ref, kc_ref, vc_ref, ovl_ref, ocmp_ref, bias_ref):
    qi = pl.program_id(2)
    ncp = kc_ref.shape[2]
    rows = NSA_HG * bq
    q2 = _heads_on_lanes(q_ref[0], bq)
    qz = jnp.concatenate([q2, jnp.zeros_like(q2)], axis=0)
    s = jnp.dot(kc_ref[0, 0], qz, preferred_element_type=F32)
    n = lax.broadcasted_iota(I32, (ncp, rows), 0)
    qpos = qi * bq + (lax.broadcasted_iota(I32, (ncp, rows), 1) & (bq - 1))
    cmask = n * CMP_STRIDE + (CMP_BLOCK - 1) <= qpos
    s = jnp.where(cmask, s, NEG_BIG)
    e = jnp.exp(s - jnp.max(s, axis=0, keepdims=True))
    p = jnp.where(cmask, e / jnp.sum(e, axis=0, keepdims=True), 0.0)
    o = jnp.dot(vc_ref[0, 0], p.astype(BF16), preferred_element_type=F32)
    for h in range(NSA_HG):
        ocmp_ref[0, 0, h * HEAD:(h + 1) * HEAD, :] = o[:, h * bq:(h + 1) * bq]

    psum = p[:, 0:bq] + p[:, bq:2 * bq] + p[:, 2 * bq:3 * bq] + p[:, 3 * bq:4 * bq]
    imp = jnp.dot(ovl_ref[...], psum, preferred_element_type=F32)
    blk = lax.broadcasted_iota(I32, (SEL_BLOCK, bq), 0)
    pos = qi * bq + lax.broadcasted_iota(I32, (SEL_BLOCK, bq), 1)
    cur = lax.shift_right_logical(pos, 6)
    valid = blk <= cur
    forced = (blk == 0) | (blk == cur) | (blk == cur - 1)
    score = jnp.where(valid, imp + jnp.where(forced, FORCE_BONUS, 0.0), -jnp.inf)
    rank = jnp.zeros((SEL_BLOCK, bq), I32)
    for i in range(SEL_BLOCK):
        other = score[i:i + 1, :]
        beats = (other > score) | ((other == score) & (blk > i))
        rank = rank + beats.astype(I32)
    keep = valid & (rank < SEL_TOPK)
    bias_ref[0, 0] = jnp.where(keep, 0.0, SEL_MASK_BIAS).astype(BF16)


def _cmp_select(qnt, kca, vct, overlap_t):
    b, g, ncp, _ = kca.shape
    s = qnt.shape[2]
    bq = 128
    dq = NSA_HG * HEAD
    return pl.pallas_call(
        functools.partial(_cmpsel_body, bq),
        grid=(b, g, s // bq),
        in_specs=[pl.BlockSpec((1, dq, bq), lambda bi, gi, qi: (bi, gi, qi)),
                  pl.BlockSpec((1, 1, ncp, LANES), lambda bi, gi, qi: (bi, gi, 0, 0)),
                  pl.BlockSpec((1, 1, HEAD, ncp), lambda bi, gi, qi: (bi, gi, 0, 0)),
                  pl.BlockSpec((SEL_BLOCK, ncp), lambda bi, gi, qi: (0, 0))],
        out_specs=[pl.BlockSpec((1, 1, dq, bq), lambda bi, gi, qi: (bi, gi, 0, qi)),
                   pl.BlockSpec((1, 1, SEL_BLOCK, bq), lambda bi, gi, qi: (bi, gi, 0, qi))],
        out_shape=[jax.ShapeDtypeStruct((b, g, dq, s), F32),
                   jax.ShapeDtypeStruct((b, g, SEL_BLOCK, s), BF16)],
        compiler_params=_params("parallel", "parallel", "parallel"),
        name="cmp_select",
    )(qnt, kca, vct, overlap_t)


def _selwin_body(bq, q_ref, bias_ref, ks_ref, vs_ref, kw_ref, vw_ref, ocmp_ref, gate_ref,
                 y_ref, m_ref, l_ref, acc_ref):
    qi = pl.program_id(2)
    rows = NSA_HG * bq
    q2 = _heads_on_lanes(q_ref[0], bq)
    bias = bias_ref[0, 0]
    qa = jnp.concatenate([q2, jnp.concatenate([bias] * NSA_HG, axis=1)], axis=0)
    qw = jnp.concatenate([q2, jnp.zeros_like(q2)], axis=0)
    r = lax.broadcasted_iota(I32, (bq, rows), 0)
    c = lax.broadcasted_iota(I32, (bq, rows), 1) & (bq - 1)

    _softmax_reset(m_ref, l_ref, acc_ref)

    def sel_step(kt, carry):
        s = jnp.dot(_key_tile(ks_ref, kt, bq), qa, preferred_element_type=F32)
        _softmax_step(s, _value_tile(vs_ref, kt, bq), m_ref, l_ref, acc_ref)
        return carry

    lax.fori_loop(0, qi, sel_step, 0)
    s = jnp.dot(_key_tile(ks_ref, qi, bq), qa, preferred_element_type=F32)
    _softmax_step(jnp.where(r <= c, s, NEG_BIG), _value_tile(vs_ref, qi, bq), m_ref, l_ref, acc_ref)
    o_sel = acc_ref[...] / l_ref[...]

    _softmax_reset(m_ref, l_ref, acc_ref)
    for back in range(WINDOW // bq, -1, -1):
        kt = qi - back

        @pl.when(kt >= 0)
        def _():
            ktc = jnp.maximum(kt, 0)
            dist = c + back * bq - r
            s = jnp.dot(_key_tile(kw_ref, ktc, bq), qw, preferred_element_type=F32)
            s = jnp.where((dist >= 0) & (dist < WINDOW), s, NEG_BIG)
            _softmax_step(s, _value_tile(vw_ref, ktc, bq), m_ref, l_ref, acc_ref)

    o_win = acc_ref[...] / l_ref[...]
    gate = gate_ref[0]
    for h in range(NSA_HG):
        sl = slice(h * bq, (h + 1) * bq)
        y = (gate[3 * h:3 * h + 1, :] * ocmp_ref[0, 0, h * HEAD:(h + 1) * HEAD, :]
             + gate[3 * h + 1:3 * h + 2, :] * o_sel[:, sl]
             + gate[3 * h + 2:3 * h + 3, :] * o_win[:, sl])
        y_ref[0, h * HEAD:(h + 1) * HEAD, :] = y.astype(BF16)


def _sel_win(qnt, bias_t, ksa, vst, kwa, vwt, ocmp_t, gnt):
    b, g, s, _ = ksa.shape
    bq = 256
    dq = NSA_HG * HEAD
    rows = NSA_HG * bq
    keys = pl.BlockSpec((1, 1, s, LANES), lambda bi, gi, qi: (bi, gi, 0, 0))
    vals = pl.BlockSpec((1, 1, HEAD, s), lambda bi, gi, qi: (bi, gi, 0, 0))
    qtile = pl.BlockSpec((1, dq, bq), lambda bi, gi, qi: (bi, gi, qi))
    return pl.pallas_call(
        functools.partial(_selwin_body, bq),
        grid=(b, g, s // bq),
        in_specs=[qtile,
                  pl.BlockSpec((1, 1, SEL_BLOCK, bq), lambda bi, gi, qi: (bi, gi, 0, qi)),
                  keys, vals, keys, vals,
                  pl.BlockSpec((1, 1, dq, bq), lambda bi, gi, qi: (bi, gi, 0, qi)),
                  pl.BlockSpec((1, LANES, bq), lambda bi, gi, qi: (bi, gi, qi))],
        out_specs=qtile,
        out_shape=jax.ShapeDtypeStruct((b, g * dq, s), BF16),
        scratch_shapes=[pltpu.VMEM((1, rows), F32), pltpu.VMEM((1, rows), F32),
                        pltpu.VMEM((HEAD, rows), F32)],
        compiler_params=_params("parallel", "parallel", "parallel"),
        name="sel_win",
    )(qnt, bias_t, ksa, vst, kwa, vwt, ocmp_t, gnt)


def _outproj_body(yda_ref, yn_ref, gm_ref, x_ref, pda_ref, pnsa_ref, wo_ref, fg_ref, wq_ref,
                  k1_ref, k2_ref, h1_ref, c_ref, s1_ref, s2_ref):
    d = x_ref.shape[1]
    a = jnp.dot(yda_ref[...], pda_ref[...], preferred_element_type=F32)
    bn = jnp.dot(yn_ref[...], pnsa_ref[...], preferred_element_type=F32)
    merged = gm_ref[:, :d].astype(F32) * a + gm_ref[:, d:].astype(F32) * bn
    h1 = x_ref[...] + jnp.dot(merged.astype(BF16), wo_ref[...], preferred_element_type=F32)
    h1_ref[...] = h1
    c = _rmsnorm(h1, fg_ref[...])
    c_ref[...] = c
    cb = c.astype(BF16)
    for h in range(PEER_HEADS):
        qh = jnp.dot(cb, wq_ref[:, h * 256:(h + 1) * 256], preferred_element_type=F32).astype(BF16)
        s1_ref[h] = lax.dot_general(k1_ref[...], qh[:, :LANES], NT_DIMS, preferred_element_type=F32)
        s2_ref[h] = lax.dot_general(k2_ref[...], qh[:, LANES:], NT_DIMS, preferred_element_type=F32)


def _out_proj(yda, yn, gm, x2, pda, pnsa, wo, ffn_g, wq, k1, k2):
    t, d = x2.shape
    tm = 256
    row = lambda w: pl.BlockSpec((tm, w), lambda i: (i, 0))
    fixed = lambda a: pl.BlockSpec(a.shape, lambda i: (0, 0))
    sspec = pl.BlockSpec((PEER_HEADS, PEER_NKEYS, tm), lambda i: (0, 0, i))
    return pl.pallas_call(
        _outproj_body,
        grid=(t // tm,),
        in_specs=[row(512), row(512), row(2 * d), row(d), fixed(pda), fixed(pnsa), fixed(wo),
                  fixed(ffn_g), fixed(wq), fixed(k1), fixed(k2)],
        out_specs=[row(d), row(d), sspec, sspec],
        out_shape=[jax.ShapeDtypeStruct((t, d), F32), jax.ShapeDtypeStruct((t, d), F32),
                   jax.ShapeDtypeStruct((PEER_HEADS, PEER_NKEYS, t), F32),
                   jax.ShapeDtypeStruct((PEER_HEADS, PEER_NKEYS, t), F32)],
        compiler_params=_params("parallel"),
        name="out_proj",
    )(yda, yn, gm, x2, pda, pnsa, wo, ffn_g, wq, k1, k2)


def _batcher_pairs(n):
    pairs = []

    def merge(lo, hi, r):
        step = r * 2
        if step < hi - lo:
            merge(lo, hi, step)
            merge(lo + r, hi, step)
            pairs.extend((i, i + r) for i in range(lo + r, hi - r, step))
        else:
            pairs.append((lo, lo + r))

    def sort(lo, hi):
        if hi - lo >= 1:
            mid = lo + (hi - lo) // 2
            sort(lo, mid)
            sort(mid + 1, hi)
            merge(lo, hi, 1)

    sort(0, n - 1)
    return pairs


_NET16 = _batcher_pairs(PEER_TOPK)


def _cmpx(a, b):
    c = a[0] >= b[0]
    return ((jnp.where(c, a[0], b[0]), jnp.where(c, a[1], b[1])),
            (jnp.where(c, b[0], a[0]), jnp.where(c, b[1], a[1])))


def _sort_lists(lists, n_real):
    lists = list(lists)
    for i, j in _NET16:
        if j < n_real:
            lists[i], lists[j] = _cmpx(lists[i], lists[j])
    return lists


def _merge_sublanes(lists):
    k = PEER_TOPK
    for dist in (4, 2, 1):
        other = [(pltpu.roll(v, dist, 0), pltpu.roll(ix, dist, 0)) for v, ix in lists]
        lists = [_cmpx(lists[i], other[k - 1 - i])[0] for i in range(k)]
        step = k // 2
        while step >= 1:
            for i in range(k):
                if i & step == 0:
                    lists[i], lists[i + step] = _cmpx(lists[i], lists[i + step])
            step //= 2
    return lists


def _spread(lists, off, sub):
    v, ix = lists[off]
    for r in range(1, SUBLANES):
        sel = sub == r
        v = jnp.where(sel, lists[off + r][0], v)
        ix = jnp.where(sel, lists[off + r][1], ix)
    return v, ix


def _peertopk_body(tt, s1_ref, s2_ref, idx_ref, gate_ref):
    sub = lax.broadcasted_iota(I32, (SUBLANES, LANES), 0)
    groups = PEER_NKEYS // SUBLANES

    def top16(ref, h, lanes):
        lists = [(ref[h, v * SUBLANES:(v + 1) * SUBLANES, lanes], sub + v * SUBLANES)
                 for v in range(groups)]
        return _merge_sublanes(_sort_lists(lists, groups))

    def unit(u, carry):
        h = u // (tt // LANES)
        lanes = pl.ds(pl.multiple_of((u % (tt // LANES)) * LANES, LANES), LANES)
        l1 = top16(s1_ref, h, lanes)
        l2 = top16(s2_ref, h, lanes)
        v2lo, v2hi, v1hi = _spread(l2, 0, sub), _spread(l2, SUBLANES, sub), _spread(l1, SUBLANES, sub)
        cands = [(l1[a][0] + v2lo[0], l1[a][1] * PEER_NKEYS + v2lo[1]) for a in range(SUBLANES)]
        cands.append((l1[0][0] + v2hi[0], l1[0][1] * PEER_NKEYS + v2hi[1]))
        cands.append((v1hi[0] + l2[0][0], v1hi[1] * PEER_NKEYS + l2[0][1]))
        n_real = len(cands)
        filler = (jnp.full((SUBLANES, LANES), -jnp.inf, F32), jnp.zeros((SUBLANES, LANES), I32))
        cands += [filler] * (PEER_TOPK - n_real)
        best = _merge_sublanes(_sort_lists(cands, n_real))
        ex = [jnp.exp(v - best[0][0]) for v, _ in best]
        z = ex[0]
        for e in ex[1:]:
            z = z + e
        gl = [(e / z, ix) for e, (_, ix) in zip(ex, best)]
        lo, hi = _spread(gl, 0, sub), _spread(gl, SUBLANES, sub)
        gate_ref[h, :, lanes] = jnp.concatenate([lo[0], hi[0]], axis=0)
        idx_ref[h, :, lanes] = jnp.concatenate([lo[1], hi[1]], axis=0)
        return carry

    lax.fori_loop(0, PEER_HEADS * (tt // LANES), unit, 0)


def _peer_topk(s1t, s2t):
    _, _, t = s1t.shape
    tt = 512
    spec_in = pl.BlockSpec((PEER_HEADS, PEER_NKEYS, tt), lambda i: (0, 0, i))
    spec_out = pl.BlockSpec((PEER_HEADS, PEER_TOPK, tt), lambda i: (0, 0, i))
    return pl.pallas_call(
        functools.partial(_peertopk_body, tt),
        grid=(t // tt,),
        in_specs=[spec_in, spec_in],
        out_specs=[spec_out, spec_out],
        out_shape=[jax.ShapeDtypeStruct((PEER_HEADS, PEER_TOPK, t), I32),
                   jax.ShapeDtypeStruct((PEER_HEADS, PEER_TOPK, t), F32)],
        compiler_params=_params("parallel"),
        name="peer_topk",
    )(s1t, s2t)


PEER_E = PEER_HEADS * PEER_TOPK


def _peerffn_body(tb, idx_ref, gate_ref, c_ref, h1_ref, fg_ref, uv_ref, out_ref, buf_ref,
                  acc_ref, sem_ref):
    d = c_ref.shape[1]

    def row_copy(t, j, slot):
        e = idx_ref[t, j]
        return pltpu.make_async_copy(uv_ref.at[pl.ds(e, 1), :],
                                     buf_ref.at[slot, pl.ds(j, 1), :], sem_ref.at[slot])

    def issue(t, slot, j0, j1):
        for j in range(j0, j1):
            row_copy(t, j, slot).start(priority=j % 2)

    def wait_all(slot):
        pltpu.make_async_copy(uv_ref.at[pl.ds(0, PEER_E), :], buf_ref.at[slot],
                              sem_ref.at[slot]).wait()

    nchunk = d // LANES
    group = PEER_E // (2 * nchunk)

    def pair(i, prefetch):
        par = i & 1
        toks = (2 * i, 2 * i + 1)
        slots = (2 * par, 2 * par + 1)
        nslots = (2 - 2 * par, 3 - 2 * par)
        pieces = iter([(a, g) for g in range(2 * nchunk) for a in range(2)])

        def issue_next():
            a, g = next(pieces)
            if prefetch:
                issue(toks[a] + 2, nslots[a], g * group, (g + 1) * group)

        wait_all(slots[0])
        wait_all(slots[1])
        xs = [jnp.broadcast_to(c_ref[pl.ds(t, 1), :], (SUBLANES, d)).astype(BF16) for t in toks]
        hid = [jnp.zeros((SUBLANES, PEER_E), F32) for _ in toks]
        for kc in range(nchunk):
            cols = slice(kc * LANES, (kc + 1) * LANES)
            for a in range(2):
                u = buf_ref[slots[a], :, cols].astype(BF16)
                hid[a] = hid[a] + lax.dot_general(xs[a][:, cols], u, NT_DIMS,
                                                  preferred_element_type=F32)
                issue_next()
        w = [(_gelu(hid[a]) * gate_ref[pl.ds(toks[a], 1), :]).astype(BF16) for a in range(2)]
        outs = ([], [])
        for nc in range(nchunk):
            for a in range(2):
                v = buf_ref[slots[a], :, d + nc * LANES:d + (nc + 1) * LANES].astype(BF16)
                outs[a].append(jnp.dot(w[a], v, preferred_element_type=F32)[0:1])
                issue_next()
        for a in range(2):
            acc_ref[pl.ds(toks[a], 1), :] = jnp.concatenate(outs[a], axis=1)

    issue(0, 0, 0, PEER_E)
    issue(1, 1, 0, PEER_E)

    def body(i, carry):
        pair(i, True)
        return carry

    lax.fori_loop(0, tb // 2 - 1, body, 0)
    pair(tb // 2 - 1, False)
    out_ref[...] = _rmsnorm(h1_ref[...] + acc_ref[...], fg_ref[...])


def _peer_ffn(idx, gate, cb, h1, final_g, uv):
    t, d = h1.shape
    tb = 64
    row = lambda w: pl.BlockSpec((tb, w), lambda i: (i, 0))
    return pl.pallas_call(
        functools.partial(_peerffn_body, tb),
        grid=(t // tb,),
        in_specs=[pl.BlockSpec((tb, PEER_E), lambda i: (i, 0), memory_space=pltpu.SMEM),
                  row(PEER_E), row(d), row(d), pl.BlockSpec((1, d), lambda i: (0, 0)),
                  pl.BlockSpec(memory_space=pl.ANY)],
        out_specs=row(d),
        out_shape=jax.ShapeDtypeStruct((t, d), F32),
        scratch_shapes=[pltpu.VMEM((4, PEER_E, 2 * d), F32), pltpu.VMEM((tb, d), F32),
                        pltpu.SemaphoreType.DMA((4,))],
        compiler_params=_params("arbitrary"),
        name="peer_ffn",
    )(idx, gate, cb, h1, final_g, uv)


def _overlap_table(seq):
    ci = jnp.arange(seq // CMP_STRIDE)[None, :] * CMP_STRIDE
    sj = jnp.arange(SEL_BLOCK)[:, None] * SEL_BLOCK
    return ((ci < sj + SEL_BLOCK) & (ci + CMP_BLOCK > sj)).astype(F32)


def _cmp_blocks(kv):
    b, g, s, dh = kv.shape
    r = kv.reshape(b * g, s // CMP_STRIDE, CMP_STRIDE * dh)
    return jnp.concatenate([r, jnp.roll(r, -1, axis=1)], axis=-1)


def _layer(h, lidx, attn_norm, w_in, lq1, lk1, lq2, lk2, subln, pe_k, pe_v, w1k, w1v, w2k, w2v,
           p_da, p_nsa, w_o, ffn_norm, wq, k1, k2, pu, pv, out_norm):
    b, s, d = h.shape
    t = b * s
    g, hg = NSA_GROUPS, NSA_HG
    lambda_init = 0.8 - 0.6 * math.exp(-0.3 * lidx)
    x2 = h.reshape(t, d)

    qda, kda, vda, qn, kvn, gm, gn = _in_proj(x2, attn_norm.reshape(1, d), _pack_w_in(w_in),
                                              _rope_tables(s), s)
    tr = lambda a2: jnp.swapaxes(a2.reshape(b, s, -1), 1, 2)
    ydat = _diff_attn(tr(qda), kda.reshape(b, s, -1), tr(vda),
                      lq1.reshape(1, -1), lk1.reshape(1, -1), lq2.reshape(1, -1),
                      lk2.reshape(1, -1), subln.reshape(-1, 1), lambda_init)
    yda = jnp.swapaxes(ydat, 1, 2).reshape(t, -1)

    assert s // SEL_BLOCK <= SEL_BLOCK, "selection bias rows hold at most 64 blocks"
    kv6 = kvn.reshape(b, s, 6, g, HEAD).transpose(2, 0, 3, 1, 4)
    kc_raw, vc_raw, ks, vs, kw, vw = (kv6[i] for i in range(6))
    kca, vc = _compress(_cmp_blocks(kc_raw), _cmp_blocks(vc_raw),
                        pe_k.reshape(1, -1), pe_v.reshape(1, -1),
                        w1k.astype(BF16), w1v.astype(BF16), w2k.astype(BF16), w2v.astype(BF16))
    ncp = s // CMP_STRIDE
    kca = kca.reshape(b, g, ncp, LANES)
    vct = jnp.swapaxes(vc.reshape(b, g, ncp, HEAD), 2, 3)

    qnt = tr(qn)
    ocmp_t, bias_t = _cmp_select(qnt, kca, vct, _overlap_table(s))
    onehot = (jnp.arange(s)[:, None] // SEL_BLOCK == jnp.arange(HEAD)[None, :]).astype(BF16)
    ksa = jnp.concatenate([ks, jnp.broadcast_to(onehot, ks.shape)], axis=-1)
    kwa = jnp.concatenate([kw, jnp.zeros_like(kw)], axis=-1)
    ynt = _sel_win(qnt, bias_t, ksa, jnp.swapaxes(vs, 2, 3), kwa, jnp.swapaxes(vw, 2, 3),
                   ocmp_t, tr(gn))
    yn = jnp.swapaxes(ynt, 1, 2).reshape(t, -1)

    h1, cb, s1t, s2t = _out_proj(yda, yn, gm, x2, p_da.astype(BF16),
                                 p_nsa.astype(BF16), w_o.astype(BF16), ffn_norm.reshape(1, d),
                                 wq.astype(BF16), k1.astype(BF16), k2.astype(BF16))
    idx_t, gate_t = _peer_topk(s1t, s2t)
    idx = idx_t.reshape(PEER_E, t).T
    gate = gate_t.reshape(PEER_E, t).T
    uv = jnp.concatenate([pu, pv], axis=1)
    out = _peer_ffn(idx, gate, cb, h1, out_norm.reshape(1, d), uv)
    return out.reshape(b, s, d)


def kernel(x, attn_norm, w_in, da_lambda_q1, da_lambda_k1, da_lambda_q2, da_lambda_k2, da_subln,
           cmp_pe_k, cmp_pe_v, cmp_w1_k, cmp_w1_v, cmp_w2_k, cmp_w2_v, p_da, p_nsa, w_o,
           ffn_norm, peer_wq, peer_k1, peer_k2, peer_u, peer_v, final_norm):
    depth = attn_norm.shape[0]
    assert depth == 1, "the final norm is fused into the last layer's PEER kernel"
    h = x
    for l in range(depth):
        h = _layer(h, l, attn_norm[l], w_in[l], da_lambda_q1[l], da_lambda_k1[l], da_lambda_q2[l],
                   da_lambda_k2[l], da_subln[l], cmp_pe_k[l], cmp_pe_v[l], cmp_w1_k[l],
                   cmp_w1_v[l], cmp_w2_k[l], cmp_w2_v[l], p_da[l], p_nsa[l], w_o[l], ffn_norm[l],
                   peer_wq[l], peer_k1[l], peer_k2[l], peer_u[l], peer_v[l], final_norm)
    return h
```

```python
import functools
import math

import jax
import jax.numpy as jnp
from jax import lax
from jax.experimental import pallas as pl
from jax.experimental.pallas import tpu as pltpu

F32 = jnp.float32
BF16 = jnp.bfloat16
I32 = jnp.int32

RMS_EPS = 1e-6
ROPE_THETA = 500000.0
ROPE_HALF = 8
HEAD = 64
DA_HEADS = 4
NSA_GROUPS = 2
NSA_HG = 4
CMP_STRIDE = 16
CMP_BLOCK = 32
SEL_BLOCK = 64
SEL_TOPK = 16
WINDOW = 512
FORCE_BONUS = 1e4
NEG_BIG = -1e30
SEL_MASK_BIAS = -32768.0
PEER_HEADS = 8
PEER_NKEYS = 128
PEER_TOPK = 16
LANES = 128
SUBLANES = 8
VMEM_LIMIT = 56 * 1024 * 1024

NT_DIMS = (((1,), (1,)), ((), ()))


def _rmsnorm(x, g):
    return x * lax.rsqrt(jnp.mean(x * x, axis=-1, keepdims=True) + RMS_EPS) * g


def _sigmoid(z):
    return 1.0 / (1.0 + jnp.exp(-z))


def _gelu(z):
    return 0.5 * z * (1.0 + lax.erf(z * (2.0 ** -0.5)))


def _params(*sem):
    return pltpu.CompilerParams(dimension_semantics=sem, vmem_limit_bytes=VMEM_LIMIT)


_QDA0, _KDA0, _VDA0, _QN0, _KVN0, _GM0, _GN0, _WCOLS = 0, 512, 1024, 1536, 2048, 2816, 4864, 5120


def _inproj_body(x_ref, g_ref, w_ref, rc_ref, rs1_ref, rs2_ref,
                 qda_ref, kda_ref, vda_ref, qn_ref, kvn_ref, gm_ref, gn_ref):
    a = _rmsnorm(x_ref[...], g_ref[...]).astype(BF16)
    rc, rs1, rs2 = rc_ref[...], rs1_ref[...], rs2_ref[...]

    def rope(z):
        return (z * rc + pltpu.roll(z, ROPE_HALF, 1) * rs1
                + pltpu.roll(z, LANES - ROPE_HALF, 1) * rs2)

    def proj(c0):
        return jnp.dot(a, w_ref[:, c0:c0 + 256], preferred_element_type=F32)

    def rope2(z):
        return jnp.concatenate([rope(z[:, :LANES]), rope(z[:, LANES:])], axis=1)

    for c in range(2):
        qda_ref[:, c * 256:(c + 1) * 256] = (rope2(proj(_QDA0 + c * 256)) * 0.125).astype(BF16)
        kda_ref[:, c * 256:(c + 1) * 256] = rope2(proj(_KDA0 + c * 256)).astype(BF16)
        vda_ref[:, c * 256:(c + 1) * 256] = proj(_VDA0 + c * 256).astype(BF16)
        qn_ref[:, c * 256:(c + 1) * 256] = (rope2(proj(_QN0 + c * 256)) * 0.125).astype(BF16)
    for c in range(3):
        z = proj(_KVN0 + c * 256)
        kvn_ref[:, c * 256:c * 256 + LANES] = rope(z[:, :LANES]).astype(BF16)
        kvn_ref[:, c * 256 + LANES:(c + 1) * 256] = z[:, LANES:].astype(BF16)
    for c in range(8):
        gm_ref[:, c * 256:(c + 1) * 256] = _sigmoid(proj(_GM0 + c * 256)).astype(BF16)
    gn_ref[...] = _sigmoid(proj(_GN0))


def _pack_w_in(w):
    d = w.shape[0]
    gn = w[:, 2816:2840]
    pad = jnp.zeros((d, LANES - 12), w.dtype)
    return jnp.concatenate([w[:, :2816], w[:, 2840:], gn[:, :12], pad, gn[:, 12:], pad],
                           axis=1).astype(BF16)


def _rope_tables(seq):
    inv = jnp.power(ROPE_THETA, -jnp.arange(ROPE_HALF, dtype=F32) * 2.0 / (2 * ROPE_HALF))
    ang = jnp.arange(seq, dtype=F32)[:, None] * inv[None, :]
    cos, sin = jnp.cos(ang), jnp.sin(ang)
    one = jnp.ones((seq, HEAD - 2 * ROPE_HALF), F32)
    zero8 = jnp.zeros((seq, ROPE_HALF), F32)
    zero48 = jnp.zeros_like(one)
    rc = jnp.concatenate([cos, cos, one], axis=1)
    rs1 = jnp.concatenate([zero8, sin, zero48], axis=1)
    rs2 = jnp.concatenate([-sin, zero8, zero48], axis=1)
    return tuple(jnp.concatenate([t, t], axis=1) for t in (rc, rs1, rs2))


def _in_proj(x2, norm_g, w_packed, rope_tabs, seq):
    t, d = x2.shape
    tm = 512
    nseq = seq // tm
    row = lambda i: (i, 0)
    fixed = lambda i: (0, 0)
    out_shapes = [
        jax.ShapeDtypeStruct((t, 512), BF16), jax.ShapeDtypeStruct((t, 512), BF16),
        jax.ShapeDtypeStruct((t, 512), BF16), jax.ShapeDtypeStruct((t, 512), BF16),
        jax.ShapeDtypeStruct((t, 768), BF16), jax.ShapeDtypeStruct((t, 2048), BF16),
        jax.ShapeDtypeStruct((t, 256), F32)]
    rope_spec = pl.BlockSpec((tm, LANES), lambda i: (i % nseq, 0))
    return pl.pallas_call(
        _inproj_body,
        grid=(t // tm,),
        in_specs=[pl.BlockSpec((tm, d), row), pl.BlockSpec((1, d), fixed),
                  pl.BlockSpec((d, _WCOLS), fixed), rope_spec, rope_spec, rope_spec],
        out_specs=[pl.BlockSpec((tm, s.shape[1]), row) for s in out_shapes],
        out_shape=out_shapes,
        compiler_params=_params("parallel"),
        name="in_proj",
    )(x2, norm_g, w_packed, *rope_tabs)


def _softmax_step(s, vt, m_ref, l_ref, acc_ref):
    m_prev = m_ref[...]
    m_new = jnp.maximum(m_prev, jnp.max(s, axis=0, keepdims=True))
    alpha = jnp.exp(m_prev - m_new)
    p = jnp.exp(s - m_new)
    l_ref[...] = alpha * l_ref[...] + jnp.sum(p, axis=0, keepdims=True)
    acc_ref[...] = alpha * acc_ref[...] + jnp.dot(vt, p.astype(BF16), preferred_element_type=F32)
    m_ref[...] = m_new


def _softmax_reset(m_ref, l_ref, acc_ref):
    m_ref[...] = jnp.full(m_ref.shape, NEG_BIG, F32)
    l_ref[...] = jnp.zeros(l_ref.shape, F32)
    acc_ref[...] = jnp.zeros(acc_ref.shape, F32)


def _key_tile(ref, kt, bk):
    return ref[(0,) * (len(ref.shape) - 2) + (pl.ds(pl.multiple_of(kt * bk, bk), bk), slice(None))]


def _value_tile(ref, kt, bk):
    return ref[(0,) * (len(ref.shape) - 2) + (slice(None), pl.ds(pl.multiple_of(kt * bk, bk), bk))]


def _diffattn_body(lambda_init, bq, bk, q_ref, k_ref, v_ref, lq1_ref, lk1_ref, lq2_ref, lk2_ref,
                   sub_ref, y_ref, qbd_ref, m_ref, l_ref, acc_ref):
    qi = pl.program_id(2)
    qt = q_ref[0]
    sub = lax.broadcasted_iota(I32, qt.shape, 0)
    zero = jnp.zeros_like(qt)
    qbd_ref[:, 0:bq] = jnp.where(sub < HEAD, qt, zero)
    qbd_ref[:, bq:2 * bq] = jnp.where(sub >= HEAD, qt, zero)
    _softmax_reset(m_ref, l_ref, acc_ref)

    def scores(kt):
        return jnp.dot(_key_tile(k_ref, kt, bk), qbd_ref[...], preferred_element_type=F32)

    def body(kt, carry):
        _softmax_step(scores(kt), _value_tile(v_ref, kt, bk), m_ref, l_ref, acc_ref)
        return carry

    first_diag = qi * (bq // bk)
    lax.fori_loop(0, first_diag, body, 0)
    r = lax.broadcasted_iota(I32, (bk, 2 * bq), 0)
    c = lax.broadcasted_iota(I32, (bk, 2 * bq), 1) & (bq - 1)
    for j in range(bq // bk):
        kt = first_diag + j
        causal = j * bk + r <= c
        _softmax_step(jnp.where(causal, scores(kt), NEG_BIG), _value_tile(v_ref, kt, bk),
                      m_ref, l_ref, acc_ref)

    o = acc_ref[...] / l_ref[...]
    lam = (jnp.exp(jnp.sum(lq1_ref[...] * lk1_ref[...], axis=1, keepdims=True))
           - jnp.exp(jnp.sum(lq2_ref[...] * lk2_ref[...], axis=1, keepdims=True)) + lambda_init)
    d = o[:, 0:bq] - lam * o[:, bq:2 * bq]
    ms = jnp.mean(d * d, axis=0, keepdims=True)
    y = d * lax.rsqrt(ms + RMS_EPS) * sub_ref[...] * (1.0 - lambda_init)
    y_ref[0] = y.astype(BF16)


def _diff_attn(qdat, kda, vdat, lq1, lk1, lq2, lk2, subln_col, lambda_init):
    b, s, _ = kda.shape
    bq, bk = 256, 256
    vec = lambda n: pl.BlockSpec((1, n), lambda bi, h, qi: (0, 0))
    dv = 2 * HEAD
    qtile = pl.BlockSpec((1, dv, bq), lambda bi, h, qi: (bi, h, qi))
    return pl.pallas_call(
        functools.partial(_diffattn_body, lambda_init, bq, bk),
        grid=(b, DA_HEADS, s // bq),
        in_specs=[qtile,
                  pl.BlockSpec((1, s, LANES), lambda bi, h, qi: (bi, 0, h)),
                  pl.BlockSpec((1, dv, s), lambda bi, h, qi: (bi, h, 0)),
                  vec(HEAD), vec(HEAD), vec(HEAD), vec(HEAD),
                  pl.BlockSpec((dv, 1), lambda bi, h, qi: (0, 0))],
        out_specs=qtile,
        out_shape=jax.ShapeDtypeStruct((b, DA_HEADS * dv, s), BF16),
        scratch_shapes=[pltpu.VMEM((LANES, 2 * bq), BF16), pltpu.VMEM((1, 2 * bq), F32),
                        pltpu.VMEM((1, 2 * bq), F32), pltpu.VMEM((dv, 2 * bq), F32)],
        compiler_params=_params("parallel", "parallel", "parallel"),
        name="diff_attn",
    )(qdat, kda, vdat, lq1, lk1, lq2, lk2, subln_col)


def _compress_body(xk_ref, xv_ref, pek_ref, pev_ref, w1k_ref, w1v_ref, w2k_ref, w2v_ref,
                   kc_ref, vc_ref):
    def mlp(x_ref, pe_ref, w1_ref, w2_ref):
        blocks = (x_ref[0].astype(F32) + pe_ref[...]).astype(BF16)
        hid = _gelu(jnp.dot(blocks, w1_ref[...], preferred_element_type=F32))
        return jnp.dot(hid.astype(BF16), w2_ref[...], preferred_element_type=F32)

    kc = mlp(xk_ref, pek_ref, w1k_ref, w2k_ref)
    kc_ref[0] = jnp.concatenate([kc, jnp.zeros_like(kc)], axis=1).astype(BF16)
    vc_ref[0] = mlp(xv_ref, pev_ref, w1v_ref, w2v_ref).astype(BF16)


def _compress(xk, xv, pe_k, pe_v, w1k, w1v, w2k, w2v):
    n, ncp, width = xk.shape
    blk = pl.BlockSpec((1, ncp, width), lambda i: (i, 0, 0))
    fixed = lambda shape: pl.BlockSpec(shape, lambda i: (0, 0))
    return pl.pallas_call(
        _compress_body,
        grid=(n,),
        in_specs=[blk, blk, fixed((1, width)), fixed((1, width)), fixed((width, HEAD)),
                  fixed((width, HEAD)), fixed((HEAD, HEAD)), fixed((HEAD, HEAD))],
        out_specs=[pl.BlockSpec((1, ncp, LANES), lambda i: (i, 0, 0)),
                   pl.BlockSpec((1, ncp, HEAD), lambda i: (i, 0, 0))],
        out_shape=[jax.ShapeDtypeStruct((n, ncp, LANES), BF16),
                   jax.ShapeDtypeStruct((n, ncp, HEAD), BF16)],
        compiler_params=_params("parallel"),
        name="compress",
    )(xk, xv, pe_k, pe_v, w1k, w1v, w2k, w2v)


def _heads_on_lanes(qt, bq):
    return jnp.concatenate([qt[h * HEAD:(h + 1) * HEAD, :] for h in range(NSA_HG)], axis=1)


def _cmpsel_body(bq, q_ref, kc_ref, vc_ref, ovl_ref, ocmp_ref, bias_ref):
    qi = pl.program_id(2)
    ncp = kc_ref.shape[2]
    rows = NSA_HG * bq
    q2 = _heads_on_lanes(q_ref[0], bq)
    qz = jnp.concatenate([q2, jnp.zeros_like(q2)], axis=0)
    s = jnp.dot(kc_ref[0, 0], qz, preferred_element_type=F32)
    n = lax.broadcasted_iota(I32, (ncp, rows), 0)
    qpos = qi * bq + (lax.broadcasted_iota(I32, (ncp, rows), 1) & (bq - 1))
    cmask = n * CMP_STRIDE + (CMP_BLOCK - 1) <= qpos
    s = jnp.where(cmask, s, NEG_BIG)
    e = jnp.exp(s - jnp.max(s, axis=0, keepdims=True))
    p = jnp.where(cmask, e / jnp.sum(e, axis=0, keepdims=True), 0.0)
    o = jnp.dot(vc_ref[0, 0], p.astype(BF16), preferred_element_type=F32)
    for h in range(NSA_HG):
        ocmp_ref[0, 0, h * HEAD:(h + 1) * HEAD, :] = o[:, h * bq:(h + 1) * bq]

    psum = p[:, 0:bq] + p[:, bq:2 * bq] + p[:, 2 * bq:3 * bq] + p[:, 3 * bq:4 * bq]
    imp = jnp.dot(ovl_ref[...], psum, preferred_element_type=F32)
    blk = lax.broadcasted_iota(I32, (SEL_BLOCK, bq), 0)
    pos = qi * bq + lax.broadcasted_iota(I32, (SEL_BLOCK, bq), 1)
    cur = lax.shift_right_logical(pos, 6)
    valid = blk <= cur
    forced = (blk == 0) | (blk == cur) | (blk == cur - 1)
    score = jnp.where(valid, imp + jnp.where(forced, FORCE_BONUS, 0.0), -jnp.inf)
    rank = jnp.zeros((SEL_BLOCK, bq), I32)
    for i in range(SEL_BLOCK):
        other = score[i:i + 1, :]
        beats = (other > score) | ((other == score) & (blk > i))
        rank = rank + beats.astype(I32)
    keep = valid & (rank < SEL_TOPK)
    bias_ref[0, 0] = jnp.where(keep, 0.0, SEL_MASK_BIAS).astype(BF16)


def _cmp_select(qnt, kca, vct, overlap_t):
    b, g, ncp, _ = kca.shape
    s = qnt.shape[2]
    bq = 128
    dq = NSA_HG * HEAD
    return pl.pallas_call(
        functools.partial(_cmpsel_body, bq),
        grid=(b, g, s // bq),
        in_specs=[pl.BlockSpec((1, dq, bq), lambda bi, gi, qi: (bi, gi, qi)),
                  pl.BlockSpec((1, 1, ncp, LANES), lambda bi, gi, qi: (bi, gi, 0, 0)),
                  pl.BlockSpec((1, 1, HEAD, ncp), lambda bi, gi, qi: (bi, gi, 0, 0)),
                  pl.BlockSpec((SEL_BLOCK, ncp), lambda bi, gi, qi: (0, 0))],
        out_specs=[pl.BlockSpec((1, 1, dq, bq), lambda bi, gi, qi: (bi, gi, 0, qi)),
                   pl.BlockSpec((1, 1, SEL_BLOCK, bq), lambda bi, gi, qi: (bi, gi, 0, qi))],
        out_shape=[jax.ShapeDtypeStruct((b, g, dq, s), F32),
                   jax.ShapeDtypeStruct((b, g, SEL_BLOCK, s), BF16)],
        compiler_params=_params("parallel", "parallel", "parallel"),
        name="cmp_select",
    )(qnt, kca, vct, overlap_t)


def _selwin_body(bq, q_ref, bias_ref, ks_ref, vs_ref, kw_ref, vw_ref, ocmp_ref, gate_ref,
                 y_ref, m_ref, l_ref, acc_ref):
    qi = pl.program_id(2)
    rows = NSA_HG * bq
    q2 = _heads_on_lanes(q_ref[0], bq)
    bias = bias_ref[0, 0]
    qa = jnp.concatenate([q2, jnp.concatenate([bias] * NSA_HG, axis=1)], axis=0)
    qw = jnp.concatenate([q2, jnp.zeros_like(q2)], axis=0)
    r = lax.broadcasted_iota(I32, (bq, rows), 0)
    c = lax.broadcasted_iota(I32, (bq, rows), 1) & (bq - 1)

    _softmax_reset(m_ref, l_ref, acc_ref)

    def sel_step(kt, carry):
        s = jnp.dot(_key_tile(ks_ref, kt, bq), qa, preferred_element_type=F32)
        _softmax_step(s, _value_tile(vs_ref, kt, bq), m_ref, l_ref, acc_ref)
        return carry

    lax.fori_loop(0, qi, sel_step, 0)
    s = jnp.dot(_key_tile(ks_ref, qi, bq), qa, preferred_element_type=F32)
    _softmax_step(jnp.where(r <= c, s, NEG_BIG), _value_tile(vs_ref, qi, bq), m_ref, l_ref, acc_ref)
    o_sel = acc_ref[...] / l_ref[...]

    _softmax_reset(m_ref, l_ref, acc_ref)
    for back in range(WINDOW // bq, -1, -1):
        kt = qi - back

        @pl.when(kt >= 0)
        def _():
            ktc = jnp.maximum(kt, 0)
            dist = c + back * bq - r
            s = jnp.dot(_key_tile(kw_ref, ktc, bq), qw, preferred_element_type=F32)
            s = jnp.where((dist >= 0) & (dist < WINDOW), s, NEG_BIG)
            _softmax_step(s, _value_tile(vw_ref, ktc, bq), m_ref, l_ref, acc_ref)

    o_win = acc_ref[...] / l_ref[...]
    gate = gate_ref[0]
    for h in range(NSA_HG):
        sl = slice(h * bq, (h + 1) * bq)
        y = (gate[3 * h:3 * h + 1, :] * ocmp_ref[0, 0, h * HEAD:(h + 1) * HEAD, :]
             + gate[3 * h + 1:3 * h + 2, :] * o_sel[:, sl]
             + gate[3 * h + 2:3 * h + 3, :] * o_win[:, sl])
        y_ref[0, h * HEAD:(h + 1) * HEAD, :] = y.astype(BF16)


def _sel_win(qnt, bias_t, ksa, vst, kwa, vwt, ocmp_t, gnt):
    b, g, s, _ = ksa.shape
    bq = 256
    dq = NSA_HG * HEAD
    rows = NSA_HG * bq
    keys = pl.BlockSpec((1, 1, s, LANES), lambda bi, gi, qi: (bi, gi, 0, 0))
    vals = pl.BlockSpec((1, 1, HEAD, s), lambda bi, gi, qi: (bi, gi, 0, 0))
    qtile = pl.BlockSpec((1, dq, bq), lambda bi, gi, qi: (bi, gi, qi))
    return pl.pallas_call(
        functools.partial(_selwin_body, bq),
        grid=(b, g, s // bq),
        in_specs=[qtile,
                  pl.BlockSpec((1, 1, SEL_BLOCK, bq), lambda bi, gi, qi: (bi, gi, 0, qi)),
                  keys, vals, keys, vals,
                  pl.BlockSpec((1, 1, dq, bq), lambda bi, gi, qi: (bi, gi, 0, qi)),
                  pl.BlockSpec((1, LANES, bq), lambda bi, gi, qi: (bi, gi, qi))],
        out_specs=qtile,
        out_shape=jax.ShapeDtypeStruct((b, g * dq, s), BF16),
        scratch_shapes=[pltpu.VMEM((1, rows), F32), pltpu.VMEM((1, rows), F32),
                        pltpu.VMEM((HEAD, rows), F32)],
        compiler_params=_params("parallel", "parallel", "parallel"),
        name="sel_win",
    )(qnt, bias_t, ksa, vst, kwa, vwt, ocmp_t, gnt)


def _outproj_body(yda_ref, yn_ref, gm_ref, x_ref, pda_ref, pnsa_ref, wo_ref, fg_ref, wq_ref,
                  k1_ref, k2_ref, h1_ref, c_ref, s1_ref, s2_ref):
    d = x_ref.shape[1]
    a = jnp.dot(yda_ref[...], pda_ref[...], preferred_element_type=F32)
    bn = jnp.dot(yn_ref[...], pnsa_ref[...], preferred_element_type=F32)
    merged = gm_ref[:, :d].astype(F32) * a + gm_ref[:, d:].astype(F32) * bn
    h1 = x_ref[...] + jnp.dot(merged.astype(BF16), wo_ref[...], preferred_element_type=F32)
    h1_ref[...] = h1
    c = _rmsnorm(h1, fg_ref[...])
    c_ref[...] = c
    cb = c.astype(BF16)
    for h in range(PEER_HEADS):
        qh = jnp.dot(cb, wq_ref[:, h * 256:(h + 1) * 256], preferred_element_type=F32).astype(BF16)
        s1_ref[h] = lax.dot_general(k1_ref[...], qh[:, :LANES], NT_DIMS, preferred_element_type=F32)
        s2_ref[h] = lax.dot_general(k2_ref[...], qh[:, LANES:], NT_DIMS, preferred_element_type=F32)


def _out_proj(yda, yn, gm, x2, pda, pnsa, wo, ffn_g, wq, k1, k2):
    t, d = x2.shape
    tm = 256
    row = lambda w: pl.BlockSpec((tm, w), lambda i: (i, 0))
    fixed = lambda a: pl.BlockSpec(a.shape, lambda i: (0, 0))
    sspec = pl.BlockSpec((PEER_HEADS, PEER_NKEYS, tm), lambda i: (0, 0, i))
    return pl.pallas_call(
        _outproj_body,
        grid=(t // tm,),
        in_specs=[row(512), row(512), row(2 * d), row(d), fixed(pda), fixed(pnsa), fixed(wo),
                  fixed(ffn_g), fixed(wq), fixed(k1), fixed(k2)],
        out_specs=[row(d), row(d), sspec, sspec],
        out_shape=[jax.ShapeDtypeStruct((t, d), F32), jax.ShapeDtypeStruct((t, d), F32),
                   jax.ShapeDtypeStruct((PEER_HEADS, PEER_NKEYS, t), F32),
                   jax.ShapeDtypeStruct((PEER_HEADS, PEER_NKEYS, t), F32)],
        compiler_params=_params("parallel"),
        name="out_proj",
    )(yda, yn, gm, x2, pda, pnsa, wo, ffn_g, wq, k1, k2)


def _batcher_pairs(n):
    pairs = []

    def merge(lo, hi, r):
        step = r * 2
        if step < hi - lo:
            merge(lo, hi, step)
            merge(lo + r, hi, step)
            pairs.extend((i, i + r) for i in range(lo + r, hi - r, step))
        else:
            pairs.append((lo, lo + r))

    def sort(lo, hi):
        if hi - lo >= 1:
            mid = lo + (hi - lo) // 2
            sort(lo, mid)
            sort(mid + 1, hi)
            merge(lo, hi, 1)

    sort(0, n - 1)
    return pairs


_NET16 = _batcher_pairs(PEER_TOPK)


def _cmpx(a, b):
    c = a[0] >= b[0]
    return ((jnp.where(c, a[0], b[0]), jnp.where(c, a[1], b[1])),
            (jnp.where(c, b[0], a[0]), jnp.where(c, b[1], a[1])))


def _sort_lists(lists, n_real):
    lists = list(lists)
    for i, j in _NET16:
        if j < n_real:
            lists[i], lists[j] = _cmpx(lists[i], lists[j])
    return lists


def _merge_sublanes(lists):
    k = PEER_TOPK
    for dist in (4, 2, 1):
        other = [(pltpu.roll(v, dist, 0), pltpu.roll(ix, dist, 0)) for v, ix in lists]
        lists = [_cmpx(lists[i], other[k - 1 - i])[0] for i in range(k)]
        step = k // 2
        while step >= 1:
            for i in range(k):
                if i & step == 0:
                    lists[i], lists[i + step] = _cmpx(lists[i], lists[i + step])
            step //= 2
    return lists


def _spread(lists, off, sub):
    v, ix = lists[off]
    for r in range(1, SUBLANES):
        sel = sub == r
        v = jnp.where(sel, lists[off + r][0], v)
        ix = jnp.where(sel, lists[off + r][1], ix)
    return v, ix


def _peertopk_body(tt, s1_ref, s2_ref, idx_ref, gate_ref):
    sub = lax.broadcasted_iota(I32, (SUBLANES, LANES), 0)
    groups = PEER_NKEYS // SUBLANES

    def top16(ref, h, lanes):
        lists = [(ref[h, v * SUBLANES:(v + 1) * SUBLANES, lanes], sub + v * SUBLANES)
                 for v in range(groups)]
        return _merge_sublanes(_sort_lists(lists, groups))

    def unit(u, carry):
        h = u // (tt // LANES)
        lanes = pl.ds(pl.multiple_of((u % (tt // LANES)) * LANES, LANES), LANES)
        l1 = top16(s1_ref, h, lanes)
        l2 = top16(s2_ref, h, lanes)
        v2lo, v2hi, v1hi = _spread(l2, 0, sub), _spread(l2, SUBLANES, sub), _spread(l1, SUBLANES, sub)
        cands = [(l1[a][0] + v2lo[0], l1[a][1] * PEER_NKEYS + v2lo[1]) for a in range(SUBLANES)]
        cands.append((l1[0][0] + v2hi[0], l1[0][1] * PEER_NKEYS + v2hi[1]))
        cands.append((v1hi[0] + l2[0][0], v1hi[1] * PEER_NKEYS + l2[0][1]))
        n_real = len(cands)
        filler = (jnp.full((SUBLANES, LANES), -jnp.inf, F32), jnp.zeros((SUBLANES, LANES), I32))
        cands += [filler] * (PEER_TOPK - n_real)
        best = _merge_sublanes(_sort_lists(cands, n_real))
        ex = [jnp.exp(v - best[0][0]) for v, _ in best]
        z = ex[0]
        for e in ex[1:]:
            z = z + e
        gl = [(e / z, ix) for e, (_, ix) in zip(ex, best)]
        lo, hi = _spread(gl, 0, sub), _spread(gl, SUBLANES, sub)
        gate_ref[h, :, lanes] = jnp.concatenate([lo[0], hi[0]], axis=0)
        idx_ref[h, :, lanes] = jnp.concatenate([lo[1], hi[1]], axis=0)
        return carry

    lax.fori_loop(0, PEER_HEADS * (tt // LANES), unit, 0)


def _peer_topk(s1t, s2t):
    _, _, t = s1t.shape
    tt = 512
    spec_in = pl.BlockSpec((PEER_HEADS, PEER_NKEYS, tt), lambda i: (0, 0, i))
    spec_out = pl.BlockSpec((PEER_HEADS, PEER_TOPK, tt), lambda i: (0, 0, i))
    return pl.pallas_call(
        functools.partial(_peertopk_body, tt),
        grid=(t // tt,),
        in_specs=[spec_in, spec_in],
        out_specs=[spec_out, spec_out],
        out_shape=[jax.ShapeDtypeStruct((PEER_HEADS, PEER_TOPK, t), I32),
                   jax.ShapeDtypeStruct((PEER_HEADS, PEER_TOPK, t), F32)],
        compiler_params=_params("parallel"),
        name="peer_topk",
    )(s1t, s2t)


PEER_E = PEER_HEADS * PEER_TOPK
PEER_TILE = 2 * SUBLANES


def _sublane_sums(a, sub):
    for dist in (4, 2, 1):
        low = (sub & dist) == 0
        half = len(a) // 2
        a = [jnp.where(low, a[i], pltpu.roll(a[i + half], dist, 0))
             + jnp.where(low, pltpu.roll(a[i], SUBLANES - dist, 0), a[i + half])
             for i in range(half)]
    return a[0]


def _peerffn_body(tb, idx_ref, gate_ref, c_ref, h1_ref, fg_ref, uv_ref, out_ref, buf_ref,
                  acc_ref, sem_ref):
    sub = lax.broadcasted_iota(I32, (SUBLANES, LANES), 0)
    eye = (lax.broadcasted_iota(I32, (PEER_E, LANES), 0)
           == lax.broadcasted_iota(I32, (PEER_E, LANES), 1))
    ones_rows = jnp.ones((SUBLANES, LANES), BF16)
    ones_sq = jnp.ones((LANES, LANES), BF16)
    ngroup = PEER_E // SUBLANES
    per_piece = PEER_E // (2 * ngroup)

    def row_copy(t, j, slot):
        return pltpu.make_async_copy(uv_ref.at[idx_ref[t, j]], buf_ref.at[slot, j],
                                     sem_ref.at[slot])

    def issue(t, slot, j0, j1):
        for j in range(j0, j1):
            row_copy(t, j, slot).start(priority=j % 2)

    def wait_all(slot):
        pltpu.make_async_copy(uv_ref.at[pl.ds(0, PEER_E)], buf_ref.at[slot],
                              sem_ref.at[slot]).wait()

    def expert_tile(slot, j):
        return buf_ref[slot, j].astype(F32)

    def pair(i, prefetch):
        par = i & 1
        toks = (2 * i, 2 * i + 1)
        slots = (2 * par, 2 * par + 1)
        nslots = (2 - 2 * par, 3 - 2 * par)
        pieces = iter([(a, g) for g in range(2 * ngroup) for a in range(2)])

        def issue_next():
            a, g = next(pieces)
            if prefetch:
                issue(toks[a] + 2, nslots[a], g * per_piece, (g + 1) * per_piece)

        wait_all(slots[0])
        wait_all(slots[1])
        x8 = [c_ref[t] for t in toks]
        qs = ([], [])
        for g in range(ngroup):
            for a in range(2):
                prods = [expert_tile(slots[a], g * SUBLANES + r)[0:SUBLANES] * x8[a]
                         for r in range(SUBLANES)]
                qs[a].append(_sublane_sums(prods, sub))
                issue_next()
        wcol = []
        for a in range(2):
            q = jnp.concatenate(qs[a], axis=0)
            q_hi = q.astype(BF16)
            q_lo = (q - q_hi.astype(F32)).astype(BF16)
            hid = (lax.dot_general(ones_rows, q_hi, NT_DIMS, preferred_element_type=F32)
                   + lax.dot_general(ones_rows, q_lo, NT_DIMS, preferred_element_type=F32))
            w = _gelu(hid[0:1]) * gate_ref[pl.ds(toks[a], 1), :]
            wd = jnp.where(eye, jnp.broadcast_to(w, (PEER_E, LANES)), 0.0).astype(BF16)
            wcol.append(jnp.dot(wd, ones_sq, preferred_element_type=F32))
        outs = [jnp.zeros((SUBLANES, LANES), F32) for _ in range(2)]
        for g in range(ngroup):
            for a in range(2):
                for r in range(SUBLANES):
                    j = g * SUBLANES + r
                    outs[a] = outs[a] + wcol[a][j:j + 1, :] * expert_tile(slots[a], j)[SUBLANES:]
                issue_next()
        for a in range(2):
            acc_ref[toks[a]] = outs[a]

    issue(0, 0, 0, PEER_E)
    issue(1, 1, 0, PEER_E)

    def body(i, carry):
        pair(i, True)
        return carry

    lax.fori_loop(0, tb // 2 - 1, body, 0)
    pair(tb // 2 - 1, False)
    hsum = h1_ref[...] + acc_ref[...]
    ms = jnp.mean(hsum * hsum, axis=(1, 2), keepdims=True)
    out_ref[...] = hsum * lax.rsqrt(ms + RMS_EPS) * fg_ref[...]


def _peer_ffn(idx, gate, c3, h13, final_g3, uv_tiles):
    t = h13.shape[0]
    tb = 64
    row3 = pl.BlockSpec((tb, SUBLANES, LANES), lambda i: (i, 0, 0))
    return pl.pallas_call(
        functools.partial(_peerffn_body, tb),
        grid=(t // tb,),
        in_specs=[pl.BlockSpec((tb, PEER_E), lambda i: (i, 0), memory_space=pltpu.SMEM),
                  pl.BlockSpec((tb, PEER_E), lambda i: (i, 0)), row3, row3,
                  pl.BlockSpec((1, SUBLANES, LANES), lambda i: (0, 0, 0)),
                  pl.BlockSpec(memory_space=pl.ANY)],
        out_specs=row3,
        out_shape=jax.ShapeDtypeStruct((t, SUBLANES, LANES), F32),
        scratch_shapes=[pltpu.VMEM((4, PEER_E, PEER_TILE, LANES), BF16),
                        pltpu.VMEM((tb, SUBLANES, LANES), F32),
                        pltpu.SemaphoreType.DMA((4,))],
        compiler_params=_params("arbitrary"),
        name="peer_ffn",
    )(idx, gate, c3, h13, final_g3, uv_tiles)


def _overlap_table(seq):
    ci = jnp.arange(seq // CMP_STRIDE)[None, :] * CMP_STRIDE
    sj = jnp.arange(SEL_BLOCK)[:, None] * SEL_BLOCK
    return ((ci < sj + SEL_BLOCK) & (ci + CMP_BLOCK > sj)).astype(F32)


def _cmp_blocks(kv):
    b, g, s, dh = kv.shape
    r = kv.reshape(b * g, s // CMP_STRIDE, CMP_STRIDE * dh)
    return jnp.concatenate([r, jnp.roll(r, -1, axis=1)], axis=-1)


def _layer(h, lidx, attn_norm, w_in, lq1, lk1, lq2, lk2, subln, pe_k, pe_v, w1k, w1v, w2k, w2v,
           p_da, p_nsa, w_o, ffn_norm, wq, k1, k2, pu, pv, out_norm):
    b, s, d = h.shape
    t = b * s
    g, hg = NSA_GROUPS, NSA_HG
    lambda_init = 0.8 - 0.6 * math.exp(-0.3 * lidx)
    x2 = h.reshape(t, d)

    qda, kda, vda, qn, kvn, gm, gn = _in_proj(x2, attn_norm.reshape(1, d), _pack_w_in(w_in),
                                              _rope_tables(s), s)
    tr = lambda a2: jnp.swapaxes(a2.reshape(b, s, -1), 1, 2)
    ydat = _diff_attn(tr(qda), kda.reshape(b, s, -1), tr(vda),
                      lq1.reshape(1, -1), lk1.reshape(1, -1), lq2.reshape(1, -1),
                      lk2.reshape(1, -1), subln.reshape(-1, 1), lambda_init)
    yda = jnp.swapaxes(ydat, 1, 2).reshape(t, -1)

    assert s // SEL_BLOCK <= SEL_BLOCK, "selection bias rows hold at most 64 blocks"
    kv6 = kvn.reshape(b, s, 6, g, HEAD).transpose(2, 0, 3, 1, 4)
    kc_raw, vc_raw, ks, vs, kw, vw = (kv6[i] for i in range(6))
    kca, vc = _compress(_cmp_blocks(kc_raw), _cmp_blocks(vc_raw),
                        pe_k.reshape(1, -1), pe_v.reshape(1, -1),
                        w1k.astype(BF16), w1v.astype(BF16), w2k.astype(BF16), w2v.astype(BF16))
    ncp = s // CMP_STRIDE
    kca = kca.reshape(b, g, ncp, LANES)
    vct = jnp.swapaxes(vc.reshape(b, g, ncp, HEAD), 2, 3)

    qnt = tr(qn)
    ocmp_t, bias_t = _cmp_select(qnt, kca, vct, _overlap_table(s))
    onehot = (jnp.arange(s)[:, None] // SEL_BLOCK == jnp.arange(HEAD)[None, :]).astype(BF16)
    ksa = jnp.concatenate([ks, jnp.broadcast_to(onehot, ks.shape)], axis=-1)
    kwa = jnp.concatenate([kw, jnp.zeros_like(kw)], axis=-1)
    ynt = _sel_win(qnt, bias_t, ksa, jnp.swapaxes(vs, 2, 3), kwa, jnp.swapaxes(vw, 2, 3),
                   ocmp_t, tr(gn))
    yn = jnp.swapaxes(ynt, 1, 2).reshape(t, -1)

    h1, cb, s1t, s2t = _out_proj(yda, yn, gm, x2, p_da.astype(BF16),
                                 p_nsa.astype(BF16), w_o.astype(BF16), ffn_norm.reshape(1, d),
                                 wq.astype(BF16), k1.astype(BF16), k2.astype(BF16))
    idx_t, gate_t = _peer_topk(s1t, s2t)
    idx = idx_t.reshape(PEER_E, t).T
    gate = gate_t.reshape(PEER_E, t).T
    uv_tiles = jnp.concatenate([pu, pv], axis=1).astype(BF16).reshape(-1, PEER_TILE, LANES)
    as_tiles = lambda a2: a2.reshape(-1, SUBLANES, LANES)
    out = _peer_ffn(idx, gate, as_tiles(cb), as_tiles(h1), as_tiles(out_norm.reshape(1, d)),
                    uv_tiles)
    return out.reshape(b, s, d)


def kernel(x, attn_norm, w_in, da_lambda_q1, da_lambda_k1, da_lambda_q2, da_lambda_k2, da_subln,
           cmp_pe_k, cmp_pe_v, cmp_w1_k, cmp_w1_v, cmp_w2_k, cmp_w2_v, p_da, p_nsa, w_o,
           ffn_norm, peer_wq, peer_k1, peer_k2, peer_u, peer_v, final_norm):
    depth = attn_norm.shape[0]
    assert depth == 1, "the final norm is fused into the last layer's PEER kernel"
    h = x
    for l in range(depth):
        h = _layer(h, l, attn_norm[l], w_in[l], da_lambda_q1[l], da_lambda_k1[l], da_lambda_q2[l],
                   da_lambda_k2[l], da_subln[l], cmp_pe_k[l], cmp_pe_v[l], cmp_w1_k[l],
                   cmp_w1_v[l], cmp_w2_k[l], cmp_w2_v[l], p_da[l], p_nsa[l], w_o[l], ffn_norm[l],
                   peer_wq[l], peer_k1[l], peer_k2[l], peer_u[l], peer_v[l], final_norm)
    return h
```

```python
import functools
import math

import jax
import jax.numpy as jnp
from jax import lax
from jax.experimental import pallas as pl
from jax.experimental.pallas import tpu as pltpu

F32 = jnp.float32
BF16 = jnp.bfloat16
I32 = jnp.int32

RMS_EPS = 1e-6
ROPE_THETA = 500000.0
ROPE_HALF = 8
HEAD = 64
DA_HEADS = 4
NSA_GROUPS = 2
NSA_HG = 4
CMP_STRIDE = 16
CMP_BLOCK = 32
SEL_BLOCK = 64
SEL_TOPK = 16
WINDOW = 512
FORCE_BONUS = 1e4
NEG_BIG = -1e30
SEL_MASK_BIAS = -32768.0
PEER_HEADS = 8
PEER_NKEYS = 128
PEER_TOPK = 16
LANES = 128
SUBLANES = 8
VMEM_LIMIT = 56 * 1024 * 1024

NT_DIMS = (((1,), (1,)), ((), ()))


def _rmsnorm(x, g):
    return x * lax.rsqrt(jnp.mean(x * x, axis=-1, keepdims=True) + RMS_EPS) * g


def _sigmoid(z):
    return 1.0 / (1.0 + jnp.exp(-z))


def _gelu(z):
    return 0.5 * z * (1.0 + lax.erf(z * (2.0 ** -0.5)))


def _params(*sem):
    return pltpu.CompilerParams(dimension_semantics=sem, vmem_limit_bytes=VMEM_LIMIT)


_QDA0, _KDA0, _VDA0, _QN0, _KVN0, _GM0, _GN0, _WCOLS = 0, 512, 1024, 1536, 2048, 2816, 4864, 5120


def _inproj_body(x_ref, g_ref, w_ref, rc_ref, rs1_ref, rs2_ref,
                 qda_ref, kda_ref, vda_ref, qn_ref, kvn_ref, gm_ref, gn_ref):
    a = _rmsnorm(x_ref[...], g_ref[...]).astype(BF16)
    rc, rs1, rs2 = rc_ref[...], rs1_ref[...], rs2_ref[...]

    def rope(z):
        return (z * rc + pltpu.roll(z, ROPE_HALF, 1) * rs1
                + pltpu.roll(z, LANES - ROPE_HALF, 1) * rs2)

    def proj(c0):
        return jnp.dot(a, w_ref[:, c0:c0 + 256], preferred_element_type=F32)

    def rope2(z):
        return jnp.concatenate([rope(z[:, :LANES]), rope(z[:, LANES:])], axis=1)

    for c in range(2):
        qda_ref[:, c * 256:(c + 1) * 256] = (rope2(proj(_QDA0 + c * 256)) * 0.125).astype(BF16)
        kda_ref[:, c * 256:(c + 1) * 256] = rope2(proj(_KDA0 + c * 256)).astype(BF16)
        vda_ref[:, c * 256:(c + 1) * 256] = proj(_VDA0 + c * 256).astype(BF16)
        qn_ref[:, c * 256:(c + 1) * 256] = (rope2(proj(_QN0 + c * 256)) * 0.125).astype(BF16)
    for c in range(3):
        z = proj(_KVN0 + c * 256)
        kvn_ref[:, c * 256:c * 256 + LANES] = rope(z[:, :LANES]).astype(BF16)
        kvn_ref[:, c * 256 + LANES:(c + 1) * 256] = z[:, LANES:].astype(BF16)
    for c in range(8):
        gm_ref[:, c * 256:(c + 1) * 256] = _sigmoid(proj(_GM0 + c * 256)).astype(BF16)
    gn_ref[...] = _sigmoid(proj(_GN0))


def _pack_w_in(w):
    d = w.shape[0]
    gn = w[:, 2816:2840]
    pad = jnp.zeros((d, LANES - 12), w.dtype)
    return jnp.concatenate([w[:, :2816], w[:, 2840:], gn[:, :12], pad, gn[:, 12:], pad],
                           axis=1).astype(BF16)


def _rope_tables(seq):
    inv = jnp.power(ROPE_THETA, -jnp.arange(ROPE_HALF, dtype=F32) * 2.0 / (2 * ROPE_HALF))
    ang = jnp.arange(seq, dtype=F32)[:, None] * inv[None, :]
    cos, sin = jnp.cos(ang), jnp.sin(ang)
    one = jnp.ones((seq, HEAD - 2 * ROPE_HALF), F32)
    zero8 = jnp.zeros((seq, ROPE_HALF), F32)
    zero48 = jnp.zeros_like(one)
    rc = jnp.concatenate([cos, cos, one], axis=1)
    rs1 = jnp.concatenate([zero8, sin, zero48], axis=1)
    rs2 = jnp.concatenate([-sin, zero8, zero48], axis=1)
    return tuple(jnp.concatenate([t, t], axis=1) for t in (rc, rs1, rs2))


def _in_proj(x2, norm_g, w_packed, rope_tabs, seq):
    t, d = x2.shape
    tm = 512
    nseq = seq // tm
    row = lambda i: (i, 0)
    fixed = lambda i: (0, 0)
    out_shapes = [
        jax.ShapeDtypeStruct((t, 512), BF16), jax.ShapeDtypeStruct((t, 512), BF16),
        jax.ShapeDtypeStruct((t, 512), BF16), jax.ShapeDtypeStruct((t, 512), BF16),
        jax.ShapeDtypeStruct((t, 768), BF16), jax.ShapeDtypeStruct((t, 2048), BF16),
        jax.ShapeDtypeStruct((t, 256), F32)]
    rope_spec = pl.BlockSpec((tm, LANES), lambda i: (i % nseq, 0))
    return pl.pallas_call(
        _inproj_body,
        grid=(t // tm,),
        in_specs=[pl.BlockSpec((tm, d), row), pl.BlockSpec((1, d), fixed),
                  pl.BlockSpec((d, _WCOLS), fixed), rope_spec, rope_spec, rope_spec],
        out_specs=[pl.BlockSpec((tm, s.shape[1]), row) for s in out_shapes],
        out_shape=out_shapes,
        compiler_params=_params("parallel"),
        name="in_proj",
    )(x2, norm_g, w_packed, *rope_tabs)


def _softmax_step(s, vt, m_ref, l_ref, acc_ref):
    m_prev = m_ref[...]
    m_new = jnp.maximum(m_prev, jnp.max(s, axis=0, keepdims=True))
    alpha = jnp.exp(m_prev - m_new)
    p = jnp.exp(s - m_new)
    l_ref[...] = alpha * l_ref[...] + jnp.sum(p, axis=0, keepdims=True)
    acc_ref[...] = alpha * acc_ref[...] + jnp.dot(vt, p.astype(BF16), preferred_element_type=F32)
    m_ref[...] = m_new


def _softmax_reset(m_ref, l_ref, acc_ref):
    m_ref[...] = jnp.full(m_ref.shape, NEG_BIG, F32)
    l_ref[...] = jnp.zeros(l_ref.shape, F32)
    acc_ref[...] = jnp.zeros(acc_ref.shape, F32)


def _key_tile(ref, kt, bk):
    return ref[(0,) * (len(ref.shape) - 2) + (pl.ds(pl.multiple_of(kt * bk, bk), bk), slice(None))]


def _value_tile(ref, kt, bk):
    return ref[(0,) * (len(ref.shape) - 2) + (slice(None), pl.ds(pl.multiple_of(kt * bk, bk), bk))]


def _diffattn_body(lambda_init, bq, bk, q_ref, k_ref, v_ref, lq1_ref, lk1_ref, lq2_ref, lk2_ref,
                   sub_ref, y_ref, qbd_ref, m_ref, l_ref, acc_ref):
    qi = pl.program_id(2)
    qt = q_ref[0]
    sub = lax.broadcasted_iota(I32, qt.shape, 0)
    zero = jnp.zeros_like(qt)
    qbd_ref[:, 0:bq] = jnp.where(sub < HEAD, qt, zero)
    qbd_ref[:, bq:2 * bq] = jnp.where(sub >= HEAD, qt, zero)
    _softmax_reset(m_ref, l_ref, acc_ref)

    def scores(kt):
        return jnp.dot(_key_tile(k_ref, kt, bk), qbd_ref[...], preferred_element_type=F32)

    def body(kt, carry):
        _softmax_step(scores(kt), _value_tile(v_ref, kt, bk), m_ref, l_ref, acc_ref)
        return carry

    first_diag = qi * (bq // bk)
    lax.fori_loop(0, first_diag, body, 0)
    r = lax.broadcasted_iota(I32, (bk, 2 * bq), 0)
    c = lax.broadcasted_iota(I32, (bk, 2 * bq), 1) & (bq - 1)
    for j in range(bq // bk):
        kt = first_diag + j
        causal = j * bk + r <= c
        _softmax_step(jnp.where(causal, scores(kt), NEG_BIG), _value_tile(v_ref, kt, bk),
                      m_ref, l_ref, acc_ref)

    o = acc_ref[...] / l_ref[...]
    lam = (jnp.exp(jnp.sum(lq1_ref[...] * lk1_ref[...], axis=1, keepdims=True))
           - jnp.exp(jnp.sum(lq2_ref[...] * lk2_ref[...], axis=1, keepdims=True)) + lambda_init)
    d = o[:, 0:bq] - lam * o[:, bq:2 * bq]
    ms = jnp.mean(d * d, axis=0, keepdims=True)
    y = d * lax.rsqrt(ms + RMS_EPS) * sub_ref[...] * (1.0 - lambda_init)
    y_ref[0] = y.astype(BF16)


def _diff_attn(qdat, kda, vdat, lq1, lk1, lq2, lk2, subln_col, lambda_init):
    b, s, _ = kda.shape
    bq, bk = 256, 256
    vec = lambda n: pl.BlockSpec((1, n), lambda bi, h, qi: (0, 0))
    dv = 2 * HEAD
    qtile = pl.BlockSpec((1, dv, bq), lambda bi, h, qi: (bi, h, qi))
    return pl.pallas_call(
        functools.partial(_diffattn_body, lambda_init, bq, bk),
        grid=(b, DA_HEADS, s // bq),
        in_specs=[qtile,
                  pl.BlockSpec((1, s, LANES), lambda bi, h, qi: (bi, 0, h)),
                  pl.BlockSpec((1, dv, s), lambda bi, h, qi: (bi, h, 0)),
                  vec(HEAD), vec(HEAD), vec(HEAD), vec(HEAD),
                  pl.BlockSpec((dv, 1), lambda bi, h, qi: (0, 0))],
        out_specs=qtile,
        out_shape=jax.ShapeDtypeStruct((b, DA_HEADS * dv, s), BF16),
        scratch_shapes=[pltpu.VMEM((LANES, 2 * bq), BF16), pltpu.VMEM((1, 2 * bq), F32),
                        pltpu.VMEM((1, 2 * bq), F32), pltpu.VMEM((dv, 2 * bq), F32)],
        compiler_params=_params("parallel", "parallel", "parallel"),
        name="diff_attn",
    )(qdat, kda, vdat, lq1, lk1, lq2, lk2, subln_col)


def _compress_body(xk_ref, xv_ref, pek_ref, pev_ref, w1k_ref, w1v_ref, w2k_ref, w2v_ref,
                   kc_ref, vc_ref):
    def mlp(x_ref, pe_ref, w1_ref, w2_ref):
        blocks = (x_ref[0].astype(F32) + pe_ref[...]).astype(BF16)
        hid = _gelu(jnp.dot(blocks, w1_ref[...], preferred_element_type=F32))
        return jnp.dot(hid.astype(BF16), w2_ref[...], preferred_element_type=F32)

    kc = mlp(xk_ref, pek_ref, w1k_ref, w2k_ref)
    kc_ref[0] = jnp.concatenate([kc, jnp.zeros_like(kc)], axis=1).astype(BF16)
    vc_ref[0] = mlp(xv_ref, pev_ref, w1v_ref, w2v_ref).astype(BF16)


def _compress(xk, xv, pe_k, pe_v, w1k, w1v, w2k, w2v):
    n, ncp, width = xk.shape
    blk = pl.BlockSpec((1, ncp, width), lambda i: (i, 0, 0))
    fixed = lambda shape: pl.BlockSpec(shape, lambda i: (0, 0))
    return pl.pallas_call(
        _compress_body,
        grid=(n,),
        in_specs=[blk, blk, fixed((1, width)), fixed((1, width)), fixed((width, HEAD)),
                  fixed((width, HEAD)), fixed((HEAD, HEAD)), fixed((HEAD, HEAD))],
        out_specs=[pl.BlockSpec((1, ncp, LANES), lambda i: (i, 0, 0)),
                   pl.BlockSpec((1, ncp, HEAD), lambda i: (i, 0, 0))],
        out_shape=[jax.ShapeDtypeStruct((n, ncp, LANES), BF16),
                   jax.ShapeDtypeStruct((n, ncp, HEAD), BF16)],
        compiler_params=_params("parallel"),
        name="compress",
    )(xk, xv, pe_k, pe_v, w1k, w1v, w2k, w2v)


def _heads_on_lanes(qt, bq):
    return jnp.concatenate([qt[h * HEAD:(h + 1) * HEAD, :] for h in range(NSA_HG)], axis=1)


def _cmpsel_body(bq, q_ref, kc_ref, vc_ref, ovl_ref, ocmp_ref, bias_ref):
    qi = pl.program_id(2)
    ncp = kc_ref.shape[2]
    rows = NSA_HG * bq
    q2 = _heads_on_lanes(q_ref[0], bq)
    qz = jnp.concatenate([q2, jnp.zeros_like(q2)], axis=0)
    s = jnp.dot(kc_ref[0, 0], qz, preferred_element_type=F32)
    n = lax.broadcasted_iota(I32, (ncp, rows), 0)
    qpos = qi * bq + (lax.broadcasted_iota(I32, (ncp, rows), 1) & (bq - 1))
    cmask = n * CMP_STRIDE + (CMP_BLOCK - 1) <= qpos
    s = jnp.where(cmask, s, NEG_BIG)
    e = jnp.exp(s - jnp.max(s, axis=0, keepdims=True))
    p = jnp.where(cmask, e / jnp.sum(e, axis=0, keepdims=True), 0.0)
    o = jnp.dot(vc_ref[0, 0], p.astype(BF16), preferred_element_type=F32)
    for h in range(NSA_HG):
        ocmp_ref[0, 0, h * HEAD:(h + 1) * HEAD, :] = o[:, h * bq:(h + 1) * bq]

    psum = p[:, 0:bq] + p[:, bq:2 * bq] + p[:, 2 * bq:3 * bq] + p[:, 3 * bq:4 * bq]
    imp = jnp.dot(ovl_ref[...], psum, preferred_element_type=F32)
    blk = lax.broadcasted_iota(I32, (SEL_BLOCK, bq), 0)
    pos = qi * bq + lax.broadcasted_iota(I32, (SEL_BLOCK, bq), 1)
    cur = lax.shift_right_logical(pos, 6)
    valid = blk <= cur
    forced = (blk == 0) | (blk == cur) | (blk == cur - 1)
    score = jnp.where(valid, imp + jnp.where(forced, FORCE_BONUS, 0.0), -jnp.inf)
    rank = jnp.zeros((SEL_BLOCK, bq), I32)
    for i in range(SEL_BLOCK):
        other = score[i:i + 1, :]
        beats = (other > score) | ((other == score) & (blk > i))
        rank = rank + beats.astype(I32)
    keep = valid & (rank < SEL_TOPK)
    bias_ref[0, 0] = jnp.where(keep, 0.0, SEL_MASK_BIAS).astype(BF16)


def _cmp_select(qnt, kca, vct, overlap_t):
    b, g, ncp, _ = kca.shape
    s = qnt.shape[2]
    bq = 128
    dq = NSA_HG * HEAD
    return pl.pallas_call(
        functools.partial(_cmpsel_body, bq),
        grid=(b, g, s // bq),
        in_specs=[pl.BlockSpec((1, dq, bq), lambda bi, gi, qi: (bi, gi, qi)),
                  pl.BlockSpec((1, 1, ncp, LANES), lambda bi, gi, qi: (bi, gi, 0, 0)),
                  pl.BlockSpec((1, 1, HEAD, ncp), lambda bi, gi, qi: (bi, gi, 0, 0)),
                  pl.BlockSpec((SEL_BLOCK, ncp), lambda bi, gi, qi: (0, 0))],
        out_specs=[pl.BlockSpec((1, 1, dq, bq), lambda bi, gi, qi: (bi, gi, 0, qi)),
                   pl.BlockSpec((1, 1, SEL_BLOCK, bq), lambda bi, gi, qi: (bi, gi, 0, qi))],
        out_shape=[jax.ShapeDtypeStruct((b, g, dq, s), F32),
                   jax.ShapeDtypeStruct((b, g, SEL_BLOCK, s), BF16)],
        compiler_params=_params("parallel", "parallel", "parallel"),
        name="cmp_select",
    )(qnt, kca, vct, overlap_t)


def _selwin_body(bq, q_ref, bias_ref, ks_ref, vs_ref, kw_ref, vw_ref, ocmp_ref, gate_ref,
                 y_ref, m_ref, l_ref, acc_ref):
    qi = pl.program_id(2)
    rows = NSA_HG * bq
    q2 = _heads_on_lanes(q_ref[0], bq)
    bias = bias_ref[0, 0]
    qa = jnp.concatenate([q2, jnp.concatenate([bias] * NSA_HG, axis=1)], axis=0)
    qw = jnp.concatenate([q2, jnp.zeros_like(q2)], axis=0)
    r = lax.broadcasted_iota(I32, (bq, rows), 0)
    c = lax.broadcasted_iota(I32, (bq, rows), 1) & (bq - 1)

    _softmax_reset(m_ref, l_ref, acc_ref)

    def sel_step(kt, carry):
        s = jnp.dot(_key_tile(ks_ref, kt, bq), qa, preferred_element_type=F32)
        _softmax_step(s, _value_tile(vs_ref, kt, bq), m_ref, l_ref, acc_ref)
        return carry

    lax.fori_loop(0, qi, sel_step, 0)
    s = jnp.dot(_key_tile(ks_ref, qi, bq), qa, preferred_element_type=F32)
    _softmax_step(jnp.where(r <= c, s, NEG_BIG), _value_tile(vs_ref, qi, bq), m_ref, l_ref, acc_ref)
    o_sel = acc_ref[...] / l_ref[...]

    _softmax_reset(m_ref, l_ref, acc_ref)
    for back in range(WINDOW // bq, -1, -1):
        kt = qi - back

        @pl.when(kt >= 0)
        def _():
            ktc = jnp.maximum(kt, 0)
            dist = c + back * bq - r
            s = jnp.dot(_key_tile(kw_ref, ktc, bq), qw, preferred_element_type=F32)
            s = jnp.where((dist >= 0) & (dist < WINDOW), s, NEG_BIG)
            _softmax_step(s, _value_tile(vw_ref, ktc, bq), m_ref, l_ref, acc_ref)

    o_win = acc_ref[...] / l_ref[...]
    gate = gate_ref[0]
    for h in range(NSA_HG):
        sl = slice(h * bq, (h + 1) * bq)
        y = (gate[3 * h:3 * h + 1, :] * ocmp_ref[0, 0, h * HEAD:(h + 1) * HEAD, :]
             + gate[3 * h + 1:3 * h + 2, :] * o_sel[:, sl]
             + gate[3 * h + 2:3 * h + 3, :] * o_win[:, sl])
        y_ref[0, h * HEAD:(h + 1) * HEAD, :] = y.astype(BF16)


def _sel_win(qnt, bias_t, ksa, vst, kwa, vwt, ocmp_t, gnt):
    b, g, s, _ = ksa.shape
    bq = 256
    dq = NSA_HG * HEAD
    rows = NSA_HG * bq
    keys = pl.BlockSpec((1, 1, s, LANES), lambda bi, gi, qi: (bi, gi, 0, 0))
    vals = pl.BlockSpec((1, 1, HEAD, s), lambda bi, gi, qi: (bi, gi, 0, 0))
    qtile = pl.BlockSpec((1, dq, bq), lambda bi, gi, qi: (bi, gi, qi))
    return pl.pallas_call(
        functools.partial(_selwin_body, bq),
        grid=(b, g, s // bq),
        in_specs=[qtile,
                  pl.BlockSpec((1, 1, SEL_BLOCK, bq), lambda bi, gi, qi: (bi, gi, 0, qi)),
                  keys, vals, keys, vals,
                  pl.BlockSpec((1, 1, dq, bq), lambda bi, gi, qi: (bi, gi, 0, qi)),
                  pl.BlockSpec((1, LANES, bq), lambda bi, gi, qi: (bi, gi, qi))],
        out_specs=qtile,
        out_shape=jax.ShapeDtypeStruct((b, g * dq, s), BF16),
        scratch_shapes=[pltpu.VMEM((1, rows), F32), pltpu.VMEM((1, rows), F32),
                        pltpu.VMEM((HEAD, rows), F32)],
        compiler_params=_params("parallel", "parallel", "parallel"),
        name="sel_win",
    )(qnt, bias_t, ksa, vst, kwa, vwt, ocmp_t, gnt)


def _outproj_body(yda_ref, yn_ref, gm_ref, x_ref, pda_ref, pnsa_ref, wo_ref, fg_ref, wq_ref,
                  k1_ref, k2_ref, h1_ref, c_ref, s1_ref, s2_ref):
    d = x_ref.shape[1]
    a = jnp.dot(yda_ref[...], pda_ref[...], preferred_element_type=F32)
    bn = jnp.dot(yn_ref[...], pnsa_ref[...], preferred_element_type=F32)
    merged = gm_ref[:, :d].astype(F32) * a + gm_ref[:, d:].astype(F32) * bn
    h1 = x_ref[...] + jnp.dot(merged.astype(BF16), wo_ref[...], preferred_element_type=F32)
    h1_ref[...] = h1
    c = _rmsnorm(h1, fg_ref[...])
    c_ref[...] = c
    cb = c.astype(BF16)
    for h in range(PEER_HEADS):
        qh = jnp.dot(cb, wq_ref[:, h * 256:(h + 1) * 256], preferred_element_type=F32).astype(BF16)
        s1_ref[h] = lax.dot_general(k1_ref[...], qh[:, :LANES], NT_DIMS, preferred_element_type=F32)
        s2_ref[h] = lax.dot_general(k2_ref[...], qh[:, LANES:], NT_DIMS, preferred_element_type=F32)


def _out_proj(yda, yn, gm, x2, pda, pnsa, wo, ffn_g, wq, k1, k2):
    t, d = x2.shape
    tm = 256
    row = lambda w: pl.BlockSpec((tm, w), lambda i: (i, 0))
    fixed = lambda a: pl.BlockSpec(a.shape, lambda i: (0, 0))
    sspec = pl.BlockSpec((PEER_HEADS, PEER_NKEYS, tm), lambda i: (0, 0, i))
    return pl.pallas_call(
        _outproj_body,
        grid=(t // tm,),
        in_specs=[row(512), row(512), row(2 * d), row(d), fixed(pda), fixed(pnsa), fixed(wo),
                  fixed(ffn_g), fixed(wq), fixed(k1), fixed(k2)],
        out_specs=[row(d), row(d), sspec, sspec],
        out_shape=[jax.ShapeDtypeStruct((t, d), F32), jax.ShapeDtypeStruct((t, d), F32),
                   jax.ShapeDtypeStruct((PEER_HEADS, PEER_NKEYS, t), F32),
                   jax.ShapeDtypeStruct((PEER_HEADS, PEER_NKEYS, t), F32)],
        compiler_params=_params("parallel"),
        name="out_proj",
    )(yda, yn, gm, x2, pda, pnsa, wo, ffn_g, wq, k1, k2)


def _batcher_pairs(n):
    pairs = []

    def merge(lo, hi, r):
        step = r * 2
        if step < hi - lo:
            merge(lo, hi, step)
            merge(lo + r, hi, step)
            pairs.extend((i, i + r) for i in range(lo + r, hi - r, step))
        else:
            pairs.append((lo, lo + r))

    def sort(lo, hi):
        if hi - lo >= 1:
            mid = lo + (hi - lo) // 2
            sort(lo, mid)
            sort(mid + 1, hi)
            merge(lo, hi, 1)

    sort(0, n - 1)
    return pairs


_NET16 = _batcher_pairs(PEER_TOPK)


def _cmpx(a, b):
    c = a[0] >= b[0]
    return ((jnp.where(c, a[0], b[0]), jnp.where(c, a[1], b[1])),
            (jnp.where(c, b[0], a[0]), jnp.where(c, b[1], a[1])))


def _sort_lists(lists, n_real):
    lists = list(lists)
    for i, j in _NET16:
        if j < n_real:
            lists[i], lists[j] = _cmpx(lists[i], lists[j])
    return lists


def _merge_sublanes(lists):
    k = PEER_TOPK
    for dist in (4, 2, 1):
        other = [(pltpu.roll(v, dist, 0), pltpu.roll(ix, dist, 0)) for v, ix in lists]
        lists = [_cmpx(lists[i], other[k - 1 - i])[0] for i in range(k)]
        step = k // 2
        while step >= 1:
            for i in range(k):
                if i & step == 0:
                    lists[i], lists[i + step] = _cmpx(lists[i], lists[i + step])
            step //= 2
    return lists


def _spread(lists, off, sub):
    v, ix = lists[off]
    for r in range(1, SUBLANES):
        sel = sub == r
        v = jnp.where(sel, lists[off + r][0], v)
        ix = jnp.where(sel, lists[off + r][1], ix)
    return v, ix


def _peertopk_body(tt, s1_ref, s2_ref, idx_ref, gate_ref):
    sub = lax.broadcasted_iota(I32, (SUBLANES, LANES), 0)
    groups = PEER_NKEYS // SUBLANES

    def top16(ref, h, lanes):
        lists = [(ref[h, v * SUBLANES:(v + 1) * SUBLANES, lanes], sub + v * SUBLANES)
                 for v in range(groups)]
        return _merge_sublanes(_sort_lists(lists, groups))

    def unit(u, carry):
        h = u // (tt // LANES)
        lanes = pl.ds(pl.multiple_of((u % (tt // LANES)) * LANES, LANES), LANES)
        l1 = top16(s1_ref, h, lanes)
        l2 = top16(s2_ref, h, lanes)
        v2lo, v2hi, v1hi = _spread(l2, 0, sub), _spread(l2, SUBLANES, sub), _spread(l1, SUBLANES, sub)
        cands = [(l1[a][0] + v2lo[0], l1[a][1] * PEER_NKEYS + v2lo[1]) for a in range(SUBLANES)]
        cands.append((l1[0][0] + v2hi[0], l1[0][1] * PEER_NKEYS + v2hi[1]))
        cands.append((v1hi[0] + l2[0][0], v1hi[1] * PEER_NKEYS + l2[0][1]))
        n_real = len(cands)
        filler = (jnp.full((SUBLANES, LANES), -jnp.inf, F32), jnp.zeros((SUBLANES, LANES), I32))
        cands += [filler] * (PEER_TOPK - n_real)
        best = _merge_sublanes(_sort_lists(cands, n_real))
        ex = [jnp.exp(v - best[0][0]) for v, _ in best]
        z = ex[0]
        for e in ex[1:]:
            z = z + e
        gl = [(e / z, ix) for e, (_, ix) in zip(ex, best)]
        lo, hi = _spread(gl, 0, sub), _spread(gl, SUBLANES, sub)
        gate_ref[h, :, lanes] = jnp.concatenate([lo[0], hi[0]], axis=0)
        idx_ref[h, :, lanes] = jnp.concatenate([lo[1], hi[1]], axis=0)
        return carry

    lax.fori_loop(0, PEER_HEADS * (tt // LANES), unit, 0)


def _peer_topk(s1t, s2t):
    _, _, t = s1t.shape
    tt = 512
    spec_in = pl.BlockSpec((PEER_HEADS, PEER_NKEYS, tt), lambda i: (0, 0, i))
    spec_out = pl.BlockSpec((PEER_HEADS, PEER_TOPK, tt), lambda i: (0, 0, i))
    return pl.pallas_call(
        functools.partial(_peertopk_body, tt),
        grid=(t // tt,),
        in_specs=[spec_in, spec_in],
        out_specs=[spec_out, spec_out],
        out_shape=[jax.ShapeDtypeStruct((PEER_HEADS, PEER_TOPK, t), I32),
                   jax.ShapeDtypeStruct((PEER_HEADS, PEER_TOPK, t), F32)],
        compiler_params=_params("parallel"),
        name="peer_topk",
    )(s1t, s2t)


PEER_E = PEER_HEADS * PEER_TOPK
PEER_TILE = 2 * SUBLANES
PEER_RING = 3


def _sublane_sums(a, sub):
    for dist in (4, 2, 1):
        low = (sub & dist) == 0
        half = len(a) // 2
        a = [jnp.where(low, a[i], pltpu.roll(a[i + half], dist, 0))
             + jnp.where(low, pltpu.roll(a[i], SUBLANES - dist, 0), a[i + half])
             for i in range(half)]
    return a[0]


def _peerffn_body(tb, idx_ref, gate_ref, c_ref, h1_ref, fg_ref, uv_ref, out_ref, buf_ref,
                  acc_ref, sem_ref):
    sub = lax.broadcasted_iota(I32, (SUBLANES, LANES), 0)
    eye = (lax.broadcasted_iota(I32, (PEER_E, LANES), 0)
           == lax.broadcasted_iota(I32, (PEER_E, LANES), 1))
    ones_rows = jnp.ones((SUBLANES, LANES), BF16)
    ones_sq = jnp.ones((LANES, LANES), BF16)
    ngroup = PEER_E // SUBLANES
    per_piece = PEER_E // (2 * ngroup)

    def row_copy(t, j, slot):
        return pltpu.make_async_copy(uv_ref.at[idx_ref[t, j]], buf_ref.at[slot, j],
                                     sem_ref.at[slot])

    def issue(t, slot, j0, j1):
        for j in range(j0, j1):
            row_copy(t, j, slot).start(priority=j % 2)

    def wait_all(slot):
        pltpu.make_async_copy(uv_ref.at[pl.ds(0, PEER_E)], buf_ref.at[slot],
                              sem_ref.at[slot]).wait()

    def expert_tile(slot, j):
        return buf_ref[slot, j].astype(F32)

    def ring(i):
        base = 2 * (i % PEER_RING)
        return (base, base + 1)

    def pair(i, prefetch):
        ahead = PEER_RING - 1
        toks = (2 * i, 2 * i + 1)
        slots = ring(i)
        nslots = ring(i + ahead)
        pieces = iter([(a, g) for g in range(2 * ngroup) for a in range(2)])

        def issue_next():
            a, g = next(pieces)
            if prefetch:
                issue(toks[a] + 2 * ahead, nslots[a], g * per_piece, (g + 1) * per_piece)

        wait_all(slots[0])
        wait_all(slots[1])
        x8 = [c_ref[t] for t in toks]
        qs = ([], [])
        for g in range(ngroup):
            for a in range(2):
                prods = [expert_tile(slots[a], g * SUBLANES + r)[0:SUBLANES] * x8[a]
                         for r in range(SUBLANES)]
                qs[a].append(_sublane_sums(prods, sub))
                issue_next()
        wcol = []
        for a in range(2):
            q = jnp.concatenate(qs[a], axis=0)
            q_hi = q.astype(BF16)
            q_lo = (q - q_hi.astype(F32)).astype(BF16)
            hid = (lax.dot_general(ones_rows, q_hi, NT_DIMS, preferred_element_type=F32)
                   + lax.dot_general(ones_rows, q_lo, NT_DIMS, preferred_element_type=F32))
            w = _gelu(hid[0:1]) * gate_ref[pl.ds(toks[a], 1), :]
            wd = jnp.where(eye, jnp.broadcast_to(w, (PEER_E, LANES)), 0.0).astype(BF16)
            wcol.append(jnp.dot(wd, ones_sq, preferred_element_type=F32))
        outs = [jnp.zeros((SUBLANES, LANES), F32) for _ in range(2)]
        for g in range(ngroup):
            for a in range(2):
                for r in range(SUBLANES):
                    j = g * SUBLANES + r
                    outs[a] = outs[a] + wcol[a][j:j + 1, :] * expert_tile(slots[a], j)[SUBLANES:]
                issue_next()
        for a in range(2):
            acc_ref[toks[a]] = outs[a]

    npairs = tb // 2
    for i in range(PEER_RING - 1):
        for a, slot in enumerate(ring(i)):
            issue(2 * i + a, slot, 0, PEER_E)

    def body(i, carry):
        pair(i, True)
        return carry

    lax.fori_loop(0, npairs - (PEER_RING - 1), body, 0)
    for i in range(npairs - (PEER_RING - 1), npairs):
        pair(i, False)
    hsum = h1_ref[...] + acc_ref[...]
    ms = jnp.mean(hsum * hsum, axis=(1, 2), keepdims=True)
    out_ref[...] = hsum * lax.rsqrt(ms + RMS_EPS) * fg_ref[...]


def _peer_ffn(idx, gate, c3, h13, final_g3, uv_tiles):
    t = h13.shape[0]
    tb = 128
    row3 = pl.BlockSpec((tb, SUBLANES, LANES), lambda i: (i, 0, 0))
    return pl.pallas_call(
        functools.partial(_peerffn_body, tb),
        grid=(t // tb,),
        in_specs=[pl.BlockSpec((tb, PEER_E), lambda i: (i, 0), memory_space=pltpu.SMEM),
                  pl.BlockSpec((tb, PEER_E), lambda i: (i, 0)), row3, row3,
                  pl.BlockSpec((1, SUBLANES, LANES), lambda i: (0, 0, 0)),
                  pl.BlockSpec(memory_space=pl.ANY)],
        out_specs=row3,
        out_shape=jax.ShapeDtypeStruct((t, SUBLANES, LANES), F32),
        scratch_shapes=[pltpu.VMEM((2 * PEER_RING, PEER_E, PEER_TILE, LANES), BF16),
                        pltpu.VMEM((tb, SUBLANES, LANES), F32),
                        pltpu.SemaphoreType.DMA((2 * PEER_RING,))],
        compiler_params=_params("arbitrary"),
        name="peer_ffn",
    )(idx, gate, c3, h13, final_g3, uv_tiles)


def _overlap_table(seq):
    ci = jnp.arange(seq // CMP_STRIDE)[None, :] * CMP_STRIDE
    sj = jnp.arange(SEL_BLOCK)[:, None] * SEL_BLOCK
    return ((ci < sj + SEL_BLOCK) & (ci + CMP_BLOCK > sj)).astype(F32)


def _cmp_blocks(kv):
    b, g, s, dh = kv.shape
    r = kv.reshape(b * g, s // CMP_STRIDE, CMP_STRIDE * dh)
    return jnp.concatenate([r, jnp.roll(r, -1, axis=1)], axis=-1)


def _layer(h, lidx, attn_norm, w_in, lq1, lk1, lq2, lk2, subln, pe_k, pe_v, w1k, w1v, w2k, w2v,
           p_da, p_nsa, w_o, ffn_norm, wq, k1, k2, pu, pv, out_norm):
    b, s, d = h.shape
    t = b * s
    g, hg = NSA_GROUPS, NSA_HG
    lambda_init = 0.8 - 0.6 * math.exp(-0.3 * lidx)
    x2 = h.reshape(t, d)

    qda, kda, vda, qn, kvn, gm, gn = _in_proj(x2, attn_norm.reshape(1, d), _pack_w_in(w_in),
                                              _rope_tables(s), s)
    tr = lambda a2: jnp.swapaxes(a2.reshape(b, s, -1), 1, 2)
    ydat = _diff_attn(tr(qda), kda.reshape(b, s, -1), tr(vda),
                      lq1.reshape(1, -1), lk1.reshape(1, -1), lq2.reshape(1, -1),
                      lk2.reshape(1, -1), subln.reshape(-1, 1), lambda_init)
    yda = jnp.swapaxes(ydat, 1, 2).reshape(t, -1)

    assert s // SEL_BLOCK <= SEL_BLOCK, "selection bias rows hold at most 64 blocks"
    kv6 = kvn.reshape(b, s, 6, g, HEAD).transpose(2, 0, 3, 1, 4)
    kc_raw, vc_raw, ks, vs, kw, vw = (kv6[i] for i in range(6))
    kca, vc = _compress(_cmp_blocks(kc_raw), _cmp_blocks(vc_raw),
                        pe_k.reshape(1, -1), pe_v.reshape(1, -1),
                        w1k.astype(BF16), w1v.astype(BF16), w2k.astype(BF16), w2v.astype(BF16))
    ncp = s // CMP_STRIDE
    kca = kca.reshape(b, g, ncp, LANES)
    vct = jnp.swapaxes(vc.reshape(b, g, ncp, HEAD), 2, 3)

    qnt = tr(qn)
    ocmp_t, bias_t = _cmp_select(qnt, kca, vct, _overlap_table(s))
    onehot = (jnp.arange(s)[:, None] // SEL_BLOCK == jnp.arange(HEAD)[None, :]).astype(BF16)
    ksa = jnp.concatenate([ks, jnp.broadcast_to(onehot, ks.shape)], axis=-1)
    kwa = jnp.concatenate([kw, jnp.zeros_like(kw)], axis=-1)
    ynt = _sel_win(qnt, bias_t, ksa, jnp.swapaxes(vs, 2, 3), kwa, jnp.swapaxes(vw, 2, 3),
                   ocmp_t, tr(gn))
    yn = jnp.swapaxes(ynt, 1, 2).reshape(t, -1)

    h1, cb, s1t, s2t = _out_proj(yda, yn, gm, x2, p_da.astype(BF16),
                                 p_nsa.astype(BF16), w_o.astype(BF16), ffn_norm.reshape(1, d),
                                 wq.astype(BF16), k1.astype(BF16), k2.astype(BF16))
    idx_t, gate_t = _peer_topk(s1t, s2t)
    idx = idx_t.reshape(PEER_E, t).T
    gate = gate_t.reshape(PEER_E, t).T
    uv_tiles = jnp.concatenate([pu, pv], axis=1).astype(BF16).reshape(-1, PEER_TILE, LANES)
    as_tiles = lambda a2: a2.reshape(-1, SUBLANES, LANES)
    out = _peer_ffn(idx, gate, as_tiles(cb), as_tiles(h1), as_tiles(out_norm.reshape(1, d)),
                    uv_tiles)
    return out.reshape(b, s, d)


def kernel(x, attn_norm, w_in, da_lambda_q1, da_lambda_k1, da_lambda_q2, da_lambda_k2, da_subln,
           cmp_pe_k, cmp_pe_v, cmp_w1_k, cmp_w1_v, cmp_w2_k, cmp_w2_v, p_da, p_nsa, w_o,
           ffn_norm, peer_wq, peer_k1, peer_k2, peer_u, peer_v, final_norm):
    depth = attn_norm.shape[0]
    assert depth == 1, "the final norm is fused into the last layer's PEER kernel"
    h = x
    for l in range(depth):
        h = _layer(h, l, attn_norm[l], w_in[l], da_lambda_q1[l], da_lambda_k1[l], da_lambda_q2[l],
                   da_lambda_k2[l], da_subln[l], cmp_pe_k[l], cmp_pe_v[l], cmp_w1_k[l],
                   cmp_w1_v[l], cmp_w2_k[l], cmp_w2_v[l], p_da[l], p_nsa[l], w_o[l], ffn_norm[l],
                   peer_wq[l], peer_k1[l], peer_k2[l], peer_u[l], peer_v[l], final_norm)
    return h
```

```python
import functools
import math

import jax
import jax.numpy as jnp
from jax import lax
from jax.experimental import pallas as pl
from jax.experimental.pallas import tpu as pltpu

F32 = jnp.float32
BF16 = jnp.bfloat16
I32 = jnp.int32

RMS_EPS = 1e-6
ROPE_THETA = 500000.0
ROPE_HALF = 8
HEAD = 64
DA_HEADS = 4
NSA_GROUPS = 2
NSA_HG = 4
CMP_STRIDE = 16
CMP_BLOCK = 32
SEL_BLOCK = 64
SEL_TOPK = 16
WINDOW = 512
FORCE_BONUS = 1e4
NEG_BIG = -1e30
SEL_MASK_BIAS = -32768.0
PEER_HEADS = 8
PEER_NKEYS = 128
PEER_TOPK = 16
LANES = 128
SUBLANES = 8
VMEM_LIMIT = 56 * 1024 * 1024

NT_DIMS = (((1,), (1,)), ((), ()))


def _rmsnorm(x, g):
    return x * lax.rsqrt(jnp.mean(x * x, axis=-1, keepdims=True) + RMS_EPS) * g


def _sigmoid(z):
    return 1.0 / (1.0 + jnp.exp(-z))


def _gelu(z):
    return 0.5 * z * (1.0 + lax.erf(z * (2.0 ** -0.5)))


def _params(*sem):
    return pltpu.CompilerParams(dimension_semantics=sem, vmem_limit_bytes=VMEM_LIMIT)


_QDA0, _KDA0, _VDA0, _QN0, _KVN0, _GM0, _GN0, _WCOLS = 0, 512, 1024, 1536, 2048, 2816, 4864, 5120


def _inproj_body(x_ref, g_ref, w_ref, rc_ref, rs1_ref, rs2_ref,
                 qda_ref, kda_ref, vda_ref, qn_ref, kvn_ref, gm_ref, gn_ref):
    a = _rmsnorm(x_ref[...], g_ref[...]).astype(BF16)
    rc, rs1, rs2 = rc_ref[...], rs1_ref[...], rs2_ref[...]

    def rope(z):
        return (z * rc + pltpu.roll(z, ROPE_HALF, 1) * rs1
                + pltpu.roll(z, LANES - ROPE_HALF, 1) * rs2)

    def proj(c0):
        return jnp.dot(a, w_ref[:, c0:c0 + 256], preferred_element_type=F32)

    def rope2(z):
        return jnp.concatenate([rope(z[:, :LANES]), rope(z[:, LANES:])], axis=1)

    for c in range(2):
        qda_ref[:, c * 256:(c + 1) * 256] = (rope2(proj(_QDA0 + c * 256)) * 0.125).astype(BF16)
        kda_ref[:, c * 256:(c + 1) * 256] = rope2(proj(_KDA0 + c * 256)).astype(BF16)
        vda_ref[:, c * 256:(c + 1) * 256] = proj(_VDA0 + c * 256).astype(BF16)
        qn_ref[:, c * 256:(c + 1) * 256] = (rope2(proj(_QN0 + c * 256)) * 0.125).astype(BF16)
    for c in range(3):
        z = proj(_KVN0 + c * 256)
        kvn_ref[:, c * 256:c * 256 + LANES] = rope(z[:, :LANES]).astype(BF16)
        kvn_ref[:, c * 256 + LANES:(c + 1) * 256] = z[:, LANES:].astype(BF16)
    for c in range(8):
        gm_ref[:, c * 256:(c + 1) * 256] = _sigmoid(proj(_GM0 + c * 256)).astype(BF16)
    gn_ref[...] = _sigmoid(proj(_GN0))


def _pack_w_in(w):
    d = w.shape[0]
    gn = w[:, 2816:2840]
    pad = jnp.zeros((d, LANES - 12), w.dtype)
    return jnp.concatenate([w[:, :2816], w[:, 2840:], gn[:, :12], pad, gn[:, 12:], pad],
                           axis=1).astype(BF16)


def _rope_tables(seq):
    inv = jnp.power(ROPE_THETA, -jnp.arange(ROPE_HALF, dtype=F32) * 2.0 / (2 * ROPE_HALF))
    ang = jnp.arange(seq, dtype=F32)[:, None] * inv[None, :]
    cos, sin = jnp.cos(ang), jnp.sin(ang)
    one = jnp.ones((seq, HEAD - 2 * ROPE_HALF), F32)
    zero8 = jnp.zeros((seq, ROPE_HALF), F32)
    zero48 = jnp.zeros_like(one)
    rc = jnp.concatenate([cos, cos, one], axis=1)
    rs1 = jnp.concatenate([zero8, sin, zero48], axis=1)
    rs2 = jnp.concatenate([-sin, zero8, zero48], axis=1)
    return tuple(jnp.concatenate([t, t], axis=1) for t in (rc, rs1, rs2))


def _in_proj(x2, norm_g, w_packed, rope_tabs, seq):
    t, d = x2.shape
    tm = 512
    nseq = seq // tm
    row = lambda i: (i, 0)
    fixed = lambda i: (0, 0)
    out_shapes = [
        jax.ShapeDtypeStruct((t, 512), BF16), jax.ShapeDtypeStruct((t, 512), BF16),
        jax.ShapeDtypeStruct((t, 512), BF16), jax.ShapeDtypeStruct((t, 512), BF16),
        jax.ShapeDtypeStruct((t, 768), BF16), jax.ShapeDtypeStruct((t, 2048), BF16),
        jax.ShapeDtypeStruct((t, 256), F32)]
    rope_spec = pl.BlockSpec((tm, LANES), lambda i: (i % nseq, 0))
    return pl.pallas_call(
        _inproj_body,
        grid=(t // tm,),
        in_specs=[pl.BlockSpec((tm, d), row), pl.BlockSpec((1, d), fixed),
                  pl.BlockSpec((d, _WCOLS), fixed), rope_spec, rope_spec, rope_spec],
        out_specs=[pl.BlockSpec((tm, s.shape[1]), row) for s in out_shapes],
        out_shape=out_shapes,
        compiler_params=_params("parallel"),
        name="in_proj",
    )(x2, norm_g, w_packed, *rope_tabs)


def _softmax_step(s, vt, m_ref, l_ref, acc_ref):
    m_prev = m_ref[...]
    m_new = jnp.maximum(m_prev, jnp.max(s, axis=0, keepdims=True))
    alpha = jnp.exp(m_prev - m_new)
    p = jnp.exp(s - m_new)
    l_ref[...] = alpha * l_ref[...] + jnp.sum(p, axis=0, keepdims=True)
    acc_ref[...] = alpha * acc_ref[...] + jnp.dot(vt, p.astype(BF16), preferred_element_type=F32)
    m_ref[...] = m_new


def _softmax_reset(m_ref, l_ref, acc_ref):
    m_ref[...] = jnp.full(m_ref.shape, NEG_BIG, F32)
    l_ref[...] = jnp.zeros(l_ref.shape, F32)
    acc_ref[...] = jnp.zeros(acc_ref.shape, F32)


def _key_tile(ref, kt, bk):
    return ref[(0,) * (len(ref.shape) - 2) + (pl.ds(pl.multiple_of(kt * bk, bk), bk), slice(None))]


def _value_tile(ref, kt, bk):
    return ref[(0,) * (len(ref.shape) - 2) + (slice(None), pl.ds(pl.multiple_of(kt * bk, bk), bk))]


def _diffattn_body(lambda_init, bq, bk, q_ref, k_ref, v_ref, lq1_ref, lk1_ref, lq2_ref, lk2_ref,
                   sub_ref, y_ref, qbd_ref, m_ref, l_ref, acc_ref):
    qi = pl.program_id(2)
    qt = q_ref[0]
    sub = lax.broadcasted_iota(I32, qt.shape, 0)
    zero = jnp.zeros_like(qt)
    qbd_ref[:, 0:bq] = jnp.where(sub < HEAD, qt, zero)
    qbd_ref[:, bq:2 * bq] = jnp.where(sub >= HEAD, qt, zero)
    _softmax_reset(m_ref, l_ref, acc_ref)

    def scores(kt):
        return jnp.dot(_key_tile(k_ref, kt, bk), qbd_ref[...], preferred_element_type=F32)

    def body(kt, carry):
        _softmax_step(scores(kt), _value_tile(v_ref, kt, bk), m_ref, l_ref, acc_ref)
        return carry

    first_diag = qi * (bq // bk)
    lax.fori_loop(0, first_diag, body, 0)
    r = lax.broadcasted_iota(I32, (bk, 2 * bq), 0)
    c = lax.broadcasted_iota(I32, (bk, 2 * bq), 1) & (bq - 1)
    for j in range(bq // bk):
        kt = first_diag + j
        causal = j * bk + r <= c
        _softmax_step(jnp.where(causal, scores(kt), NEG_BIG), _value_tile(v_ref, kt, bk),
                      m_ref, l_ref, acc_ref)

    o = acc_ref[...] / l_ref[...]
    lam = (jnp.exp(jnp.sum(lq1_ref[...] * lk1_ref[...], axis=1, keepdims=True))
           - jnp.exp(jnp.sum(lq2_ref[...] * lk2_ref[...], axis=1, keepdims=True)) + lambda_init)
    d = o[:, 0:bq] - lam * o[:, bq:2 * bq]
    ms = jnp.mean(d * d, axis=0, keepdims=True)
    y = d * lax.rsqrt(ms + RMS_EPS) * sub_ref[...] * (1.0 - lambda_init)
    y_ref[0] = y.astype(BF16)


def _diff_attn(qdat, kda, vdat, lq1, lk1, lq2, lk2, subln_col, lambda_init):
    b, s, _ = kda.shape
    bq, bk = 256, 256
    vec = lambda n: pl.BlockSpec((1, n), lambda bi, h, qi: (0, 0))
    dv = 2 * HEAD
    qtile = pl.BlockSpec((1, dv, bq), lambda bi, h, qi: (bi, h, qi))
    return pl.pallas_call(
        functools.partial(_diffattn_body, lambda_init, bq, bk),
        grid=(b, DA_HEADS, s // bq),
        in_specs=[qtile,
                  pl.BlockSpec((1, s, LANES), lambda bi, h, qi: (bi, 0, h)),
                  pl.BlockSpec((1, dv, s), lambda bi, h, qi: (bi, h, 0)),
                  vec(HEAD), vec(HEAD), vec(HEAD), vec(HEAD),
                  pl.BlockSpec((dv, 1), lambda bi, h, qi: (0, 0))],
        out_specs=qtile,
        out_shape=jax.ShapeDtypeStruct((b, DA_HEADS * dv, s), BF16),
        scratch_shapes=[pltpu.VMEM((LANES, 2 * bq), BF16), pltpu.VMEM((1, 2 * bq), F32),
                        pltpu.VMEM((1, 2 * bq), F32), pltpu.VMEM((dv, 2 * bq), F32)],
        compiler_params=_params("parallel", "parallel", "parallel"),
        name="diff_attn",
    )(qdat, kda, vdat, lq1, lk1, lq2, lk2, subln_col)


def _compress_body(xk_ref, xv_ref, pek_ref, pev_ref, w1k_ref, w1v_ref, w2k_ref, w2v_ref,
                   kc_ref, vc_ref):
    def mlp(x_ref, pe_ref, w1_ref, w2_ref):
        blocks = (x_ref[0].astype(F32) + pe_ref[...]).astype(BF16)
        hid = _gelu(jnp.dot(blocks, w1_ref[...], preferred_element_type=F32))
        return jnp.dot(hid.astype(BF16), w2_ref[...], preferred_element_type=F32)

    kc = mlp(xk_ref, pek_ref, w1k_ref, w2k_ref)
    kc_ref[0] = jnp.concatenate([kc, jnp.zeros_like(kc)], axis=1).astype(BF16)
    vc_ref[0] = mlp(xv_ref, pev_ref, w1v_ref, w2v_ref).astype(BF16)


def _compress(xk, xv, pe_k, pe_v, w1k, w1v, w2k, w2v):
    n, ncp, width = xk.shape
    blk = pl.BlockSpec((1, ncp, width), lambda i: (i, 0, 0))
    fixed = lambda shape: pl.BlockSpec(shape, lambda i: (0, 0))
    return pl.pallas_call(
        _compress_body,
        grid=(n,),
        in_specs=[blk, blk, fixed((1, width)), fixed((1, width)), fixed((width, HEAD)),
                  fixed((width, HEAD)), fixed((HEAD, HEAD)), fixed((HEAD, HEAD))],
        out_specs=[pl.BlockSpec((1, ncp, LANES), lambda i: (i, 0, 0)),
                   pl.BlockSpec((1, ncp, HEAD), lambda i: (i, 0, 0))],
        out_shape=[jax.ShapeDtypeStruct((n, ncp, LANES), BF16),
                   jax.ShapeDtypeStruct((n, ncp, HEAD), BF16)],
        compiler_params=_params("parallel"),
        name="compress",
    )(xk, xv, pe_k, pe_v, w1k, w1v, w2k, w2v)


def _heads_on_lanes(qt, bq):
    return jnp.concatenate([qt[h * HEAD:(h + 1) * HEAD, :] for h in range(NSA_HG)], axis=1)


def _cmpsel_body(bq, q_ref, kc_ref, vc_ref, ovl_ref, ocmp_ref, bias_ref):
    qi = pl.program_id(2)
    ncp = kc_ref.shape[2]
    rows = NSA_HG * bq
    q2 = _heads_on_lanes(q_ref[0], bq)
    qz = jnp.concatenate([q2, jnp.zeros_like(q2)], axis=0)
    s = jnp.dot(kc_ref[0, 0], qz, preferred_element_type=F32)
    n = lax.broadcasted_iota(I32, (ncp, rows), 0)
    qpos = qi * bq + (lax.broadcasted_iota(I32, (ncp, rows), 1) & (bq - 1))
    cmask = n * CMP_STRIDE + (CMP_BLOCK - 1) <= qpos
    s = jnp.where(cmask, s, NEG_BIG)
    e = jnp.exp(s - jnp.max(s, axis=0, keepdims=True))
    p = jnp.where(cmask, e / jnp.sum(e, axis=0, keepdims=True), 0.0)
    o = jnp.dot(vc_ref[0, 0], p.astype(BF16), preferred_element_type=F32)
    for h in range(NSA_HG):
        ocmp_ref[0, 0, h * HEAD:(h + 1) * HEAD, :] = o[:, h * bq:(h + 1) * bq]

    psum = p[:, 0:bq] + p[:, bq:2 * bq] + p[:, 2 * bq:3 * bq] + p[:, 3 * bq:4 * bq]
    imp = jnp.dot(ovl_ref[...], psum, preferred_element_type=F32)
    blk = lax.broadcasted_iota(I32, (SEL_BLOCK, bq), 0)
    pos = qi * bq + lax.broadcasted_iota(I32, (SEL_BLOCK, bq), 1)
    cur = lax.shift_right_logical(pos, 6)
    valid = blk <= cur
    forced = (blk == 0) | (blk == cur) | (blk == cur - 1)
    score = jnp.where(valid, imp + jnp.where(forced, FORCE_BONUS, 0.0), -jnp.inf)
    rank = jnp.zeros((SEL_BLOCK, bq), I32)
    for i in range(SEL_BLOCK):
        other = score[i:i + 1, :]
        beats = (other > score) | ((other == score) & (blk > i))
        rank = rank + beats.astype(I32)
    keep = valid & (rank < SEL_TOPK)
    bias_ref[0, 0] = jnp.where(keep, 0.0, SEL_MASK_BIAS).astype(BF16)


def _cmp_select(qnt, kca, vct, overlap_t):
    b, g, ncp, _ = kca.shape
    s = qnt.shape[2]
    bq = 128
    dq = NSA_HG * HEAD
    return pl.pallas_call(
        functools.partial(_cmpsel_body, bq),
        grid=(b, g, s // bq),
        in_specs=[pl.BlockSpec((1, dq, bq), lambda bi, gi, qi: (bi, gi, qi)),
                  pl.BlockSpec((1, 1, ncp, LANES), lambda bi, gi, qi: (bi, gi, 0, 0)),
                  pl.BlockSpec((1, 1, HEAD, ncp), lambda bi, gi, qi: (bi, gi, 0, 0)),
                  pl.BlockSpec((SEL_BLOCK, ncp), lambda bi, gi, qi: (0, 0))],
        out_specs=[pl.BlockSpec((1, 1, dq, bq), lambda bi, gi, qi: (bi, gi, 0, qi)),
                   pl.BlockSpec((1, 1, SEL_BLOCK, bq), lambda bi, gi, qi: (bi, gi, 0, qi))],
        out_shape=[jax.ShapeDtypeStruct((b, g, dq, s), F32),
                   jax.ShapeDtypeStruct((b, g, SEL_BLOCK, s), BF16)],
        compiler_params=_params("parallel", "parallel", "parallel"),
        name="cmp_select",
    )(qnt, kca, vct, overlap_t)


def _selwin_body(bq, q_ref, bias_ref, ks_ref, vs_ref, kw_ref, vw_ref, ocmp_ref, gate_ref,
                 y_ref, m_ref, l_ref, acc_ref):
    qi = pl.program_id(2)
    rows = NSA_HG * bq
    q2 = _heads_on_lanes(q_ref[0], bq)
    bias = bias_ref[0, 0]
    qa = jnp.concatenate([q2, jnp.concatenate([bias] * NSA_HG, axis=1)], axis=0)
    qw = jnp.concatenate([q2, jnp.zeros_like(q2)], axis=0)
    r = lax.broadcasted_iota(I32, (bq, rows), 0)
    c = lax.broadcasted_iota(I32, (bq, rows), 1) & (bq - 1)

    _softmax_reset(m_ref, l_ref, acc_ref)

    def sel_step(kt, carry):
        s = jnp.dot(_key_tile(ks_ref, kt, bq), qa, preferred_element_type=F32)
        _softmax_step(s, _value_tile(vs_ref, kt, bq), m_ref, l_ref, acc_ref)
        return carry

    lax.fori_loop(0, qi, sel_step, 0)
    s = jnp.dot(_key_tile(ks_ref, qi, bq), qa, preferred_element_type=F32)
    _softmax_step(jnp.where(r <= c, s, NEG_BIG), _value_tile(vs_ref, qi, bq), m_ref, l_ref, acc_ref)
    o_sel = acc_ref[...] / l_ref[...]

    _softmax_reset(m_ref, l_ref, acc_ref)
    for back in range(WINDOW // bq, -1, -1):
        kt = qi - back

        @pl.when(kt >= 0)
        def _():
            ktc = jnp.maximum(kt, 0)
            dist = c + back * bq - r
            s = jnp.dot(_key_tile(kw_ref, ktc, bq), qw, preferred_element_type=F32)
            s = jnp.where((dist >= 0) & (dist < WINDOW), s, NEG_BIG)
            _softmax_step(s, _value_tile(vw_ref, ktc, bq), m_ref, l_ref, acc_ref)

    o_win = acc_ref[...] / l_ref[...]
    gate = gate_ref[0]
    for h in range(NSA_HG):
        sl = slice(h * bq, (h + 1) * bq)
        y = (gate[3 * h:3 * h + 1, :] * ocmp_ref[0, 0, h * HEAD:(h + 1) * HEAD, :]
             + gate[3 * h + 1:3 * h + 2, :] * o_sel[:, sl]
             + gate[3 * h + 2:3 * h + 3, :] * o_win[:, sl])
        y_ref[0, h * HEAD:(h + 1) * HEAD, :] = y.astype(BF16)


def _sel_win(qnt, bias_t, ksa, vst, kwa, vwt, ocmp_t, gnt):
    b, g, s, _ = ksa.shape
    bq = 256
    dq = NSA_HG * HEAD
    rows = NSA_HG * bq
    keys = pl.BlockSpec((1, 1, s, LANES), lambda bi, gi, qi: (bi, gi, 0, 0))
    vals = pl.BlockSpec((1, 1, HEAD, s), lambda bi, gi, qi: (bi, gi, 0, 0))
    qtile = pl.BlockSpec((1, dq, bq), lambda bi, gi, qi: (bi, gi, qi))
    return pl.pallas_call(
        functools.partial(_selwin_body, bq),
        grid=(b, g, s // bq),
        in_specs=[qtile,
                  pl.BlockSpec((1, 1, SEL_BLOCK, bq), lambda bi, gi, qi: (bi, gi, 0, qi)),
                  keys, vals, keys, vals,
                  pl.BlockSpec((1, 1, dq, bq), lambda bi, gi, qi: (bi, gi, 0, qi)),
                  pl.BlockSpec((1, LANES, bq), lambda bi, gi, qi: (bi, gi, qi))],
        out_specs=qtile,
        out_shape=jax.ShapeDtypeStruct((b, g * dq, s), BF16),
        scratch_shapes=[pltpu.VMEM((1, rows), F32), pltpu.VMEM((1, rows), F32),
                        pltpu.VMEM((HEAD, rows), F32)],
        compiler_params=_params("parallel", "parallel", "parallel"),
        name="sel_win",
    )(qnt, bias_t, ksa, vst, kwa, vwt, ocmp_t, gnt)


def _outproj_body(yda_ref, yn_ref, gm_ref, x_ref, pda_ref, pnsa_ref, wo_ref, fg_ref, wq_ref,
                  k1_ref, k2_ref, h1_ref, c_ref, s1_ref, s2_ref):
    d = x_ref.shape[1]
    a = jnp.dot(yda_ref[...], pda_ref[...], preferred_element_type=F32)
    bn = jnp.dot(yn_ref[...], pnsa_ref[...], preferred_element_type=F32)
    merged = gm_ref[:, :d].astype(F32) * a + gm_ref[:, d:].astype(F32) * bn
    h1 = x_ref[...] + jnp.dot(merged.astype(BF16), wo_ref[...], preferred_element_type=F32)
    h1_ref[...] = h1
    c = _rmsnorm(h1, fg_ref[...])
    c_ref[...] = c
    cb = c.astype(BF16)
    for h in range(PEER_HEADS):
        qh = jnp.dot(cb, wq_ref[:, h * 256:(h + 1) * 256], preferred_element_type=F32).astype(BF16)
        s1_ref[h] = lax.dot_general(k1_ref[...], qh[:, :LANES], NT_DIMS, preferred_element_type=F32)
        s2_ref[h] = lax.dot_general(k2_ref[...], qh[:, LANES:], NT_DIMS, preferred_element_type=F32)


def _out_proj(yda, yn, gm, x2, pda, pnsa, wo, ffn_g, wq, k1, k2):
    t, d = x2.shape
    tm = 256
    row = lambda w: pl.BlockSpec((tm, w), lambda i: (i, 0))
    fixed = lambda a: pl.BlockSpec(a.shape, lambda i: (0, 0))
    sspec = pl.BlockSpec((PEER_HEADS, PEER_NKEYS, tm), lambda i: (0, 0, i))
    return pl.pallas_call(
        _outproj_body,
        grid=(t // tm,),
        in_specs=[row(512), row(512), row(2 * d), row(d), fixed(pda), fixed(pnsa), fixed(wo),
                  fixed(ffn_g), fixed(wq), fixed(k1), fixed(k2)],
        out_specs=[row(d), row(d), sspec, sspec],
        out_shape=[jax.ShapeDtypeStruct((t, d), F32), jax.ShapeDtypeStruct((t, d), F32),
                   jax.ShapeDtypeStruct((PEER_HEADS, PEER_NKEYS, t), F32),
                   jax.ShapeDtypeStruct((PEER_HEADS, PEER_NKEYS, t), F32)],
        compiler_params=_params("parallel"),
        name="out_proj",
    )(yda, yn, gm, x2, pda, pnsa, wo, ffn_g, wq, k1, k2)


def _batcher_pairs(n):
    pairs = []

    def merge(lo, hi, r):
        step = r * 2
        if step < hi - lo:
            merge(lo, hi, step)
            merge(lo + r, hi, step)
            pairs.extend((i, i + r) for i in range(lo + r, hi - r, step))
        else:
            pairs.append((lo, lo + r))

    def sort(lo, hi):
        if hi - lo >= 1:
            mid = lo + (hi - lo) // 2
            sort(lo, mid)
            sort(mid + 1, hi)
            merge(lo, hi, 1)

    sort(0, n - 1)
    return pairs


_NET16 = _batcher_pairs(PEER_TOPK)


def _cmpx(a, b):
    c = a[0] >= b[0]
    return ((jnp.where(c, a[0], b[0]), jnp.where(c, a[1], b[1])),
            (jnp.where(c, b[0], a[0]), jnp.where(c, b[1], a[1])))


def _sort_lists(lists, n_real):
    lists = list(lists)
    for i, j in _NET16:
        if j < n_real:
            lists[i], lists[j] = _cmpx(lists[i], lists[j])
    return lists


def _merge_sublanes(lists):
    k = PEER_TOPK
    for dist in (4, 2, 1):
        other = [(pltpu.roll(v, dist, 0), pltpu.roll(ix, dist, 0)) for v, ix in lists]
        lists = [_cmpx(lists[i], other[k - 1 - i])[0] for i in range(k)]
        step = k // 2
        while step >= 1:
            for i in range(k):
                if i & step == 0:
                    lists[i], lists[i + step] = _cmpx(lists[i], lists[i + step])
            step //= 2
    return lists


def _spread(lists, off, sub):
    v, ix = lists[off]
    for r in range(1, SUBLANES):
        sel = sub == r
        v = jnp.where(sel, lists[off + r][0], v)
        ix = jnp.where(sel, lists[off + r][1], ix)
    return v, ix


def _peertopk_body(tt, s1_ref, s2_ref, idx_ref, gate_ref):
    sub = lax.broadcasted_iota(I32, (SUBLANES, LANES), 0)
    groups = PEER_NKEYS // SUBLANES

    def top16(ref, h, lanes):
        lists = [(ref[h, v * SUBLANES:(v + 1) * SUBLANES, lanes], sub + v * SUBLANES)
                 for v in range(groups)]
        return _merge_sublanes(_sort_lists(lists, groups))

    def unit(u, carry):
        h = u // (tt // LANES)
        lanes = pl.ds(pl.multiple_of((u % (tt // LANES)) * LANES, LANES), LANES)
        l1 = top16(s1_ref, h, lanes)
        l2 = top16(s2_ref, h, lanes)
        v2lo, v2hi, v1hi = _spread(l2, 0, sub), _spread(l2, SUBLANES, sub), _spread(l1, SUBLANES, sub)
        cands = [(l1[a][0] + v2lo[0], l1[a][1] * PEER_NKEYS + v2lo[1]) for a in range(SUBLANES)]
        cands.append((l1[0][0] + v2hi[0], l1[0][1] * PEER_NKEYS + v2hi[1]))
        cands.append((v1hi[0] + l2[0][0], v1hi[1] * PEER_NKEYS + l2[0][1]))
        n_real = len(cands)
        filler = (jnp.full((SUBLANES, LANES), -jnp.inf, F32), jnp.zeros((SUBLANES, LANES), I32))
        cands += [filler] * (PEER_TOPK - n_real)
        best = _merge_sublanes(_sort_lists(cands, n_real))
        ex = [jnp.exp(v - best[0][0]) for v, _ in best]
        z = ex[0]
        for e in ex[1:]:
            z = z + e
        gl = [(e / z, ix) for e, (_, ix) in zip(ex, best)]
        lo, hi = _spread(gl, 0, sub), _spread(gl, SUBLANES, sub)
        gate_ref[h, :, lanes] = jnp.concatenate([lo[0], hi[0]], axis=0)
        idx_ref[h, :, lanes] = jnp.concatenate([lo[1], hi[1]], axis=0)
        return carry

    lax.fori_loop(0, PEER_HEADS * (tt // LANES), unit, 0)


def _peer_topk(s1t, s2t):
    _, _, t = s1t.shape
    tt = 512
    spec_in = pl.BlockSpec((PEER_HEADS, PEER_NKEYS, tt), lambda i: (0, 0, i))
    spec_out = pl.BlockSpec((PEER_HEADS, PEER_TOPK, tt), lambda i: (0, 0, i))
    return pl.pallas_call(
        functools.partial(_peertopk_body, tt),
        grid=(t // tt,),
        in_specs=[spec_in, spec_in],
        out_specs=[spec_out, spec_out],
        out_shape=[jax.ShapeDtypeStruct((PEER_HEADS, PEER_TOPK, t), I32),
                   jax.ShapeDtypeStruct((PEER_HEADS, PEER_TOPK, t), F32)],
        compiler_params=_params("parallel"),
        name="peer_topk",
    )(s1t, s2t)


PEER_E = PEER_HEADS * PEER_TOPK
PEER_TILE = 2 * SUBLANES
PEER_RING = 3
PEER_MID_ROWS = 48


def _sublane_sums(a, sub):
    for dist in (4, 2, 1):
        low = (sub & dist) == 0
        half = len(a) // 2
        a = [jnp.where(low, a[i], pltpu.roll(a[i + half], dist, 0))
             + jnp.where(low, pltpu.roll(a[i], SUBLANES - dist, 0), a[i + half])
             for i in range(half)]
    return a[0]


def _peerffn_body(tb, idx_ref, gate_ref, c_ref, h1_ref, fg_ref, uv_ref, out_ref, buf_ref,
                  acc_ref, sem_ref):
    sub = lax.broadcasted_iota(I32, (SUBLANES, LANES), 0)
    eye = (lax.broadcasted_iota(I32, (PEER_E, LANES), 0)
           == lax.broadcasted_iota(I32, (PEER_E, LANES), 1))
    ones_rows = jnp.ones((SUBLANES, LANES), BF16)
    ones_sq = jnp.ones((LANES, LANES), BF16)
    ngroup = PEER_E // SUBLANES

    def row_copy(t, j, slot):
        return pltpu.make_async_copy(uv_ref.at[idx_ref[t, j]], buf_ref.at[slot, j],
                                     sem_ref.at[slot])

    def issue(t, slot, j0, j1):
        for j in range(j0, j1):
            row_copy(t, j, slot).start(priority=j % 2)

    def wait_all(slot):
        pltpu.make_async_copy(uv_ref.at[pl.ds(0, PEER_E)], buf_ref.at[slot],
                              sem_ref.at[slot]).wait()

    def expert_u(slot, j):
        return buf_ref[slot, j, 0:SUBLANES, :].astype(F32)

    def expert_v(slot, j):
        return buf_ref[slot, j, SUBLANES:PEER_TILE, :].astype(F32)

    def ring(i):
        base = 2 * (i % PEER_RING)
        return (base, base + 1)

    def pair(i, prefetch):
        ahead = PEER_RING - 1
        toks = (2 * i, 2 * i + 1)
        slots = ring(i)
        nslots = ring(i + ahead)
        todo = [(a, j) for a in range(2) for j in range(PEER_E)]

        def issue_some(n):
            for a, j in todo[:n]:
                if prefetch:
                    row_copy(toks[a] + 2 * ahead, j, nslots[a]).start(priority=j % 2)
            del todo[:n]

        wait_all(slots[0])
        wait_all(slots[1])
        x8 = [c_ref[t] for t in toks]
        gates = [gate_ref[pl.ds(t, 1), :] for t in toks]

        def hidden(a):
            groups = []
            for g in range(ngroup):
                prods = [expert_u(slots[a], g * SUBLANES + r) * x8[a] for r in range(SUBLANES)]
                groups.append(_sublane_sums(prods, sub))
                issue_some(3 - g % 2)
            return jnp.concatenate(groups, axis=0)

        def expert_weights(a, q):
            q_hi = q.astype(BF16)
            q_lo = (q - q_hi.astype(F32)).astype(BF16)
            hid = (lax.dot_general(ones_rows, q_hi, NT_DIMS, preferred_element_type=F32)
                   + lax.dot_general(ones_rows, q_lo, NT_DIMS, preferred_element_type=F32))
            issue_some(PEER_MID_ROWS // 2)
            w = _gelu(hid[0:1]) * gates[a]
            wd = jnp.where(eye, jnp.broadcast_to(w, (PEER_E, LANES)), 0.0).astype(BF16)
            wcol = jnp.dot(wd, ones_sq, preferred_element_type=F32)
            issue_some(PEER_MID_ROWS // 2)
            return wcol

        def combine(a, wcol):
            out = jnp.zeros((SUBLANES, LANES), F32)
            for g in range(ngroup):
                for r in range(SUBLANES):
                    j = g * SUBLANES + r
                    out = out + wcol[j:j + 1, :] * expert_v(slots[a], j)
                issue_some(3 - g % 2)
            acc_ref[toks[a]] = out

        wcol0 = expert_weights(0, hidden(0))
        wcol1 = expert_weights(1, hidden(1))
        combine(0, wcol0)
        combine(1, wcol1)
        assert not todo, "every prefetch row DMA must be issued exactly once"

    npairs = tb // 2
    for i in range(PEER_RING - 1):
        for a, slot in enumerate(ring(i)):
            issue(2 * i + a, slot, 0, PEER_E)

    def body(i, carry):
        pair(i, True)
        return carry

    lax.fori_loop(0, npairs - (PEER_RING - 1), body, 0)
    for i in range(npairs - (PEER_RING - 1), npairs):
        pair(i, False)
    hsum = h1_ref[...] + acc_ref[...]
    ms = jnp.mean(hsum * hsum, axis=(1, 2), keepdims=True)
    out_ref[...] = hsum * lax.rsqrt(ms + RMS_EPS) * fg_ref[...]


def _peer_ffn(idx, gate, c3, h13, final_g3, uv_tiles):
    t = h13.shape[0]
    tb = 128
    row3 = pl.BlockSpec((tb, SUBLANES, LANES), lambda i: (i, 0, 0))
    return pl.pallas_call(
        functools.partial(_peerffn_body, tb),
        grid=(t // tb,),
        in_specs=[pl.BlockSpec((tb, PEER_E), lambda i: (i, 0), memory_space=pltpu.SMEM),
                  pl.BlockSpec((tb, PEER_E), lambda i: (i, 0)), row3, row3,
                  pl.BlockSpec((1, SUBLANES, LANES), lambda i: (0, 0, 0)),
                  pl.BlockSpec(memory_space=pl.ANY)],
        out_specs=row3,
        out_shape=jax.ShapeDtypeStruct((t, SUBLANES, LANES), F32),
        scratch_shapes=[pltpu.VMEM((2 * PEER_RING, PEER_E, PEER_TILE, LANES), BF16),
                        pltpu.VMEM((tb, SUBLANES, LANES), F32),
                        pltpu.SemaphoreType.DMA((2 * PEER_RING,))],
        compiler_params=_params("arbitrary"),
        name="peer_ffn",
    )(idx, gate, c3, h13, final_g3, uv_tiles)


def _overlap_table(seq):
    ci = jnp.arange(seq // CMP_STRIDE)[None, :] * CMP_STRIDE
    sj = jnp.arange(SEL_BLOCK)[:, None] * SEL_BLOCK
    return ((ci < sj + SEL_BLOCK) & (ci + CMP_BLOCK > sj)).astype(F32)


def _cmp_blocks(kv):
    b, g, s, dh = kv.shape
    r = kv.reshape(b * g, s // CMP_STRIDE, CMP_STRIDE * dh)
    return jnp.concatenate([r, jnp.roll(r, -1, axis=1)], axis=-1)


def _layer(h, lidx, attn_norm, w_in, lq1, lk1, lq2, lk2, subln, pe_k, pe_v, w1k, w1v, w2k, w2v,
           p_da, p_nsa, w_o, ffn_norm, wq, k1, k2, pu, pv, out_norm):
    b, s, d = h.shape
    t = b * s
    g, hg = NSA_GROUPS, NSA_HG
    lambda_init = 0.8 - 0.6 * math.exp(-0.3 * lidx)
    x2 = h.reshape(t, d)

    qda, kda, vda, qn, kvn, gm, gn = _in_proj(x2, attn_norm.reshape(1, d), _pack_w_in(w_in),
                                              _rope_tables(s), s)
    tr = lambda a2: jnp.swapaxes(a2.reshape(b, s, -1), 1, 2)
    ydat = _diff_attn(tr(qda), kda.reshape(b, s, -1), tr(vda),
                      lq1.reshape(1, -1), lk1.reshape(1, -1), lq2.reshape(1, -1),
                      lk2.reshape(1, -1), subln.reshape(-1, 1), lambda_init)
    yda = jnp.swapaxes(ydat, 1, 2).reshape(t, -1)

    assert s // SEL_BLOCK <= SEL_BLOCK, "selection bias rows hold at most 64 blocks"
    kv6 = kvn.reshape(b, s, 6, g, HEAD).transpose(2, 0, 3, 1, 4)
    kc_raw, vc_raw, ks, vs, kw, vw = (kv6[i] for i in range(6))
    kca, vc = _compress(_cmp_blocks(kc_raw), _cmp_blocks(vc_raw),
                        pe_k.reshape(1, -1), pe_v.reshape(1, -1),
                        w1k.astype(BF16), w1v.astype(BF16), w2k.astype(BF16), w2v.astype(BF16))
    ncp = s // CMP_STRIDE
    kca = kca.reshape(b, g, ncp, LANES)
    vct = jnp.swapaxes(vc.reshape(b, g, ncp, HEAD), 2, 3)

    qnt = tr(qn)
    ocmp_t, bias_t = _cmp_select(qnt, kca, vct, _overlap_table(s))
    onehot = (jnp.arange(s)[:, None] // SEL_BLOCK == jnp.arange(HEAD)[None, :]).astype(BF16)
    ksa = jnp.concatenate([ks, jnp.broadcast_to(onehot, ks.shape)], axis=-1)
    kwa = jnp.concatenate([kw, jnp.zeros_like(kw)], axis=-1)
    ynt = _sel_win(qnt, bias_t, ksa, jnp.swapaxes(vs, 2, 3), kwa, jnp.swapaxes(vw, 2, 3),
                   ocmp_t, tr(gn))
    yn = jnp.swapaxes(ynt, 1, 2).reshape(t, -1)

    h1, cb, s1t, s2t = _out_proj(yda, yn, gm, x2, p_da.astype(BF16),
                                 p_nsa.astype(BF16), w_o.astype(BF16), ffn_norm.reshape(1, d),
                                 wq.astype(BF16), k1.astype(BF16), k2.astype(BF16))
    idx_t, gate_t = _peer_topk(s1t, s2t)
    idx = idx_t.reshape(PEER_E, t).T
    gate = gate_t.reshape(PEER_E, t).T
    uv_tiles = jnp.concatenate([pu, pv], axis=1).astype(BF16).reshape(-1, PEER_TILE, LANES)
    as_tiles = lambda a2: a2.reshape(-1, SUBLANES, LANES)
    out = _peer_ffn(idx, gate, as_tiles(cb), as_tiles(h1), as_tiles(out_norm.reshape(1, d)),
                    uv_tiles)
    return out.reshape(b, s, d)


def kernel(x, attn_norm, w_in, da_lambda_q1, da_lambda_k1, da_lambda_q2, da_lambda_k2, da_subln,
           cmp_pe_k, cmp_pe_v, cmp_w1_k, cmp_w1_v, cmp_w2_k, cmp_w2_v, p_da, p_nsa, w_o,
           ffn_norm, peer_wq, peer_k1, peer_k2, peer_u, peer_v, final_norm):
    depth = attn_norm.shape[0]
    assert depth == 1, "the final norm is fused into the last layer's PEER kernel"
    h = x
    for l in range(depth):
        h = _layer(h, l, attn_norm[l], w_in[l], da_lambda_q1[l], da_lambda_k1[l], da_lambda_q2[l],
                   da_lambda_k2[l], da_subln[l], cmp_pe_k[l], cmp_pe_v[l], cmp_w1_k[l],
                   cmp_w1_v[l], cmp_w2_k[l], cmp_w2_v[l], p_da[l], p_nsa[l], w_o[l], ffn_norm[l],
                   peer_wq[l], peer_k1[l], peer_k2[l], peer_u[l], peer_v[l], final_norm)
    return h
```

```python
import functools
import math

import jax
import jax.numpy as jnp
from jax import lax
from jax.experimental import pallas as pl
from jax.experimental.pallas import tpu as pltpu

F32 = jnp.float32
BF16 = jnp.bfloat16
I32 = jnp.int32

RMS_EPS = 1e-6
ROPE_THETA = 500000.0
ROPE_HALF = 8
HEAD = 64
DA_HEADS = 4
NSA_GROUPS = 2
NSA_HG = 4
CMP_STRIDE = 16
CMP_BLOCK = 32
SEL_BLOCK = 64
SEL_TOPK = 16
WINDOW = 512
FORCE_BONUS = 1e4
NEG_BIG = -1e30
SEL_MASK_BIAS = -32768.0
PEER_HEADS = 8
PEER_NKEYS = 128
PEER_TOPK = 16
LANES = 128
SUBLANES = 8
VMEM_LIMIT = 56 * 1024 * 1024

NT_DIMS = (((1,), (1,)), ((), ()))


def _rmsnorm(x, g):
    return x * lax.rsqrt(jnp.mean(x * x, axis=-1, keepdims=True) + RMS_EPS) * g


def _sigmoid(z):
    return 1.0 / (1.0 + jnp.exp(-z))


def _gelu(z):
    return 0.5 * z * (1.0 + lax.erf(z * (2.0 ** -0.5)))


def _params(*sem):
    return pltpu.CompilerParams(dimension_semantics=sem, vmem_limit_bytes=VMEM_LIMIT)


_QDA0, _KDA0, _VDA0, _QN0, _KVN0, _GM0, _GN0, _WCOLS = 0, 512, 1024, 1536, 2048, 2816, 4864, 5120


def _inproj_body(x_ref, g_ref, w_ref, rc_ref, rs1_ref, rs2_ref,
                 qda_ref, kda_ref, vda_ref, qn_ref, kvn_ref, gm_ref, gn_ref):
    a = _rmsnorm(x_ref[...], g_ref[...]).astype(BF16)
    rc, rs1, rs2 = rc_ref[...], rs1_ref[...], rs2_ref[...]

    def rope(z):
        return (z * rc + pltpu.roll(z, ROPE_HALF, 1) * rs1
                + pltpu.roll(z, LANES - ROPE_HALF, 1) * rs2)

    def proj(c0):
        return jnp.dot(a, w_ref[:, c0:c0 + 256], preferred_element_type=F32)

    def rope2(z):
        return jnp.concatenate([rope(z[:, :LANES]), rope(z[:, LANES:])], axis=1)

    for c in range(2):
        qda_ref[:, c * 256:(c + 1) * 256] = (rope2(proj(_QDA0 + c * 256)) * 0.125).astype(BF16)
        kda_ref[:, c * 256:(c + 1) * 256] = rope2(proj(_KDA0 + c * 256)).astype(BF16)
        vda_ref[:, c * 256:(c + 1) * 256] = proj(_VDA0 + c * 256).astype(BF16)
        qn_ref[:, c * 256:(c + 1) * 256] = (rope2(proj(_QN0 + c * 256)) * 0.125).astype(BF16)
    for c in range(3):
        z = proj(_KVN0 + c * 256)
        kvn_ref[:, c * 256:c * 256 + LANES] = rope(z[:, :LANES]).astype(BF16)
        kvn_ref[:, c * 256 + LANES:(c + 1) * 256] = z[:, LANES:].astype(BF16)
    for c in range(8):
        gm_ref[:, c * 256:(c + 1) * 256] = _sigmoid(proj(_GM0 + c * 256)).astype(BF16)
    gn_ref[...] = _sigmoid(proj(_GN0))


def _pack_w_in(w):
    d = w.shape[0]
    gn = w[:, 2816:2840]
    pad = jnp.zeros((d, LANES - 12), w.dtype)
    return jnp.concatenate([w[:, :2816], w[:, 2840:], gn[:, :12], pad, gn[:, 12:], pad],
                           axis=1).astype(BF16)


def _rope_tables(seq):
    inv = jnp.power(ROPE_THETA, -jnp.arange(ROPE_HALF, dtype=F32) * 2.0 / (2 * ROPE_HALF))
    ang = jnp.arange(seq, dtype=F32)[:, None] * inv[None, :]
    cos, sin = jnp.cos(ang), jnp.sin(ang)
    one = jnp.ones((seq, HEAD - 2 * ROPE_HALF), F32)
    zero8 = jnp.zeros((seq, ROPE_HALF), F32)
    zero48 = jnp.zeros_like(one)
    rc = jnp.concatenate([cos, cos, one], axis=1)
    rs1 = jnp.concatenate([zero8, sin, zero48], axis=1)
    rs2 = jnp.concatenate([-sin, zero8, zero48], axis=1)
    return tuple(jnp.concatenate([t, t], axis=1) for t in (rc, rs1, rs2))


def _in_proj(x2, norm_g, w_packed, rope_tabs, seq):
    t, d = x2.shape
    tm = 512
    nseq = seq // tm
    row = lambda i: (i, 0)
    fixed = lambda i: (0, 0)
    out_shapes = [
        jax.ShapeDtypeStruct((t, 512), BF16), jax.ShapeDtypeStruct((t, 512), BF16),
        jax.ShapeDtypeStruct((t, 512), BF16), jax.ShapeDtypeStruct((t, 512), BF16),
        jax.ShapeDtypeStruct((t, 768), BF16), jax.ShapeDtypeStruct((t, 2048), BF16),
        jax.ShapeDtypeStruct((t, 256), F32)]
    rope_spec = pl.BlockSpec((tm, LANES), lambda i: (i % nseq, 0))
    return pl.pallas_call(
        _inproj_body,
        grid=(t // tm,),
        in_specs=[pl.BlockSpec((tm, d), row), pl.BlockSpec((1, d), fixed),
                  pl.BlockSpec((d, _WCOLS), fixed), rope_spec, rope_spec, rope_spec],
        out_specs=[pl.BlockSpec((tm, s.shape[1]), row) for s in out_shapes],
        out_shape=out_shapes,
        compiler_params=_params("parallel"),
        name="in_proj",
    )(x2, norm_g, w_packed, *rope_tabs)


def _softmax_step(s, vt, m_ref, l_ref, acc_ref):
    m_prev = m_ref[...]
    m_new = jnp.maximum(m_prev, jnp.max(s, axis=0, keepdims=True))
    alpha = jnp.exp(m_prev - m_new)
    p = jnp.exp(s - m_new)
    l_ref[...] = alpha * l_ref[...] + jnp.sum(p, axis=0, keepdims=True)
    acc_ref[...] = alpha * acc_ref[...] + jnp.dot(vt, p.astype(BF16), preferred_element_type=F32)
    m_ref[...] = m_new


def _softmax_reset(m_ref, l_ref, acc_ref):
    m_ref[...] = jnp.full(m_ref.shape, NEG_BIG, F32)
    l_ref[...] = jnp.zeros(l_ref.shape, F32)
    acc_ref[...] = jnp.zeros(acc_ref.shape, F32)


def _attend_tiles(n_full, scores, values, mask_last, sa_ref, sb_ref, m_ref, l_ref, acc_ref):
    step = lambda s, t: _softmax_step(s, values(t), m_ref, l_ref, acc_ref)
    sa_ref[...] = scores(0)

    def two_tiles(i, carry):
        t = 2 * i
        sb_ref[...] = scores(t + 1)
        step(sa_ref[...], t)
        sa_ref[...] = scores(t + 2)
        step(sb_ref[...], t + 1)
        return carry

    lax.fori_loop(0, n_full // 2, two_tiles, 0)
    odd = (n_full & 1) == 1

    @pl.when(odd)
    def _():
        sb_ref[...] = scores(n_full)
        step(sa_ref[...], n_full - 1)
        step(mask_last(sb_ref[...]), n_full)

    @pl.when(jnp.logical_not(odd))
    def _():
        step(mask_last(sa_ref[...]), n_full)


def _key_tile(ref, kt, bk):
    return ref[(0,) * (len(ref.shape) - 2) + (pl.ds(pl.multiple_of(kt * bk, bk), bk), slice(None))]


def _value_tile(ref, kt, bk):
    return ref[(0,) * (len(ref.shape) - 2) + (slice(None), pl.ds(pl.multiple_of(kt * bk, bk), bk))]


def _diffattn_body(lambda_init, bq, q_ref, k_ref, v_ref, lq1_ref, lk1_ref, lq2_ref, lk2_ref,
                   sub_ref, y_ref, qbd_ref, m_ref, l_ref, acc_ref, sa_ref, sb_ref):
    qi = pl.program_id(2)
    bk = bq
    qt = q_ref[0]
    sub = lax.broadcasted_iota(I32, qt.shape, 0)
    zero = jnp.zeros_like(qt)
    qbd_ref[:, 0:bq] = jnp.where(sub < HEAD, qt, zero)
    qbd_ref[:, bq:2 * bq] = jnp.where(sub >= HEAD, qt, zero)
    _softmax_reset(m_ref, l_ref, acc_ref)

    def scores(kt):
        return jnp.dot(_key_tile(k_ref, kt, bk), qbd_ref[...], preferred_element_type=F32)

    def causal(s):
        r = lax.broadcasted_iota(I32, (bk, 2 * bq), 0)
        c = lax.broadcasted_iota(I32, (bk, 2 * bq), 1) & (bq - 1)
        return jnp.where(r <= c, s, NEG_BIG)

    _attend_tiles(qi, scores, lambda kt: _value_tile(v_ref, kt, bk), causal,
                  sa_ref, sb_ref, m_ref, l_ref, acc_ref)

    o = acc_ref[...] / l_ref[...]
    lam = (jnp.exp(jnp.sum(lq1_ref[...] * lk1_ref[...], axis=1, keepdims=True))
           - jnp.exp(jnp.sum(lq2_ref[...] * lk2_ref[...], axis=1, keepdims=True)) + lambda_init)
    d = o[:, 0:bq] - lam * o[:, bq:2 * bq]
    ms = jnp.mean(d * d, axis=0, keepdims=True)
    y = d * lax.rsqrt(ms + RMS_EPS) * sub_ref[...] * (1.0 - lambda_init)
    y_ref[0] = y.astype(BF16)


def _diff_attn(qdat, kda, vdat, lq1, lk1, lq2, lk2, subln_col, lambda_init):
    b, s, _ = kda.shape
    bq = 512
    vec = lambda n: pl.BlockSpec((1, n), lambda bi, h, qi: (0, 0))
    dv = 2 * HEAD
    qtile = pl.BlockSpec((1, dv, bq), lambda bi, h, qi: (bi, h, qi))
    return pl.pallas_call(
        functools.partial(_diffattn_body, lambda_init, bq),
        grid=(b, DA_HEADS, s // bq),
        in_specs=[qtile,
                  pl.BlockSpec((1, s, LANES), lambda bi, h, qi: (bi, 0, h)),
                  pl.BlockSpec((1, dv, s), lambda bi, h, qi: (bi, h, 0)),
                  vec(HEAD), vec(HEAD), vec(HEAD), vec(HEAD),
                  pl.BlockSpec((dv, 1), lambda bi, h, qi: (0, 0))],
        out_specs=qtile,
        out_shape=jax.ShapeDtypeStruct((b, DA_HEADS * dv, s), BF16),
        scratch_shapes=[pltpu.VMEM((LANES, 2 * bq), BF16), pltpu.VMEM((1, 2 * bq), F32),
                        pltpu.VMEM((1, 2 * bq), F32), pltpu.VMEM((dv, 2 * bq), F32),
                        pltpu.VMEM((bq, 2 * bq), F32), pltpu.VMEM((bq, 2 * bq), F32)],
        compiler_params=_params("parallel", "parallel", "parallel"),
        name="diff_attn",
    )(qdat, kda, vdat, lq1, lk1, lq2, lk2, subln_col)


def _compress_body(xk_ref, xv_ref, pek_ref, pev_ref, w1k_ref, w1v_ref, w2k_ref, w2v_ref,
                   kc_ref, vc_ref):
    def mlp(x_ref, pe_ref, w1_ref, w2_ref):
        blocks = (x_ref[0].astype(F32) + pe_ref[...]).astype(BF16)
        hid = _gelu(jnp.dot(blocks, w1_ref[...], preferred_element_type=F32))
        return jnp.dot(hid.astype(BF16), w2_ref[...], preferred_element_type=F32)

    kc = mlp(xk_ref, pek_ref, w1k_ref, w2k_ref)
    kc_ref[0] = jnp.concatenate([kc, jnp.zeros_like(kc)], axis=1).astype(BF16)
    vc_ref[0] = mlp(xv_ref, pev_ref, w1v_ref, w2v_ref).astype(BF16)


def _compress(xk, xv, pe_k, pe_v, w1k, w1v, w2k, w2v):
    n, ncp, width = xk.shape
    blk = pl.BlockSpec((1, ncp, width), lambda i: (i, 0, 0))
    fixed = lambda shape: pl.BlockSpec(shape, lambda i: (0, 0))
    return pl.pallas_call(
        _compress_body,
        grid=(n,),
        in_specs=[blk, blk, fixed((1, width)), fixed((1, width)), fixed((width, HEAD)),
                  fixed((width, HEAD)), fixed((HEAD, HEAD)), fixed((HEAD, HEAD))],
        out_specs=[pl.BlockSpec((1, ncp, LANES), lambda i: (i, 0, 0)),
                   pl.BlockSpec((1, ncp, HEAD), lambda i: (i, 0, 0))],
        out_shape=[jax.ShapeDtypeStruct((n, ncp, LANES), BF16),
                   jax.ShapeDtypeStruct((n, ncp, HEAD), BF16)],
        compiler_params=_params("parallel"),
        name="compress",
    )(xk, xv, pe_k, pe_v, w1k, w1v, w2k, w2v)


def _heads_on_lanes(qt, bq):
    return jnp.concatenate([qt[h * HEAD:(h + 1) * HEAD, :] for h in range(NSA_HG)], axis=1)


def _cmpsel_body(bq, q_ref, kc_ref, vc_ref, ovl_ref, ocmp_ref, bias_ref):
    qi = pl.program_id(2)
    ncp = kc_ref.shape[2]
    rows = NSA_HG * bq
    q2 = _heads_on_lanes(q_ref[0], bq)
    qz = jnp.concatenate([q2, jnp.zeros_like(q2)], axis=0)
    s = jnp.dot(kc_ref[0, 0], qz, preferred_element_type=F32)
    n = lax.broadcasted_iota(I32, (ncp, rows), 0)
    qpos = qi * bq + (lax.broadcasted_iota(I32, (ncp, rows), 1) & (bq - 1))
    cmask = n * CMP_STRIDE + (CMP_BLOCK - 1) <= qpos
    s = jnp.where(cmask, s, NEG_BIG)
    e = jnp.exp(s - jnp.max(s, axis=0, keepdims=True))
    p = jnp.where(cmask, e / jnp.sum(e, axis=0, keepdims=True), 0.0)
    o = jnp.dot(vc_ref[0, 0], p.astype(BF16), preferred_element_type=F32)
    for h in range(NSA_HG):
        ocmp_ref[0, 0, h * HEAD:(h + 1) * HEAD, :] = o[:, h * bq:(h + 1) * bq]

    psum = p[:, 0:bq] + p[:, bq:2 * bq] + p[:, 2 * bq:3 * bq] + p[:, 3 * bq:4 * bq]
    imp = jnp.dot(ovl_ref[...], psum, preferred_element_type=F32)
    blk = lax.broadcasted_iota(I32, (SEL_BLOCK, bq), 0)
    pos = qi * bq + lax.broadcasted_iota(I32, (SEL_BLOCK, bq), 1)
    cur = lax.shift_right_logical(pos, 6)
    valid = blk <= cur
    forced = (blk == 0) | (blk == cur) | (blk == cur - 1)
    score = jnp.where(valid, imp + jnp.where(forced, FORCE_BONUS, 0.0), -jnp.inf)
    rank = jnp.zeros((SEL_BLOCK, bq), I32)
    for i in range(SEL_BLOCK):
        other = score[i:i + 1, :]
        beats = (other > score) | ((other == score) & (blk > i))
        rank = rank + beats.astype(I32)
    keep = valid & (rank < SEL_TOPK)
    bias_ref[0, 0] = jnp.where(keep, 0.0, SEL_MASK_BIAS).astype(BF16)


def _cmp_select(qnt, kca, vct, overlap_t):
    b, g, ncp, _ = kca.shape
    s = qnt.shape[2]
    bq = 128
    dq = NSA_HG * HEAD
    return pl.pallas_call(
        functools.partial(_cmpsel_body, bq),
        grid=(b, g, s // bq),
        in_specs=[pl.BlockSpec((1, dq, bq), lambda bi, gi, qi: (bi, gi, qi)),
                  pl.BlockSpec((1, 1, ncp, LANES), lambda bi, gi, qi: (bi, gi, 0, 0)),
                  pl.BlockSpec((1, 1, HEAD, ncp), lambda bi, gi, qi: (bi, gi, 0, 0)),
                  pl.BlockSpec((SEL_BLOCK, ncp), lambda bi, gi, qi: (0, 0))],
        out_specs=[pl.BlockSpec((1, 1, dq, bq), lambda bi, gi, qi: (bi, gi, 0, qi)),
                   pl.BlockSpec((1, 1, SEL_BLOCK, bq), lambda bi, gi, qi: (bi, gi, 0, qi))],
        out_shape=[jax.ShapeDtypeStruct((b, g, dq, s), F32),
                   jax.ShapeDtypeStruct((b, g, SEL_BLOCK, s), BF16)],
        compiler_params=_params("parallel", "parallel", "parallel"),
        name="cmp_select",
    )(qnt, kca, vct, overlap_t)


def _selwin_body(bq, bks, q_ref, bias_ref, ks_ref, vs_ref, kw_ref, vw_ref, ocmp_ref, gate_ref,
                 y_ref, m_ref, l_ref, acc_ref, qa_ref, sa_ref, sb_ref):
    qi = pl.program_id(2)
    rows = NSA_HG * bq
    q2 = _heads_on_lanes(q_ref[0], bq)
    bias = bias_ref[0, 0]
    qa = jnp.concatenate([q2, jnp.concatenate([bias] * NSA_HG, axis=1)], axis=0)
    qw = jnp.concatenate([q2, jnp.zeros_like(q2)], axis=0)
    r = lax.broadcasted_iota(I32, (bq, rows), 0)
    c = lax.broadcasted_iota(I32, (bq, rows), 1) & (bq - 1)

    _softmax_reset(m_ref, l_ref, acc_ref)

    qa_ref[...] = qa
    last = (qi * bq) // bks

    def causal(s):
        kpos = last * bks + lax.broadcasted_iota(I32, (bks, rows), 0)
        qpos = qi * bq + (lax.broadcasted_iota(I32, (bks, rows), 1) & (bq - 1))
        return jnp.where(kpos <= qpos, s, NEG_BIG)

    _attend_tiles(last,
                  lambda kt: jnp.dot(_key_tile(ks_ref, kt, bks), qa_ref[...],
                                     preferred_element_type=F32),
                  lambda kt: _value_tile(vs_ref, kt, bks), causal,
                  sa_ref, sb_ref, m_ref, l_ref, acc_ref)
    o_sel = acc_ref[...] / l_ref[...]

    _softmax_reset(m_ref, l_ref, acc_ref)
    for back in range(WINDOW // bq, -1, -1):
        kt = qi - back

        @pl.when(kt >= 0)
        def _():
            ktc = jnp.maximum(kt, 0)
            dist = c + back * bq - r
            s = jnp.dot(_key_tile(kw_ref, ktc, bq), qw, preferred_element_type=F32)
            s = jnp.where((dist >= 0) & (dist < WINDOW), s, NEG_BIG)
            _softmax_step(s, _value_tile(vw_ref, ktc, bq), m_ref, l_ref, acc_ref)

    o_win = acc_ref[...] / l_ref[...]
    gate = gate_ref[0]
    for h in range(NSA_HG):
        sl = slice(h * bq, (h + 1) * bq)
        y = (gate[3 * h:3 * h + 1, :] * ocmp_ref[0, 0, h * HEAD:(h + 1) * HEAD, :]
             + gate[3 * h + 1:3 * h + 2, :] * o_sel[:, sl]
             + gate[3 * h + 2:3 * h + 3, :] * o_win[:, sl])
        y_ref[0, h * HEAD:(h + 1) * HEAD, :] = y.astype(BF16)


def _sel_win(qnt, bias_t, ksa, vst, kwa, vwt, ocmp_t, gnt):
    b, g, s, _ = ksa.shape
    bq, bks = 256, 512
    dq = NSA_HG * HEAD
    rows = NSA_HG * bq
    keys = pl.BlockSpec((1, 1, s, LANES), lambda bi, gi, qi: (bi, gi, 0, 0))
    vals = pl.BlockSpec((1, 1, HEAD, s), lambda bi, gi, qi: (bi, gi, 0, 0))
    qtile = pl.BlockSpec((1, dq, bq), lambda bi, gi, qi: (bi, gi, qi))
    return pl.pallas_call(
        functools.partial(_selwin_body, bq, bks),
        grid=(b, g, s // bq),
        in_specs=[qtile,
                  pl.BlockSpec((1, 1, SEL_BLOCK, bq), lambda bi, gi, qi: (bi, gi, 0, qi)),
                  keys, vals, keys, vals,
                  pl.BlockSpec((1, 1, dq, bq), lambda bi, gi, qi: (bi, gi, 0, qi)),
                  pl.BlockSpec((1, LANES, bq), lambda bi, gi, qi: (bi, gi, qi))],
        out_specs=qtile,
        out_shape=jax.ShapeDtypeStruct((b, g * dq, s), BF16),
        scratch_shapes=[pltpu.VMEM((1, rows), F32), pltpu.VMEM((1, rows), F32),
                        pltpu.VMEM((HEAD, rows), F32), pltpu.VMEM((LANES, rows), BF16),
                        pltpu.VMEM((bks, rows), F32), pltpu.VMEM((bks, rows), F32)],
        compiler_params=_params("parallel", "parallel", "parallel"),
        name="sel_win",
    )(qnt, bias_t, ksa, vst, kwa, vwt, ocmp_t, gnt)


def _outproj_body(yda_ref, yn_ref, gm_ref, x_ref, pda_ref, pnsa_ref, wo_ref, fg_ref, wq_ref,
                  k1_ref, k2_ref, h1_ref, c_ref, s1_ref, s2_ref):
    d = x_ref.shape[1]
    a = jnp.dot(yda_ref[...], pda_ref[...], preferred_element_type=F32)
    bn = jnp.dot(yn_ref[...], pnsa_ref[...], preferred_element_type=F32)
    merged = gm_ref[:, :d].astype(F32) * a + gm_ref[:, d:].astype(F32) * bn
    h1 = x_ref[...] + jnp.dot(merged.astype(BF16), wo_ref[...], preferred_element_type=F32)
    h1_ref[...] = h1
    c = _rmsnorm(h1, fg_ref[...])
    c_ref[...] = c
    cb = c.astype(BF16)
    for h in range(PEER_HEADS):
        qh = jnp.dot(cb, wq_ref[:, h * 256:(h + 1) * 256], preferred_element_type=F32).astype(BF16)
        s1_ref[h] = lax.dot_general(k1_ref[...], qh[:, :LANES], NT_DIMS, preferred_element_type=F32)
        s2_ref[h] = lax.dot_general(k2_ref[...], qh[:, LANES:], NT_DIMS, preferred_element_type=F32)


def _out_proj(yda, yn, gm, x2, pda, pnsa, wo, ffn_g, wq, k1, k2):
    t, d = x2.shape
    tm = 256
    row = lambda w: pl.BlockSpec((tm, w), lambda i: (i, 0))
    fixed = lambda a: pl.BlockSpec(a.shape, lambda i: (0, 0))
    sspec = pl.BlockSpec((PEER_HEADS, PEER_NKEYS, tm), lambda i: (0, 0, i))
    return pl.pallas_call(
        _outproj_body,
        grid=(t // tm,),
        in_specs=[row(512), row(512), row(2 * d), row(d), fixed(pda), fixed(pnsa), fixed(wo),
                  fixed(ffn_g), fixed(wq), fixed(k1), fixed(k2)],
        out_specs=[row(d), row(d), sspec, sspec],
        out_shape=[jax.ShapeDtypeStruct((t, d), F32), jax.ShapeDtypeStruct((t, d), F32),
                   jax.ShapeDtypeStruct((PEER_HEADS, PEER_NKEYS, t), F32),
                   jax.ShapeDtypeStruct((PEER_HEADS, PEER_NKEYS, t), F32)],
        compiler_params=_params("parallel"),
        name="out_proj",
    )(yda, yn, gm, x2, pda, pnsa, wo, ffn_g, wq, k1, k2)


def _batcher_pairs(n):
    pairs = []

    def merge(lo, hi, r):
        step = r * 2
        if step < hi - lo:
            merge(lo, hi, step)
            merge(lo + r, hi, step)
            pairs.extend((i, i + r) for i in range(lo + r, hi - r, step))
        else:
            pairs.append((lo, lo + r))

    def sort(lo, hi):
        if hi - lo >= 1:
            mid = lo + (hi - lo) // 2
            sort(lo, mid)
            sort(mid + 1, hi)
            merge(lo, hi, 1)

    sort(0, n - 1)
    return pairs


_NET16 = _batcher_pairs(PEER_TOPK)


def _cmpx(a, b):
    c = a[0] >= b[0]
    return ((jnp.where(c, a[0], b[0]), jnp.where(c, a[1], b[1])),
            (jnp.where(c, b[0], a[0]), jnp.where(c, b[1], a[1])))


def _sort_lists(lists, n_real):
    lists = list(lists)
    for i, j in _NET16:
        if j < n_real:
            lists[i], lists[j] = _cmpx(lists[i], lists[j])
    return lists


def _merge_sublanes(lists):
    k = PEER_TOPK
    for dist in (4, 2, 1):
        other = [(pltpu.roll(v, dist, 0), pltpu.roll(ix, dist, 0)) for v, ix in lists]
        lists = [_cmpx(lists[i], other[k - 1 - i])[0] for i in range(k)]
        step = k // 2
        while step >= 1:
            for i in range(k):
                if i & step == 0:
                    lists[i], lists[i + step] = _cmpx(lists[i], lists[i + step])
            step //= 2
    return lists


def _spread(lists, off, sub):
    v, ix = lists[off]
    for r in range(1, SUBLANES):
        sel = sub == r
        v = jnp.where(sel, lists[off + r][0], v)
        ix = jnp.where(sel, lists[off + r][1], ix)
    return v, ix


def _peertopk_body(tt, s1_ref, s2_ref, idx_ref, gate_ref):
    sub = lax.broadcasted_iota(I32, (SUBLANES, LANES), 0)
    groups = PEER_NKEYS // SUBLANES

    def top16(ref, h, lanes):
        lists = [(ref[h, v * SUBLANES:(v + 1) * SUBLANES, lanes], sub + v * SUBLANES)
                 for v in range(groups)]
        return _merge_sublanes(_sort_lists(lists, groups))

    def unit(u, carry):
        h = u // (tt // LANES)
        lanes = pl.ds(pl.multiple_of((u % (tt // LANES)) * LANES, LANES), LANES)
        l1 = top16(s1_ref, h, lanes)
        l2 = top16(s2_ref, h, lanes)
        v2lo, v2hi, v1hi = _spread(l2, 0, sub), _spread(l2, SUBLANES, sub), _spread(l1, SUBLANES, sub)
        cands = [(l1[a][0] + v2lo[0], l1[a][1] * PEER_NKEYS + v2lo[1]) for a in range(SUBLANES)]
        cands.append((l1[0][0] + v2hi[0], l1[0][1] * PEER_NKEYS + v2hi[1]))
        cands.append((v1hi[0] + l2[0][0], v1hi[1] * PEER_NKEYS + l2[0][1]))
        n_real = len(cands)
        filler = (jnp.full((SUBLANES, LANES), -jnp.inf, F32), jnp.zeros((SUBLANES, LANES), I32))
        cands += [filler] * (PEER_TOPK - n_real)
        best = _merge_sublanes(_sort_lists(cands, n_real))
        ex = [jnp.exp(v - best[0][0]) for v, _ in best]
        z = ex[0]
        for e in ex[1:]:
            z = z + e
        gl = [(e / z, ix) for e, (_, ix) in zip(ex, best)]
        lo, hi = _spread(gl, 0, sub), _spread(gl, SUBLANES, sub)
        gate_ref[h, :, lanes] = jnp.concatenate([lo[0], hi[0]], axis=0)
        idx_ref[h, :, lanes] = jnp.concatenate([lo[1], hi[1]], axis=0)
        return carry

    lax.fori_loop(0, PEER_HEADS * (tt // LANES), unit, 0)


def _peer_topk(s1t, s2t):
    _, _, t = s1t.shape
    tt = 512
    spec_in = pl.BlockSpec((PEER_HEADS, PEER_NKEYS, tt), lambda i: (0, 0, i))
    spec_out = pl.BlockSpec((PEER_HEADS, PEER_TOPK, tt), lambda i: (0, 0, i))
    return pl.pallas_call(
        functools.partial(_peertopk_body, tt),
        grid=(t // tt,),
        in_specs=[spec_in, spec_in],
        out_specs=[spec_out, spec_out],
        out_shape=[jax.ShapeDtypeStruct((PEER_HEADS, PEER_TOPK, t), I32),
                   jax.ShapeDtypeStruct((PEER_HEADS, PEER_TOPK, t), F32)],
        compiler_params=_params("parallel"),
        name="peer_topk",
    )(s1t, s2t)


PEER_E = PEER_HEADS * PEER_TOPK
PEER_TILE = 2 * SUBLANES
PEER_RING = 3
PEER_MID_ROWS = 48


def _sublane_sums(a, sub):
    for dist in (4, 2, 1):
        low = (sub & dist) == 0
        half = len(a) // 2
        a = [jnp.where(low, a[i], pltpu.roll(a[i + half], dist, 0))
             + jnp.where(low, pltpu.roll(a[i], SUBLANES - dist, 0), a[i + half])
             for i in range(half)]
    return a[0]


def _peerffn_body(tb, idx_ref, gate_ref, c_ref, h1_ref, fg_ref, uv_ref, out_ref, buf_ref,
                  acc_ref, sem_ref):
    sub = lax.broadcasted_iota(I32, (SUBLANES, LANES), 0)
    eye = (lax.broadcasted_iota(I32, (PEER_E, LANES), 0)
           == lax.broadcasted_iota(I32, (PEER_E, LANES), 1))
    ones_rows = jnp.ones((SUBLANES, LANES), BF16)
    ones_sq = jnp.ones((LANES, LANES), BF16)
    ngroup = PEER_E // SUBLANES

    def row_copy(t, j, slot):
        return pltpu.make_async_copy(uv_ref.at[idx_ref[t, j]], buf_ref.at[slot, j],
                                     sem_ref.at[slot])

    def issue(t, slot, j0, j1):
        for j in range(j0, j1):
            row_copy(t, j, slot).start(priority=j % 2)

    def wait_all(slot):
        pltpu.make_async_copy(uv_ref.at[pl.ds(0, PEER_E)], buf_ref.at[slot],
                              sem_ref.at[slot]).wait()

    def expert_u(slot, j):
        return buf_ref[slot, j, 0:SUBLANES, :].astype(F32)

    def expert_v(slot, j):
        return buf_ref[slot, j, SUBLANES:PEER_TILE, :].astype(F32)

    def ring(i):
        base = 2 * (i % PEER_RING)
        return (base, base + 1)

    def pair(i, prefetch):
        ahead = PEER_RING - 1
        toks = (2 * i, 2 * i + 1)
        slots = ring(i)
        nslots = ring(i + ahead)
        todo = [(a, j) for a in range(2) for j in range(PEER_E)]

        def issue_some(n):
            for a, j in todo[:n]:
                if prefetch:
                    row_copy(toks[a] + 2 * ahead, j, nslots[a]).start(priority=j % 2)
            del todo[:n]

        wait_all(slots[0])
        wait_all(slots[1])
        x8 = [c_ref[t] for t in toks]
        gates = [gate_ref[pl.ds(t, 1), :] for t in toks]

        def hidden(a):
            groups = []
            for g in range(ngroup):
                prods = [expert_u(slots[a], g * SUBLANES + r) * x8[a] for r in range(SUBLANES)]
                groups.append(_sublane_sums(prods, sub))
                issue_some(3 - g % 2)
            return jnp.concatenate(groups, axis=0)

        def expert_weights(a, q):
            q_hi = q.astype(BF16)
            q_lo = (q - q_hi.astype(F32)).astype(BF16)
            hid = (lax.dot_general(ones_rows, q_hi, NT_DIMS, preferred_element_type=F32)
                   + lax.dot_general(ones_rows, q_lo, NT_DIMS, preferred_element_type=F32))
            issue_some(PEER_MID_ROWS // 2)
            w = _gelu(hid[0:1]) * gates[a]
            wd = jnp.where(eye, jnp.broadcast_to(w, (PEER_E, LANES)), 0.0).astype(BF16)
            wcol = jnp.dot(wd, ones_sq, preferred_element_type=F32)
            issue_some(PEER_MID_ROWS // 2)
            return wcol

        def combine(a, wcol):
            out = jnp.zeros((SUBLANES, LANES), F32)
            for g in range(ngroup):
                for r in range(SUBLANES):
                    j = g * SUBLANES + r
                    out = out + wcol[j:j + 1, :] * expert_v(slots[a], j)
                issue_some(3 - g % 2)
            acc_ref[toks[a]] = out

        wcol0 = expert_weights(0, hidden(0))
        wcol1 = expert_weights(1, hidden(1))
        combine(0, wcol0)
        combine(1, wcol1)
        assert not todo, "every prefetch row DMA must be issued exactly once"

    npairs = tb // 2
    for i in range(PEER_RING - 1):
        for a, slot in enumerate(ring(i)):
            issue(2 * i + a, slot, 0, PEER_E)

    def body(i, carry):
        pair(i, True)
        return carry

    lax.fori_loop(0, npairs - (PEER_RING - 1), body, 0)
    for i in range(npairs - (PEER_RING - 1), npairs):
        pair(i, False)
    hsum = h1_ref[...] + acc_ref[...]
    ms = jnp.mean(hsum * hsum, axis=(1, 2), keepdims=True)
    out_ref[...] = hsum * lax.rsqrt(ms + RMS_EPS) * fg_ref[...]


def _peer_ffn(idx, gate, c3, h13, final_g3, uv_tiles):
    t = h13.shape[0]
    tb = 128
    row3 = pl.BlockSpec((tb, SUBLANES, LANES), lambda i: (i, 0, 0))
    return pl.pallas_call(
        functools.partial(_peerffn_body, tb),
        grid=(t // tb,),
        in_specs=[pl.BlockSpec((tb, PEER_E), lambda i: (i, 0), memory_space=pltpu.SMEM),
                  pl.BlockSpec((tb, PEER_E), lambda i: (i, 0)), row3, row3,
                  pl.BlockSpec((1, SUBLANES, LANES), lambda i: (0, 0, 0)),
                  pl.BlockSpec(memory_space=pl.ANY)],
        out_specs=row3,
        out_shape=jax.ShapeDtypeStruct((t, SUBLANES, LANES), F32),
        scratch_shapes=[pltpu.VMEM((2 * PEER_RING, PEER_E, PEER_TILE, LANES), BF16),
                        pltpu.VMEM((tb, SUBLANES, LANES), F32),
                        pltpu.SemaphoreType.DMA((2 * PEER_RING,))],
        compiler_params=_params("arbitrary"),
        name="peer_ffn",
    )(idx, gate, c3, h13, final_g3, uv_tiles)


def _overlap_table(seq):
    ci = jnp.arange(seq // CMP_STRIDE)[None, :] * CMP_STRIDE
    sj = jnp.arange(SEL_BLOCK)[:, None] * SEL_BLOCK
    return ((ci < sj + SEL_BLOCK) & (ci + CMP_BLOCK > sj)).astype(F32)


def _cmp_blocks(kv):
    b, g, s, dh = kv.shape
    r = kv.reshape(b * g, s // CMP_STRIDE, CMP_STRIDE * dh)
    return jnp.concatenate([r, jnp.roll(r, -1, axis=1)], axis=-1)


def _layer(h, lidx, attn_norm, w_in, lq1, lk1, lq2, lk2, subln, pe_k, pe_v, w1k, w1v, w2k, w2v,
           p_da, p_nsa, w_o, ffn_norm, wq, k1, k2, pu, pv, out_norm):
    b, s, d = h.shape
    t = b * s
    g, hg = NSA_GROUPS, NSA_HG
    lambda_init = 0.8 - 0.6 * math.exp(-0.3 * lidx)
    x2 = h.reshape(t, d)

    qda, kda, vda, qn, kvn, gm, gn = _in_proj(x2, attn_norm.reshape(1, d), _pack_w_in(w_in),
                                              _rope_tables(s), s)
    tr = lambda a2: jnp.swapaxes(a2.reshape(b, s, -1), 1, 2)
    ydat = _diff_attn(tr(qda), kda.reshape(b, s, -1), tr(vda),
                      lq1.reshape(1, -1), lk1.reshape(1, -1), lq2.reshape(1, -1),
                      lk2.reshape(1, -1), subln.reshape(-1, 1), lambda_init)
    yda = jnp.swapaxes(ydat, 1, 2).reshape(t, -1)

    assert s // SEL_BLOCK <= SEL_BLOCK, "selection bias rows hold at most 64 blocks"
    kv6 = kvn.reshape(b, s, 6, g, HEAD).transpose(2, 0, 3, 1, 4)
    kc_raw, vc_raw, ks, vs, kw, vw = (kv6[i] for i in range(6))
    kca, vc = _compress(_cmp_blocks(kc_raw), _cmp_blocks(vc_raw),
                        pe_k.reshape(1, -1), pe_v.reshape(1, -1),
                        w1k.astype(BF16), w1v.astype(BF16), w2k.astype(BF16), w2v.astype(BF16))
    ncp = s // CMP_STRIDE
    kca = kca.reshape(b, g, ncp, LANES)
    vct = jnp.swapaxes(vc.reshape(b, g, ncp, HEAD), 2, 3)

    qnt = tr(qn)
    ocmp_t, bias_t = _cmp_select(qnt, kca, vct, _overlap_table(s))
    onehot = (jnp.arange(s)[:, None] // SEL_BLOCK == jnp.arange(HEAD)[None, :]).astype(BF16)
    ksa = jnp.concatenate([ks, jnp.broadcast_to(onehot, ks.shape)], axis=-1)
    kwa = jnp.concatenate([kw, jnp.zeros_like(kw)], axis=-1)
    ynt = _sel_win(qnt, bias_t, ksa, jnp.swapaxes(vs, 2, 3), kwa, jnp.swapaxes(vw, 2, 3),
                   ocmp_t, tr(gn))
    yn = jnp.swapaxes(ynt, 1, 2).reshape(t, -1)

    h1, cb, s1t, s2t = _out_proj(yda, yn, gm, x2, p_da.astype(BF16),
                                 p_nsa.astype(BF16), w_o.astype(BF16), ffn_norm.reshape(1, d),
                                 wq.astype(BF16), k1.astype(BF16), k2.astype(BF16))
    idx_t, gate_t = _peer_topk(s1t, s2t)
    idx = idx_t.reshape(PEER_E, t).T
    gate = gate_t.reshape(PEER_E, t).T
    uv_tiles = jnp.concatenate([pu, pv], axis=1).astype(BF16).reshape(-1, PEER_TILE, LANES)
    as_tiles = lambda a2: a2.reshape(-1, SUBLANES, LANES)
    out = _peer_ffn(idx, gate, as_tiles(cb), as_tiles(h1), as_tiles(out_norm.reshape(1, d)),
                    uv_tiles)
    return out.reshape(b, s, d)


def kernel(x, attn_norm, w_in, da_lambda_q1, da_lambda_k1, da_lambda_q2, da_lambda_k2, da_subln,
           cmp_pe_k, cmp_pe_v, cmp_w1_k, cmp_w1_v, cmp_w2_k, cmp_w2_v, p_da, p_nsa, w_o,
           ffn_norm, peer_wq, peer_k1, peer_k2, peer_u, peer_v, final_norm):
    depth = attn_norm.shape[0]
    assert depth == 1, "the final norm is fused into the last layer's PEER kernel"
    h = x
    for l in range(depth):
        h = _layer(h, l, attn_norm[l], w_in[l], da_lambda_q1[l], da_lambda_k1[l], da_lambda_q2[l],
                   da_lambda_k2[l], da_subln[l], cmp_pe_k[l], cmp_pe_v[l], cmp_w1_k[l],
                   cmp_w1_v[l], cmp_w2_k[l], cmp_w2_v[l], p_da[l], p_nsa[l], w_o[l], ffn_norm[l],
                   peer_wq[l], peer_k1[l], peer_k2[l], peer_u[l], peer_v[l], final_norm)
    return h
```

```python
import functools
import math

import jax
import jax.numpy as jnp
from jax import lax
from jax.experimental import pallas as pl
from jax.experimental.pallas import tpu as pltpu
from jax.experimental.pallas import tpu_sc as plsc

F32 = jnp.float32
BF16 = jnp.bfloat16
I32 = jnp.int32

RMS_EPS = 1e-6
ROPE_THETA = 500000.0
ROPE_HALF = 8
HEAD = 64
DA_HEADS = 4
NSA_GROUPS = 2
NSA_HG = 4
CMP_STRIDE = 16
CMP_BLOCK = 32
SEL_BLOCK = 64
SEL_TOPK = 16
WINDOW = 512
FORCE_BONUS = 1e4
NEG_BIG = -1e30
SEL_MASK_BIAS = -32768.0
PEER_HEADS = 8
PEER_NKEYS = 128
PEER_TOPK = 16
LANES = 128
SUBLANES = 8
VMEM_LIMIT = 56 * 1024 * 1024

NT_DIMS = (((1,), (1,)), ((), ()))


def _rmsnorm(x, g):
    return x * lax.rsqrt(jnp.mean(x * x, axis=-1, keepdims=True) + RMS_EPS) * g


def _sigmoid(z):
    return 1.0 / (1.0 + jnp.exp(-z))


def _gelu(z):
    return 0.5 * z * (1.0 + lax.erf(z * (2.0 ** -0.5)))


def _params(*sem):
    return pltpu.CompilerParams(dimension_semantics=sem, vmem_limit_bytes=VMEM_LIMIT)


_QDA0, _KDA0, _VDA0, _QN0, _KVN0, _GM0, _GN0, _WCOLS = 0, 512, 1024, 1536, 2048, 2816, 4864, 5120


def _inproj_body(x_ref, g_ref, w_ref, rc_ref, rs1_ref, rs2_ref,
                 qda_ref, kda_ref, vda_ref, qn_ref, kvn_ref, gm_ref, gn_ref):
    a = _rmsnorm(x_ref[...], g_ref[...]).astype(BF16)
    rc, rs1, rs2 = rc_ref[...], rs1_ref[...], rs2_ref[...]

    def rope(z):
        return (z * rc + pltpu.roll(z, ROPE_HALF, 1) * rs1
                + pltpu.roll(z, LANES - ROPE_HALF, 1) * rs2)

    def proj(c0):
        return jnp.dot(a, w_ref[:, c0:c0 + 256], preferred_element_type=F32)

    def rope2(z):
        return jnp.concatenate([rope(z[:, :LANES]), rope(z[:, LANES:])], axis=1)

    for c in range(2):
        qda_ref[:, c * 256:(c + 1) * 256] = (rope2(proj(_QDA0 + c * 256)) * 0.125).astype(BF16)
        kda_ref[:, c * 256:(c + 1) * 256] = rope2(proj(_KDA0 + c * 256)).astype(BF16)
        vda_ref[:, c * 256:(c + 1) * 256] = proj(_VDA0 + c * 256).astype(BF16)
        qn_ref[:, c * 256:(c + 1) * 256] = (rope2(proj(_QN0 + c * 256)) * 0.125).astype(BF16)
    for c in range(3):
        z = proj(_KVN0 + c * 256)
        kvn_ref[:, c * 256:c * 256 + LANES] = rope(z[:, :LANES]).astype(BF16)
        kvn_ref[:, c * 256 + LANES:(c + 1) * 256] = z[:, LANES:].astype(BF16)
    for c in range(8):
        gm_ref[:, c * 256:(c + 1) * 256] = _sigmoid(proj(_GM0 + c * 256)).astype(BF16)
    gn_ref[...] = _sigmoid(proj(_GN0))


def _pack_w_in(w):
    d = w.shape[0]
    gn = w[:, 2816:2840]
    pad = jnp.zeros((d, LANES - 12), w.dtype)
    return jnp.concatenate([w[:, :2816], w[:, 2840:], gn[:, :12], pad, gn[:, 12:], pad],
                           axis=1).astype(BF16)


def _rope_tables(seq):
    inv = jnp.power(ROPE_THETA, -jnp.arange(ROPE_HALF, dtype=F32) * 2.0 / (2 * ROPE_HALF))
    ang = jnp.arange(seq, dtype=F32)[:, None] * inv[None, :]
    cos, sin = jnp.cos(ang), jnp.sin(ang)
    one = jnp.ones((seq, HEAD - 2 * ROPE_HALF), F32)
    zero8 = jnp.zeros((seq, ROPE_HALF), F32)
    zero48 = jnp.zeros_like(one)
    rc = jnp.concatenate([cos, cos, one], axis=1)
    rs1 = jnp.concatenate([zero8, sin, zero48], axis=1)
    rs2 = jnp.concatenate([-sin, zero8, zero48], axis=1)
    return tuple(jnp.concatenate([t, t], axis=1) for t in (rc, rs1, rs2))


def _in_proj(x2, norm_g, w_packed, rope_tabs, seq):
    t, d = x2.shape
    tm = 512
    nseq = seq // tm
    row = lambda i: (i, 0)
    fixed = lambda i: (0, 0)
    out_shapes = [
        jax.ShapeDtypeStruct((t, 512), BF16), jax.ShapeDtypeStruct((t, 512), BF16),
        jax.ShapeDtypeStruct((t, 512), BF16), jax.ShapeDtypeStruct((t, 512), BF16),
        jax.ShapeDtypeStruct((t, 768), BF16), jax.ShapeDtypeStruct((t, 2048), BF16),
        jax.ShapeDtypeStruct((t, 256), F32)]
    rope_spec = pl.BlockSpec((tm, LANES), lambda i: (i % nseq, 0))
    return pl.pallas_call(
        _inproj_body,
        grid=(t // tm,),
        in_specs=[pl.BlockSpec((tm, d), row), pl.BlockSpec((1, d), fixed),
                  pl.BlockSpec((d, _WCOLS), fixed), rope_spec, rope_spec, rope_spec],
        out_specs=[pl.BlockSpec((tm, s.shape[1]), row) for s in out_shapes],
        out_shape=out_shapes,
        compiler_params=_params("parallel"),
        name="in_proj",
    )(x2, norm_g, w_packed, *rope_tabs)


def _softmax_step(s, vt, m_ref, l_ref, acc_ref):
    m_prev = m_ref[...]
    m_new = jnp.maximum(m_prev, jnp.max(s, axis=0, keepdims=True))
    alpha = jnp.exp(m_prev - m_new)
    p = jnp.exp(s - m_new)
    l_ref[...] = alpha * l_ref[...] + jnp.sum(p, axis=0, keepdims=True)
    acc_ref[...] = alpha * acc_ref[...] + jnp.dot(vt, p.astype(BF16), preferred_element_type=F32)
    m_ref[...] = m_new


def _softmax_reset(m_ref, l_ref, acc_ref):
    m_ref[...] = jnp.full(m_ref.shape, NEG_BIG, F32)
    l_ref[...] = jnp.zeros(l_ref.shape, F32)
    acc_ref[...] = jnp.zeros(acc_ref.shape, F32)


def _attend_tiles(n_full, scores, values, mask_last, sa_ref, sb_ref, m_ref, l_ref, acc_ref):
    step = lambda s, t: _softmax_step(s, values(t), m_ref, l_ref, acc_ref)
    sa_ref[...] = scores(0)

    def two_tiles(i, carry):
        t = 2 * i
        sb_ref[...] = scores(t + 1)
        step(sa_ref[...], t)
        sa_ref[...] = scores(t + 2)
        step(sb_ref[...], t + 1)
        return carry

    lax.fori_loop(0, n_full // 2, two_tiles, 0)
    odd = (n_full & 1) == 1

    @pl.when(odd)
    def _():
        sb_ref[...] = scores(n_full)
        step(sa_ref[...], n_full - 1)
        step(mask_last(sb_ref[...]), n_full)

    @pl.when(jnp.logical_not(odd))
    def _():
        step(mask_last(sa_ref[...]), n_full)


def _key_tile(ref, kt, bk):
    return ref[(0,) * (len(ref.shape) - 2) + (pl.ds(pl.multiple_of(kt * bk, bk), bk), slice(None))]


def _value_tile(ref, kt, bk):
    return ref[(0,) * (len(ref.shape) - 2) + (slice(None), pl.ds(pl.multiple_of(kt * bk, bk), bk))]


def _diffattn_body(lambda_init, bq, q_ref, k_ref, v_ref, lq1_ref, lk1_ref, lq2_ref, lk2_ref,
                   sub_ref, y_ref, qbd_ref, m_ref, l_ref, acc_ref, sa_ref, sb_ref):
    qi = pl.program_id(2)
    bk = bq
    qt = q_ref[0]
    sub = lax.broadcasted_iota(I32, qt.shape, 0)
    zero = jnp.zeros_like(qt)
    qbd_ref[:, 0:bq] = jnp.where(sub < HEAD, qt, zero)
    qbd_ref[:, bq:2 * bq] = jnp.where(sub >= HEAD, qt, zero)
    _softmax_reset(m_ref, l_ref, acc_ref)

    def scores(kt):
        return jnp.dot(_key_tile(k_ref, kt, bk), qbd_ref[...], preferred_element_type=F32)

    def causal(s):
        r = lax.broadcasted_iota(I32, (bk, 2 * bq), 0)
        c = lax.broadcasted_iota(I32, (bk, 2 * bq), 1) & (bq - 1)
        return jnp.where(r <= c, s, NEG_BIG)

    _attend_tiles(qi, scores, lambda kt: _value_tile(v_ref, kt, bk), causal,
                  sa_ref, sb_ref, m_ref, l_ref, acc_ref)

    o = acc_ref[...] / l_ref[...]
    lam = (jnp.exp(jnp.sum(lq1_ref[...] * lk1_ref[...], axis=1, keepdims=True))
           - jnp.exp(jnp.sum(lq2_ref[...] * lk2_ref[...], axis=1, keepdims=True)) + lambda_init)
    d = o[:, 0:bq] - lam * o[:, bq:2 * bq]
    ms = jnp.mean(d * d, axis=0, keepdims=True)
    y = d * lax.rsqrt(ms + RMS_EPS) * sub_ref[...] * (1.0 - lambda_init)
    y_ref[0] = y.astype(BF16)


def _diff_attn(qdat, kda, vdat, lq1, lk1, lq2, lk2, subln_col, lambda_init):
    b, s, _ = kda.shape
    bq = 512
    vec = lambda n: pl.BlockSpec((1, n), lambda bi, h, qi: (0, 0))
    dv = 2 * HEAD
    qtile = pl.BlockSpec((1, dv, bq), lambda bi, h, qi: (bi, h, qi))
    return pl.pallas_call(
        functools.partial(_diffattn_body, lambda_init, bq),
        grid=(b, DA_HEADS, s // bq),
        in_specs=[qtile,
                  pl.BlockSpec((1, s, LANES), lambda bi, h, qi: (bi, 0, h)),
                  pl.BlockSpec((1, dv, s), lambda bi, h, qi: (bi, h, 0)),
                  vec(HEAD), vec(HEAD), vec(HEAD), vec(HEAD),
                  pl.BlockSpec((dv, 1), lambda bi, h, qi: (0, 0))],
        out_specs=qtile,
        out_shape=jax.ShapeDtypeStruct((b, DA_HEADS * dv, s), BF16),
        scratch_shapes=[pltpu.VMEM((LANES, 2 * bq), BF16), pltpu.VMEM((1, 2 * bq), F32),
                        pltpu.VMEM((1, 2 * bq), F32), pltpu.VMEM((dv, 2 * bq), F32),
                        pltpu.VMEM((bq, 2 * bq), F32), pltpu.VMEM((bq, 2 * bq), F32)],
        compiler_params=_params("parallel", "parallel", "parallel"),
        name="diff_attn",
    )(qdat, kda, vdat, lq1, lk1, lq2, lk2, subln_col)


def _compress_body(xk_ref, xv_ref, pek_ref, pev_ref, w1k_ref, w1v_ref, w2k_ref, w2v_ref,
                   kc_ref, vc_ref):
    def mlp(x_ref, pe_ref, w1_ref, w2_ref):
        blocks = (x_ref[0].astype(F32) + pe_ref[...]).astype(BF16)
        hid = _gelu(jnp.dot(blocks, w1_ref[...], preferred_element_type=F32))
        return jnp.dot(hid.astype(BF16), w2_ref[...], preferred_element_type=F32)

    kc = mlp(xk_ref, pek_ref, w1k_ref, w2k_ref)
    kc_ref[0] = jnp.concatenate([kc, jnp.zeros_like(kc)], axis=1).astype(BF16)
    vc_ref[0] = mlp(xv_ref, pev_ref, w1v_ref, w2v_ref).astype(BF16)


def _compress(xk, xv, pe_k, pe_v, w1k, w1v, w2k, w2v):
    n, ncp, width = xk.shape
    blk = pl.BlockSpec((1, ncp, width), lambda i: (i, 0, 0))
    fixed = lambda shape: pl.BlockSpec(shape, lambda i: (0, 0))
    return pl.pallas_call(
        _compress_body,
        grid=(n,),
        in_specs=[blk, blk, fixed((1, width)), fixed((1, width)), fixed((width, HEAD)),
                  fixed((width, HEAD)), fixed((HEAD, HEAD)), fixed((HEAD, HEAD))],
        out_specs=[pl.BlockSpec((1, ncp, LANES), lambda i: (i, 0, 0)),
                   pl.BlockSpec((1, ncp, HEAD), lambda i: (i, 0, 0))],
        out_shape=[jax.ShapeDtypeStruct((n, ncp, LANES), BF16),
                   jax.ShapeDtypeStruct((n, ncp, HEAD), BF16)],
        compiler_params=_params("parallel"),
        name="compress",
    )(xk, xv, pe_k, pe_v, w1k, w1v, w2k, w2v)


def _heads_on_lanes(qt, bq):
    return jnp.concatenate([qt[h * HEAD:(h + 1) * HEAD, :] for h in range(NSA_HG)], axis=1)


def _cmpsel_body(bq, q_ref, kc_ref, vc_ref, ovl_ref, ocmp_ref, bias_ref):
    qi = pl.program_id(2)
    ncp = kc_ref.shape[2]
    rows = NSA_HG * bq
    q2 = _heads_on_lanes(q_ref[0], bq)
    qz = jnp.concatenate([q2, jnp.zeros_like(q2)], axis=0)
    s = jnp.dot(kc_ref[0, 0], qz, preferred_element_type=F32)
    n = lax.broadcasted_iota(I32, (ncp, rows), 0)
    qpos = qi * bq + (lax.broadcasted_iota(I32, (ncp, rows), 1) & (bq - 1))
    cmask = n * CMP_STRIDE + (CMP_BLOCK - 1) <= qpos
    s = jnp.where(cmask, s, NEG_BIG)
    e = jnp.exp(s - jnp.max(s, axis=0, keepdims=True))
    p = jnp.where(cmask, e / jnp.sum(e, axis=0, keepdims=True), 0.0)
    o = jnp.dot(vc_ref[0, 0], p.astype(BF16), preferred_element_type=F32)
    for h in range(NSA_HG):
        ocmp_ref[0, 0, h * HEAD:(h + 1) * HEAD, :] = o[:, h * bq:(h + 1) * bq]

    psum = p[:, 0:bq] + p[:, bq:2 * bq] + p[:, 2 * bq:3 * bq] + p[:, 3 * bq:4 * bq]
    imp = jnp.dot(ovl_ref[...], psum, preferred_element_type=F32)
    blk = lax.broadcasted_iota(I32, (SEL_BLOCK, bq), 0)
    pos = qi * bq + lax.broadcasted_iota(I32, (SEL_BLOCK, bq), 1)
    cur = lax.shift_right_logical(pos, 6)
    valid = blk <= cur
    forced = (blk == 0) | (blk == cur) | (blk == cur - 1)
    score = jnp.where(valid, imp + jnp.where(forced, FORCE_BONUS, 0.0), -jnp.inf)
    rank = jnp.zeros((SEL_BLOCK, bq), I32)
    for i in range(SEL_BLOCK):
        other = score[i:i + 1, :]
        beats = (other > score) | ((other == score) & (blk > i))
        rank = rank + beats.astype(I32)
    keep = valid & (rank < SEL_TOPK)
    bias_ref[0, 0] = jnp.where(keep, 0.0, SEL_MASK_BIAS).astype(BF16)


def _cmp_select(qnt, kca, vct, overlap_t):
    b, g, ncp, _ = kca.shape
    s = qnt.shape[2]
    bq = 128
    dq = NSA_HG * HEAD
    return pl.pallas_call(
        functools.partial(_cmpsel_body, bq),
        grid=(b, g, s // bq),
        in_specs=[pl.BlockSpec((1, dq, bq), lambda bi, gi, qi: (bi, gi, qi)),
                  pl.BlockSpec((1, 1, ncp, LANES), lambda bi, gi, qi: (bi, gi, 0, 0)),
                  pl.BlockSpec((1, 1, HEAD, ncp), lambda bi, gi, qi: (bi, gi, 0, 0)),
                  pl.BlockSpec((SEL_BLOCK, ncp), lambda bi, gi, qi: (0, 0))],
        out_specs=[pl.BlockSpec((1, 1, dq, bq), lambda bi, gi, qi: (bi, gi, 0, qi)),
                   pl.BlockSpec((1, 1, SEL_BLOCK, bq), lambda bi, gi, qi: (bi, gi, 0, qi))],
        out_shape=[jax.ShapeDtypeStruct((b, g, dq, s), F32),
                   jax.ShapeDtypeStruct((b, g, SEL_BLOCK, s), BF16)],
        compiler_params=_params("parallel", "parallel", "parallel"),
        name="cmp_select",
    )(qnt, kca, vct, overlap_t)


def _selwin_body(bq, bks, q_ref, bias_ref, ks_ref, vs_ref, kw_ref, vw_ref, ocmp_ref, gate_ref,
                 y_ref, m_ref, l_ref, acc_ref, qa_ref, sa_ref, sb_ref):
    qi = pl.program_id(2)
    rows = NSA_HG * bq
    q2 = _heads_on_lanes(q_ref[0], bq)
    bias = bias_ref[0, 0]
    qa = jnp.concatenate([q2, jnp.concatenate([bias] * NSA_HG, axis=1)], axis=0)
    qw = jnp.concatenate([q2, jnp.zeros_like(q2)], axis=0)
    r = lax.broadcasted_iota(I32, (bq, rows), 0)
    c = lax.broadcasted_iota(I32, (bq, rows), 1) & (bq - 1)

    _softmax_reset(m_ref, l_ref, acc_ref)

    qa_ref[...] = qa
    last = (qi * bq) // bks

    def causal(s):
        kpos = last * bks + lax.broadcasted_iota(I32, (bks, rows), 0)
        qpos = qi * bq + (lax.broadcasted_iota(I32, (bks, rows), 1) & (bq - 1))
        return jnp.where(kpos <= qpos, s, NEG_BIG)

    _attend_tiles(last,
                  lambda kt: jnp.dot(_key_tile(ks_ref, kt, bks), qa_ref[...],
                                     preferred_element_type=F32),
                  lambda kt: _value_tile(vs_ref, kt, bks), causal,
                  sa_ref, sb_ref, m_ref, l_ref, acc_ref)
    o_sel = acc_ref[...] / l_ref[...]

    _softmax_reset(m_ref, l_ref, acc_ref)
    for back in range(WINDOW // bq, -1, -1):
        kt = qi - back

        @pl.when(kt >= 0)
        def _():
            ktc = jnp.maximum(kt, 0)
            dist = c + back * bq - r
            s = jnp.dot(_key_tile(kw_ref, ktc, bq), qw, preferred_element_type=F32)
            s = jnp.where((dist >= 0) & (dist < WINDOW), s, NEG_BIG)
            _softmax_step(s, _value_tile(vw_ref, ktc, bq), m_ref, l_ref, acc_ref)

    o_win = acc_ref[...] / l_ref[...]
    gate = gate_ref[0]
    for h in range(NSA_HG):
        sl = slice(h * bq, (h + 1) * bq)
        y = (gate[3 * h:3 * h + 1, :] * ocmp_ref[0, 0, h * HEAD:(h + 1) * HEAD, :]
             + gate[3 * h + 1:3 * h + 2, :] * o_sel[:, sl]
             + gate[3 * h + 2:3 * h + 3, :] * o_win[:, sl])
        y_ref[0, h * HEAD:(h + 1) * HEAD, :] = y.astype(BF16)


def _sel_win(qnt, bias_t, ksa, vst, kwa, vwt, ocmp_t, gnt):
    b, g, s, _ = ksa.shape
    bq, bks = 256, 512
    dq = NSA_HG * HEAD
    rows = NSA_HG * bq
    keys = pl.BlockSpec((1, 1, s, LANES), lambda bi, gi, qi: (bi, gi, 0, 0))
    vals = pl.BlockSpec((1, 1, HEAD, s), lambda bi, gi, qi: (bi, gi, 0, 0))
    qtile = pl.BlockSpec((1, dq, bq), lambda bi, gi, qi: (bi, gi, qi))
    return pl.pallas_call(
        functools.partial(_selwin_body, bq, bks),
        grid=(b, g, s // bq),
        in_specs=[qtile,
                  pl.BlockSpec((1, 1, SEL_BLOCK, bq), lambda bi, gi, qi: (bi, gi, 0, qi)),
                  keys, vals, keys, vals,
                  pl.BlockSpec((1, 1, dq, bq), lambda bi, gi, qi: (bi, gi, 0, qi)),
                  pl.BlockSpec((1, LANES, bq), lambda bi, gi, qi: (bi, gi, qi))],
        out_specs=qtile,
        out_shape=jax.ShapeDtypeStruct((b, g * dq, s), BF16),
        scratch_shapes=[pltpu.VMEM((1, rows), F32), pltpu.VMEM((1, rows), F32),
                        pltpu.VMEM((HEAD, rows), F32), pltpu.VMEM((LANES, rows), BF16),
                        pltpu.VMEM((bks, rows), F32), pltpu.VMEM((bks, rows), F32)],
        compiler_params=_params("parallel", "parallel", "parallel"),
        name="sel_win",
    )(qnt, bias_t, ksa, vst, kwa, vwt, ocmp_t, gnt)


def _outproj_body(yda_ref, yn_ref, gm_ref, x_ref, pda_ref, pnsa_ref, wo_ref, fg_ref, wq_ref,
                  k1_ref, k2_ref, h1_ref, c_ref, s1_ref, s2_ref):
    d = x_ref.shape[1]
    a = jnp.dot(yda_ref[...], pda_ref[...], preferred_element_type=F32)
    bn = jnp.dot(yn_ref[...], pnsa_ref[...], preferred_element_type=F32)
    merged = gm_ref[:, :d].astype(F32) * a + gm_ref[:, d:].astype(F32) * bn
    h1 = x_ref[...] + jnp.dot(merged.astype(BF16), wo_ref[...], preferred_element_type=F32)
    h1_ref[...] = h1
    c = _rmsnorm(h1, fg_ref[...])
    c_ref[...] = c
    cb = c.astype(BF16)
    for h in range(PEER_HEADS):
        qh = jnp.dot(cb, wq_ref[:, h * 256:(h + 1) * 256], preferred_element_type=F32).astype(BF16)
        s1_ref[h] = lax.dot_general(k1_ref[...], qh[:, :LANES], NT_DIMS, preferred_element_type=F32)
        s2_ref[h] = lax.dot_general(k2_ref[...], qh[:, LANES:], NT_DIMS, preferred_element_type=F32)


def _out_proj(yda, yn, gm, x2, pda, pnsa, wo, ffn_g, wq, k1, k2):
    t, d = x2.shape
    tm = 256
    row = lambda w: pl.BlockSpec((tm, w), lambda i: (i, 0))
    fixed = lambda a: pl.BlockSpec(a.shape, lambda i: (0, 0))
    sspec = pl.BlockSpec((PEER_HEADS, PEER_NKEYS, tm), lambda i: (0, 0, i))
    return pl.pallas_call(
        _outproj_body,
        grid=(t // tm,),
        in_specs=[row(512), row(512), row(2 * d), row(d), fixed(pda), fixed(pnsa), fixed(wo),
                  fixed(ffn_g), fixed(wq), fixed(k1), fixed(k2)],
        out_specs=[row(d), row(d), sspec, sspec],
        out_shape=[jax.ShapeDtypeStruct((t, d), F32), jax.ShapeDtypeStruct((t, d), F32),
                   jax.ShapeDtypeStruct((PEER_HEADS, PEER_NKEYS, t), F32),
                   jax.ShapeDtypeStruct((PEER_HEADS, PEER_NKEYS, t), F32)],
        compiler_params=_params("parallel"),
        name="out_proj",
    )(yda, yn, gm, x2, pda, pnsa, wo, ffn_g, wq, k1, k2)


def _batcher_pairs(n):
    pairs = []

    def merge(lo, hi, r):
        step = r * 2
        if step < hi - lo:
            merge(lo, hi, step)
            merge(lo + r, hi, step)
            pairs.extend((i, i + r) for i in range(lo + r, hi - r, step))
        else:
            pairs.append((lo, lo + r))

    def sort(lo, hi):
        if hi - lo >= 1:
            mid = lo + (hi - lo) // 2
            sort(lo, mid)
            sort(mid + 1, hi)
            merge(lo, hi, 1)

    sort(0, n - 1)
    return pairs


_NET16 = _batcher_pairs(PEER_TOPK)


def _cmpx(a, b):
    c = a[0] >= b[0]
    return ((jnp.where(c, a[0], b[0]), jnp.where(c, a[1], b[1])),
            (jnp.where(c, b[0], a[0]), jnp.where(c, b[1], a[1])))


def _sort_lists(lists, n_real):
    lists = list(lists)
    for i, j in _NET16:
        if j < n_real:
            lists[i], lists[j] = _cmpx(lists[i], lists[j])
    return lists


def _merge_sublanes(lists):
    k = PEER_TOPK
    for dist in (4, 2, 1):
        other = [(pltpu.roll(v, dist, 0), pltpu.roll(ix, dist, 0)) for v, ix in lists]
        lists = [_cmpx(lists[i], other[k - 1 - i])[0] for i in range(k)]
        step = k // 2
        while step >= 1:
            for i in range(k):
                if i & step == 0:
                    lists[i], lists[i + step] = _cmpx(lists[i], lists[i + step])
            step //= 2
    return lists


def _spread(lists, off, sub):
    v, ix = lists[off]
    for r in range(1, SUBLANES):
        sel = sub == r
        v = jnp.where(sel, lists[off + r][0], v)
        ix = jnp.where(sel, lists[off + r][1], ix)
    return v, ix


def _peertopk_body(tt, s1_ref, s2_ref, idx_ref, gate_ref):
    sub = lax.broadcasted_iota(I32, (SUBLANES, LANES), 0)
    groups = PEER_NKEYS // SUBLANES

    def top16(ref, h, lanes):
        lists = [(ref[h, v * SUBLANES:(v + 1) * SUBLANES, lanes], sub + v * SUBLANES)
                 for v in range(groups)]
        return _merge_sublanes(_sort_lists(lists, groups))

    def unit(u, carry):
        h = u // (tt // LANES)
        lanes = pl.ds(pl.multiple_of((u % (tt // LANES)) * LANES, LANES), LANES)
        l1 = top16(s1_ref, h, lanes)
        l2 = top16(s2_ref, h, lanes)
        v2lo, v2hi, v1hi = _spread(l2, 0, sub), _spread(l2, SUBLANES, sub), _spread(l1, SUBLANES, sub)
        cands = [(l1[a][0] + v2lo[0], l1[a][1] * PEER_NKEYS + v2lo[1]) for a in range(SUBLANES)]
        cands.append((l1[0][0] + v2hi[0], l1[0][1] * PEER_NKEYS + v2hi[1]))
        cands.append((v1hi[0] + l2[0][0], v1hi[1] * PEER_NKEYS + l2[0][1]))
        n_real = len(cands)
        filler = (jnp.full((SUBLANES, LANES), -jnp.inf, F32), jnp.zeros((SUBLANES, LANES), I32))
        cands += [filler] * (PEER_TOPK - n_real)
        best = _merge_sublanes(_sort_lists(cands, n_real))
        ex = [jnp.exp(v - best[0][0]) for v, _ in best]
        z = ex[0]
        for e in ex[1:]:
            z = z + e
        gl = [(e / z, ix) for e, (_, ix) in zip(ex, best)]
        lo, hi = _spread(gl, 0, sub), _spread(gl, SUBLANES, sub)
        gate_ref[h, :, lanes] = jnp.concatenate([lo[0], hi[0]], axis=0)
        idx_ref[h, :, lanes] = jnp.concatenate([lo[1], hi[1]], axis=0)
        return carry

    lax.fori_loop(0, PEER_HEADS * (tt // LANES), unit, 0)


def _peer_topk(s1t, s2t):
    _, _, t = s1t.shape
    tt = 512
    spec_in = pl.BlockSpec((PEER_HEADS, PEER_NKEYS, tt), lambda i: (0, 0, i))
    spec_out = pl.BlockSpec((PEER_HEADS, PEER_TOPK, tt), lambda i: (0, 0, i))
    return pl.pallas_call(
        functools.partial(_peertopk_body, tt),
        grid=(t // tt,),
        in_specs=[spec_in, spec_in],
        out_specs=[spec_out, spec_out],
        out_shape=[jax.ShapeDtypeStruct((PEER_HEADS, PEER_TOPK, t), I32),
                   jax.ShapeDtypeStruct((PEER_HEADS, PEER_TOPK, t), F32)],
        compiler_params=_params("parallel"),
        name="peer_topk",
    )(s1t, s2t)


PEER_E = PEER_HEADS * PEER_TOPK
PEER_RING = 3
PEER_MID_ROWS = 48
PEER_SC_SHARE = (1, 4)


def _pack_expert_rows(pu, pv):
    bits = lambda a: lax.bitcast_convert_type(a.astype(BF16), jnp.uint16).astype(jnp.uint32)
    words = (bits(pu) << 16) | bits(pv)
    return lax.bitcast_convert_type(words, I32).reshape(-1, SUBLANES, LANES)


def _word_hi(w):
    return lax.bitcast_convert_type(w & jnp.int32(-65536), F32)


def _word_lo(w):
    return lax.bitcast_convert_type(lax.shift_left(w, jnp.int32(16)), F32)


def _sublane_sums(a, sub):
    for dist in (4, 2, 1):
        low = (sub & dist) == 0
        half = len(a) // 2
        a = [jnp.where(low, a[i], pltpu.roll(a[i + half], dist, 0))
             + jnp.where(low, pltpu.roll(a[i], SUBLANES - dist, 0), a[i + half])
             for i in range(half)]
    return a[0]


def _peer_pair_math(expert_u, expert_v, x8, gates, store, issue_some):
    sub = lax.broadcasted_iota(I32, (SUBLANES, LANES), 0)
    eye = (lax.broadcasted_iota(I32, (PEER_E, LANES), 0)
           == lax.broadcasted_iota(I32, (PEER_E, LANES), 1))
    ones_rows = jnp.ones((SUBLANES, LANES), BF16)
    ones_sq = jnp.ones((LANES, LANES), BF16)
    ngroup = PEER_E // SUBLANES

    def hidden(a):
        groups = []
        for g in range(ngroup):
            prods = [expert_u(a, g * SUBLANES + r) * x8[a] for r in range(SUBLANES)]
            groups.append(_sublane_sums(prods, sub))
            issue_some(3 - g % 2)
        return jnp.concatenate(groups, axis=0)

    def expert_weights(a, q):
        q_hi = q.astype(BF16)
        q_lo = (q - q_hi.astype(F32)).astype(BF16)
        hid = (lax.dot_general(ones_rows, q_hi, NT_DIMS, preferred_element_type=F32)
               + lax.dot_general(ones_rows, q_lo, NT_DIMS, preferred_element_type=F32))
        issue_some(PEER_MID_ROWS // 2)
        w = _gelu(hid[0:1]) * gates[a]
        wd = jnp.where(eye, jnp.broadcast_to(w, (PEER_E, LANES)), 0.0).astype(BF16)
        wcol = jnp.dot(wd, ones_sq, preferred_element_type=F32)
        issue_some(PEER_MID_ROWS // 2)
        return wcol

    def combine(a, wcol):
        out = jnp.zeros((SUBLANES, LANES), F32)
        for g in range(ngroup):
            for r in range(SUBLANES):
                j = g * SUBLANES + r
                out = out + wcol[j:j + 1, :] * expert_v(a, j)
            issue_some(3 - g % 2)
        store(a, out)

    wcol0 = expert_weights(0, hidden(0))
    wcol1 = expert_weights(1, hidden(1))
    combine(0, wcol0)
    combine(1, wcol1)


def _peer_finish(h1_ref, acc_ref, fg_ref, out_ref):
    hsum = h1_ref[...] + acc_ref[...]
    ms = jnp.mean(hsum * hsum, axis=(1, 2), keepdims=True)
    out_ref[...] = hsum * lax.rsqrt(ms + RMS_EPS) * fg_ref[...]


def _peerffn_body(tb, idx_ref, gate_ref, c_ref, h1_ref, fg_ref, uv_ref, out_ref, buf_ref,
                  acc_ref, sem_ref):
    def row_copy(t, j, slot):
        return pltpu.make_async_copy(uv_ref.at[idx_ref[t, j]], buf_ref.at[slot, j],
                                     sem_ref.at[slot])

    def issue(t, slot, j0, j1):
        for j in range(j0, j1):
            row_copy(t, j, slot).start(priority=j % 2)

    def wait_all(slot):
        pltpu.make_async_copy(uv_ref.at[pl.ds(0, PEER_E)], buf_ref.at[slot],
                              sem_ref.at[slot]).wait()

    def ring(i):
        base = 2 * (i % PEER_RING)
        return (base, base + 1)

    def pair(i, prefetch):
        ahead = PEER_RING - 1
        toks = (2 * i, 2 * i + 1)
        slots = ring(i)
        nslots = ring(i + ahead)
        todo = [(a, j) for a in range(2) for j in range(PEER_E)]

        def issue_some(n):
            for a, j in todo[:n]:
                if prefetch:
                    row_copy(toks[a] + 2 * ahead, j, nslots[a]).start(priority=j % 2)
            del todo[:n]

        wait_all(slots[0])
        wait_all(slots[1])

        def store(a, out):
            acc_ref[toks[a]] = out

        _peer_pair_math(
            lambda a, j: _word_hi(buf_ref[slots[a], j]),
            lambda a, j: _word_lo(buf_ref[slots[a], j]),
            [c_ref[t] for t in toks], [gate_ref[pl.ds(t, 1), :] for t in toks], store, issue_some)
        assert not todo, "every prefetch row DMA must be issued exactly once"

    npairs = tb // 2
    for i in range(PEER_RING - 1):
        for a, slot in enumerate(ring(i)):
            issue(2 * i + a, slot, 0, PEER_E)

    def body(i, carry):
        pair(i, True)
        return carry

    lax.fori_loop(0, npairs - (PEER_RING - 1), body, 0)
    for i in range(npairs - (PEER_RING - 1), npairs):
        pair(i, False)
    _peer_finish(h1_ref, acc_ref, fg_ref, out_ref)


def _peerstaged_body(tb, gate_ref, c_ref, h1_ref, fg_ref, rows_ref, out_ref, acc_ref):
    def pair(i):
        toks = (2 * i, 2 * i + 1)

        def store(a, out):
            acc_ref[toks[a]] = out

        _peer_pair_math(
            lambda a, j: _word_hi(rows_ref[toks[a] * PEER_E + j]),
            lambda a, j: _word_lo(rows_ref[toks[a] * PEER_E + j]),
            [c_ref[t] for t in toks], [gate_ref[pl.ds(t, 1), :] for t in toks], store,
            lambda n: None)

    def two_pairs(i, carry):
        pair(2 * i)
        pair(2 * i + 1)
        return carry

    lax.fori_loop(0, tb // 4, two_pairs, 0)
    _peer_finish(h1_ref, acc_ref, fg_ref, out_ref)


def _peer_ffn_staged(gate, c3, h13, final_g3, rows):
    t = h13.shape[0]
    tb = 8
    row3 = pl.BlockSpec((tb, SUBLANES, LANES), lambda i: (i, 0, 0))
    return pl.pallas_call(
        functools.partial(_peerstaged_body, tb),
        grid=(t // tb,),
        in_specs=[pl.BlockSpec((tb, PEER_E), lambda i: (i, 0)), row3, row3,
                  pl.BlockSpec((1, SUBLANES, LANES), lambda i: (0, 0, 0)),
                  pl.BlockSpec((tb * PEER_E, SUBLANES, LANES), lambda i: (i, 0, 0))],
        out_specs=row3,
        out_shape=jax.ShapeDtypeStruct((t, SUBLANES, LANES), F32),
        scratch_shapes=[pltpu.VMEM((tb, SUBLANES, LANES), F32)],
        compiler_params=_params("arbitrary"),
        name="peer_ffn_staged",
    )(gate, c3, h13, final_g3, rows)


SC_CORES, SC_SUBCORES = 2, 16
SC_CHUNK = 64


def _sc_gather(uv_tiles, flat_idx):
    n = flat_idx.shape[0]
    workers = SC_CORES * SC_SUBCORES
    per_worker = n // workers
    assert n % (workers * SC_CHUNK) == 0
    mesh = plsc.VectorSubcoreMesh(core_axis_name="c", subcore_axis_name="s")

    @functools.partial(
        pl.kernel, mesh=mesh,
        out_type=jax.ShapeDtypeStruct((n, SUBLANES, LANES), I32),
        scratch_types=[pltpu.VMEM((SC_CHUNK,), I32),
                       pltpu.VMEM((SC_CHUNK, SUBLANES, LANES), I32),
                       pltpu.SemaphoreType.DMA])
    def gather(table_hbm, idx_hbm, out_hbm, idx_v, rows_v, sem):
        wid = lax.axis_index("s") * SC_CORES + lax.axis_index("c")
        base = wid * per_worker

        @pl.loop(0, per_worker // SC_CHUNK)
        def _(ci):
            off = pl.multiple_of(base + ci * SC_CHUNK, SC_CHUNK)
            pltpu.sync_copy(idx_hbm.at[pl.ds(off, SC_CHUNK)], idx_v)
            pltpu.async_copy(table_hbm.at[idx_v], rows_v, sem).wait()
            pltpu.sync_copy(rows_v, out_hbm.at[pl.ds(off, SC_CHUNK)])

    return gather(uv_tiles, flat_idx)


def _peer_ffn(idx, gate, c3, h13, final_g3, uv_tiles):
    t = h13.shape[0]
    tb = 128
    row3 = pl.BlockSpec((tb, SUBLANES, LANES), lambda i: (i, 0, 0))
    return pl.pallas_call(
        functools.partial(_peerffn_body, tb),
        grid=(t // tb,),
        in_specs=[pl.BlockSpec((tb, PEER_E), lambda i: (i, 0), memory_space=pltpu.SMEM),
                  pl.BlockSpec((tb, PEER_E), lambda i: (i, 0)), row3, row3,
                  pl.BlockSpec((1, SUBLANES, LANES), lambda i: (0, 0, 0)),
                  pl.BlockSpec(memory_space=pl.ANY)],
        out_specs=row3,
        out_shape=jax.ShapeDtypeStruct((t, SUBLANES, LANES), F32),
        scratch_shapes=[pltpu.VMEM((2 * PEER_RING, PEER_E, SUBLANES, LANES), I32),
                        pltpu.VMEM((tb, SUBLANES, LANES), F32),
                        pltpu.SemaphoreType.DMA((2 * PEER_RING,))],
        compiler_params=_params("arbitrary"),
        name="peer_ffn",
    )(idx, gate, c3, h13, final_g3, uv_tiles)


def _overlap_table(seq):
    ci = jnp.arange(seq // CMP_STRIDE)[None, :] * CMP_STRIDE
    sj = jnp.arange(SEL_BLOCK)[:, None] * SEL_BLOCK
    return ((ci < sj + SEL_BLOCK) & (ci + CMP_BLOCK > sj)).astype(F32)


def _cmp_blocks(kv):
    b, g, s, dh = kv.shape
    r = kv.reshape(b * g, s // CMP_STRIDE, CMP_STRIDE * dh)
    return jnp.concatenate([r, jnp.roll(r, -1, axis=1)], axis=-1)


def _layer(h, lidx, attn_norm, w_in, lq1, lk1, lq2, lk2, subln, pe_k, pe_v, w1k, w1v, w2k, w2v,
           p_da, p_nsa, w_o, ffn_norm, wq, k1, k2, pu, pv, out_norm):
    b, s, d = h.shape
    t = b * s
    g, hg = NSA_GROUPS, NSA_HG
    lambda_init = 0.8 - 0.6 * math.exp(-0.3 * lidx)
    x2 = h.reshape(t, d)

    qda, kda, vda, qn, kvn, gm, gn = _in_proj(x2, attn_norm.reshape(1, d), _pack_w_in(w_in),
                                              _rope_tables(s), s)
    tr = lambda a2: jnp.swapaxes(a2.reshape(b, s, -1), 1, 2)
    ydat = _diff_attn(tr(qda), kda.reshape(b, s, -1), tr(vda),
                      lq1.reshape(1, -1), lk1.reshape(1, -1), lq2.reshape(1, -1),
                      lk2.reshape(1, -1), subln.reshape(-1, 1), lambda_init)
    yda = jnp.swapaxes(ydat, 1, 2).reshape(t, -1)

    assert s // SEL_BLOCK <= SEL_BLOCK, "selection bias rows hold at most 64 blocks"
    kv6 = kvn.reshape(b, s, 6, g, HEAD).transpose(2, 0, 3, 1, 4)
    kc_raw, vc_raw, ks, vs, kw, vw = (kv6[i] for i in range(6))
    kca, vc = _compress(_cmp_blocks(kc_raw), _cmp_blocks(vc_raw),
                        pe_k.reshape(1, -1), pe_v.reshape(1, -1),
                        w1k.astype(BF16), w1v.astype(BF16), w2k.astype(BF16), w2v.astype(BF16))
    ncp = s // CMP_STRIDE
    kca = kca.reshape(b, g, ncp, LANES)
    vct = jnp.swapaxes(vc.reshape(b, g, ncp, HEAD), 2, 3)

    qnt = tr(qn)
    ocmp_t, bias_t = _cmp_select(qnt, kca, vct, _overlap_table(s))
    onehot = (jnp.arange(s)[:, None] // SEL_BLOCK == jnp.arange(HEAD)[None, :]).astype(BF16)
    ksa = jnp.concatenate([ks, jnp.broadcast_to(onehot, ks.shape)], axis=-1)
    kwa = jnp.concatenate([kw, jnp.zeros_like(kw)], axis=-1)
    ynt = _sel_win(qnt, bias_t, ksa, jnp.swapaxes(vs, 2, 3), kwa, jnp.swapaxes(vw, 2, 3),
                   ocmp_t, tr(gn))
    yn = jnp.swapaxes(ynt, 1, 2).reshape(t, -1)

    h1, cb, s1t, s2t = _out_proj(yda, yn, gm, x2, p_da.astype(BF16),
                                 p_nsa.astype(BF16), w_o.astype(BF16), ffn_norm.reshape(1, d),
                                 wq.astype(BF16), k1.astype(BF16), k2.astype(BF16))
    idx_t, gate_t = _peer_topk(s1t, s2t)
    idx = idx_t.reshape(PEER_E, t).T
    gate = gate_t.reshape(PEER_E, t).T
    uv_tiles = _pack_expert_rows(pu, pv)
    as_tiles = lambda a2: a2.reshape(-1, SUBLANES, LANES)
    c3, h13, fg3 = as_tiles(cb), as_tiles(h1), as_tiles(out_norm.reshape(1, d))
    t_sc = (t * PEER_SC_SHARE[0] // PEER_SC_SHARE[1]) // LANES * LANES
    t_tc = t - t_sc
    rows = _sc_gather(uv_tiles, idx[t_tc:].reshape(-1))
    out_tc = _peer_ffn(idx[:t_tc], gate[:t_tc], c3[:t_tc], h13[:t_tc], fg3, uv_tiles)
    out_sc = _peer_ffn_staged(gate[t_tc:], c3[t_tc:], h13[t_tc:], fg3, rows)
    return jnp.concatenate([out_tc, out_sc], axis=0).reshape(b, s, d)


def kernel(x, attn_norm, w_in, da_lambda_q1, da_lambda_k1, da_lambda_q2, da_lambda_k2, da_subln,
           cmp_pe_k, cmp_pe_v, cmp_w1_k, cmp_w1_v, cmp_w2_k, cmp_w2_v, p_da, p_nsa, w_o,
           ffn_norm, peer_wq, peer_k1, peer_k2, peer_u, peer_v, final_norm):
    depth = attn_norm.shape[0]
    assert depth == 1, "the final norm is fused into the last layer's PEER kernel"
    h = x
    for l in range(depth):
        h = _layer(h, l, attn_norm[l], w_in[l], da_lambda_q1[l], da_lambda_k1[l], da_lambda_q2[l],
                   da_lambda_k2[l], da_subln[l], cmp_pe_k[l], cmp_pe_v[l], cmp_w1_k[l],
                   cmp_w1_v[l], cmp_w2_k[l], cmp_w2_v[l], p_da[l], p_nsa[l], w_o[l], ffn_norm[l],
                   peer_wq[l], peer_k1[l], peer_k2[l], peer_u[l], peer_v[l], final_norm)
    return h
```

```python
import functools
import math

import jax
import jax.numpy as jnp
from jax import lax
from jax.experimental import pallas as pl
from jax.experimental.pallas import tpu as pltpu

F32 = jnp.float32
BF16 = jnp.bfloat16
I32 = jnp.int32

RMS_EPS = 1e-6
ROPE_THETA = 500000.0
ROPE_HALF = 8
HEAD = 64
DA_HEADS = 4
NSA_GROUPS = 2
NSA_HG = 4
CMP_STRIDE = 16
CMP_BLOCK = 32
SEL_BLOCK = 64
SEL_TOPK = 16
WINDOW = 512
FORCE_BONUS = 1e4
NEG_BIG = -1e30
SEL_MASK_BIAS = -32768.0
PEER_HEADS = 8
PEER_NKEYS = 128
PEER_TOPK = 16
LANES = 128
SUBLANES = 8
VMEM_LIMIT = 56 * 1024 * 1024

NT_DIMS = (((1,), (1,)), ((), ()))


def _rmsnorm(x, g):
    return x * lax.rsqrt(jnp.mean(x * x, axis=-1, keepdims=True) + RMS_EPS) * g


def _sigmoid(z):
    return 1.0 / (1.0 + jnp.exp(-z))


def _gelu(z):
    return 0.5 * z * (1.0 + lax.erf(z * (2.0 ** -0.5)))


def _params(*sem):
    return pltpu.CompilerParams(dimension_semantics=sem, vmem_limit_bytes=VMEM_LIMIT)


_QDA0, _KDA0, _VDA0, _QN0, _KVN0, _GM0, _GN0, _WCOLS = 0, 512, 1024, 1536, 2048, 2816, 4864, 5120


def _inproj_body(x_ref, g_ref, w_ref, rc_ref, rs1_ref, rs2_ref,
                 qda_ref, kda_ref, vda_ref, qn_ref, kvn_ref, gm_ref, gn_ref):
    a = _rmsnorm(x_ref[...], g_ref[...]).astype(BF16)
    rc, rs1, rs2 = rc_ref[...], rs1_ref[...], rs2_ref[...]

    def rope(z):
        return (z * rc + pltpu.roll(z, ROPE_HALF, 1) * rs1
                + pltpu.roll(z, LANES - ROPE_HALF, 1) * rs2)

    def proj(c0):
        return jnp.dot(a, w_ref[:, c0:c0 + 256], preferred_element_type=F32)

    def rope2(z):
        return jnp.concatenate([rope(z[:, :LANES]), rope(z[:, LANES:])], axis=1)

    for c in range(2):
        qda_ref[:, c * 256:(c + 1) * 256] = (rope2(proj(_QDA0 + c * 256)) * 0.125).astype(BF16)
        kda_ref[:, c * 256:(c + 1) * 256] = rope2(proj(_KDA0 + c * 256)).astype(BF16)
        vda_ref[:, c * 256:(c + 1) * 256] = proj(_VDA0 + c * 256).astype(BF16)
        qn_ref[:, c * 256:(c + 1) * 256] = (rope2(proj(_QN0 + c * 256)) * 0.125).astype(BF16)
    for c in range(3):
        z = proj(_KVN0 + c * 256)
        kvn_ref[:, c * 256:c * 256 + LANES] = rope(z[:, :LANES]).astype(BF16)
        kvn_ref[:, c * 256 + LANES:(c + 1) * 256] = z[:, LANES:].astype(BF16)
    for c in range(8):
        gm_ref[:, c * 256:(c + 1) * 256] = _sigmoid(proj(_GM0 + c * 256)).astype(BF16)
    gn_ref[...] = _sigmoid(proj(_GN0))


def _pack_w_in(w):
    d = w.shape[0]
    gn = w[:, 2816:2840]
    pad = jnp.zeros((d, LANES - 12), w.dtype)
    return jnp.concatenate([w[:, :2816], w[:, 2840:], gn[:, :12], pad, gn[:, 12:], pad],
                           axis=1).astype(BF16)


def _rope_tables(seq):
    inv = jnp.power(ROPE_THETA, -jnp.arange(ROPE_HALF, dtype=F32) * 2.0 / (2 * ROPE_HALF))
    ang = jnp.arange(seq, dtype=F32)[:, None] * inv[None, :]
    cos, sin = jnp.cos(ang), jnp.sin(ang)
    one = jnp.ones((seq, HEAD - 2 * ROPE_HALF), F32)
    zero8 = jnp.zeros((seq, ROPE_HALF), F32)
    zero48 = jnp.zeros_like(one)
    rc = jnp.concatenate([cos, cos, one], axis=1)
    rs1 = jnp.concatenate([zero8, sin, zero48], axis=1)
    rs2 = jnp.concatenate([-sin, zero8, zero48], axis=1)
    return tuple(jnp.concatenate([t, t], axis=1) for t in (rc, rs1, rs2))


def _in_proj(x2, norm_g, w_packed, rope_tabs, seq):
    t, d = x2.shape
    tm = 512
    nseq = seq // tm
    row = lambda i: (i, 0)
    fixed = lambda i: (0, 0)
    out_shapes = [
        jax.ShapeDtypeStruct((t, 512), BF16), jax.ShapeDtypeStruct((t, 512), BF16),
        jax.ShapeDtypeStruct((t, 512), BF16), jax.ShapeDtypeStruct((t, 512), BF16),
        jax.ShapeDtypeStruct((t, 768), BF16), jax.ShapeDtypeStruct((t, 2048), BF16),
        jax.ShapeDtypeStruct((t, 256), F32)]
    rope_spec = pl.BlockSpec((tm, LANES), lambda i: (i % nseq, 0))
    return pl.pallas_call(
        _inproj_body,
        grid=(t // tm,),
        in_specs=[pl.BlockSpec((tm, d), row), pl.BlockSpec((1, d), fixed),
                  pl.BlockSpec((d, _WCOLS), fixed), rope_spec, rope_spec, rope_spec],
        out_specs=[pl.BlockSpec((tm, s.shape[1]), row) for s in out_shapes],
        out_shape=out_shapes,
        compiler_params=_params("parallel"),
        name="in_proj",
    )(x2, norm_g, w_packed, *rope_tabs)


def _softmax_step(s, vt, m_ref, l_ref, acc_ref):
    m_prev = m_ref[...]
    m_new = jnp.maximum(m_prev, jnp.max(s, axis=0, keepdims=True))
    alpha = jnp.exp(m_prev - m_new)
    p = jnp.exp(s - m_new)
    l_ref[...] = alpha * l_ref[...] + jnp.sum(p, axis=0, keepdims=True)
    acc_ref[...] = alpha * acc_ref[...] + jnp.dot(vt, p.astype(BF16), preferred_element_type=F32)
    m_ref[...] = m_new


def _softmax_reset(m_ref, l_ref, acc_ref):
    m_ref[...] = jnp.full(m_ref.shape, NEG_BIG, F32)
    l_ref[...] = jnp.zeros(l_ref.shape, F32)
    acc_ref[...] = jnp.zeros(acc_ref.shape, F32)


def _attend_tiles(n_full, scores, values, mask_last, sa_ref, sb_ref, m_ref, l_ref, acc_ref):
    step = lambda s, t: _softmax_step(s, values(t), m_ref, l_ref, acc_ref)
    sa_ref[...] = scores(0)

    def two_tiles(i, carry):
        t = 2 * i
        sb_ref[...] = scores(t + 1)
        step(sa_ref[...], t)
        sa_ref[...] = scores(t + 2)
        step(sb_ref[...], t + 1)
        return carry

    lax.fori_loop(0, n_full // 2, two_tiles, 0)
    odd = (n_full & 1) == 1

    @pl.when(odd)
    def _():
        sb_ref[...] = scores(n_full)
        step(sa_ref[...], n_full - 1)
        step(mask_last(sb_ref[...]), n_full)

    @pl.when(jnp.logical_not(odd))
    def _():
        step(mask_last(sa_ref[...]), n_full)


def _key_tile(ref, kt, bk):
    return ref[(0,) * (len(ref.shape) - 2) + (pl.ds(pl.multiple_of(kt * bk, bk), bk), slice(None))]


def _value_tile(ref, kt, bk):
    return ref[(0,) * (len(ref.shape) - 2) + (slice(None), pl.ds(pl.multiple_of(kt * bk, bk), bk))]


def _diffattn_body(lambda_init, bq, q_ref, k_ref, v_ref, lq1_ref, lk1_ref, lq2_ref, lk2_ref,
                   sub_ref, y_ref, qbd_ref, m_ref, l_ref, acc_ref, sa_ref, sb_ref):
    qi = pl.program_id(2)
    bk = bq
    qt = q_ref[0]
    sub = lax.broadcasted_iota(I32, qt.shape, 0)
    zero = jnp.zeros_like(qt)
    qbd_ref[:, 0:bq] = jnp.where(sub < HEAD, qt, zero)
    qbd_ref[:, bq:2 * bq] = jnp.where(sub >= HEAD, qt, zero)
    _softmax_reset(m_ref, l_ref, acc_ref)

    def scores(kt):
        return jnp.dot(_key_tile(k_ref, kt, bk), qbd_ref[...], preferred_element_type=F32)

    def causal(s):
        r = lax.broadcasted_iota(I32, (bk, 2 * bq), 0)
        c = lax.broadcasted_iota(I32, (bk, 2 * bq), 1) & (bq - 1)
        return jnp.where(r <= c, s, NEG_BIG)

    _attend_tiles(qi, scores, lambda kt: _value_tile(v_ref, kt, bk), causal,
                  sa_ref, sb_ref, m_ref, l_ref, acc_ref)

    o = acc_ref[...] / l_ref[...]
    lam = (jnp.exp(jnp.sum(lq1_ref[...] * lk1_ref[...], axis=1, keepdims=True))
           - jnp.exp(jnp.sum(lq2_ref[...] * lk2_ref[...], axis=1, keepdims=True)) + lambda_init)
    d = o[:, 0:bq] - lam * o[:, bq:2 * bq]
    ms = jnp.mean(d * d, axis=0, keepdims=True)
    y = d * lax.rsqrt(ms + RMS_EPS) * sub_ref[...] * (1.0 - lambda_init)
    y_ref[0] = y.astype(BF16)


def _diff_attn(qdat, kda, vdat, lq1, lk1, lq2, lk2, subln_col, lambda_init):
    b, s, _ = kda.shape
    bq = 512
    vec = lambda n: pl.BlockSpec((1, n), lambda bi, h, qi: (0, 0))
    dv = 2 * HEAD
    qtile = pl.BlockSpec((1, dv, bq), lambda bi, h, qi: (bi, h, qi))
    return pl.pallas_call(
        functools.partial(_diffattn_body, lambda_init, bq),
        grid=(b, DA_HEADS, s // bq),
        in_specs=[qtile,
                  pl.BlockSpec((1, s, LANES), lambda bi, h, qi: (bi, 0, h)),
                  pl.BlockSpec((1, dv, s), lambda bi, h, qi: (bi, h, 0)),
                  vec(HEAD), vec(HEAD), vec(HEAD), vec(HEAD),
                  pl.BlockSpec((dv, 1), lambda bi, h, qi: (0, 0))],
        out_specs=qtile,
        out_shape=jax.ShapeDtypeStruct((b, DA_HEADS * dv, s), BF16),
        scratch_shapes=[pltpu.VMEM((LANES, 2 * bq), BF16), pltpu.VMEM((1, 2 * bq), F32),
                        pltpu.VMEM((1, 2 * bq), F32), pltpu.VMEM((dv, 2 * bq), F32),
                        pltpu.VMEM((bq, 2 * bq), F32), pltpu.VMEM((bq, 2 * bq), F32)],
        compiler_params=_params("parallel", "parallel", "parallel"),
        name="diff_attn",
    )(qdat, kda, vdat, lq1, lk1, lq2, lk2, subln_col)


def _compress_body(xk_ref, xv_ref, pek_ref, pev_ref, w1k_ref, w1v_ref, w2k_ref, w2v_ref,
                   kc_ref, vc_ref):
    def mlp(x_ref, pe_ref, w1_ref, w2_ref):
        blocks = (x_ref[0].astype(F32) + pe_ref[...]).astype(BF16)
        hid = _gelu(jnp.dot(blocks, w1_ref[...], preferred_element_type=F32))
        return jnp.dot(hid.astype(BF16), w2_ref[...], preferred_element_type=F32)

    kc = mlp(xk_ref, pek_ref, w1k_ref, w2k_ref)
    kc_ref[0] = jnp.concatenate([kc, jnp.zeros_like(kc)], axis=1).astype(BF16)
    vc_ref[0] = mlp(xv_ref, pev_ref, w1v_ref, w2v_ref).astype(BF16)


def _compress(xk, xv, pe_k, pe_v, w1k, w1v, w2k, w2v):
    n, ncp, width = xk.shape
    blk = pl.BlockSpec((1, ncp, width), lambda i: (i, 0, 0))
    fixed = lambda shape: pl.BlockSpec(shape, lambda i: (0, 0))
    return pl.pallas_call(
        _compress_body,
        grid=(n,),
        in_specs=[blk, blk, fixed((1, width)), fixed((1, width)), fixed((width, HEAD)),
                  fixed((width, HEAD)), fixed((HEAD, HEAD)), fixed((HEAD, HEAD))],
        out_specs=[pl.BlockSpec((1, ncp, LANES), lambda i: (i, 0, 0)),
                   pl.BlockSpec((1, ncp, HEAD), lambda i: (i, 0, 0))],
        out_shape=[jax.ShapeDtypeStruct((n, ncp, LANES), BF16),
                   jax.ShapeDtypeStruct((n, ncp, HEAD), BF16)],
        compiler_params=_params("parallel"),
        name="compress",
    )(xk, xv, pe_k, pe_v, w1k, w1v, w2k, w2v)


def _heads_on_lanes(qt, bq):
    return jnp.concatenate([qt[h * HEAD:(h + 1) * HEAD, :] for h in range(NSA_HG)], axis=1)


def _cmpsel_body(bq, q_ref, kc_ref, vc_ref, ovl_ref, ocmp_ref, bias_ref):
    qi = pl.program_id(2)
    ncp = kc_ref.shape[2]
    rows = NSA_HG * bq
    q2 = _heads_on_lanes(q_ref[0], bq)
    qz = jnp.concatenate([q2, jnp.zeros_like(q2)], axis=0)
    s = jnp.dot(kc_ref[0, 0], qz, preferred_element_type=F32)
    n = lax.broadcasted_iota(I32, (ncp, rows), 0)
    qpos = qi * bq + (lax.broadcasted_iota(I32, (ncp, rows), 1) & (bq - 1))
    cmask = n * CMP_STRIDE + (CMP_BLOCK - 1) <= qpos
    s = jnp.where(cmask, s, NEG_BIG)
    e = jnp.exp(s - jnp.max(s, axis=0, keepdims=True))
    p = jnp.where(cmask, e / jnp.sum(e, axis=0, keepdims=True), 0.0)
    o = jnp.dot(vc_ref[0, 0], p.astype(BF16), preferred_element_type=F32)
    for h in range(NSA_HG):
        ocmp_ref[0, 0, h * HEAD:(h + 1) * HEAD, :] = o[:, h * bq:(h + 1) * bq]

    psum = p[:, 0:bq] + p[:, bq:2 * bq] + p[:, 2 * bq:3 * bq] + p[:, 3 * bq:4 * bq]
    imp = jnp.dot(ovl_ref[...], psum, preferred_element_type=F32)
    blk = lax.broadcasted_iota(I32, (SEL_BLOCK, bq), 0)
    pos = qi * bq + lax.broadcasted_iota(I32, (SEL_BLOCK, bq), 1)
    cur = lax.shift_right_logical(pos, 6)
    valid = blk <= cur
    forced = (blk == 0) | (blk == cur) | (blk == cur - 1)
    score = jnp.where(valid, imp + jnp.where(forced, FORCE_BONUS, 0.0), -jnp.inf)
    rank = jnp.zeros((SEL_BLOCK, bq), I32)
    for i in range(SEL_BLOCK):
        other = score[i:i + 1, :]
        beats = (other > score) | ((other == score) & (blk > i))
        rank = rank + beats.astype(I32)
    keep = valid & (rank < SEL_TOPK)
    bias_ref[0, 0] = jnp.where(keep, 0.0, SEL_MASK_BIAS).astype(BF16)


def _cmp_select(qnt, kca, vct, overlap_t):
    b, g, ncp, _ = kca.shape
    s = qnt.shape[2]
    bq = 128
    dq = NSA_HG * HEAD
    return pl.pallas_call(
        functools.partial(_cmpsel_body, bq),
        grid=(b, g, s // bq),
        in_specs=[pl.BlockSpec((1, dq, bq), lambda bi, gi, qi: (bi, gi, qi)),
                  pl.BlockSpec((1, 1, ncp, LANES), lambda bi, gi, qi: (bi, gi, 0, 0)),
                  pl.BlockSpec((1, 1, HEAD, ncp), lambda bi, gi, qi: (bi, gi, 0, 0)),
                  pl.BlockSpec((SEL_BLOCK, ncp), lambda bi, gi, qi: (0, 0))],
        out_specs=[pl.BlockSpec((1, 1, dq, bq), lambda bi, gi, qi: (bi, gi, 0, qi)),
                   pl.BlockSpec((1, 1, SEL_BLOCK, bq), lambda bi, gi, qi: (bi, gi, 0, qi))],
        out_shape=[jax.ShapeDtypeStruct((b, g, dq, s), F32),
                   jax.ShapeDtypeStruct((b, g, SEL_BLOCK, s), BF16)],
        compiler_params=_params("parallel", "parallel", "parallel"),
        name="cmp_select",
    )(qnt, kca, vct, overlap_t)


def _selwin_body(bq, bks, q_ref, bias_ref, ks_ref, vs_ref, kw_ref, vw_ref, ocmp_ref, gate_ref,
                 y_ref, m_ref, l_ref, acc_ref, qa_ref, sa_ref, sb_ref):
    qi = pl.program_id(2)
    rows = NSA_HG * bq
    q2 = _heads_on_lanes(q_ref[0], bq)
    bias = bias_ref[0, 0]
    qa = jnp.concatenate([q2, jnp.concatenate([bias] * NSA_HG, axis=1)], axis=0)
    qw = jnp.concatenate([q2, jnp.zeros_like(q2)], axis=0)
    r = lax.broadcasted_iota(I32, (bq, rows), 0)
    c = lax.broadcasted_iota(I32, (bq, rows), 1) & (bq - 1)

    _softmax_reset(m_ref, l_ref, acc_ref)

    qa_ref[...] = qa
    last = (qi * bq) // bks

    def causal(s):
        kpos = last * bks + lax.broadcasted_iota(I32, (bks, rows), 0)
        qpos = qi * bq + (lax.broadcasted_iota(I32, (bks, rows), 1) & (bq - 1))
        return jnp.where(kpos <= qpos, s, NEG_BIG)

    _attend_tiles(last,
                  lambda kt: jnp.dot(_key_tile(ks_ref, kt, bks), qa_ref[...],
                                     preferred_element_type=F32),
                  lambda kt: _value_tile(vs_ref, kt, bks), causal,
                  sa_ref, sb_ref, m_ref, l_ref, acc_ref)
    o_sel = acc_ref[...] / l_ref[...]

    _softmax_reset(m_ref, l_ref, acc_ref)
    backs = list(range(WINDOW // bq, -1, -1))
    tiles = [jnp.maximum(qi - back, 0) for back in backs]
    raw = [jnp.dot(_key_tile(kw_ref, kt, bq), qw, preferred_element_type=F32) for kt in tiles]
    for back, kt, s in zip(backs, tiles, raw):
        dist = c + back * bq - r
        inside = (dist >= 0) & (dist < WINDOW) & (qi >= back)
        _softmax_step(jnp.where(inside, s, NEG_BIG), _value_tile(vw_ref, kt, bq),
                      m_ref, l_ref, acc_ref)

    o_win = acc_ref[...] / l_ref[...]
    gate = gate_ref[0]
    for h in range(NSA_HG):
        sl = slice(h * bq, (h + 1) * bq)
        y = (gate[3 * h:3 * h + 1, :] * ocmp_ref[0, 0, h * HEAD:(h + 1) * HEAD, :]
             + gate[3 * h + 1:3 * h + 2, :] * o_sel[:, sl]
             + gate[3 * h + 2:3 * h + 3, :] * o_win[:, sl])
        y_ref[0, h * HEAD:(h + 1) * HEAD, :] = y.astype(BF16)


def _sel_win(qnt, bias_t, ksa, vst, kwa, vwt, ocmp_t, gnt):
    b, g, s, _ = ksa.shape
    bq, bks = 256, 512
    dq = NSA_HG * HEAD
    rows = NSA_HG * bq
    keys = pl.BlockSpec((1, 1, s, LANES), lambda bi, gi, qi: (bi, gi, 0, 0))
    vals = pl.BlockSpec((1, 1, HEAD, s), lambda bi, gi, qi: (bi, gi, 0, 0))
    qtile = pl.BlockSpec((1, dq, bq), lambda bi, gi, qi: (bi, gi, qi))
    return pl.pallas_call(
        functools.partial(_selwin_body, bq, bks),
        grid=(b, g, s // bq),
        in_specs=[qtile,
                  pl.BlockSpec((1, 1, SEL_BLOCK, bq), lambda bi, gi, qi: (bi, gi, 0, qi)),
                  keys, vals, keys, vals,
                  pl.BlockSpec((1, 1, dq, bq), lambda bi, gi, qi: (bi, gi, 0, qi)),
                  pl.BlockSpec((1, LANES, bq), lambda bi, gi, qi: (bi, gi, qi))],
        out_specs=qtile,
        out_shape=jax.ShapeDtypeStruct((b, g * dq, s), BF16),
        scratch_shapes=[pltpu.VMEM((1, rows), F32), pltpu.VMEM((1, rows), F32),
                        pltpu.VMEM((HEAD, rows), F32), pltpu.VMEM((LANES, rows), BF16),
                        pltpu.VMEM((bks, rows), F32), pltpu.VMEM((bks, rows), F32)],
        compiler_params=_params("parallel", "parallel", "parallel"),
        name="sel_win",
    )(qnt, bias_t, ksa, vst, kwa, vwt, ocmp_t, gnt)


def _outproj_body(yda_ref, yn_ref, gm_ref, x_ref, pda_ref, pnsa_ref, wo_ref, fg_ref, wq_ref,
                  k1_ref, k2_ref, h1_ref, c_ref, s1_ref, s2_ref):
    d = x_ref.shape[1]
    a = jnp.dot(yda_ref[...], pda_ref[...], preferred_element_type=F32)
    bn = jnp.dot(yn_ref[...], pnsa_ref[...], preferred_element_type=F32)
    merged = gm_ref[:, :d].astype(F32) * a + gm_ref[:, d:].astype(F32) * bn
    h1 = x_ref[...] + jnp.dot(merged.astype(BF16), wo_ref[...], preferred_element_type=F32)
    h1_ref[...] = h1
    c = _rmsnorm(h1, fg_ref[...])
    c_ref[...] = c
    cb = c.astype(BF16)
    for h in range(PEER_HEADS):
        qh = jnp.dot(cb, wq_ref[:, h * 256:(h + 1) * 256], preferred_element_type=F32).astype(BF16)
        s1_ref[h] = lax.dot_general(k1_ref[...], qh[:, :LANES], NT_DIMS, preferred_element_type=F32)
        s2_ref[h] = lax.dot_general(k2_ref[...], qh[:, LANES:], NT_DIMS, preferred_element_type=F32)


def _out_proj(yda, yn, gm, x2, pda, pnsa, wo, ffn_g, wq, k1, k2):
    t, d = x2.shape
    tm = 512
    row = lambda w: pl.BlockSpec((tm, w), lambda i: (i, 0))
    fixed = lambda a: pl.BlockSpec(a.shape, lambda i: (0, 0))
    sspec = pl.BlockSpec((PEER_HEADS, PEER_NKEYS, tm), lambda i: (0, 0, i))
    return pl.pallas_call(
        _outproj_body,
        grid=(t // tm,),
        in_specs=[row(512), row(512), row(2 * d), row(d), fixed(pda), fixed(pnsa), fixed(wo),
                  fixed(ffn_g), fixed(wq), fixed(k1), fixed(k2)],
        out_specs=[row(d), row(d), sspec, sspec],
        out_shape=[jax.ShapeDtypeStruct((t, d), F32), jax.ShapeDtypeStruct((t, d), F32),
                   jax.ShapeDtypeStruct((PEER_HEADS, PEER_NKEYS, t), F32),
                   jax.ShapeDtypeStruct((PEER_HEADS, PEER_NKEYS, t), F32)],
        compiler_params=_params("parallel"),
        name="out_proj",
    )(yda, yn, gm, x2, pda, pnsa, wo, ffn_g, wq, k1, k2)


def _batcher_pairs(n):
    pairs = []

    def merge(lo, hi, r):
        step = r * 2
        if step < hi - lo:
            merge(lo, hi, step)
            merge(lo + r, hi, step)
            pairs.extend((i, i + r) for i in range(lo + r, hi - r, step))
        else:
            pairs.append((lo, lo + r))

    def sort(lo, hi):
        if hi - lo >= 1:
            mid = lo + (hi - lo) // 2
            sort(lo, mid)
            sort(mid + 1, hi)
            merge(lo, hi, 1)

    sort(0, n - 1)
    return pairs


_NET16 = _batcher_pairs(PEER_TOPK)


def _cmpx(a, b):
    c = a[0] >= b[0]
    return ((jnp.where(c, a[0], b[0]), jnp.where(c, a[1], b[1])),
            (jnp.where(c, b[0], a[0]), jnp.where(c, b[1], a[1])))


def _sort_lists(lists, n_real):
    lists = list(lists)
    for i, j in _NET16:
        if j < n_real:
            lists[i], lists[j] = _cmpx(lists[i], lists[j])
    return lists


def _merge_sublanes(lists):
    k = PEER_TOPK
    for dist in (4, 2, 1):
        other = [(pltpu.roll(v, dist, 0), pltpu.roll(ix, dist, 0)) for v, ix in lists]
        lists = [_cmpx(lists[i], other[k - 1 - i])[0] for i in range(k)]
        step = k // 2
        while step >= 1:
            for i in range(k):
                if i & step == 0:
                    lists[i], lists[i + step] = _cmpx(lists[i], lists[i + step])
            step //= 2
    return lists


def _spread(lists, off, sub):
    v, ix = lists[off]
    for r in range(1, SUBLANES):
        sel = sub == r
        v = jnp.where(sel, lists[off + r][0], v)
        ix = jnp.where(sel, lists[off + r][1], ix)
    return v, ix


def _peertopk_body(tt, s1_ref, s2_ref, idx_ref, gate_ref):
    sub = lax.broadcasted_iota(I32, (SUBLANES, LANES), 0)
    groups = PEER_NKEYS // SUBLANES

    def top16(ref, h, lanes):
        lists = [(ref[h, v * SUBLANES:(v + 1) * SUBLANES, lanes], sub + v * SUBLANES)
                 for v in range(groups)]
        return _merge_sublanes(_sort_lists(lists, groups))

    def unit(u, carry):
        h = u // (tt // LANES)
        lanes = pl.ds(pl.multiple_of((u % (tt // LANES)) * LANES, LANES), LANES)
        l1 = top16(s1_ref, h, lanes)
        l2 = top16(s2_ref, h, lanes)
        v2lo, v2hi, v1hi = _spread(l2, 0, sub), _spread(l2, SUBLANES, sub), _spread(l1, SUBLANES, sub)
        cands = [(l1[a][0] + v2lo[0], l1[a][1] * PEER_NKEYS + v2lo[1]) for a in range(SUBLANES)]
        cands.append((l1[0][0] + v2hi[0], l1[0][1] * PEER_NKEYS + v2hi[1]))
        cands.append((v1hi[0] + l2[0][0], v1hi[1] * PEER_NKEYS + l2[0][1]))
        n_real = len(cands)
        filler = (jnp.full((SUBLANES, LANES), -jnp.inf, F32), jnp.zeros((SUBLANES, LANES), I32))
        cands += [filler] * (PEER_TOPK - n_real)
        best = _merge_sublanes(_sort_lists(cands, n_real))
        ex = [jnp.exp(v - best[0][0]) for v, _ in best]
        z = ex[0]
        for e in ex[1:]:
            z = z + e
        gl = [(e / z, ix) for e, (_, ix) in zip(ex, best)]
        lo, hi = _spread(gl, 0, sub), _spread(gl, SUBLANES, sub)
        gate_ref[h, :, lanes] = jnp.concatenate([lo[0], hi[0]], axis=0)
        idx_ref[h, :, lanes] = jnp.concatenate([lo[1], hi[1]], axis=0)
        return carry

    lax.fori_loop(0, PEER_HEADS * (tt // LANES), unit, 0)


def _peer_topk(s1t, s2t):
    _, _, t = s1t.shape
    tt = 512
    spec_in = pl.BlockSpec((PEER_HEADS, PEER_NKEYS, tt), lambda i: (0, 0, i))
    spec_out = pl.BlockSpec((PEER_HEADS, PEER_TOPK, tt), lambda i: (0, 0, i))
    return pl.pallas_call(
        functools.partial(_peertopk_body, tt),
        grid=(t // tt,),
        in_specs=[spec_in, spec_in],
        out_specs=[spec_out, spec_out],
        out_shape=[jax.ShapeDtypeStruct((PEER_HEADS, PEER_TOPK, t), I32),
                   jax.ShapeDtypeStruct((PEER_HEADS, PEER_TOPK, t), F32)],
        compiler_params=_params("parallel"),
        name="peer_topk",
    )(s1t, s2t)


PEER_E = PEER_HEADS * PEER_TOPK
PEER_RING = 3
PEER_MID_ROWS = 48


def _pack_expert_rows(pu, pv):
    bits = lambda a: lax.bitcast_convert_type(a.astype(BF16), jnp.uint16).astype(jnp.uint32)
    words = (bits(pu) << 16) | bits(pv)
    return lax.bitcast_convert_type(words, I32).reshape(-1, SUBLANES, LANES)


def _word_hi(w):
    return lax.bitcast_convert_type(w & jnp.int32(-65536), F32)


def _word_lo(w):
    return lax.bitcast_convert_type(lax.shift_left(w, jnp.int32(16)), F32)


def _sublane_sums(a, sub):
    for dist in (4, 2, 1):
        low = (sub & dist) == 0
        half = len(a) // 2
        a = [jnp.where(low, a[i], pltpu.roll(a[i + half], dist, 0))
             + jnp.where(low, pltpu.roll(a[i], SUBLANES - dist, 0), a[i + half])
             for i in range(half)]
    return a[0]


def _peer_pair_math(expert_u, expert_v, x8, gates, store, issue_some):
    sub = lax.broadcasted_iota(I32, (SUBLANES, LANES), 0)
    eye = (lax.broadcasted_iota(I32, (PEER_E, LANES), 0)
           == lax.broadcasted_iota(I32, (PEER_E, LANES), 1))
    ones_rows = jnp.ones((SUBLANES, LANES), BF16)
    ones_sq = jnp.ones((LANES, LANES), BF16)
    ngroup = PEER_E // SUBLANES

    def hidden(a):
        groups = []
        for g in range(ngroup):
            prods = [expert_u(a, g * SUBLANES + r) * x8[a] for r in range(SUBLANES)]
            groups.append(_sublane_sums(prods, sub))
            issue_some(3 - g % 2)
        return jnp.concatenate(groups, axis=0)

    def expert_weights(a, q):
        q_hi = q.astype(BF16)
        q_lo = (q - q_hi.astype(F32)).astype(BF16)
        hid = (lax.dot_general(ones_rows, q_hi, NT_DIMS, preferred_element_type=F32)
               + lax.dot_general(ones_rows, q_lo, NT_DIMS, preferred_element_type=F32))
        issue_some(PEER_MID_ROWS // 2)
        w = _gelu(hid[0:1]) * gates[a]
        wd = jnp.where(eye, jnp.broadcast_to(w, (PEER_E, LANES)), 0.0).astype(BF16)
        wcol = jnp.dot(wd, ones_sq, preferred_element_type=F32)
        issue_some(PEER_MID_ROWS // 2)
        return wcol

    def combine(a, wcol):
        out = jnp.zeros((SUBLANES, LANES), F32)
        for g in range(ngroup):
            for r in range(SUBLANES):
                j = g * SUBLANES + r
                out = out + wcol[j:j + 1, :] * expert_v(a, j)
            issue_some(3 - g % 2)
        store(a, out)

    wcol0 = expert_weights(0, hidden(0))
    wcol1 = expert_weights(1, hidden(1))
    combine(0, wcol0)
    combine(1, wcol1)


def _peer_finish(h1_ref, acc_ref, fg_ref, out_ref):
    hsum = h1_ref[...] + acc_ref[...]
    ms = jnp.mean(hsum * hsum, axis=(1, 2), keepdims=True)
    out_ref[...] = hsum * lax.rsqrt(ms + RMS_EPS) * fg_ref[...]


def _peerffn_body(tb, idx_ref, gate_ref, c_ref, h1_ref, fg_ref, uv_ref, out_ref, buf_ref,
                  acc_ref, sem_ref):
    def row_copy(t, j, slot):
        return pltpu.make_async_copy(uv_ref.at[idx_ref[t, j]], buf_ref.at[slot, j],
                                     sem_ref.at[slot])

    def issue(t, slot, j0, j1):
        for j in range(j0, j1):
            row_copy(t, j, slot).start(priority=j % 2)

    def wait_all(slot):
        pltpu.make_async_copy(uv_ref.at[pl.ds(0, PEER_E)], buf_ref.at[slot],
                              sem_ref.at[slot]).wait()

    def ring(i):
        base = 2 * (i % PEER_RING)
        return (base, base + 1)

    def pair(i, prefetch):
        ahead = PEER_RING - 1
        toks = (2 * i, 2 * i + 1)
        slots = ring(i)
        nslots = ring(i + ahead)
        todo = [(a, j) for a in range(2) for j in range(PEER_E)]

        def issue_some(n):
            for a, j in todo[:n]:
                if prefetch:
                    row_copy(toks[a] + 2 * ahead, j, nslots[a]).start(priority=j % 2)
            del todo[:n]

        wait_all(slots[0])
        wait_all(slots[1])

        def store(a, out):
            acc_ref[toks[a]] = out

        _peer_pair_math(
            lambda a, j: _word_hi(buf_ref[slots[a], j]),
            lambda a, j: _word_lo(buf_ref[slots[a], j]),
            [c_ref[t] for t in toks], [gate_ref[pl.ds(t, 1), :] for t in toks], store, issue_some)
        assert not todo, "every prefetch row DMA must be issued exactly once"

    npairs = tb // 2
    for i in range(PEER_RING - 1):
        for a, slot in enumerate(ring(i)):
            issue(2 * i + a, slot, 0, PEER_E)

    def body(i, carry):
        pair(i, True)
        return carry

    lax.fori_loop(0, npairs - (PEER_RING - 1), body, 0)
    for i in range(npairs - (PEER_RING - 1), npairs):
        pair(i, False)
    _peer_finish(h1_ref, acc_ref, fg_ref, out_ref)


def _peer_ffn(idx, gate, c3, h13, final_g3, uv_tiles):
    t = h13.shape[0]
    tb = 128
    row3 = pl.BlockSpec((tb, SUBLANES, LANES), lambda i: (i, 0, 0))
    return pl.pallas_call(
        functools.partial(_peerffn_body, tb),
        grid=(t // tb,),
        in_specs=[pl.BlockSpec((tb, PEER_E), lambda i: (i, 0), memory_space=pltpu.SMEM),
                  pl.BlockSpec((tb, PEER_E), lambda i: (i, 0)), row3, row3,
                  pl.BlockSpec((1, SUBLANES, LANES), lambda i: (0, 0, 0)),
                  pl.BlockSpec(memory_space=pl.ANY)],
        out_specs=row3,
        out_shape=jax.ShapeDtypeStruct((t, SUBLANES, LANES), F32),
        scratch_shapes=[pltpu.VMEM((2 * PEER_RING, PEER_E, SUBLANES, LANES), I32),
                        pltpu.VMEM((tb, SUBLANES, LANES), F32),
                        pltpu.SemaphoreType.DMA((2 * PEER_RING,))],
        compiler_params=_params("arbitrary"),
        name="peer_ffn",
    )(idx, gate, c3, h13, final_g3, uv_tiles)


def _overlap_table(seq):
    ci = jnp.arange(seq // CMP_STRIDE)[None, :] * CMP_STRIDE
    sj = jnp.arange(SEL_BLOCK)[:, None] * SEL_BLOCK
    return ((ci < sj + SEL_BLOCK) & (ci + CMP_BLOCK > sj)).astype(F32)


def _cmp_blocks(kv):
    b, g, s, dh = kv.shape
    r = kv.reshape(b * g, s // CMP_STRIDE, CMP_STRIDE * dh)
    return jnp.concatenate([r, jnp.roll(r, -1, axis=1)], axis=-1)


def _layer(h, lidx, attn_norm, w_in, lq1, lk1, lq2, lk2, subln, pe_k, pe_v, w1k, w1v, w2k, w2v,
           p_da, p_nsa, w_o, ffn_norm, wq, k1, k2, pu, pv, out_norm):
    b, s, d = h.shape
    t = b * s
    g, hg = NSA_GROUPS, NSA_HG
    lambda_init = 0.8 - 0.6 * math.exp(-0.3 * lidx)
    x2 = h.reshape(t, d)

    qda, kda, vda, qn, kvn, gm, gn = _in_proj(x2, attn_norm.reshape(1, d), _pack_w_in(w_in),
                                              _rope_tables(s), s)
    tr = lambda a2: jnp.swapaxes(a2.reshape(b, s, -1), 1, 2)
    ydat = _diff_attn(tr(qda), kda.reshape(b, s, -1), tr(vda),
                      lq1.reshape(1, -1), lk1.reshape(1, -1), lq2.reshape(1, -1),
                      lk2.reshape(1, -1), subln.reshape(-1, 1), lambda_init)
    yda = jnp.swapaxes(ydat, 1, 2).reshape(t, -1)

    assert s // SEL_BLOCK <= SEL_BLOCK, "selection bias rows hold at most 64 blocks"
    kv6 = kvn.reshape(b, s, 6, g, HEAD).transpose(2, 0, 3, 1, 4)
    kc_raw, vc_raw, ks, vs, kw, vw = (kv6[i] for i in range(6))
    kca, vc = _compress(_cmp_blocks(kc_raw), _cmp_blocks(vc_raw),
                        pe_k.reshape(1, -1), pe_v.reshape(1, -1),
                        w1k.astype(BF16), w1v.astype(BF16), w2k.astype(BF16), w2v.astype(BF16))
    ncp = s // CMP_STRIDE
    kca = kca.reshape(b, g, ncp, LANES)
    vct = jnp.swapaxes(vc.reshape(b, g, ncp, HEAD), 2, 3)

    qnt = tr(qn)
    ocmp_t, bias_t = _cmp_select(qnt, kca, vct, _overlap_table(s))
    onehot = (jnp.arange(s)[:, None] // SEL_BLOCK == jnp.arange(HEAD)[None, :]).astype(BF16)
    ksa = jnp.concatenate([ks, jnp.broadcast_to(onehot, ks.shape)], axis=-1)
    kwa = jnp.concatenate([kw, jnp.zeros_like(kw)], axis=-1)
    ynt = _sel_win(qnt, bias_t, ksa, jnp.swapaxes(vs, 2, 3), kwa, jnp.swapaxes(vw, 2, 3),
                   ocmp_t, tr(gn))
    yn = jnp.swapaxes(ynt, 1, 2).reshape(t, -1)

    h1, cb, s1t, s2t = _out_proj(yda, yn, gm, x2, p_da.astype(BF16),
                                 p_nsa.astype(BF16), w_o.astype(BF16), ffn_norm.reshape(1, d),
                                 wq.astype(BF16), k1.astype(BF16), k2.astype(BF16))
    idx_t, gate_t = _peer_topk(s1t, s2t)
    idx = idx_t.reshape(PEER_E, t).T
    gate = gate_t.reshape(PEER_E, t).T
    uv_tiles = _pack_expert_rows(pu, pv)
    as_tiles = lambda a2: a2.reshape(-1, SUBLANES, LANES)
    out = _peer_ffn(idx, gate, as_tiles(cb), as_tiles(h1), as_tiles(out_norm.reshape(1, d)),
                    uv_tiles)
    return out.reshape(b, s, d)


def kernel(x, attn_norm, w_in, da_lambda_q1, da_lambda_k1, da_lambda_q2, da_lambda_k2, da_subln,
           cmp_pe_k, cmp_pe_v, cmp_w1_k, cmp_w1_v, cmp_w2_k, cmp_w2_v, p_da, p_nsa, w_o,
           ffn_norm, peer_wq, peer_k1, peer_k2, peer_u, peer_v, final_norm):
    depth = attn_norm.shape[0]
    assert depth == 1, "the final norm is fused into the last layer's PEER kernel"
    h = x
    for l in range(depth):
        h = _layer(h, l, attn_norm[l], w_in[l], da_lambda_q1[l], da_lambda_k1[l], da_lambda_q2[l],
                   da_lambda_k2[l], da_subln[l], cmp_pe_k[l], cmp_pe_v[l], cmp_w1_k[l],
                   cmp_w1_v[l], cmp_w2_k[l], cmp_w2_v[l], p_da[l], p_nsa[l], w_o[l], ffn_norm[l],
                   peer_wq[l], peer_k1[l], peer_k2[l], peer_u[l], peer_v[l], final_norm)
    return h
```

```python
import functools
import math

import jax
import jax.numpy as jnp
from jax import lax
from jax.experimental import pallas as pl
from jax.experimental.pallas import tpu as pltpu

F32 = jnp.float32
BF16 = jnp.bfloat16
I32 = jnp.int32

RMS_EPS = 1e-6
ROPE_THETA = 500000.0
ROPE_HALF = 8
HEAD = 64
DA_HEADS = 4
NSA_GROUPS = 2
NSA_HG = 4
CMP_STRIDE = 16
CMP_BLOCK = 32
SEL_BLOCK = 64
SEL_TOPK = 16
WINDOW = 512
FORCE_BONUS = 1e4
NEG_BIG = -1e30
SEL_MASK_BIAS = -32768.0
PEER_HEADS = 8
PEER_NKEYS = 128
PEER_TOPK = 16
LANES = 128
SUBLANES = 8
VMEM_LIMIT = 56 * 1024 * 1024

NT_DIMS = (((1,), (1,)), ((), ()))


def _rmsnorm(x, g):
    return x * lax.rsqrt(jnp.mean(x * x, axis=-1, keepdims=True) + RMS_EPS) * g


def _sigmoid(z):
    return 1.0 / (1.0 + jnp.exp(-z))


def _gelu(z):
    return 0.5 * z * (1.0 + lax.erf(z * (2.0 ** -0.5)))


def _params(*sem):
    return pltpu.CompilerParams(dimension_semantics=sem, vmem_limit_bytes=VMEM_LIMIT)


_QDA0, _KDA0, _VDA0, _QN0, _KVN0, _GM0, _GN0, _WCOLS = 0, 512, 1024, 1536, 2048, 2816, 4864, 5120


def _inproj_body(x_ref, g_ref, w_ref, rc_ref, rs1_ref, rs2_ref,
                 qda_ref, kda_ref, vda_ref, qn_ref, kvn_ref, gm_ref, gn_ref):
    a = _rmsnorm(x_ref[...], g_ref[...]).astype(BF16)
    rc, rs1, rs2 = rc_ref[...], rs1_ref[...], rs2_ref[...]

    def rope(z):
        return (z * rc + pltpu.roll(z, ROPE_HALF, 1) * rs1
                + pltpu.roll(z, LANES - ROPE_HALF, 1) * rs2)

    def proj(c0):
        return jnp.dot(a, w_ref[:, c0:c0 + 256], preferred_element_type=F32)

    def rope2(z):
        return jnp.concatenate([rope(z[:, :LANES]), rope(z[:, LANES:])], axis=1)

    for c in range(2):
        qda_ref[:, c * 256:(c + 1) * 256] = (rope2(proj(_QDA0 + c * 256)) * 0.125).astype(BF16)
        kda_ref[:, c * 256:(c + 1) * 256] = rope2(proj(_KDA0 + c * 256)).astype(BF16)
        vda_ref[:, c * 256:(c + 1) * 256] = proj(_VDA0 + c * 256).astype(BF16)
        qn_ref[:, c * 256:(c + 1) * 256] = (rope2(proj(_QN0 + c * 256)) * 0.125).astype(BF16)
    for c in range(3):
        z = proj(_KVN0 + c * 256)
        kvn_ref[:, c * 256:c * 256 + LANES] = rope(z[:, :LANES]).astype(BF16)
        kvn_ref[:, c * 256 + LANES:(c + 1) * 256] = z[:, LANES:].astype(BF16)
    for c in range(8):
        gm_ref[:, c * 256:(c + 1) * 256] = _sigmoid(proj(_GM0 + c * 256)).astype(BF16)
    gn_ref[...] = _sigmoid(proj(_GN0))


def _pack_w_in(w):
    d = w.shape[0]
    gn = w[:, 2816:2840]
    pad = jnp.zeros((d, LANES - 12), w.dtype)
    return jnp.concatenate([w[:, :2816], w[:, 2840:], gn[:, :12], pad, gn[:, 12:], pad],
                           axis=1).astype(BF16)


def _rope_tables(seq):
    inv = jnp.power(ROPE_THETA, -jnp.arange(ROPE_HALF, dtype=F32) * 2.0 / (2 * ROPE_HALF))
    ang = jnp.arange(seq, dtype=F32)[:, None] * inv[None, :]
    cos, sin = jnp.cos(ang), jnp.sin(ang)
    one = jnp.ones((seq, HEAD - 2 * ROPE_HALF), F32)
    zero8 = jnp.zeros((seq, ROPE_HALF), F32)
    zero48 = jnp.zeros_like(one)
    rc = jnp.concatenate([cos, cos, one], axis=1)
    rs1 = jnp.concatenate([zero8, sin, zero48], axis=1)
    rs2 = jnp.concatenate([-sin, zero8, zero48], axis=1)
    return tuple(jnp.concatenate([t, t], axis=1) for t in (rc, rs1, rs2))


def _in_proj(x2, norm_g, w_packed, rope_tabs, seq):
    t, d = x2.shape
    tm = 512
    nseq = seq // tm
    row = lambda i: (i, 0)
    fixed = lambda i: (0, 0)
    out_shapes = [
        jax.ShapeDtypeStruct((t, 512), BF16), jax.ShapeDtypeStruct((t, 512), BF16),
        jax.ShapeDtypeStruct((t, 512), BF16), jax.ShapeDtypeStruct((t, 512), BF16),
        jax.ShapeDtypeStruct((t, 768), BF16), jax.ShapeDtypeStruct((t, 2048), BF16),
        jax.ShapeDtypeStruct((t, 256), F32)]
    rope_spec = pl.BlockSpec((tm, LANES), lambda i: (i % nseq, 0))
    return pl.pallas_call(
        _inproj_body,
        grid=(t // tm,),
        in_specs=[pl.BlockSpec((tm, d), row), pl.BlockSpec((1, d), fixed),
                  pl.BlockSpec((d, _WCOLS), fixed), rope_spec, rope_spec, rope_spec],
        out_specs=[pl.BlockSpec((tm, s.shape[1]), row) for s in out_shapes],
        out_shape=out_shapes,
        compiler_params=_params("parallel"),
        name="in_proj",
    )(x2, norm_g, w_packed, *rope_tabs)


def _softmax_step(s, vt, m_ref, l_ref, acc_ref):
    m_prev = m_ref[...]
    m_new = jnp.maximum(m_prev, jnp.max(s, axis=0, keepdims=True))
    alpha = jnp.exp(m_prev - m_new)
    p = jnp.exp(s - m_new)
    l_ref[...] = alpha * l_ref[...] + jnp.sum(p, axis=0, keepdims=True)
    acc_ref[...] = alpha * acc_ref[...] + jnp.dot(vt, p.astype(BF16), preferred_element_type=F32)
    m_ref[...] = m_new


def _softmax_reset(m_ref, l_ref, acc_ref):
    m_ref[...] = jnp.full(m_ref.shape, NEG_BIG, F32)
    l_ref[...] = jnp.zeros(l_ref.shape, F32)
    acc_ref[...] = jnp.zeros(acc_ref.shape, F32)


def _attend_tiles(n_full, scores, values, mask_last, sa_ref, sb_ref, m_ref, l_ref, acc_ref):
    step = lambda s, t: _softmax_step(s, values(t), m_ref, l_ref, acc_ref)
    sa_ref[...] = scores(0)

    def two_tiles(i, carry):
        t = 2 * i
        sb_ref[...] = scores(t + 1)
        step(sa_ref[...], t)
        sa_ref[...] = scores(t + 2)
        step(sb_ref[...], t + 1)
        return carry

    lax.fori_loop(0, n_full // 2, two_tiles, 0)
    odd = (n_full & 1) == 1

    @pl.when(odd)
    def _():
        sb_ref[...] = scores(n_full)
        step(sa_ref[...], n_full - 1)
        step(mask_last(sb_ref[...]), n_full)

    @pl.when(jnp.logical_not(odd))
    def _():
        step(mask_last(sa_ref[...]), n_full)


def _key_tile(ref, kt, bk):
    return ref[(0,) * (len(ref.shape) - 2) + (pl.ds(pl.multiple_of(kt * bk, bk), bk), slice(None))]


def _value_tile(ref, kt, bk):
    return ref[(0,) * (len(ref.shape) - 2) + (slice(None), pl.ds(pl.multiple_of(kt * bk, bk), bk))]


def _diffattn_body(lambda_init, bq, q_ref, k_ref, v_ref, lq1_ref, lk1_ref, lq2_ref, lk2_ref,
                   sub_ref, y_ref, qbd_ref, m_ref, l_ref, acc_ref, sa_ref, sb_ref):
    qi = pl.program_id(2)
    bk = bq
    qt = q_ref[0]
    sub = lax.broadcasted_iota(I32, qt.shape, 0)
    zero = jnp.zeros_like(qt)
    qbd_ref[:, 0:bq] = jnp.where(sub < HEAD, qt, zero)
    qbd_ref[:, bq:2 * bq] = jnp.where(sub >= HEAD, qt, zero)
    _softmax_reset(m_ref, l_ref, acc_ref)

    def scores(kt):
        return jnp.dot(_key_tile(k_ref, kt, bk), qbd_ref[...], preferred_element_type=F32)

    def causal(s):
        r = lax.broadcasted_iota(I32, (bk, 2 * bq), 0)
        c = lax.broadcasted_iota(I32, (bk, 2 * bq), 1) & (bq - 1)
        return jnp.where(r <= c, s, NEG_BIG)

    _attend_tiles(qi, scores, lambda kt: _value_tile(v_ref, kt, bk), causal,
                  sa_ref, sb_ref, m_ref, l_ref, acc_ref)

    o = acc_ref[...] / l_ref[...]
    lam = (jnp.exp(jnp.sum(lq1_ref[...] * lk1_ref[...], axis=1, keepdims=True))
           - jnp.exp(jnp.sum(lq2_ref[...] * lk2_ref[...], axis=1, keepdims=True)) + lambda_init)
    d = o[:, 0:bq] - lam * o[:, bq:2 * bq]
    ms = jnp.mean(d * d, axis=0, keepdims=True)
    y = d * lax.rsqrt(ms + RMS_EPS) * sub_ref[...] * (1.0 - lambda_init)
    y_ref[0] = y.astype(BF16)


def _diff_attn(qdat, kda, vdat, lq1, lk1, lq2, lk2, subln_col, lambda_init):
    b, s, _ = kda.shape
    bq = 512
    vec = lambda n: pl.BlockSpec((1, n), lambda bi, h, qi: (0, 0))
    dv = 2 * HEAD
    qtile = pl.BlockSpec((1, dv, bq), lambda bi, h, qi: (bi, h, qi))
    return pl.pallas_call(
        functools.partial(_diffattn_body, lambda_init, bq),
        grid=(b, DA_HEADS, s // bq),
        in_specs=[qtile,
                  pl.BlockSpec((1, s, LANES), lambda bi, h, qi: (bi, 0, h)),
                  pl.BlockSpec((1, dv, s), lambda bi, h, qi: (bi, h, 0)),
                  vec(HEAD), vec(HEAD), vec(HEAD), vec(HEAD),
                  pl.BlockSpec((dv, 1), lambda bi, h, qi: (0, 0))],
        out_specs=qtile,
        out_shape=jax.ShapeDtypeStruct((b, DA_HEADS * dv, s), BF16),
        scratch_shapes=[pltpu.VMEM((LANES, 2 * bq), BF16), pltpu.VMEM((1, 2 * bq), F32),
                        pltpu.VMEM((1, 2 * bq), F32), pltpu.VMEM((dv, 2 * bq), F32),
                        pltpu.VMEM((bq, 2 * bq), F32), pltpu.VMEM((bq, 2 * bq), F32)],
        compiler_params=_params("parallel", "parallel", "parallel"),
        name="diff_attn",
    )(qdat, kda, vdat, lq1, lk1, lq2, lk2, subln_col)


def _compress_body(xk_ref, xv_ref, pek_ref, pev_ref, w1k_ref, w1v_ref, w2k_ref, w2v_ref,
                   kc_ref, vc_ref):
    def mlp(x_ref, pe_ref, w1_ref, w2_ref):
        blocks = (x_ref[0].astype(F32) + pe_ref[...]).astype(BF16)
        hid = _gelu(jnp.dot(blocks, w1_ref[...], preferred_element_type=F32))
        return jnp.dot(hid.astype(BF16), w2_ref[...], preferred_element_type=F32)

    kc = mlp(xk_ref, pek_ref, w1k_ref, w2k_ref)
    kc_ref[0] = jnp.concatenate([kc, jnp.zeros_like(kc)], axis=1).astype(BF16)
    vc_ref[0] = mlp(xv_ref, pev_ref, w1v_ref, w2v_ref).astype(BF16)


def _compress(xk, xv, pe_k, pe_v, w1k, w1v, w2k, w2v):
    n, ncp, width = xk.shape
    blk = pl.BlockSpec((1, ncp, width), lambda i: (i, 0, 0))
    fixed = lambda shape: pl.BlockSpec(shape, lambda i: (0, 0))
    return pl.pallas_call(
        _compress_body,
        grid=(n,),
        in_specs=[blk, blk, fixed((1, width)), fixed((1, width)), fixed((width, HEAD)),
                  fixed((width, HEAD)), fixed((HEAD, HEAD)), fixed((HEAD, HEAD))],
        out_specs=[pl.BlockSpec((1, ncp, LANES), lambda i: (i, 0, 0)),
                   pl.BlockSpec((1, ncp, HEAD), lambda i: (i, 0, 0))],
        out_shape=[jax.ShapeDtypeStruct((n, ncp, LANES), BF16),
                   jax.ShapeDtypeStruct((n, ncp, HEAD), BF16)],
        compiler_params=_params("parallel"),
        name="compress",
    )(xk, xv, pe_k, pe_v, w1k, w1v, w2k, w2v)


def _heads_on_lanes(qt, bq):
    return jnp.concatenate([qt[h * HEAD:(h + 1) * HEAD, :] for h in range(NSA_HG)], axis=1)


def _cmpsel_body(bq, q_ref, kc_ref, vc_ref, ovl_ref, ocmp_ref, bias_ref):
    qi = pl.program_id(2)
    ncp = kc_ref.shape[2]
    rows = NSA_HG * bq
    q2 = _heads_on_lanes(q_ref[0], bq)
    qz = jnp.concatenate([q2, jnp.zeros_like(q2)], axis=0)
    s = jnp.dot(kc_ref[0, 0], qz, preferred_element_type=F32)
    n = lax.broadcasted_iota(I32, (ncp, rows), 0)
    qpos = qi * bq + (lax.broadcasted_iota(I32, (ncp, rows), 1) & (bq - 1))
    cmask = n * CMP_STRIDE + (CMP_BLOCK - 1) <= qpos
    s = jnp.where(cmask, s, NEG_BIG)
    e = jnp.exp(s - jnp.max(s, axis=0, keepdims=True))
    p = jnp.where(cmask, e / jnp.sum(e, axis=0, keepdims=True), 0.0)
    o = jnp.dot(vc_ref[0, 0], p.astype(BF16), preferred_element_type=F32)
    for h in range(NSA_HG):
        ocmp_ref[0, 0, h * HEAD:(h + 1) * HEAD, :] = o[:, h * bq:(h + 1) * bq]

    psum = p[:, 0:bq] + p[:, bq:2 * bq] + p[:, 2 * bq:3 * bq] + p[:, 3 * bq:4 * bq]
    imp = jnp.dot(ovl_ref[...], psum, preferred_element_type=F32)
    blk = lax.broadcasted_iota(I32, (SEL_BLOCK, bq), 0)
    pos = qi * bq + lax.broadcasted_iota(I32, (SEL_BLOCK, bq), 1)
    cur = lax.shift_right_logical(pos, 6)
    valid = blk <= cur
    forced = (blk == 0) | (blk == cur) | (blk == cur - 1)
    score = jnp.where(valid, imp + jnp.where(forced, FORCE_BONUS, 0.0), -jnp.inf)
    rank = jnp.zeros((SEL_BLOCK, bq), I32)
    for i in range(SEL_BLOCK):
        other = score[i:i + 1, :]
        beats = (other > score) | ((other == score) & (blk > i))
        rank = rank + beats.astype(I32)
    keep = valid & (rank < SEL_TOPK)
    bias_ref[0, 0] = jnp.where(keep, 0.0, SEL_MASK_BIAS).astype(BF16)


def _cmp_select(qnt, kca, vct, overlap_t):
    b, g, ncp, _ = kca.shape
    s = qnt.shape[2]
    bq = 128
    dq = NSA_HG * HEAD
    return pl.pallas_call(
        functools.partial(_cmpsel_body, bq),
        grid=(b, g, s // bq),
        in_specs=[pl.BlockSpec((1, dq, bq), lambda bi, gi, qi: (bi, gi, qi)),
                  pl.BlockSpec((1, 1, ncp, LANES), lambda bi, gi, qi: (bi, gi, 0, 0)),
                  pl.BlockSpec((1, 1, HEAD, ncp), lambda bi, gi, qi: (bi, gi, 0, 0)),
                  pl.BlockSpec((SEL_BLOCK, ncp), lambda bi, gi, qi: (0, 0))],
        out_specs=[pl.BlockSpec((1, 1, dq, bq), lambda bi, gi, qi: (bi, gi, 0, qi)),
                   pl.BlockSpec((1, 1, SEL_BLOCK, bq), lambda bi, gi, qi: (bi, gi, 0, qi))],
        out_shape=[jax.ShapeDtypeStruct((b, g, dq, s), F32),
                   jax.ShapeDtypeStruct((b, g, SEL_BLOCK, s), BF16)],
        compiler_params=_params("parallel", "parallel", "parallel"),
        name="cmp_select",
    )(qnt, kca, vct, overlap_t)


def _selwin_body(bq, bks, q_ref, bias_ref, ks_ref, vs_ref, kw_ref, vw_ref, ocmp_ref, gate_ref,
                 y_ref, m_ref, l_ref, acc_ref, qa_ref, sa_ref, sb_ref):
    qi = pl.program_id(2)
    rows = NSA_HG * bq
    q2 = _heads_on_lanes(q_ref[0], bq)
    bias = bias_ref[0, 0]
    qa = jnp.concatenate([q2, jnp.concatenate([bias] * NSA_HG, axis=1)], axis=0)
    qw = jnp.concatenate([q2, jnp.zeros_like(q2)], axis=0)
    r = lax.broadcasted_iota(I32, (bq, rows), 0)
    c = lax.broadcasted_iota(I32, (bq, rows), 1) & (bq - 1)

    _softmax_reset(m_ref, l_ref, acc_ref)

    qa_ref[...] = qa
    last = (qi * bq) // bks

    def causal(s):
        kpos = last * bks + lax.broadcasted_iota(I32, (bks, rows), 0)
        qpos = qi * bq + (lax.broadcasted_iota(I32, (bks, rows), 1) & (bq - 1))
        return jnp.where(kpos <= qpos, s, NEG_BIG)

    _attend_tiles(last,
                  lambda kt: jnp.dot(_key_tile(ks_ref, kt, bks), qa_ref[...],
                                     preferred_element_type=F32),
                  lambda kt: _value_tile(vs_ref, kt, bks), causal,
                  sa_ref, sb_ref, m_ref, l_ref, acc_ref)
    o_sel = acc_ref[...] / l_ref[...]

    _softmax_reset(m_ref, l_ref, acc_ref)
    backs = list(range(WINDOW // bq, -1, -1))
    tiles = [jnp.maximum(qi - back, 0) for back in backs]
    raw = [jnp.dot(_key_tile(kw_ref, kt, bq), qw, preferred_element_type=F32) for kt in tiles]
    for back, kt, s in zip(backs, tiles, raw):
        dist = c + back * bq - r
        inside = (dist >= 0) & (dist < WINDOW) & (qi >= back)
        _softmax_step(jnp.where(inside, s, NEG_BIG), _value_tile(vw_ref, kt, bq),
                      m_ref, l_ref, acc_ref)

    o_win = acc_ref[...] / l_ref[...]
    gate = gate_ref[0]
    for h in range(NSA_HG):
        sl = slice(h * bq, (h + 1) * bq)
        y = (gate[3 * h:3 * h + 1, :] * ocmp_ref[0, 0, h * HEAD:(h + 1) * HEAD, :]
             + gate[3 * h + 1:3 * h + 2, :] * o_sel[:, sl]
             + gate[3 * h + 2:3 * h + 3, :] * o_win[:, sl])
        y_ref[0, h * HEAD:(h + 1) * HEAD, :] = y.astype(BF16)


def _sel_win(qnt, bias_t, ksa, vst, kwa, vwt, ocmp_t, gnt):
    b, g, s, _ = ksa.shape
    bq, bks = 256, 512
    dq = NSA_HG * HEAD
    rows = NSA_HG * bq
    keys = pl.BlockSpec((1, 1, s, LANES), lambda bi, gi, qi: (bi, gi, 0, 0))
    vals = pl.BlockSpec((1, 1, HEAD, s), lambda bi, gi, qi: (bi, gi, 0, 0))
    qtile = pl.BlockSpec((1, dq, bq), lambda bi, gi, qi: (bi, gi, qi))
    return pl.pallas_call(
        functools.partial(_selwin_body, bq, bks),
        grid=(b, g, s // bq),
        in_specs=[qtile,
                  pl.BlockSpec((1, 1, SEL_BLOCK, bq), lambda bi, gi, qi: (bi, gi, 0, qi)),
                  keys, vals, keys, vals,
                  pl.BlockSpec((1, 1, dq, bq), lambda bi, gi, qi: (bi, gi, 0, qi)),
                  pl.BlockSpec((1, LANES, bq), lambda bi, gi, qi: (bi, gi, qi))],
        out_specs=qtile,
        out_shape=jax.ShapeDtypeStruct((b, g * dq, s), BF16),
        scratch_shapes=[pltpu.VMEM((1, rows), F32), pltpu.VMEM((1, rows), F32),
                        pltpu.VMEM((HEAD, rows), F32), pltpu.VMEM((LANES, rows), BF16),
                        pltpu.VMEM((bks, rows), F32), pltpu.VMEM((bks, rows), F32)],
        compiler_params=_params("parallel", "parallel", "parallel"),
        name="sel_win",
    )(qnt, bias_t, ksa, vst, kwa, vwt, ocmp_t, gnt)


def _outproj_body(yda_ref, yn_ref, gm_ref, x_ref, pda_ref, pnsa_ref, wo_ref, fg_ref, wq_ref,
                  k1_ref, k2_ref, h1_ref, c_ref, s1_ref, s2_ref):
    d = x_ref.shape[1]
    a = jnp.dot(yda_ref[...], pda_ref[...], preferred_element_type=F32)
    bn = jnp.dot(yn_ref[...], pnsa_ref[...], preferred_element_type=F32)
    merged = gm_ref[:, :d].astype(F32) * a + gm_ref[:, d:].astype(F32) * bn
    h1 = x_ref[...] + jnp.dot(merged.astype(BF16), wo_ref[...], preferred_element_type=F32)
    h1_ref[...] = h1
    c = _rmsnorm(h1, fg_ref[...])
    c_ref[...] = c
    cb = c.astype(BF16)
    for h in range(PEER_HEADS):
        qh = jnp.dot(cb, wq_ref[:, h * 256:(h + 1) * 256], preferred_element_type=F32).astype(BF16)
        s1_ref[h] = lax.dot_general(k1_ref[...], qh[:, :LANES], NT_DIMS, preferred_element_type=F32)
        s2_ref[h] = lax.dot_general(k2_ref[...], qh[:, LANES:], NT_DIMS, preferred_element_type=F32)


def _out_proj(yda, yn, gm, x2, pda, pnsa, wo, ffn_g, wq, k1, k2):
    t, d = x2.shape
    tm = 512
    row = lambda w: pl.BlockSpec((tm, w), lambda i: (i, 0))
    fixed = lambda a: pl.BlockSpec(a.shape, lambda i: (0, 0))
    sspec = pl.BlockSpec((PEER_HEADS, PEER_NKEYS, tm), lambda i: (0, 0, i))
    return pl.pallas_call(
        _outproj_body,
        grid=(t // tm,),
        in_specs=[row(512), row(512), row(2 * d), row(d), fixed(pda), fixed(pnsa), fixed(wo),
                  fixed(ffn_g), fixed(wq), fixed(k1), fixed(k2)],
        out_specs=[row(d), row(d), sspec, sspec],
        out_shape=[jax.ShapeDtypeStruct((t, d), F32), jax.ShapeDtypeStruct((t, d), F32),
                   jax.ShapeDtypeStruct((PEER_HEADS, PEER_NKEYS, t), F32),
                   jax.ShapeDtypeStruct((PEER_HEADS, PEER_NKEYS, t), F32)],
        compiler_params=_params("parallel"),
        name="out_proj",
    )(yda, yn, gm, x2, pda, pnsa, wo, ffn_g, wq, k1, k2)


def _batcher_pairs(n):
    pairs = []

    def merge(lo, hi, r):
        step = r * 2
        if step < hi - lo:
            merge(lo, hi, step)
            merge(lo + r, hi, step)
            pairs.extend((i, i + r) for i in range(lo + r, hi - r, step))
        else:
            pairs.append((lo, lo + r))

    def sort(lo, hi):
        if hi - lo >= 1:
            mid = lo + (hi - lo) // 2
            sort(lo, mid)
            sort(mid + 1, hi)
            merge(lo, hi, 1)

    sort(0, n - 1)
    return pairs


_NET16 = _batcher_pairs(PEER_TOPK)


def _cmpx(a, b):
    c = a[0] >= b[0]
    return ((jnp.where(c, a[0], b[0]), jnp.where(c, a[1], b[1])),
            (jnp.where(c, b[0], a[0]), jnp.where(c, b[1], a[1])))


def _sort_lists(lists, n_real):
    lists = list(lists)
    for i, j in _NET16:
        if j < n_real:
            lists[i], lists[j] = _cmpx(lists[i], lists[j])
    return lists


def _merge_sublanes(lists):
    k = PEER_TOPK
    for dist in (4, 2, 1):
        other = [(pltpu.roll(v, dist, 0), pltpu.roll(ix, dist, 0)) for v, ix in lists]
        lists = [_cmpx(lists[i], other[k - 1 - i])[0] for i in range(k)]
        step = k // 2
        while step >= 1:
            for i in range(k):
                if i & step == 0:
                    lists[i], lists[i + step] = _cmpx(lists[i], lists[i + step])
            step //= 2
    return lists


def _spread(lists, off, sub):
    v, ix = lists[off]
    for r in range(1, SUBLANES):
        sel = sub == r
        v = jnp.where(sel, lists[off + r][0], v)
        ix = jnp.where(sel, lists[off + r][1], ix)
    return v, ix


def _peertopk_body(tt, s1_ref, s2_ref, idx_ref, gate_ref):
    sub = lax.broadcasted_iota(I32, (SUBLANES, LANES), 0)
    groups = PEER_NKEYS // SUBLANES

    def top16(ref, h, lanes):
        lists = [(ref[h, v * SUBLANES:(v + 1) * SUBLANES, lanes], sub + v * SUBLANES)
                 for v in range(groups)]
        return _merge_sublanes(_sort_lists(lists, groups))

    def unit(u, carry):
        h = u // (tt // LANES)
        lanes = pl.ds(pl.multiple_of((u % (tt // LANES)) * LANES, LANES), LANES)
        l1 = top16(s1_ref, h, lanes)
        l2 = top16(s2_ref, h, lanes)
        v2lo, v2hi, v1hi = _spread(l2, 0, sub), _spread(l2, SUBLANES, sub), _spread(l1, SUBLANES, sub)
        cands = [(l1[a][0] + v2lo[0], l1[a][1] * PEER_NKEYS + v2lo[1]) for a in range(SUBLANES)]
        cands.append((l1[0][0] + v2hi[0], l1[0][1] * PEER_NKEYS + v2hi[1]))
        cands.append((v1hi[0] + l2[0][0], v1hi[1] * PEER_NKEYS + l2[0][1]))
        n_real = len(cands)
        filler = (jnp.full((SUBLANES, LANES), -jnp.inf, F32), jnp.zeros((SUBLANES, LANES), I32))
        cands += [filler] * (PEER_TOPK - n_real)
        best = _merge_sublanes(_sort_lists(cands, n_real))
        ex = [jnp.exp(v - best[0][0]) for v, _ in best]
        z = ex[0]
        for e in ex[1:]:
            z = z + e
        gl = [(e / z, ix) for e, (_, ix) in zip(ex, best)]
        lo, hi = _spread(gl, 0, sub), _spread(gl, SUBLANES, sub)
        gate_ref[h, :, lanes] = jnp.concatenate([lo[0], hi[0]], axis=0)
        idx_ref[h, :, lanes] = jnp.concatenate([lo[1], hi[1]], axis=0)
        return carry

    lax.fori_loop(0, PEER_HEADS * (tt // LANES), unit, 0)


def _peer_topk(s1t, s2t):
    _, _, t = s1t.shape
    tt = 512
    spec_in = pl.BlockSpec((PEER_HEADS, PEER_NKEYS, tt), lambda i: (0, 0, i))
    spec_out = pl.BlockSpec((PEER_HEADS, PEER_TOPK, tt), lambda i: (0, 0, i))
    return pl.pallas_call(
        functools.partial(_peertopk_body, tt),
        grid=(t // tt,),
        in_specs=[spec_in, spec_in],
        out_specs=[spec_out, spec_out],
        out_shape=[jax.ShapeDtypeStruct((PEER_HEADS, PEER_TOPK, t), I32),
                   jax.ShapeDtypeStruct((PEER_HEADS, PEER_TOPK, t), F32)],
        compiler_params=_params("parallel"),
        name="peer_topk",
    )(s1t, s2t)


PEER_E = PEER_HEADS * PEER_TOPK
PEER_RING = 3
PEER_MID_ROWS = 48


def _pack_body(u_ref, v_ref, out_ref):
    te, d = u_ref.shape
    sub = lax.broadcasted_iota(I32, (SUBLANES, LANES), 0)
    bf16_bits = lambda a: lax.bitcast_convert_type(a.astype(BF16).astype(F32), I32)
    for g in range(te // SUBLANES):
        rows = slice(g * SUBLANES, (g + 1) * SUBLANES)
        words = [bf16_bits(u_ref[rows, c * LANES:(c + 1) * LANES])
                 | lax.shift_right_logical(bf16_bits(v_ref[rows, c * LANES:(c + 1) * LANES]),
                                           jnp.int32(16))
                 for c in range(d // LANES)]
        for dist in (4, 2, 1):
            low = (sub & dist) == 0
            nxt = list(words)
            for i in range(SUBLANES):
                if i & dist == 0:
                    nxt[i] = jnp.where(low, words[i], pltpu.roll(words[i + dist], dist, 0))
                    nxt[i + dist] = jnp.where(low, pltpu.roll(words[i], SUBLANES - dist, 0),
                                              words[i + dist])
            words = nxt
        for e in range(SUBLANES):
            out_ref[g * SUBLANES + e] = words[e]


def _pack_expert_rows(pu, pv):
    n, d = pu.shape
    assert d == SUBLANES * LANES
    te = 256
    blk = pl.BlockSpec((te, d), lambda i: (i, 0))
    return pl.pallas_call(
        _pack_body,
        grid=(n // te,),
        in_specs=[blk, blk],
        out_specs=pl.BlockSpec((te, SUBLANES, LANES), lambda i: (i, 0, 0)),
        out_shape=jax.ShapeDtypeStruct((n, SUBLANES, LANES), I32),
        compiler_params=_params("parallel"),
        name="pack_experts",
    )(pu, pv)


def _word_hi(w):
    return lax.bitcast_convert_type(w & jnp.int32(-65536), F32)


def _word_lo(w):
    return lax.bitcast_convert_type(lax.shift_left(w, jnp.int32(16)), F32)


def _sublane_sums(a, sub):
    for dist in (4, 2, 1):
        low = (sub & dist) == 0
        half = len(a) // 2
        a = [jnp.where(low, a[i], pltpu.roll(a[i + half], dist, 0))
             + jnp.where(low, pltpu.roll(a[i], SUBLANES - dist, 0), a[i + half])
             for i in range(half)]
    return a[0]


def _peer_pair_math(expert_u, expert_v, x8, gates, store, issue_some):
    sub = lax.broadcasted_iota(I32, (SUBLANES, LANES), 0)
    eye = (lax.broadcasted_iota(I32, (PEER_E, LANES), 0)
           == lax.broadcasted_iota(I32, (PEER_E, LANES), 1))
    ones_rows = jnp.ones((SUBLANES, LANES), BF16)
    ones_sq = jnp.ones((LANES, LANES), BF16)
    ngroup = PEER_E // SUBLANES

    def hidden(a):
        groups = []
        for g in range(ngroup):
            prods = [expert_u(a, g * SUBLANES + r) * x8[a] for r in range(SUBLANES)]
            groups.append(_sublane_sums(prods, sub))
            issue_some(3 - g % 2)
        return jnp.concatenate(groups, axis=0)

    def expert_weights(a, q):
        q_hi = q.astype(BF16)
        q_lo = (q - q_hi.astype(F32)).astype(BF16)
        hid = (lax.dot_general(ones_rows, q_hi, NT_DIMS, preferred_element_type=F32)
               + lax.dot_general(ones_rows, q_lo, NT_DIMS, preferred_element_type=F32))
        issue_some(PEER_MID_ROWS // 2)
        w = _gelu(hid[0:1]) * gates[a]
        wd = jnp.where(eye, jnp.broadcast_to(w, (PEER_E, LANES)), 0.0).astype(BF16)
        wcol = jnp.dot(wd, ones_sq, preferred_element_type=F32)
        issue_some(PEER_MID_ROWS // 2)
        return wcol

    def combine(a, wcol):
        out = jnp.zeros((SUBLANES, LANES), F32)
        for g in range(ngroup):
            for r in range(SUBLANES):
                j = g * SUBLANES + r
                out = out + wcol[j:j + 1, :] * expert_v(a, j)
            issue_some(3 - g % 2)
        store(a, out)

    wcol0 = expert_weights(0, hidden(0))
    wcol1 = expert_weights(1, hidden(1))
    combine(0, wcol0)
    combine(1, wcol1)


def _peer_finish(h1_ref, acc_ref, fg_ref, out_ref):
    hsum = h1_ref[...] + acc_ref[...]
    ms = jnp.mean(hsum * hsum, axis=(1, 2), keepdims=True)
    out_ref[...] = hsum * lax.rsqrt(ms + RMS_EPS) * fg_ref[...]


def _peerffn_body(tb, idx_ref, gate_ref, c_ref, h1_ref, fg_ref, uv_ref, out_ref, buf_ref,
                  acc_ref, sem_ref):
    def row_copy(t, j, slot):
        return pltpu.make_async_copy(uv_ref.at[idx_ref[t, j]], buf_ref.at[slot, j],
                                     sem_ref.at[slot])

    def issue(t, slot, j0, j1):
        for j in range(j0, j1):
            row_copy(t, j, slot).start(priority=j % 2)

    def wait_all(slot):
        pltpu.make_async_copy(uv_ref.at[pl.ds(0, PEER_E)], buf_ref.at[slot],
                              sem_ref.at[slot]).wait()

    def ring(i):
        base = 2 * (i % PEER_RING)
        return (base, base + 1)

    def pair(i, prefetch):
        ahead = PEER_RING - 1
        toks = (2 * i, 2 * i + 1)
        slots = ring(i)
        nslots = ring(i + ahead)
        todo = [(a, j) for a in range(2) for j in range(PEER_E)]

        def issue_some(n):
            for a, j in todo[:n]:
                if prefetch:
                    row_copy(toks[a] + 2 * ahead, j, nslots[a]).start(priority=j % 2)
            del todo[:n]

        wait_all(slots[0])
        wait_all(slots[1])

        def store(a, out):
            acc_ref[toks[a]] = out

        _peer_pair_math(
            lambda a, j: _word_hi(buf_ref[slots[a], j]),
            lambda a, j: _word_lo(buf_ref[slots[a], j]),
            [c_ref[t] for t in toks], [gate_ref[pl.ds(t, 1), :] for t in toks], store, issue_some)
        assert not todo, "every prefetch row DMA must be issued exactly once"

    npairs = tb // 2
    for i in range(PEER_RING - 1):
        for a, slot in enumerate(ring(i)):
            issue(2 * i + a, slot, 0, PEER_E)

    def body(i, carry):
        pair(i, True)
        return carry

    lax.fori_loop(0, npairs - (PEER_RING - 1), body, 0)
    for i in range(npairs - (PEER_RING - 1), npairs):
        pair(i, False)
    _peer_finish(h1_ref, acc_ref, fg_ref, out_ref)


def _peer_ffn(idx, gate, c3, h13, final_g3, uv_tiles):
    t = h13.shape[0]
    tb = 256
    row3 = pl.BlockSpec((tb, SUBLANES, LANES), lambda i: (i, 0, 0))
    return pl.pallas_call(
        functools.partial(_peerffn_body, tb),
        grid=(t // tb,),
        in_specs=[pl.BlockSpec((tb, PEER_E), lambda i: (i, 0), memory_space=pltpu.SMEM),
                  pl.BlockSpec((tb, PEER_E), lambda i: (i, 0)), row3, row3,
                  pl.BlockSpec((1, SUBLANES, LANES), lambda i: (0, 0, 0)),
                  pl.BlockSpec(memory_space=pl.ANY)],
        out_specs=row3,
        out_shape=jax.ShapeDtypeStruct((t, SUBLANES, LANES), F32),
        scratch_shapes=[pltpu.VMEM((2 * PEER_RING, PEER_E, SUBLANES, LANES), I32),
                        pltpu.VMEM((tb, SUBLANES, LANES), F32),
                        pltpu.SemaphoreType.DMA((2 * PEER_RING,))],
        compiler_params=_params("arbitrary"),
        name="peer_ffn",
    )(idx, gate, c3, h13, final_g3, uv_tiles)


def _overlap_table(seq):
    ci = jnp.arange(seq // CMP_STRIDE)[None, :] * CMP_STRIDE
    sj = jnp.arange(SEL_BLOCK)[:, None] * SEL_BLOCK
    return ((ci < sj + SEL_BLOCK) & (ci + CMP_BLOCK > sj)).astype(F32)


def _cmp_blocks(kv):
    b, g, s, dh = kv.shape
    r = kv.reshape(b * g, s // CMP_STRIDE, CMP_STRIDE * dh)
    return jnp.concatenate([r, jnp.roll(r, -1, axis=1)], axis=-1)


def _layer(h, lidx, attn_norm, w_in, lq1, lk1, lq2, lk2, subln, pe_k, pe_v, w1k, w1v, w2k, w2v,
           p_da, p_nsa, w_o, ffn_norm, wq, k1, k2, pu, pv, out_norm):
    b, s, d = h.shape
    t = b * s
    g, hg = NSA_GROUPS, NSA_HG
    lambda_init = 0.8 - 0.6 * math.exp(-0.3 * lidx)
    x2 = h.reshape(t, d)

    qda, kda, vda, qn, kvn, gm, gn = _in_proj(x2, attn_norm.reshape(1, d), _pack_w_in(w_in),
                                              _rope_tables(s), s)
    tr = lambda a2: jnp.swapaxes(a2.reshape(b, s, -1), 1, 2)
    ydat = _diff_attn(tr(qda), kda.reshape(b, s, -1), tr(vda),
                      lq1.reshape(1, -1), lk1.reshape(1, -1), lq2.reshape(1, -1),
                      lk2.reshape(1, -1), subln.reshape(-1, 1), lambda_init)
    yda = jnp.swapaxes(ydat, 1, 2).reshape(t, -1)

    assert s // SEL_BLOCK <= SEL_BLOCK, "selection bias rows hold at most 64 blocks"
    kv6 = kvn.reshape(b, s, 6, g, HEAD).transpose(2, 0, 3, 1, 4)
    kc_raw, vc_raw, ks, vs, kw, vw = (kv6[i] for i in range(6))
    kca, vc = _compress(_cmp_blocks(kc_raw), _cmp_blocks(vc_raw),
                        pe_k.reshape(1, -1), pe_v.reshape(1, -1),
                        w1k.astype(BF16), w1v.astype(BF16), w2k.astype(BF16), w2v.astype(BF16))
    ncp = s // CMP_STRIDE
    kca = kca.reshape(b, g, ncp, LANES)
    vct = jnp.swapaxes(vc.reshape(b, g, ncp, HEAD), 2, 3)

    qnt = tr(qn)
    ocmp_t, bias_t = _cmp_select(qnt, kca, vct, _overlap_table(s))
    onehot = (jnp.arange(s)[:, None] // SEL_BLOCK == jnp.arange(HEAD)[None, :]).astype(BF16)
    ksa = jnp.concatenate([ks, jnp.broadcast_to(onehot, ks.shape)], axis=-1)
    kwa = jnp.concatenate([kw, jnp.zeros_like(kw)], axis=-1)
    ynt = _sel_win(qnt, bias_t, ksa, jnp.swapaxes(vs, 2, 3), kwa, jnp.swapaxes(vw, 2, 3),
                   ocmp_t, tr(gn))
    yn = jnp.swapaxes(ynt, 1, 2).reshape(t, -1)

    h1, cb, s1t, s2t = _out_proj(yda, yn, gm, x2, p_da.astype(BF16),
                                 p_nsa.astype(BF16), w_o.astype(BF16), ffn_norm.reshape(1, d),
                                 wq.astype(BF16), k1.astype(BF16), k2.astype(BF16))
    idx_t, gate_t = _peer_topk(s1t, s2t)
    idx = idx_t.reshape(PEER_E, t).T
    gate = gate_t.reshape(PEER_E, t).T
    uv_tiles = _pack_expert_rows(pu, pv)
    as_tiles = lambda a2: a2.reshape(-1, SUBLANES, LANES)
    out = _peer_ffn(idx, gate, as_tiles(cb), as_tiles(h1), as_tiles(out_norm.reshape(1, d)),
                    uv_tiles)
    return out.reshape(b, s, d)


def kernel(x, attn_norm, w_in, da_lambda_q1, da_lambda_k1, da_lambda_q2, da_lambda_k2, da_subln,
           cmp_pe_k, cmp_pe_v, cmp_w1_k, cmp_w1_v, cmp_w2_k, cmp_w2_v, p_da, p_nsa, w_o,
           ffn_norm, peer_wq, peer_k1, peer_k2, peer_u, peer_v, final_norm):
    depth = attn_norm.shape[0]
    assert depth == 1, "the final norm is fused into the last layer's PEER kernel"
    h = x
    for l in range(depth):
        h = _layer(h, l, attn_norm[l], w_in[l], da_lambda_q1[l], da_lambda_k1[l], da_lambda_q2[l],
                   da_lambda_k2[l], da_subln[l], cmp_pe_k[l], cmp_pe_v[l], cmp_w1_k[l],
                   cmp_w1_v[l], cmp_w2_k[l], cmp_w2_v[l], p_da[l], p_nsa[l], w_o[l], ffn_norm[l],
                   peer_wq[l], peer_k1[l], peer_k2[l], peer_u[l], peer_v[l], final_norm)
    return h
```

```python
import functools
import math

import jax
import jax.numpy as jnp
from jax import lax
from jax.experimental import pallas as pl
from jax.experimental.pallas import tpu as pltpu

F32 = jnp.float32
BF16 = jnp.bfloat16
I32 = jnp.int32

RMS_EPS = 1e-6
ROPE_THETA = 500000.0
ROPE_HALF = 8
HEAD = 64
DA_HEADS = 4
NSA_GROUPS = 2
NSA_HG = 4
CMP_STRIDE = 16
CMP_BLOCK = 32
SEL_BLOCK = 64
SEL_TOPK = 16
WINDOW = 512
FORCE_BONUS = 1e4
NEG_BIG = -1e30
SEL_MASK_BIAS = -32768.0
PEER_HEADS = 8
PEER_NKEYS = 128
PEER_TOPK = 16
LANES = 128
SUBLANES = 8
VMEM_LIMIT = 56 * 1024 * 1024

NT_DIMS = (((1,), (1,)), ((), ()))


def _rmsnorm(x, g):
    return x * lax.rsqrt(jnp.mean(x * x, axis=-1, keepdims=True) + RMS_EPS) * g


def _sigmoid(z):
    return 1.0 / (1.0 + jnp.exp(-z))


def _gelu(z):
    return 0.5 * z * (1.0 + lax.erf(z * (2.0 ** -0.5)))


def _params(*sem):
    return pltpu.CompilerParams(dimension_semantics=sem, vmem_limit_bytes=VMEM_LIMIT)


def _sublane_transpose(v):
    sub = lax.broadcasted_iota(I32, (SUBLANES, LANES), 0)
    v = list(v)
    for dist in (4, 2, 1):
        low = (sub & dist) == 0
        nxt = list(v)
        for i in range(SUBLANES):
            if i & dist == 0:
                nxt[i] = jnp.where(low, v[i], pltpu.roll(v[i + dist], dist, 0))
                nxt[i + dist] = jnp.where(low, pltpu.roll(v[i], SUBLANES - dist, 0), v[i + dist])
        v = nxt
    return v


def _rows_to_tiles(x, tile_ref):
    for g in range(x.shape[0] // SUBLANES):
        rows = slice(g * SUBLANES, (g + 1) * SUBLANES)
        tiles = _sublane_transpose([x[rows, c * LANES:(c + 1) * LANES] for c in range(SUBLANES)])
        for e in range(SUBLANES):
            tile_ref[g * SUBLANES + e] = tiles[e]


def _tiles_to_rows(t, row_ref):
    for g in range(t.shape[0] // SUBLANES):
        chunks = _sublane_transpose([t[g * SUBLANES + e] for e in range(SUBLANES)])
        for c in range(SUBLANES):
            row_ref[g * SUBLANES:(g + 1) * SUBLANES, c * LANES:(c + 1) * LANES] = chunks[c]


_QDA0, _KDA0, _VDA0, _QN0 = 0, 512, 1024, 1536
_KC0, _VC0, _KS0, _KW0, _VSW0, _GM0, _GN0, _WCOLS = 2048, 2304, 2560, 2816, 3072, 3328, 5376, 5632


def _inproj_body(x_ref, g_ref, w_ref, rc_ref, rs1_ref, rs2_ref,
                 qda_ref, kda_ref, vda_ref, qn_ref, cmp_ref, ks_ref, kw_ref, vsw_ref, gm_ref, gn_ref):
    a = _rmsnorm(x_ref[...], g_ref[...]).astype(BF16)
    rc, rs1, rs2 = rc_ref[...], rs1_ref[...], rs2_ref[...]

    def rope(z):
        return (z * rc + pltpu.roll(z, ROPE_HALF, 1) * rs1
                + pltpu.roll(z, LANES - ROPE_HALF, 1) * rs2)

    def proj(c0):
        return jnp.dot(a, w_ref[:, c0:c0 + 256], preferred_element_type=F32)

    def rope2(z):
        return jnp.concatenate([rope(z[:, :LANES]), rope(z[:, LANES:])], axis=1)

    for c in range(2):
        qda_ref[:, c * 256:(c + 1) * 256] = (rope2(proj(_QDA0 + c * 256)) * 0.125).astype(BF16)
        kda_ref[:, c * 256:(c + 1) * 256] = rope2(proj(_KDA0 + c * 256)).astype(BF16)
        vda_ref[:, c * 256:(c + 1) * 256] = proj(_VDA0 + c * 256).astype(BF16)
        qn_ref[:, c * 256:(c + 1) * 256] = (rope2(proj(_QN0 + c * 256)) * 0.125).astype(BF16)
    cmp_ref[:, 0:256] = rope2(proj(_KC0)).astype(BF16)
    cmp_ref[:, 256:512] = proj(_VC0).astype(BF16)
    ks_ref[...] = rope2(proj(_KS0)).astype(BF16)
    kw_ref[...] = rope2(proj(_KW0)).astype(BF16)
    vsw_ref[...] = proj(_VSW0).astype(BF16)
    for c in range(8):
        gm_ref[:, c * 256:(c + 1) * 256] = _sigmoid(proj(_GM0 + c * 256)).astype(BF16)
    gn_ref[...] = _sigmoid(proj(_GN0))


def _pack_w_in(w):
    d = w.shape[0]
    zeros = lambda n: jnp.zeros((d, n), w.dtype)

    def spread_groups(c0):
        return [w[:, c0:c0 + HEAD], zeros(HEAD), w[:, c0 + HEAD:c0 + 2 * HEAD], zeros(HEAD)]

    kc, vc, ks, vs, kw, vw = (2048 + 128 * i for i in range(6))
    gn = w[:, 2816:2840]
    cols = ([w[:, :2048]] + spread_groups(kc) + spread_groups(vc) + spread_groups(ks)
            + spread_groups(kw) + [w[:, vs:vs + 128], w[:, vw:vw + 128], w[:, 2840:],
                                   gn[:, :12], zeros(LANES - 12), gn[:, 12:], zeros(LANES - 12)])
    packed = jnp.concatenate(cols, axis=1).astype(BF16)
    assert packed.shape[1] == _WCOLS
    return packed


def _rope_tables(seq):
    inv = jnp.power(ROPE_THETA, -jnp.arange(ROPE_HALF, dtype=F32) * 2.0 / (2 * ROPE_HALF))
    ang = jnp.arange(seq, dtype=F32)[:, None] * inv[None, :]
    cos, sin = jnp.cos(ang), jnp.sin(ang)
    one = jnp.ones((seq, HEAD - 2 * ROPE_HALF), F32)
    zero8 = jnp.zeros((seq, ROPE_HALF), F32)
    zero48 = jnp.zeros_like(one)
    rc = jnp.concatenate([cos, cos, one], axis=1)
    rs1 = jnp.concatenate([zero8, sin, zero48], axis=1)
    rs2 = jnp.concatenate([-sin, zero8, zero48], axis=1)
    return tuple(jnp.concatenate([t, t], axis=1) for t in (rc, rs1, rs2))


def _in_proj(x2, norm_g, w_packed, rope_tabs, seq):
    t, d = x2.shape
    tm = 512
    nseq = seq // tm
    row = lambda i: (i, 0)
    fixed = lambda i: (0, 0)
    out_shapes = [
        jax.ShapeDtypeStruct((t, 512), BF16), jax.ShapeDtypeStruct((t, 512), BF16),
        jax.ShapeDtypeStruct((t, 512), BF16), jax.ShapeDtypeStruct((t, 512), BF16),
        jax.ShapeDtypeStruct((t, 512), BF16), jax.ShapeDtypeStruct((t, 256), BF16),
        jax.ShapeDtypeStruct((t, 256), BF16), jax.ShapeDtypeStruct((t, 256), BF16),
        jax.ShapeDtypeStruct((t, 2048), BF16), jax.ShapeDtypeStruct((t, 256), F32)]
    rope_spec = pl.BlockSpec((tm, LANES), lambda i: (i % nseq, 0))
    return pl.pallas_call(
        _inproj_body,
        grid=(t // tm,),
        in_specs=[pl.BlockSpec((tm, d), row), pl.BlockSpec((1, d), fixed),
                  pl.BlockSpec((d, _WCOLS), fixed), rope_spec, rope_spec, rope_spec],
        out_specs=[pl.BlockSpec((tm, s.shape[1]), row) for s in out_shapes],
        out_shape=out_shapes,
        compiler_params=_params("parallel"),
        name="in_proj",
    )(x2, norm_g, w_packed, *rope_tabs)


def _softmax_step(s, vt, m_ref, l_ref, acc_ref):
    m_prev = m_ref[...]
    m_new = jnp.maximum(m_prev, jnp.max(s, axis=0, keepdims=True))
    alpha = jnp.exp(m_prev - m_new)
    p = jnp.exp(s - m_new)
    l_ref[...] = alpha * l_ref[...] + jnp.sum(p, axis=0, keepdims=True)
    acc_ref[...] = alpha * acc_ref[...] + jnp.dot(vt, p.astype(BF16), preferred_element_type=F32)
    m_ref[...] = m_new


def _softmax_reset(m_ref, l_ref, acc_ref):
    m_ref[...] = jnp.full(m_ref.shape, NEG_BIG, F32)
    l_ref[...] = jnp.zeros(l_ref.shape, F32)
    acc_ref[...] = jnp.zeros(acc_ref.shape, F32)


def _attend_tiles(n_full, scores, values, mask_last, sa_ref, sb_ref, m_ref, l_ref, acc_ref):
    step = lambda s, t: _softmax_step(s, values(t), m_ref, l_ref, acc_ref)
    sa_ref[...] = scores(0)

    def two_tiles(i, carry):
        t = 2 * i
        sb_ref[...] = scores(t + 1)
        step(sa_ref[...], t)
        sa_ref[...] = scores(t + 2)
        step(sb_ref[...], t + 1)
        return carry

    lax.fori_loop(0, n_full // 2, two_tiles, 0)
    odd = (n_full & 1) == 1

    @pl.when(odd)
    def _():
        sb_ref[...] = scores(n_full)
        step(sa_ref[...], n_full - 1)
        step(mask_last(sb_ref[...]), n_full)

    @pl.when(jnp.logical_not(odd))
    def _():
        step(mask_last(sa_ref[...]), n_full)


def _key_tile(ref, kt, bk):
    return ref[(0,) * (len(ref.shape) - 2) + (pl.ds(pl.multiple_of(kt * bk, bk), bk), slice(None))]


def _value_tile(ref, kt, bk):
    return ref[(0,) * (len(ref.shape) - 2) + (slice(None), pl.ds(pl.multiple_of(kt * bk, bk), bk))]


def _diffattn_body(lambda_init, bq, q_ref, k_ref, v_ref, lq1_ref, lk1_ref, lq2_ref, lk2_ref,
                   sub_ref, y_ref, qbd_ref, m_ref, l_ref, acc_ref, sa_ref, sb_ref):
    qi = pl.program_id(2)
    bk = bq
    qt = q_ref[0]
    sub = lax.broadcasted_iota(I32, qt.shape, 0)
    zero = jnp.zeros_like(qt)
    qbd_ref[:, 0:bq] = jnp.where(sub < HEAD, qt, zero)
    qbd_ref[:, bq:2 * bq] = jnp.where(sub >= HEAD, qt, zero)
    _softmax_reset(m_ref, l_ref, acc_ref)

    def scores(kt):
        return jnp.dot(_key_tile(k_ref, kt, bk), qbd_ref[...], preferred_element_type=F32)

    def causal(s):
        r = lax.broadcasted_iota(I32, (bk, 2 * bq), 0)
        c = lax.broadcasted_iota(I32, (bk, 2 * bq), 1) & (bq - 1)
        return jnp.where(r <= c, s, NEG_BIG)

    _attend_tiles(qi, scores, lambda kt: _value_tile(v_ref, kt, bk), causal,
                  sa_ref, sb_ref, m_ref, l_ref, acc_ref)

    o = acc_ref[...] / l_ref[...]
    lam = (jnp.exp(jnp.sum(lq1_ref[...] * lk1_ref[...], axis=1, keepdims=True))
           - jnp.exp(jnp.sum(lq2_ref[...] * lk2_ref[...], axis=1, keepdims=True)) + lambda_init)
    d = o[:, 0:bq] - lam * o[:, bq:2 * bq]
    ms = jnp.mean(d * d, axis=0, keepdims=True)
    y = d * lax.rsqrt(ms + RMS_EPS) * sub_ref[...] * (1.0 - lambda_init)
    y_ref[0] = y.astype(BF16)


def _diff_attn(qdat, kda, vdat, lq1, lk1, lq2, lk2, subln_col, lambda_init):
    b, s, _ = kda.shape
    bq = 512
    vec = lambda n: pl.BlockSpec((1, n), lambda bi, h, qi: (0, 0))
    dv = 2 * HEAD
    qtile = pl.BlockSpec((1, dv, bq), lambda bi, h, qi: (bi, h, qi))
    return pl.pallas_call(
        functools.partial(_diffattn_body, lambda_init, bq),
        grid=(b, DA_HEADS, s // bq),
        in_specs=[qtile,
                  pl.BlockSpec((1, s, LANES), lambda bi, h, qi: (bi, 0, h)),
                  pl.BlockSpec((1, dv, s), lambda bi, h, qi: (bi, h, 0)),
                  vec(HEAD), vec(HEAD), vec(HEAD), vec(HEAD),
                  pl.BlockSpec((dv, 1), lambda bi, h, qi: (0, 0))],
        out_specs=qtile,
        out_shape=jax.ShapeDtypeStruct((b, DA_HEADS * dv, s), BF16),
        scratch_shapes=[pltpu.VMEM((LANES, 2 * bq), BF16), pltpu.VMEM((1, 2 * bq), F32),
                        pltpu.VMEM((1, 2 * bq), F32), pltpu.VMEM((dv, 2 * bq), F32),
                        pltpu.VMEM((bq, 2 * bq), F32), pltpu.VMEM((bq, 2 * bq), F32)],
        compiler_params=_params("parallel", "parallel", "parallel"),
        name="diff_attn",
    )(qdat, kda, vdat, lq1, lk1, lq2, lk2, subln_col)


def _compress_body(xk_ref, xv_ref, pek_ref, pev_ref, w1k_ref, w1v_ref, w2k_ref, w2v_ref,
                   kc_ref, vc_ref):
    def mlp(x_ref, pe_ref, w1_ref, w2_ref):
        blocks = (x_ref[0].astype(F32) + pe_ref[...]).astype(BF16)
        hid = _gelu(jnp.dot(blocks, w1_ref[...], preferred_element_type=F32))
        return jnp.dot(hid.astype(BF16), w2_ref[...], preferred_element_type=F32)

    kc = mlp(xk_ref, pek_ref, w1k_ref, w2k_ref)
    kc_ref[0] = jnp.concatenate([kc, jnp.zeros_like(kc)], axis=1).astype(BF16)
    vc_ref[0] = mlp(xv_ref, pev_ref, w1v_ref, w2v_ref).astype(BF16)


def _compress(xk, xv, pe_k, pe_v, w1k, w1v, w2k, w2v):
    n, ncp, width = xk.shape
    blk = pl.BlockSpec((1, ncp, width), lambda i: (i, 0, 0))
    fixed = lambda shape: pl.BlockSpec(shape, lambda i: (0, 0))
    return pl.pallas_call(
        _compress_body,
        grid=(n,),
        in_specs=[blk, blk, fixed((1, width)), fixed((1, width)), fixed((width, HEAD)),
                  fixed((width, HEAD)), fixed((HEAD, HEAD)), fixed((HEAD, HEAD))],
        out_specs=[pl.BlockSpec((1, ncp, LANES), lambda i: (i, 0, 0)),
                   pl.BlockSpec((1, ncp, HEAD), lambda i: (i, 0, 0))],
        out_shape=[jax.ShapeDtypeStruct((n, ncp, LANES), BF16),
                   jax.ShapeDtypeStruct((n, ncp, HEAD), BF16)],
        compiler_params=_params("parallel"),
        name="compress",
    )(xk, xv, pe_k, pe_v, w1k, w1v, w2k, w2v)


def _heads_on_lanes(qt, bq):
    return jnp.concatenate([qt[h * HEAD:(h + 1) * HEAD, :] for h in range(NSA_HG)], axis=1)


def _cmpsel_body(bq, q_ref, kc_ref, vc_ref, ovl_ref, ocmp_ref, bias_ref):
    qi = pl.program_id(2)
    ncp = kc_ref.shape[2]
    rows = NSA_HG * bq
    q2 = _heads_on_lanes(q_ref[0], bq)
    qz = jnp.concatenate([q2, jnp.zeros_like(q2)], axis=0)
    s = jnp.dot(kc_ref[0, 0], qz, preferred_element_type=F32)
    n = lax.broadcasted_iota(I32, (ncp, rows), 0)
    qpos = qi * bq + (lax.broadcasted_iota(I32, (ncp, rows), 1) & (bq - 1))
    cmask = n * CMP_STRIDE + (CMP_BLOCK - 1) <= qpos
    s = jnp.where(cmask, s, NEG_BIG)
    e = jnp.exp(s - jnp.max(s, axis=0, keepdims=True))
    p = jnp.where(cmask, e / jnp.sum(e, axis=0, keepdims=True), 0.0)
    o = jnp.dot(vc_ref[0, 0], p.astype(BF16), preferred_element_type=F32)
    for h in range(NSA_HG):
        ocmp_ref[0, 0, h * HEAD:(h + 1) * HEAD, :] = o[:, h * bq:(h + 1) * bq]

    psum = p[:, 0:bq] + p[:, bq:2 * bq] + p[:, 2 * bq:3 * bq] + p[:, 3 * bq:4 * bq]
    imp = jnp.dot(ovl_ref[...], psum, preferred_element_type=F32)
    blk = lax.broadcasted_iota(I32, (SEL_BLOCK, bq), 0)
    pos = qi * bq + lax.broadcasted_iota(I32, (SEL_BLOCK, bq), 1)
    cur = lax.shift_right_logical(pos, 6)
    valid = blk <= cur
    forced = (blk == 0) | (blk == cur) | (blk == cur - 1)
    score = jnp.where(valid, imp + jnp.where(forced, FORCE_BONUS, 0.0), -jnp.inf)
    rank = jnp.zeros((SEL_BLOCK, bq), I32)
    for i in range(SEL_BLOCK):
        other = score[i:i + 1, :]
        beats = (other > score) | ((other == score) & (blk > i))
        rank = rank + beats.astype(I32)
    keep = valid & (rank < SEL_TOPK)
    bias_ref[0, 0] = jnp.where(keep, 0.0, SEL_MASK_BIAS).astype(BF16)


def _cmp_select(qnt, kca, vct, overlap_t):
    b, g, ncp, _ = kca.shape
    s = qnt.shape[2]
    bq = 128
    dq = NSA_HG * HEAD
    return pl.pallas_call(
        functools.partial(_cmpsel_body, bq),
        grid=(b, g, s // bq),
        in_specs=[pl.BlockSpec((1, dq, bq), lambda bi, gi, qi: (bi, gi, qi)),
                  pl.BlockSpec((1, 1, ncp, LANES), lambda bi, gi, qi: (bi, gi, 0, 0)),
                  pl.BlockSpec((1, 1, HEAD, ncp), lambda bi, gi, qi: (bi, gi, 0, 0)),
                  pl.BlockSpec((SEL_BLOCK, ncp), lambda bi, gi, qi: (0, 0))],
        out_specs=[pl.BlockSpec((1, 1, dq, bq), lambda bi, gi, qi: (bi, gi, 0, qi)),
                   pl.BlockSpec((1, 1, SEL_BLOCK, bq), lambda bi, gi, qi: (bi, gi, 0, qi))],
        out_shape=[jax.ShapeDtypeStruct((b, g, dq, s), F32),
                   jax.ShapeDtypeStruct((b, g, SEL_BLOCK, s), BF16)],
        compiler_params=_params("parallel", "parallel", "parallel"),
        name="cmp_select",
    )(qnt, kca, vct, overlap_t)


def _selwin_body(bq, bks, q_ref, bias_ref, ks_ref, vs_ref, kw_ref, vw_ref, ocmp_ref, gate_ref,
                 y_ref, m_ref, l_ref, acc_ref, qa_ref, sa_ref, sb_ref):
    qi = pl.program_id(2)
    rows = NSA_HG * bq
    q2 = _heads_on_lanes(q_ref[0], bq)
    bias = bias_ref[0, 0]
    qa = jnp.concatenate([q2, jnp.concatenate([bias] * NSA_HG, axis=1)], axis=0)
    qw = jnp.concatenate([q2, jnp.zeros_like(q2)], axis=0)
    r = lax.broadcasted_iota(I32, (bq, rows), 0)
    c = lax.broadcasted_iota(I32, (bq, rows), 1) & (bq - 1)

    _softmax_reset(m_ref, l_ref, acc_ref)

    qa_ref[...] = qa
    last = (qi * bq) // bks

    def causal(s):
        kpos = last * bks + lax.broadcasted_iota(I32, (bks, rows), 0)
        qpos = qi * bq + (lax.broadcasted_iota(I32, (bks, rows), 1) & (bq - 1))
        return jnp.where(kpos <= qpos, s, NEG_BIG)

    _attend_tiles(last,
                  lambda kt: jnp.dot(_key_tile(ks_ref, kt, bks), qa_ref[...],
                                     preferred_element_type=F32),
                  lambda kt: _value_tile(vs_ref, kt, bks), causal,
                  sa_ref, sb_ref, m_ref, l_ref, acc_ref)
    o_sel = acc_ref[...] / l_ref[...]

    _softmax_reset(m_ref, l_ref, acc_ref)
    backs = list(range(WINDOW // bq, -1, -1))
    tiles = [jnp.maximum(qi - back, 0) for back in backs]
    raw = [jnp.dot(_key_tile(kw_ref, kt, bq), qw, preferred_element_type=F32) for kt in tiles]
    for back, kt, s in zip(backs, tiles, raw):
        dist = c + back * bq - r
        inside = (dist >= 0) & (dist < WINDOW) & (qi >= back)
        _softmax_step(jnp.where(inside, s, NEG_BIG), _value_tile(vw_ref, kt, bq),
                      m_ref, l_ref, acc_ref)

    o_win = acc_ref[...] / l_ref[...]
    gate = gate_ref[0]
    for h in range(NSA_HG):
        sl = slice(h * bq, (h + 1) * bq)
        y = (gate[3 * h:3 * h + 1, :] * ocmp_ref[0, 0, h * HEAD:(h + 1) * HEAD, :]
             + gate[3 * h + 1:3 * h + 2, :] * o_sel[:, sl]
             + gate[3 * h + 2:3 * h + 3, :] * o_win[:, sl])
        y_ref[0, h * HEAD:(h + 1) * HEAD, :] = y.astype(BF16)


def _sel_win(qnt, bias_t, ksa, kwa, vswt, ocmp_t, gnt):
    b, s, _ = ksa.shape
    g = NSA_GROUPS
    bq, bks = 256, 512
    dq = NSA_HG * HEAD
    rows = NSA_HG * bq
    keys = pl.BlockSpec((1, s, LANES), lambda bi, gi, qi: (bi, 0, gi))
    vals = lambda first: pl.BlockSpec((1, HEAD, s), lambda bi, gi, qi: (bi, first + gi, 0))
    qtile = pl.BlockSpec((1, dq, bq), lambda bi, gi, qi: (bi, gi, qi))
    return pl.pallas_call(
        functools.partial(_selwin_body, bq, bks),
        grid=(b, g, s // bq),
        in_specs=[qtile,
                  pl.BlockSpec((1, 1, SEL_BLOCK, bq), lambda bi, gi, qi: (bi, gi, 0, qi)),
                  keys, vals(0), keys, vals(g),
                  pl.BlockSpec((1, 1, dq, bq), lambda bi, gi, qi: (bi, gi, 0, qi)),
                  pl.BlockSpec((1, LANES, bq), lambda bi, gi, qi: (bi, gi, qi))],
        out_specs=qtile,
        out_shape=jax.ShapeDtypeStruct((b, g * dq, s), BF16),
        scratch_shapes=[pltpu.VMEM((1, rows), F32), pltpu.VMEM((1, rows), F32),
                        pltpu.VMEM((HEAD, rows), F32), pltpu.VMEM((LANES, rows), BF16),
                        pltpu.VMEM((bks, rows), F32), pltpu.VMEM((bks, rows), F32)],
        compiler_params=_params("parallel", "parallel", "parallel"),
        name="sel_win",
    )(qnt, bias_t, ksa, vswt, kwa, vswt, ocmp_t, gnt)


def _outproj_body(yda_ref, yn_ref, gm_ref, x_ref, pda_ref, pnsa_ref, wo_ref, fg_ref, wq_ref,
                  k1_ref, k2_ref, h1_ref, c_ref, s1_ref, s2_ref):
    d = x_ref.shape[1]
    a = jnp.dot(yda_ref[...], pda_ref[...], preferred_element_type=F32)
    bn = jnp.dot(yn_ref[...], pnsa_ref[...], preferred_element_type=F32)
    merged = gm_ref[:, :d].astype(F32) * a + gm_ref[:, d:].astype(F32) * bn
    h1 = x_ref[...] + jnp.dot(merged.astype(BF16), wo_ref[...], preferred_element_type=F32)
    c = _rmsnorm(h1, fg_ref[...])
    _rows_to_tiles(h1, h1_ref)
    _rows_to_tiles(c, c_ref)
    cb = c.astype(BF16)
    for h in range(PEER_HEADS):
        qh = jnp.dot(cb, wq_ref[:, h * 256:(h + 1) * 256], preferred_element_type=F32).astype(BF16)
        s1_ref[h] = lax.dot_general(k1_ref[...], qh[:, :LANES], NT_DIMS, preferred_element_type=F32)
        s2_ref[h] = lax.dot_general(k2_ref[...], qh[:, LANES:], NT_DIMS, preferred_element_type=F32)


def _out_proj(yda, yn, gm, x2, pda, pnsa, wo, ffn_g, wq, k1, k2):
    t, d = x2.shape
    tm = 512
    row = lambda w: pl.BlockSpec((tm, w), lambda i: (i, 0))
    fixed = lambda a: pl.BlockSpec(a.shape, lambda i: (0, 0))
    sspec = pl.BlockSpec((PEER_HEADS, PEER_NKEYS, tm), lambda i: (0, 0, i))
    assert d == SUBLANES * LANES
    tiles = pl.BlockSpec((tm, SUBLANES, LANES), lambda i: (i, 0, 0))
    return pl.pallas_call(
        _outproj_body,
        grid=(t // tm,),
        in_specs=[row(512), row(512), row(2 * d), row(d), fixed(pda), fixed(pnsa), fixed(wo),
                  fixed(ffn_g), fixed(wq), fixed(k1), fixed(k2)],
        out_specs=[tiles, tiles, sspec, sspec],
        out_shape=[jax.ShapeDtypeStruct((t, SUBLANES, LANES), F32),
                   jax.ShapeDtypeStruct((t, SUBLANES, LANES), F32),
                   jax.ShapeDtypeStruct((PEER_HEADS, PEER_NKEYS, t), F32),
                   jax.ShapeDtypeStruct((PEER_HEADS, PEER_NKEYS, t), F32)],
        compiler_params=_params("parallel"),
        name="out_proj",
    )(yda, yn, gm, x2, pda, pnsa, wo, ffn_g, wq, k1, k2)


def _batcher_pairs(n):
    pairs = []

    def merge(lo, hi, r):
        step = r * 2
        if step < hi - lo:
            merge(lo, hi, step)
            merge(lo + r, hi, step)
            pairs.extend((i, i + r) for i in range(lo + r, hi - r, step))
        else:
            pairs.append((lo, lo + r))

    def sort(lo, hi):
        if hi - lo >= 1:
            mid = lo + (hi - lo) // 2
            sort(lo, mid)
            sort(mid + 1, hi)
            merge(lo, hi, 1)

    sort(0, n - 1)
    return pairs


_NET16 = _batcher_pairs(PEER_TOPK)


def _cmpx(a, b):
    c = a[0] >= b[0]
    return ((jnp.where(c, a[0], b[0]), jnp.where(c, a[1], b[1])),
            (jnp.where(c, b[0], a[0]), jnp.where(c, b[1], a[1])))


def _sort_lists(lists, n_real):
    lists = list(lists)
    for i, j in _NET16:
        if j < n_real:
            lists[i], lists[j] = _cmpx(lists[i], lists[j])
    return lists


def _merge_sublanes(lists):
    k = PEER_TOPK
    for dist in (4, 2, 1):
        other = [(pltpu.roll(v, dist, 0), pltpu.roll(ix, dist, 0)) for v, ix in lists]
        lists = [_cmpx(lists[i], other[k - 1 - i])[0] for i in range(k)]
        step = k // 2
        while step >= 1:
            for i in range(k):
                if i & step == 0:
                    lists[i], lists[i + step] = _cmpx(lists[i], lists[i + step])
            step //= 2
    return lists


def _spread(lists, off, sub):
    v, ix = lists[off]
    for r in range(1, SUBLANES):
        sel = sub == r
        v = jnp.where(sel, lists[off + r][0], v)
        ix = jnp.where(sel, lists[off + r][1], ix)
    return v, ix


def _peertopk_body(tt, s1_ref, s2_ref, idx_ref, gate_ref):
    sub = lax.broadcasted_iota(I32, (SUBLANES, LANES), 0)
    groups = PEER_NKEYS // SUBLANES

    def top16(ref, h, lanes):
        lists = [(ref[h, v * SUBLANES:(v + 1) * SUBLANES, lanes], sub + v * SUBLANES)
                 for v in range(groups)]
        return _merge_sublanes(_sort_lists(lists, groups))

    def unit(u, carry):
        h = u // (tt // LANES)
        lanes = pl.ds(pl.multiple_of((u % (tt // LANES)) * LANES, LANES), LANES)
        l1 = top16(s1_ref, h, lanes)
        l2 = top16(s2_ref, h, lanes)
        v2lo, v2hi, v1hi = _spread(l2, 0, sub), _spread(l2, SUBLANES, sub), _spread(l1, SUBLANES, sub)
        cands = [(l1[a][0] + v2lo[0], l1[a][1] * PEER_NKEYS + v2lo[1]) for a in range(SUBLANES)]
        cands.append((l1[0][0] + v2hi[0], l1[0][1] * PEER_NKEYS + v2hi[1]))
        cands.append((v1hi[0] + l2[0][0], v1hi[1] * PEER_NKEYS + l2[0][1]))
        n_real = len(cands)
        filler = (jnp.full((SUBLANES, LANES), -jnp.inf, F32), jnp.zeros((SUBLANES, LANES), I32))
        cands += [filler] * (PEER_TOPK - n_real)
        best = _merge_sublanes(_sort_lists(cands, n_real))
        ex = [jnp.exp(v - best[0][0]) for v, _ in best]
        z = ex[0]
        for e in ex[1:]:
            z = z + e
        gl = [(e / z, ix) for e, (_, ix) in zip(ex, best)]
        lo, hi = _spread(gl, 0, sub), _spread(gl, SUBLANES, sub)
        gate_ref[h, :, lanes] = jnp.concatenate([lo[0], hi[0]], axis=0)
        idx_ref[h, :, lanes] = jnp.concatenate([lo[1], hi[1]], axis=0)
        return carry

    lax.fori_loop(0, PEER_HEADS * (tt // LANES), unit, 0)


def _peer_topk(s1t, s2t):
    _, _, t = s1t.shape
    tt = 512
    spec_in = pl.BlockSpec((PEER_HEADS, PEER_NKEYS, tt), lambda i: (0, 0, i))
    spec_out = pl.BlockSpec((PEER_HEADS, PEER_TOPK, tt), lambda i: (0, 0, i))
    return pl.pallas_call(
        functools.partial(_peertopk_body, tt),
        grid=(t // tt,),
        in_specs=[spec_in, spec_in],
        out_specs=[spec_out, spec_out],
        out_shape=[jax.ShapeDtypeStruct((PEER_HEADS, PEER_TOPK, t), I32),
                   jax.ShapeDtypeStruct((PEER_HEADS, PEER_TOPK, t), F32)],
        compiler_params=_params("parallel"),
        name="peer_topk",
    )(s1t, s2t)


PEER_E = PEER_HEADS * PEER_TOPK
PEER_RING = 3
PEER_MID_ROWS = 48


def _pack_body(u_ref, v_ref, out_ref):
    bf16_bits = lambda a: lax.bitcast_convert_type(a.astype(BF16).astype(F32), I32)
    words = bf16_bits(u_ref[...]) | lax.shift_right_logical(bf16_bits(v_ref[...]), jnp.int32(16))
    _rows_to_tiles(words, out_ref)


def _pack_expert_rows(pu, pv):
    n, d = pu.shape
    assert d == SUBLANES * LANES
    te = 256
    blk = pl.BlockSpec((te, d), lambda i: (i, 0))
    return pl.pallas_call(
        _pack_body,
        grid=(n // te,),
        in_specs=[blk, blk],
        out_specs=pl.BlockSpec((te, SUBLANES, LANES), lambda i: (i, 0, 0)),
        out_shape=jax.ShapeDtypeStruct((n, SUBLANES, LANES), I32),
        compiler_params=_params("parallel"),
        name="pack_experts",
    )(pu, pv)


def _word_hi(w):
    return lax.bitcast_convert_type(w & jnp.int32(-65536), F32)


def _word_lo(w):
    return lax.bitcast_convert_type(lax.shift_left(w, jnp.int32(16)), F32)


def _sublane_sums(a, sub):
    for dist in (4, 2, 1):
        low = (sub & dist) == 0
        half = len(a) // 2
        a = [jnp.where(low, a[i], pltpu.roll(a[i + half], dist, 0))
             + jnp.where(low, pltpu.roll(a[i], SUBLANES - dist, 0), a[i + half])
             for i in range(half)]
    return a[0]


def _peer_pair_math(expert_u, expert_v, x8, gates, store, issue_some):
    sub = lax.broadcasted_iota(I32, (SUBLANES, LANES), 0)
    eye = (lax.broadcasted_iota(I32, (PEER_E, LANES), 0)
           == lax.broadcasted_iota(I32, (PEER_E, LANES), 1))
    ones_rows = jnp.ones((SUBLANES, LANES), BF16)
    ones_sq = jnp.ones((LANES, LANES), BF16)
    ngroup = PEER_E // SUBLANES

    def hidden(a):
        groups = []
        for g in range(ngroup):
            prods = [expert_u(a, g * SUBLANES + r) * x8[a] for r in range(SUBLANES)]
            groups.append(_sublane_sums(prods, sub))
            issue_some(3 - g % 2)
        return jnp.concatenate(groups, axis=0)

    def expert_weights(a, q):
        q_hi = q.astype(BF16)
        q_lo = (q - q_hi.astype(F32)).astype(BF16)
        hid = (lax.dot_general(ones_rows, q_hi, NT_DIMS, preferred_element_type=F32)
               + lax.dot_general(ones_rows, q_lo, NT_DIMS, preferred_element_type=F32))
        issue_some(PEER_MID_ROWS // 2)
        w = _gelu(hid[0:1]) * gates[a]
        wd = jnp.where(eye, jnp.broadcast_to(w, (PEER_E, LANES)), 0.0).astype(BF16)
        wcol = jnp.dot(wd, ones_sq, preferred_element_type=F32)
        issue_some(PEER_MID_ROWS // 2)
        return wcol

    def combine(a, wcol):
        out = jnp.zeros((SUBLANES, LANES), F32)
        for g in range(ngroup):
            for r in range(SUBLANES):
                j = g * SUBLANES + r
                out = out + wcol[j:j + 1, :] * expert_v(a, j)
            issue_some(3 - g % 2)
        store(a, out)

    wcol0 = expert_weights(0, hidden(0))
    wcol1 = expert_weights(1, hidden(1))
    combine(0, wcol0)
    combine(1, wcol1)


def _peer_finish(h1_ref, acc_ref, fg_ref, out_ref):
    hsum = h1_ref[...] + acc_ref[...]
    ms = jnp.mean(hsum * hsum, axis=(1, 2), keepdims=True)
    _tiles_to_rows(hsum * lax.rsqrt(ms + RMS_EPS) * fg_ref[...], out_ref)


def _peerffn_body(tb, idx_ref, gate_ref, c_ref, h1_ref, fg_ref, uv_ref, out_ref, buf_ref,
                  acc_ref, sem_ref):
    def row_copy(t, j, slot):
        return pltpu.make_async_copy(uv_ref.at[idx_ref[t, j]], buf_ref.at[slot, j],
                                     sem_ref.at[slot])

    def issue(t, slot, j0, j1):
        for j in range(j0, j1):
            row_copy(t, j, slot).start(priority=j % 2)

    def wait_all(slot):
        pltpu.make_async_copy(uv_ref.at[pl.ds(0, PEER_E)], buf_ref.at[slot],
                              sem_ref.at[slot]).wait()

    def ring(i):
        base = 2 * (i % PEER_RING)
        return (base, base + 1)

    def pair(i, prefetch):
        ahead = PEER_RING - 1
        toks = (2 * i, 2 * i + 1)
        slots = ring(i)
        nslots = ring(i + ahead)
        todo = [(a, j) for a in range(2) for j in range(PEER_E)]

        def issue_some(n):
            for a, j in todo[:n]:
                if prefetch:
                    row_copy(toks[a] + 2 * ahead, j, nslots[a]).start(priority=j % 2)
            del todo[:n]

        wait_all(slots[0])
        wait_all(slots[1])

        def store(a, out):
            acc_ref[toks[a]] = out

        _peer_pair_math(
            lambda a, j: _word_hi(buf_ref[slots[a], j]),
            lambda a, j: _word_lo(buf_ref[slots[a], j]),
            [c_ref[t] for t in toks], [gate_ref[pl.ds(t, 1), :] for t in toks], store, issue_some)
        assert not todo, "every prefetch row DMA must be issued exactly once"

    npairs = tb // 2
    for i in range(PEER_RING - 1):
        for a, slot in enumerate(ring(i)):
            issue(2 * i + a, slot, 0, PEER_E)

    def body(i, carry):
        pair(i, True)
        return carry

    lax.fori_loop(0, npairs - (PEER_RING - 1), body, 0)
    for i in range(npairs - (PEER_RING - 1), npairs):
        pair(i, False)
    _peer_finish(h1_ref, acc_ref, fg_ref, out_ref)


def _peer_ffn(idx, gate, c3, h13, final_g3, uv_tiles):
    t = h13.shape[0]
    tb = 256
    row3 = pl.BlockSpec((tb, SUBLANES, LANES), lambda i: (i, 0, 0))
    return pl.pallas_call(
        functools.partial(_peerffn_body, tb),
        grid=(t // tb,),
        in_specs=[pl.BlockSpec((tb, PEER_E), lambda i: (i, 0), memory_space=pltpu.SMEM),
                  pl.BlockSpec((tb, PEER_E), lambda i: (i, 0)), row3, row3,
                  pl.BlockSpec((1, SUBLANES, LANES), lambda i: (0, 0, 0)),
                  pl.BlockSpec(memory_space=pl.ANY)],
        out_specs=pl.BlockSpec((tb, SUBLANES * LANES), lambda i: (i, 0)),
        out_shape=jax.ShapeDtypeStruct((t, SUBLANES * LANES), F32),
        scratch_shapes=[pltpu.VMEM((2 * PEER_RING, PEER_E, SUBLANES, LANES), I32),
                        pltpu.VMEM((tb, SUBLANES, LANES), F32),
                        pltpu.SemaphoreType.DMA((2 * PEER_RING,))],
        compiler_params=_params("arbitrary"),
        name="peer_ffn",
    )(idx, gate, c3, h13, final_g3, uv_tiles)


def _overlap_table(seq):
    ci = jnp.arange(seq // CMP_STRIDE)[None, :] * CMP_STRIDE
    sj = jnp.arange(SEL_BLOCK)[:, None] * SEL_BLOCK
    return ((ci < sj + SEL_BLOCK) & (ci + CMP_BLOCK > sj)).astype(F32)


def _cmp_blocks(kv):
    b, g, s, dh = kv.shape
    r = kv.reshape(b * g, s // CMP_STRIDE, CMP_STRIDE * dh)
    return jnp.concatenate([r, jnp.roll(r, -1, axis=1)], axis=-1)


def _pad_cmp_params(pe, w1):
    pe_p = jnp.pad(pe, ((0, 0), (0, LANES - HEAD))).reshape(1, -1)
    w1_p = jnp.pad(w1.reshape(CMP_BLOCK, HEAD, -1), ((0, 0), (0, LANES - HEAD), (0, 0)))
    return pe_p, w1_p.reshape(CMP_BLOCK * LANES, -1).astype(BF16)


def _layer(h, lidx, attn_norm, w_in, lq1, lk1, lq2, lk2, subln, pe_k, pe_v, w1k, w1v, w2k, w2v,
           p_da, p_nsa, w_o, ffn_norm, wq, k1, k2, pu, pv, out_norm):
    b, s, d = h.shape
    t = b * s
    g, hg = NSA_GROUPS, NSA_HG
    lambda_init = 0.8 - 0.6 * math.exp(-0.3 * lidx)
    x2 = h.reshape(t, d)

    qda, kda, vda, qn, cmp_in, ksx, kwx, vsw, gm, gn = _in_proj(
        x2, attn_norm.reshape(1, d), _pack_w_in(w_in), _rope_tables(s), s)
    tr = lambda a2: jnp.swapaxes(a2.reshape(b, s, -1), 1, 2)
    ydat = _diff_attn(tr(qda), kda.reshape(b, s, -1), tr(vda),
                      lq1.reshape(1, -1), lk1.reshape(1, -1), lq2.reshape(1, -1),
                      lk2.reshape(1, -1), subln.reshape(-1, 1), lambda_init)
    yda = jnp.swapaxes(ydat, 1, 2).reshape(t, -1)

    assert s // SEL_BLOCK <= SEL_BLOCK, "selection bias rows hold at most 64 blocks"
    cmp4 = jnp.swapaxes(cmp_in.reshape(b, s, 2 * g, LANES), 1, 2)
    pe_kp, w1_kp = _pad_cmp_params(pe_k, w1k)
    pe_vp, w1_vp = _pad_cmp_params(pe_v, w1v)
    kca, vc = _compress(_cmp_blocks(cmp4[:, :g]), _cmp_blocks(cmp4[:, g:]), pe_kp, pe_vp,
                        w1_kp, w1_vp, w2k.astype(BF16), w2v.astype(BF16))
    ncp = s // CMP_STRIDE
    kca = kca.reshape(b, g, ncp, LANES)
    vct = jnp.swapaxes(vc.reshape(b, g, ncp, HEAD), 2, 3)

    qnt = tr(qn)
    ocmp_t, bias_t = _cmp_select(qnt, kca, vct, _overlap_table(s))
    onehot = (jnp.arange(s)[:, None] // SEL_BLOCK == jnp.arange(HEAD)[None, :]).astype(BF16)
    pad_hot = jnp.concatenate([jnp.zeros_like(onehot), onehot] * g, axis=1)
    ksa = ksx.reshape(b, s, -1) + pad_hot[None]
    ynt = _sel_win(qnt, bias_t, ksa, kwx.reshape(b, s, -1), tr(vsw), ocmp_t, tr(gn))
    yn = jnp.swapaxes(ynt, 1, 2).reshape(t, -1)

    h1_tiles, c_tiles, s1t, s2t = _out_proj(
        yda, yn, gm, x2, p_da.astype(BF16), p_nsa.astype(BF16), w_o.astype(BF16),
        ffn_norm.reshape(1, d), wq.astype(BF16), k1.astype(BF16), k2.astype(BF16))
    idx_t, gate_t = _peer_topk(s1t, s2t)
    idx = idx_t.reshape(PEER_E, t).T
    gate = gate_t.reshape(PEER_E, t).T
    out = _peer_ffn(idx, gate, c_tiles, h1_tiles, out_norm.reshape(1, SUBLANES, LANES),
                    _pack_expert_rows(pu, pv))
    return out.reshape(b, s, d)


def kernel(x, attn_norm, w_in, da_lambda_q1, da_lambda_k1, da_lambda_q2, da_lambda_k2, da_subln,
           cmp_pe_k, cmp_pe_v, cmp_w1_k, cmp_w1_v, cmp_w2_k, cmp_w2_v, p_da, p_nsa, w_o,
           ffn_norm, peer_wq, peer_k1, peer_k2, peer_u, peer_v, final_norm):
    depth = attn_norm.shape[0]
    assert depth == 1, "the final norm is fused into the last layer's PEER kernel"
    h = x
    for l in range(depth):
        h = _layer(h, l, attn_norm[l], w_in[l], da_lambda_q1[l], da_lambda_k1[l], da_lambda_q2[l],
                   da_lambda_k2[l], da_subln[l], cmp_pe_k[l], cmp_pe_v[l], cmp_w1_k[l],
                   cmp_w1_v[l], cmp_w2_k[l], cmp_w2_v[l], p_da[l], p_nsa[l], w_o[l], ffn_norm[l],
                   peer_wq[l], peer_k1[l], peer_k2[l], peer_u[l], peer_v[l], final_norm)
    return h
```

```python
import functools
import math

import jax
import jax.numpy as jnp
from jax import lax
from jax.experimental import pallas as pl
from jax.experimental.pallas import tpu as pltpu

F32 = jnp.float32
BF16 = jnp.bfloat16
I32 = jnp.int32

RMS_EPS = 1e-6
ROPE_THETA = 500000.0
ROPE_HALF = 8
HEAD = 64
DA_HEADS = 4
NSA_GROUPS = 2
NSA_HG = 4
CMP_STRIDE = 16
CMP_BLOCK = 32
SEL_BLOCK = 64
SEL_TOPK = 16
WINDOW = 512
FORCE_BONUS = 1e4
NEG_BIG = -1e30
SEL_MASK_BIAS = -32768.0
PEER_HEADS = 8
PEER_NKEYS = 128
PEER_TOPK = 16
LANES = 128
SUBLANES = 8
VMEM_LIMIT = 56 * 1024 * 1024

NT_DIMS = (((1,), (1,)), ((), ()))


def _rmsnorm(x, g):
    return x * lax.rsqrt(jnp.mean(x * x, axis=-1, keepdims=True) + RMS_EPS) * g


def _sigmoid(z):
    return 1.0 / (1.0 + jnp.exp(-z))


def _gelu(z):
    return 0.5 * z * (1.0 + lax.erf(z * (2.0 ** -0.5)))


def _params(*sem):
    return pltpu.CompilerParams(dimension_semantics=sem, vmem_limit_bytes=VMEM_LIMIT)


def _sublane_transpose(v):
    sub = lax.broadcasted_iota(I32, (SUBLANES, LANES), 0)
    v = list(v)
    for dist in (4, 2, 1):
        low = (sub & dist) == 0
        nxt = list(v)
        for i in range(SUBLANES):
            if i & dist == 0:
                nxt[i] = jnp.where(low, v[i], pltpu.roll(v[i + dist], dist, 0))
                nxt[i + dist] = jnp.where(low, pltpu.roll(v[i], SUBLANES - dist, 0), v[i + dist])
        v = nxt
    return v


def _rows_to_tiles(x, tile_ref):
    for g in range(x.shape[0] // SUBLANES):
        rows = slice(g * SUBLANES, (g + 1) * SUBLANES)
        tiles = _sublane_transpose([x[rows, c * LANES:(c + 1) * LANES] for c in range(SUBLANES)])
        for e in range(SUBLANES):
            tile_ref[g * SUBLANES + e] = tiles[e]


def _tiles_to_rows(t, row_ref):
    for g in range(t.shape[0] // SUBLANES):
        chunks = _sublane_transpose([t[g * SUBLANES + e] for e in range(SUBLANES)])
        for c in range(SUBLANES):
            row_ref[g * SUBLANES:(g + 1) * SUBLANES, c * LANES:(c + 1) * LANES] = chunks[c]


_QDA0, _KDA0, _VDA0, _QN0 = 0, 512, 1024, 1536
_KC0, _VC0, _KS0, _KW0, _VSW0, _GM0, _GN0, _WCOLS = 2048, 2304, 2560, 2816, 3072, 3328, 5376, 5632


def _inproj_body(x_ref, g_ref, w_ref, rc_ref, rs1_ref, rs2_ref,
                 qda_ref, kda_ref, vda_ref, qn_ref, cmp_ref, ks_ref, kw_ref, vsw_ref, gm_ref, gn_ref):
    a = _rmsnorm(x_ref[...], g_ref[...]).astype(BF16)
    rc, rs1, rs2 = rc_ref[...], rs1_ref[...], rs2_ref[...]

    def rope(z):
        return (z * rc + pltpu.roll(z, ROPE_HALF, 1) * rs1
                + pltpu.roll(z, LANES - ROPE_HALF, 1) * rs2)

    def proj(c0):
        return jnp.dot(a, w_ref[:, c0:c0 + 256], preferred_element_type=F32)

    def rope2(z):
        return jnp.concatenate([rope(z[:, :LANES]), rope(z[:, LANES:])], axis=1)

    for c in range(2):
        qda_ref[:, c * 256:(c + 1) * 256] = (rope2(proj(_QDA0 + c * 256)) * 0.125).astype(BF16)
        kda_ref[:, c * 256:(c + 1) * 256] = rope2(proj(_KDA0 + c * 256)).astype(BF16)
        vda_ref[:, c * 256:(c + 1) * 256] = proj(_VDA0 + c * 256).astype(BF16)
        qn_ref[:, c * 256:(c + 1) * 256] = (rope2(proj(_QN0 + c * 256)) * 0.125).astype(BF16)
    cmp_ref[:, 0:256] = rope2(proj(_KC0)).astype(BF16)
    cmp_ref[:, 256:512] = proj(_VC0).astype(BF16)
    ks_ref[...] = rope2(proj(_KS0)).astype(BF16)
    kw_ref[...] = rope2(proj(_KW0)).astype(BF16)
    vsw_ref[...] = proj(_VSW0).astype(BF16)
    for c in range(8):
        gm_ref[:, c * 256:(c + 1) * 256] = _sigmoid(proj(_GM0 + c * 256)).astype(BF16)
    gn_ref[...] = _sigmoid(proj(_GN0))


def _pack_w_in(w):
    d = w.shape[0]
    zeros = lambda n: jnp.zeros((d, n), w.dtype)

    def spread_groups(c0):
        return [w[:, c0:c0 + HEAD], zeros(HEAD), w[:, c0 + HEAD:c0 + 2 * HEAD], zeros(HEAD)]

    kc, vc, ks, vs, kw, vw = (2048 + 128 * i for i in range(6))
    gn = w[:, 2816:2840]
    cols = ([w[:, :2048]] + spread_groups(kc) + spread_groups(vc) + spread_groups(ks)
            + spread_groups(kw) + [w[:, vs:vs + 128], w[:, vw:vw + 128], w[:, 2840:],
                                   gn[:, :12], zeros(LANES - 12), gn[:, 12:], zeros(LANES - 12)])
    packed = jnp.concatenate(cols, axis=1).astype(BF16)
    assert packed.shape[1] == _WCOLS
    return packed


def _rope_tables(seq):
    inv = jnp.power(ROPE_THETA, -jnp.arange(ROPE_HALF, dtype=F32) * 2.0 / (2 * ROPE_HALF))
    ang = jnp.arange(seq, dtype=F32)[:, None] * inv[None, :]
    cos, sin = jnp.cos(ang), jnp.sin(ang)
    one = jnp.ones((seq, HEAD - 2 * ROPE_HALF), F32)
    zero8 = jnp.zeros((seq, ROPE_HALF), F32)
    zero48 = jnp.zeros_like(one)
    rc = jnp.concatenate([cos, cos, one], axis=1)
    rs1 = jnp.concatenate([zero8, sin, zero48], axis=1)
    rs2 = jnp.concatenate([-sin, zero8, zero48], axis=1)
    return tuple(jnp.concatenate([t, t], axis=1) for t in (rc, rs1, rs2))


def _in_proj(x2, norm_g, w_packed, rope_tabs, seq):
    t, d = x2.shape
    tm = 512
    nseq = seq // tm
    row = lambda i: (i, 0)
    fixed = lambda i: (0, 0)
    out_shapes = [
        jax.ShapeDtypeStruct((t, 512), BF16), jax.ShapeDtypeStruct((t, 512), BF16),
        jax.ShapeDtypeStruct((t, 512), BF16), jax.ShapeDtypeStruct((t, 512), BF16),
        jax.ShapeDtypeStruct((t, 512), BF16), jax.ShapeDtypeStruct((t, 256), BF16),
        jax.ShapeDtypeStruct((t, 256), BF16), jax.ShapeDtypeStruct((t, 256), BF16),
        jax.ShapeDtypeStruct((t, 2048), BF16), jax.ShapeDtypeStruct((t, 256), F32)]
    rope_spec = pl.BlockSpec((tm, LANES), lambda i: (i % nseq, 0))
    return pl.pallas_call(
        _inproj_body,
        grid=(t // tm,),
        in_specs=[pl.BlockSpec((tm, d), row), pl.BlockSpec((1, d), fixed),
                  pl.BlockSpec((d, _WCOLS), fixed), rope_spec, rope_spec, rope_spec],
        out_specs=[pl.BlockSpec((tm, s.shape[1]), row) for s in out_shapes],
        out_shape=out_shapes,
        compiler_params=_params("parallel"),
        name="in_proj",
    )(x2, norm_g, w_packed, *rope_tabs)


def _softmax_step(s, vt, m_ref, l_ref, acc_ref):
    m_prev = m_ref[...]
    m_new = jnp.maximum(m_prev, jnp.max(s, axis=0, keepdims=True))
    alpha = jnp.exp(m_prev - m_new)
    p = jnp.exp(s - m_new)
    l_ref[...] = alpha * l_ref[...] + jnp.sum(p, axis=0, keepdims=True)
    acc_ref[...] = alpha * acc_ref[...] + jnp.dot(vt, p.astype(BF16), preferred_element_type=F32)
    m_ref[...] = m_new


def _softmax_reset(m_ref, l_ref, acc_ref):
    m_ref[...] = jnp.full(m_ref.shape, NEG_BIG, F32)
    l_ref[...] = jnp.zeros(l_ref.shape, F32)
    acc_ref[...] = jnp.zeros(acc_ref.shape, F32)


def _attend_tiles(n_full, scores, values, mask_last, sa_ref, sb_ref, m_ref, l_ref, acc_ref):
    step = lambda s, t: _softmax_step(s, values(t), m_ref, l_ref, acc_ref)
    sa_ref[...] = scores(0)

    def two_tiles(i, carry):
        t = 2 * i
        sb_ref[...] = scores(t + 1)
        step(sa_ref[...], t)
        sa_ref[...] = scores(t + 2)
        step(sb_ref[...], t + 1)
        return carry

    lax.fori_loop(0, n_full // 2, two_tiles, 0)
    odd = (n_full & 1) == 1

    @pl.when(odd)
    def _():
        sb_ref[...] = scores(n_full)
        step(sa_ref[...], n_full - 1)
        step(mask_last(sb_ref[...]), n_full)

    @pl.when(jnp.logical_not(odd))
    def _():
        step(mask_last(sa_ref[...]), n_full)


def _key_tile(ref, kt, bk):
    return ref[(0,) * (len(ref.shape) - 2) + (pl.ds(pl.multiple_of(kt * bk, bk), bk), slice(None))]


def _value_tile(ref, kt, bk):
    return ref[(0,) * (len(ref.shape) - 2) + (slice(None), pl.ds(pl.multiple_of(kt * bk, bk), bk))]


def _diffattn_body(lambda_init, bq, q_ref, k_ref, v_ref, lq1_ref, lk1_ref, lq2_ref, lk2_ref,
                   sub_ref, y_ref, qbd_ref, m_ref, l_ref, acc_ref, sa_ref, sb_ref):
    qi = pl.program_id(2)
    bk = bq
    qt = q_ref[0]
    sub = lax.broadcasted_iota(I32, qt.shape, 0)
    zero = jnp.zeros_like(qt)
    qbd_ref[:, 0:bq] = jnp.where(sub < HEAD, qt, zero)
    qbd_ref[:, bq:2 * bq] = jnp.where(sub >= HEAD, qt, zero)
    _softmax_reset(m_ref, l_ref, acc_ref)

    def scores(kt):
        return jnp.dot(_key_tile(k_ref, kt, bk), qbd_ref[...], preferred_element_type=F32)

    def causal(s):
        r = lax.broadcasted_iota(I32, (bk, 2 * bq), 0)
        c = lax.broadcasted_iota(I32, (bk, 2 * bq), 1) & (bq - 1)
        return jnp.where(r <= c, s, NEG_BIG)

    _attend_tiles(qi, scores, lambda kt: _value_tile(v_ref, kt, bk), causal,
                  sa_ref, sb_ref, m_ref, l_ref, acc_ref)

    o = acc_ref[...] / l_ref[...]
    lam = (jnp.exp(jnp.sum(lq1_ref[...] * lk1_ref[...], axis=1, keepdims=True))
           - jnp.exp(jnp.sum(lq2_ref[...] * lk2_ref[...], axis=1, keepdims=True)) + lambda_init)
    d = o[:, 0:bq] - lam * o[:, bq:2 * bq]
    ms = jnp.mean(d * d, axis=0, keepdims=True)
    y = d * lax.rsqrt(ms + RMS_EPS) * sub_ref[...] * (1.0 - lambda_init)
    y_ref[0] = y.astype(BF16)


def _diff_attn(qdat, kda, vdat, lq1, lk1, lq2, lk2, subln_col, lambda_init):
    b, s, _ = kda.shape
    bq = 512
    vec = lambda n: pl.BlockSpec((1, n), lambda bi, h, qi: (0, 0))
    dv = 2 * HEAD
    qtile = pl.BlockSpec((1, dv, bq), lambda bi, h, qi: (bi, h, qi))
    return pl.pallas_call(
        functools.partial(_diffattn_body, lambda_init, bq),
        grid=(b, DA_HEADS, s // bq),
        in_specs=[qtile,
                  pl.BlockSpec((1, s, LANES), lambda bi, h, qi: (bi, 0, h)),
                  pl.BlockSpec((1, dv, s), lambda bi, h, qi: (bi, h, 0)),
                  vec(HEAD), vec(HEAD), vec(HEAD), vec(HEAD),
                  pl.BlockSpec((dv, 1), lambda bi, h, qi: (0, 0))],
        out_specs=qtile,
        out_shape=jax.ShapeDtypeStruct((b, DA_HEADS * dv, s), BF16),
        scratch_shapes=[pltpu.VMEM((LANES, 2 * bq), BF16), pltpu.VMEM((1, 2 * bq), F32),
                        pltpu.VMEM((1, 2 * bq), F32), pltpu.VMEM((dv, 2 * bq), F32),
                        pltpu.VMEM((bq, 2 * bq), F32), pltpu.VMEM((bq, 2 * bq), F32)],
        compiler_params=_params("parallel", "parallel", "parallel"),
        name="diff_attn",
    )(qdat, kda, vdat, lq1, lk1, lq2, lk2, subln_col)


def _compress_body(xk_ref, xv_ref, pek_ref, pev_ref, w1k_ref, w1v_ref, w2k_ref, w2v_ref,
                   kc_ref, vc_ref):
    def mlp(x_ref, pe_ref, w1_ref, w2_ref):
        blocks = (x_ref[0].astype(F32) + pe_ref[...]).astype(BF16)
        hid = _gelu(jnp.dot(blocks, w1_ref[...], preferred_element_type=F32))
        return jnp.dot(hid.astype(BF16), w2_ref[...], preferred_element_type=F32)

    kc = mlp(xk_ref, pek_ref, w1k_ref, w2k_ref)
    kc_ref[0] = jnp.concatenate([kc, jnp.zeros_like(kc)], axis=1).astype(BF16)
    vc_ref[0] = mlp(xv_ref, pev_ref, w1v_ref, w2v_ref).astype(BF16)


def _compress(xk, xv, pe_k, pe_v, w1k, w1v, w2k, w2v):
    n, ncp, width = xk.shape
    blk = pl.BlockSpec((1, ncp, width), lambda i: (i, 0, 0))
    fixed = lambda shape: pl.BlockSpec(shape, lambda i: (0, 0))
    return pl.pallas_call(
        _compress_body,
        grid=(n,),
        in_specs=[blk, blk, fixed((1, width)), fixed((1, width)), fixed((width, HEAD)),
                  fixed((width, HEAD)), fixed((HEAD, HEAD)), fixed((HEAD, HEAD))],
        out_specs=[pl.BlockSpec((1, ncp, LANES), lambda i: (i, 0, 0)),
                   pl.BlockSpec((1, ncp, HEAD), lambda i: (i, 0, 0))],
        out_shape=[jax.ShapeDtypeStruct((n, ncp, LANES), BF16),
                   jax.ShapeDtypeStruct((n, ncp, HEAD), BF16)],
        compiler_params=_params("parallel"),
        name="compress",
    )(xk, xv, pe_k, pe_v, w1k, w1v, w2k, w2v)


def _heads_on_lanes(qt, bq):
    return jnp.concatenate([qt[h * HEAD:(h + 1) * HEAD, :] for h in range(NSA_HG)], axis=1)


def _cmpsel_body(bq, q_ref, kc_ref, vc_ref, ovl_ref, ocmp_ref, bias_ref):
    qi = pl.program_id(2)
    ncp = kc_ref.shape[2]
    rows = NSA_HG * bq
    q2 = _heads_on_lanes(q_ref[0], bq)
    qz = jnp.concatenate([q2, jnp.zeros_like(q2)], axis=0)
    s = jnp.dot(kc_ref[0, 0], qz, preferred_element_type=F32)
    n = lax.broadcasted_iota(I32, (ncp, rows), 0)
    qpos = qi * bq + (lax.broadcasted_iota(I32, (ncp, rows), 1) & (bq - 1))
    cmask = n * CMP_STRIDE + (CMP_BLOCK - 1) <= qpos
    s = jnp.where(cmask, s, NEG_BIG)
    e = jnp.exp(s - jnp.max(s, axis=0, keepdims=True))
    p = jnp.where(cmask, e / jnp.sum(e, axis=0, keepdims=True), 0.0)
    o = jnp.dot(vc_ref[0, 0], p.astype(BF16), preferred_element_type=F32)
    for h in range(NSA_HG):
        ocmp_ref[0, 0, h * HEAD:(h + 1) * HEAD, :] = o[:, h * bq:(h + 1) * bq]

    psum = p[:, 0:bq] + p[:, bq:2 * bq] + p[:, 2 * bq:3 * bq] + p[:, 3 * bq:4 * bq]
    imp = jnp.dot(ovl_ref[...], psum, preferred_element_type=F32)
    blk = lax.broadcasted_iota(I32, (SEL_BLOCK, bq), 0)
    pos = qi * bq + lax.broadcasted_iota(I32, (SEL_BLOCK, bq), 1)
    cur = lax.shift_right_logical(pos, 6)
    valid = blk <= cur
    forced = (blk == 0) | (blk == cur) | (blk == cur - 1)
    score = jnp.where(valid, imp + jnp.where(forced, FORCE_BONUS, 0.0), -jnp.inf)
    rank = jnp.zeros((SEL_BLOCK, bq), I32)
    for i in range(SEL_BLOCK):
        other = score[i:i + 1, :]
        beats = (other > score) | ((other == score) & (blk > i))
        rank = rank + beats.astype(I32)
    keep = valid & (rank < SEL_TOPK)
    bias_ref[0, 0] = jnp.where(keep, 0.0, SEL_MASK_BIAS).astype(BF16)


def _cmp_select(qnt, kca, vct, overlap_t):
    b, g, ncp, _ = kca.shape
    s = qnt.shape[2]
    bq = 128
    dq = NSA_HG * HEAD
    return pl.pallas_call(
        functools.partial(_cmpsel_body, bq),
        grid=(b, g, s // bq),
        in_specs=[pl.BlockSpec((1, dq, bq), lambda bi, gi, qi: (bi, gi, qi)),
                  pl.BlockSpec((1, 1, ncp, LANES), lambda bi, gi, qi: (bi, gi, 0, 0)),
                  pl.BlockSpec((1, 1, HEAD, ncp), lambda bi, gi, qi: (bi, gi, 0, 0)),
                  pl.BlockSpec((SEL_BLOCK, ncp), lambda bi, gi, qi: (0, 0))],
        out_specs=[pl.BlockSpec((1, 1, dq, bq), lambda bi, gi, qi: (bi, gi, 0, qi)),
                   pl.BlockSpec((1, 1, SEL_BLOCK, bq), lambda bi, gi, qi: (bi, gi, 0, qi))],
        out_shape=[jax.ShapeDtypeStruct((b, g, dq, s), F32),
                   jax.ShapeDtypeStruct((b, g, SEL_BLOCK, s), BF16)],
        compiler_params=_params("parallel", "parallel", "parallel"),
        name="cmp_select",
    )(qnt, kca, vct, overlap_t)


def _selwin_body(bq, bks, q_ref, bias_ref, ks_ref, vs_ref, kw_ref, vw_ref, ocmp_ref, gate_ref,
                 y_ref, m_ref, l_ref, acc_ref, qa_ref, sa_ref, sb_ref):
    qi = pl.program_id(2)
    rows = NSA_HG * bq
    q2 = _heads_on_lanes(q_ref[0], bq)
    bias = bias_ref[0, 0]
    qa = jnp.concatenate([q2, jnp.concatenate([bias] * NSA_HG, axis=1)], axis=0)
    qw = jnp.concatenate([q2, jnp.zeros_like(q2)], axis=0)
    r = lax.broadcasted_iota(I32, (bq, rows), 0)
    c = lax.broadcasted_iota(I32, (bq, rows), 1) & (bq - 1)

    _softmax_reset(m_ref, l_ref, acc_ref)

    qa_ref[...] = qa
    last = (qi * bq) // bks

    def causal(s):
        kpos = last * bks + lax.broadcasted_iota(I32, (bks, rows), 0)
        qpos = qi * bq + (lax.broadcasted_iota(I32, (bks, rows), 1) & (bq - 1))
        return jnp.where(kpos <= qpos, s, NEG_BIG)

    _attend_tiles(last,
                  lambda kt: jnp.dot(_key_tile(ks_ref, kt, bks), qa_ref[...],
                                     preferred_element_type=F32),
                  lambda kt: _value_tile(vs_ref, kt, bks), causal,
                  sa_ref, sb_ref, m_ref, l_ref, acc_ref)
    o_sel = acc_ref[...] / l_ref[...]

    _softmax_reset(m_ref, l_ref, acc_ref)
    backs = list(range(WINDOW // bq, -1, -1))
    tiles = [jnp.maximum(qi - back, 0) for back in backs]
    raw = [jnp.dot(_key_tile(kw_ref, kt, bq), qw, preferred_element_type=F32) for kt in tiles]
    for back, kt, s in zip(backs, tiles, raw):
        dist = c + back * bq - r
        inside = (dist >= 0) & (dist < WINDOW) & (qi >= back)
        _softmax_step(jnp.where(inside, s, NEG_BIG), _value_tile(vw_ref, kt, bq),
                      m_ref, l_ref, acc_ref)

    o_win = acc_ref[...] / l_ref[...]
    gate = gate_ref[0]
    for h in range(NSA_HG):
        sl = slice(h * bq, (h + 1) * bq)
        y = (gate[3 * h:3 * h + 1, :] * ocmp_ref[0, 0, h * HEAD:(h + 1) * HEAD, :]
             + gate[3 * h + 1:3 * h + 2, :] * o_sel[:, sl]
             + gate[3 * h + 2:3 * h + 3, :] * o_win[:, sl])
        y_ref[0, h * HEAD:(h + 1) * HEAD, :] = y.astype(BF16)


def _sel_win(qnt, bias_t, ksa, kwa, vswt, ocmp_t, gnt):
    b, s, _ = ksa.shape
    g = NSA_GROUPS
    bq, bks = 256, 512
    dq = NSA_HG * HEAD
    rows = NSA_HG * bq
    keys = pl.BlockSpec((1, s, LANES), lambda bi, gi, qi: (bi, 0, gi))
    vals = lambda first: pl.BlockSpec((1, HEAD, s), lambda bi, gi, qi: (bi, first + gi, 0))
    qtile = pl.BlockSpec((1, dq, bq), lambda bi, gi, qi: (bi, gi, qi))
    return pl.pallas_call(
        functools.partial(_selwin_body, bq, bks),
        grid=(b, g, s // bq),
        in_specs=[qtile,
                  pl.BlockSpec((1, 1, SEL_BLOCK, bq), lambda bi, gi, qi: (bi, gi, 0, qi)),
                  keys, vals(0), keys, vals(g),
                  pl.BlockSpec((1, 1, dq, bq), lambda bi, gi, qi: (bi, gi, 0, qi)),
                  pl.BlockSpec((1, LANES, bq), lambda bi, gi, qi: (bi, gi, qi))],
        out_specs=qtile,
        out_shape=jax.ShapeDtypeStruct((b, g * dq, s), BF16),
        scratch_shapes=[pltpu.VMEM((1, rows), F32), pltpu.VMEM((1, rows), F32),
                        pltpu.VMEM((HEAD, rows), F32), pltpu.VMEM((LANES, rows), BF16),
                        pltpu.VMEM((bks, rows), F32), pltpu.VMEM((bks, rows), F32)],
        compiler_params=_params("parallel", "parallel", "parallel"),
        name="sel_win",
    )(qnt, bias_t, ksa, vswt, kwa, vswt, ocmp_t, gnt)


def _outproj_body(yda_ref, yn_ref, gm_ref, x_ref, pda_ref, pnsa_ref, wo_ref, fg_ref, wq_ref,
                  k1_ref, k2_ref, h1_ref, c_ref, s1_ref, s2_ref):
    d = x_ref.shape[1]
    a = jnp.dot(yda_ref[...], pda_ref[...], preferred_element_type=F32)
    bn = jnp.dot(yn_ref[...], pnsa_ref[...], preferred_element_type=F32)
    merged = gm_ref[:, :d].astype(F32) * a + gm_ref[:, d:].astype(F32) * bn
    h1 = x_ref[...] + jnp.dot(merged.astype(BF16), wo_ref[...], preferred_element_type=F32)
    c = _rmsnorm(h1, fg_ref[...])
    _rows_to_tiles(h1, h1_ref)
    _rows_to_tiles(c, c_ref)
    cb = c.astype(BF16)
    for h in range(PEER_HEADS):
        qh = jnp.dot(cb, wq_ref[:, h * 256:(h + 1) * 256], preferred_element_type=F32).astype(BF16)
        s1_ref[h] = lax.dot_general(k1_ref[...], qh[:, :LANES], NT_DIMS, preferred_element_type=F32)
        s2_ref[h] = lax.dot_general(k2_ref[...], qh[:, LANES:], NT_DIMS, preferred_element_type=F32)


def _out_proj(yda, yn, gm, x2, pda, pnsa, wo, ffn_g, wq, k1, k2):
    t, d = x2.shape
    tm = 512
    row = lambda w: pl.BlockSpec((tm, w), lambda i: (i, 0))
    fixed = lambda a: pl.BlockSpec(a.shape, lambda i: (0, 0))
    sspec = pl.BlockSpec((PEER_HEADS, PEER_NKEYS, tm), lambda i: (0, 0, i))
    assert d == SUBLANES * LANES
    tiles = pl.BlockSpec((tm, SUBLANES, LANES), lambda i: (i, 0, 0))
    return pl.pallas_call(
        _outproj_body,
        grid=(t // tm,),
        in_specs=[row(512), row(512), row(2 * d), row(d), fixed(pda), fixed(pnsa), fixed(wo),
                  fixed(ffn_g), fixed(wq), fixed(k1), fixed(k2)],
        out_specs=[tiles, tiles, sspec, sspec],
        out_shape=[jax.ShapeDtypeStruct((t, SUBLANES, LANES), F32),
                   jax.ShapeDtypeStruct((t, SUBLANES, LANES), F32),
                   jax.ShapeDtypeStruct((PEER_HEADS, PEER_NKEYS, t), F32),
                   jax.ShapeDtypeStruct((PEER_HEADS, PEER_NKEYS, t), F32)],
        compiler_params=_params("parallel"),
        name="out_proj",
    )(yda, yn, gm, x2, pda, pnsa, wo, ffn_g, wq, k1, k2)


def _batcher_pairs(n):
    pairs = []

    def merge(lo, hi, r):
        step = r * 2
        if step < hi - lo:
            merge(lo, hi, step)
            merge(lo + r, hi, step)
            pairs.extend((i, i + r) for i in range(lo + r, hi - r, step))
        else:
            pairs.append((lo, lo + r))

    def sort(lo, hi):
        if hi - lo >= 1:
            mid = lo + (hi - lo) // 2
            sort(lo, mid)
            sort(mid + 1, hi)
            merge(lo, hi, 1)

    sort(0, n - 1)
    return pairs


_NET16 = _batcher_pairs(PEER_TOPK)


def _cmpx(a, b):
    c = a[0] >= b[0]
    return ((jnp.where(c, a[0], b[0]), jnp.where(c, a[1], b[1])),
            (jnp.where(c, b[0], a[0]), jnp.where(c, b[1], a[1])))


def _sort_lists(lists, n_real):
    lists = list(lists)
    for i, j in _NET16:
        if j < n_real:
            lists[i], lists[j] = _cmpx(lists[i], lists[j])
    return lists


def _merge_sublanes(lists):
    k = PEER_TOPK
    for dist in (4, 2, 1):
        other = [(pltpu.roll(v, dist, 0), pltpu.roll(ix, dist, 0)) for v, ix in lists]
        lists = [_cmpx(lists[i], other[k - 1 - i])[0] for i in range(k)]
        step = k // 2
        while step >= 1:
            for i in range(k):
                if i & step == 0:
                    lists[i], lists[i + step] = _cmpx(lists[i], lists[i + step])
            step //= 2
    return lists


def _spread(lists, off, sub):
    v, ix = lists[off]
    for r in range(1, SUBLANES):
        sel = sub == r
        v = jnp.where(sel, lists[off + r][0], v)
        ix = jnp.where(sel, lists[off + r][1], ix)
    return v, ix


def _peertopk_body(tt, s1_ref, s2_ref, idx_ref, gate_ref):
    sub = lax.broadcasted_iota(I32, (SUBLANES, LANES), 0)
    groups = PEER_NKEYS // SUBLANES

    def top16(ref, h, lanes):
        lists = [(ref[h, v * SUBLANES:(v + 1) * SUBLANES, lanes], sub + v * SUBLANES)
                 for v in range(groups)]
        return _merge_sublanes(_sort_lists(lists, groups))

    def unit(u, carry):
        h = u // (tt // LANES)
        lanes = pl.ds(pl.multiple_of((u % (tt // LANES)) * LANES, LANES), LANES)
        l1 = top16(s1_ref, h, lanes)
        l2 = top16(s2_ref, h, lanes)
        v2lo, v2hi, v1hi = _spread(l2, 0, sub), _spread(l2, SUBLANES, sub), _spread(l1, SUBLANES, sub)
        cands = [(l1[a][0] + v2lo[0], l1[a][1] * PEER_NKEYS + v2lo[1]) for a in range(SUBLANES)]
        cands.append((l1[0][0] + v2hi[0], l1[0][1] * PEER_NKEYS + v2hi[1]))
        cands.append((v1hi[0] + l2[0][0], v1hi[1] * PEER_NKEYS + l2[0][1]))
        n_real = len(cands)
        filler = (jnp.full((SUBLANES, LANES), -jnp.inf, F32), jnp.zeros((SUBLANES, LANES), I32))
        cands += [filler] * (PEER_TOPK - n_real)
        best = _merge_sublanes(_sort_lists(cands, n_real))
        ex = [jnp.exp(v - best[0][0]) for v, _ in best]
        z = ex[0]
        for e in ex[1:]:
            z = z + e
        gl = [(e / z, ix) for e, (_, ix) in zip(ex, best)]
        lo, hi = _spread(gl, 0, sub), _spread(gl, SUBLANES, sub)
        gate_ref[h, :, lanes] = jnp.concatenate([lo[0], hi[0]], axis=0)
        idx_ref[h, :, lanes] = jnp.concatenate([lo[1], hi[1]], axis=0)
        return carry

    lax.fori_loop(0, PEER_HEADS * (tt // LANES), unit, 0)


def _peer_topk(s1t, s2t):
    _, _, t = s1t.shape
    tt = 512
    spec_in = pl.BlockSpec((PEER_HEADS, PEER_NKEYS, tt), lambda i: (0, 0, i))
    spec_out = pl.BlockSpec((PEER_HEADS, PEER_TOPK, tt), lambda i: (0, 0, i))
    return pl.pallas_call(
        functools.partial(_peertopk_body, tt),
        grid=(t // tt,),
        in_specs=[spec_in, spec_in],
        out_specs=[spec_out, spec_out],
        out_shape=[jax.ShapeDtypeStruct((PEER_HEADS, PEER_TOPK, t), I32),
                   jax.ShapeDtypeStruct((PEER_HEADS, PEER_TOPK, t), F32)],
        compiler_params=_params("parallel"),
        name="peer_topk",
    )(s1t, s2t)


PEER_E = PEER_HEADS * PEER_TOPK
PEER_RING = 3
PEER_MID_ROWS = 48


def _pack_body(u_ref, v_ref, out_ref):
    bf16_bits = lambda a: lax.bitcast_convert_type(a.astype(BF16).astype(F32), I32)
    words = bf16_bits(u_ref[...]) | lax.shift_right_logical(bf16_bits(v_ref[...]), jnp.int32(16))
    _rows_to_tiles(words, out_ref)


def _pack_expert_rows(pu, pv):
    n, d = pu.shape
    assert d == SUBLANES * LANES
    te = 256
    blk = pl.BlockSpec((te, d), lambda i: (i, 0))
    return pl.pallas_call(
        _pack_body,
        grid=(n // te,),
        in_specs=[blk, blk],
        out_specs=pl.BlockSpec((te, SUBLANES, LANES), lambda i: (i, 0, 0)),
        out_shape=jax.ShapeDtypeStruct((n, SUBLANES, LANES), I32),
        compiler_params=_params("parallel"),
        name="pack_experts",
    )(pu, pv)


def _word_hi(w):
    return lax.bitcast_convert_type(w & jnp.int32(-65536), F32)


def _word_lo(w):
    return lax.bitcast_convert_type(lax.shift_left(w, jnp.int32(16)), F32)


def _sublane_sums(a, sub):
    for dist in (4, 2, 1):
        low = (sub & dist) == 0
        half = len(a) // 2
        a = [jnp.where(low, a[i], pltpu.roll(a[i + half], dist, 0))
             + jnp.where(low, pltpu.roll(a[i], SUBLANES - dist, 0), a[i + half])
             for i in range(half)]
    return a[0]


def _peer_pair_math(expert_u, expert_v, x8, gates, store, issue_some):
    sub = lax.broadcasted_iota(I32, (SUBLANES, LANES), 0)
    eye = (lax.broadcasted_iota(I32, (PEER_E, LANES), 0)
           == lax.broadcasted_iota(I32, (PEER_E, LANES), 1))
    ones_rows = jnp.ones((SUBLANES, LANES), BF16)
    ones_sq = jnp.ones((LANES, LANES), BF16)
    ngroup = PEER_E // SUBLANES

    def hidden(a):
        groups = []
        for g in range(ngroup):
            prods = [expert_u(a, g * SUBLANES + r) * x8[a] for r in range(SUBLANES)]
            groups.append(_sublane_sums(prods, sub))
            issue_some(3 - g % 2)
        return jnp.concatenate(groups, axis=0)

    def expert_weights(a, q):
        q_hi = q.astype(BF16)
        q_lo = (q - q_hi.astype(F32)).astype(BF16)
        hid = (lax.dot_general(ones_rows, q_hi, NT_DIMS, preferred_element_type=F32)
               + lax.dot_general(ones_rows, q_lo, NT_DIMS, preferred_element_type=F32))
        issue_some(PEER_MID_ROWS // 2)
        w = _gelu(hid[0:1]) * gates[a]
        wd = jnp.where(eye, jnp.broadcast_to(w, (PEER_E, LANES)), 0.0).astype(BF16)
        wcol = jnp.dot(wd, ones_sq, preferred_element_type=F32)
        issue_some(PEER_MID_ROWS // 2)
        return wcol

    def combine(a, wcol):
        out = jnp.zeros((SUBLANES, LANES), F32)
        for g in range(ngroup):
            for r in range(SUBLANES):
                j = g * SUBLANES + r
                out = out + wcol[j:j + 1, :] * expert_v(a, j)
            issue_some(3 - g % 2)
        store(a, out)

    wcol0 = expert_weights(0, hidden(0))
    wcol1 = expert_weights(1, hidden(1))
    combine(0, wcol0)
    combine(1, wcol1)


def _peer_finish(h1_ref, acc_ref, fg_ref, out_ref):
    hsum = h1_ref[...] + acc_ref[...]
    ms = jnp.mean(hsum * hsum, axis=(1, 2), keepdims=True)
    _tiles_to_rows(hsum * lax.rsqrt(ms + RMS_EPS) * fg_ref[...], out_ref)


def _peerffn_body(tb, idx_ref, idx_next_ref, gate_ref, c_ref, h1_ref, fg_ref, uv_ref, out_ref,
                  buf_ref, acc_ref, sem_ref):
    step = pl.program_id(0)
    not_last = step + 1 < pl.num_programs(0)
    npairs = tb // 2
    ahead = PEER_RING - 1

    def row_copy(index_ref, t, j, slot):
        return pltpu.make_async_copy(uv_ref.at[index_ref[t, j]], buf_ref.at[slot, j],
                                     sem_ref.at[slot])

    def wait_all(slot):
        pltpu.make_async_copy(uv_ref.at[pl.ds(0, PEER_E)], buf_ref.at[slot],
                              sem_ref.at[slot]).wait()

    def ring(i):
        base = 2 * ((step * npairs + i) % PEER_RING)
        return (base, base + 1)

    def pair(i, into_next_step):
        toks = (2 * i, 2 * i + 1)
        slots = ring(i)
        nslots = ring(i + ahead)
        todo = [(a, j) for a in range(2) for j in range(PEER_E)]

        def start_rows(rows, index_ref, first_tok):
            for a, j in rows:
                row_copy(index_ref, first_tok + a, j, nslots[a]).start(priority=j % 2)

        def issue_some(n):
            rows = todo[:n]
            del todo[:n]
            if not rows:
                return
            if into_next_step:
                pl.when(not_last)(lambda: start_rows(rows, idx_next_ref, 2 * (i + ahead) - tb))
            else:
                start_rows(rows, idx_ref, 2 * (i + ahead))

        wait_all(slots[0])
        wait_all(slots[1])

        def store(a, out):
            acc_ref[toks[a]] = out

        _peer_pair_math(
            lambda a, j: _word_hi(buf_ref[slots[a], j]),
            lambda a, j: _word_lo(buf_ref[slots[a], j]),
            [c_ref[t] for t in toks], [gate_ref[pl.ds(t, 1), :] for t in toks], store, issue_some)
        assert not todo, "every prefetch row DMA must be issued exactly once"

    @pl.when(step == 0)
    def _():
        for i in range(ahead):
            for a, slot in enumerate(ring(i)):
                for j in range(PEER_E):
                    row_copy(idx_ref, 2 * i + a, j, slot).start(priority=j % 2)

    def body(i, carry):
        pair(i, False)
        return carry

    lax.fori_loop(0, npairs - ahead, body, 0)
    for i in range(npairs - ahead, npairs):
        pair(i, True)
    _peer_finish(h1_ref, acc_ref, fg_ref, out_ref)


def _peer_ffn(idx, gate, c3, h13, final_g3, uv_tiles):
    t = h13.shape[0]
    tb = 256
    row3 = pl.BlockSpec((tb, SUBLANES, LANES), lambda i: (i, 0, 0))
    nsteps = t // tb
    return pl.pallas_call(
        functools.partial(_peerffn_body, tb),
        grid=(nsteps,),
        in_specs=[pl.BlockSpec((tb, PEER_E), lambda i: (i, 0), memory_space=pltpu.SMEM),
                  pl.BlockSpec((tb, PEER_E), lambda i: (jnp.minimum(i + 1, nsteps - 1), 0),
                               memory_space=pltpu.SMEM),
                  pl.BlockSpec((tb, PEER_E), lambda i: (i, 0)), row3, row3,
                  pl.BlockSpec((1, SUBLANES, LANES), lambda i: (0, 0, 0)),
                  pl.BlockSpec(memory_space=pl.ANY)],
        out_specs=pl.BlockSpec((tb, SUBLANES * LANES), lambda i: (i, 0)),
        out_shape=jax.ShapeDtypeStruct((t, SUBLANES * LANES), F32),
        scratch_shapes=[pltpu.VMEM((2 * PEER_RING, PEER_E, SUBLANES, LANES), I32),
                        pltpu.VMEM((tb, SUBLANES, LANES), F32),
                        pltpu.SemaphoreType.DMA((2 * PEER_RING,))],
        compiler_params=_params("arbitrary"),
        name="peer_ffn",
    )(idx, idx, gate, c3, h13, final_g3, uv_tiles)


def _overlap_table(seq):
    ci = jnp.arange(seq // CMP_STRIDE)[None, :] * CMP_STRIDE
    sj = jnp.arange(SEL_BLOCK)[:, None] * SEL_BLOCK
    return ((ci < sj + SEL_BLOCK) & (ci + CMP_BLOCK > sj)).astype(F32)


def _cmp_blocks(kv):
    b, g, s, dh = kv.shape
    r = kv.reshape(b * g, s // CMP_STRIDE, CMP_STRIDE * dh)
    return jnp.concatenate([r, jnp.roll(r, -1, axis=1)], axis=-1)


def _pad_cmp_params(pe, w1):
    pe_p = jnp.pad(pe, ((0, 0), (0, LANES - HEAD))).reshape(1, -1)
    w1_p = jnp.pad(w1.reshape(CMP_BLOCK, HEAD, -1), ((0, 0), (0, LANES - HEAD), (0, 0)))
    return pe_p, w1_p.reshape(CMP_BLOCK * LANES, -1).astype(BF16)


def _layer(h, lidx, attn_norm, w_in, lq1, lk1, lq2, lk2, subln, pe_k, pe_v, w1k, w1v, w2k, w2v,
           p_da, p_nsa, w_o, ffn_norm, wq, k1, k2, pu, pv, out_norm):
    b, s, d = h.shape
    t = b * s
    g, hg = NSA_GROUPS, NSA_HG
    lambda_init = 0.8 - 0.6 * math.exp(-0.3 * lidx)
    x2 = h.reshape(t, d)

    qda, kda, vda, qn, cmp_in, ksx, kwx, vsw, gm, gn = _in_proj(
        x2, attn_norm.reshape(1, d), _pack_w_in(w_in), _rope_tables(s), s)
    tr = lambda a2: jnp.swapaxes(a2.reshape(b, s, -1), 1, 2)
    ydat = _diff_attn(tr(qda), kda.reshape(b, s, -1), tr(vda),
                      lq1.reshape(1, -1), lk1.reshape(1, -1), lq2.reshape(1, -1),
                      lk2.reshape(1, -1), subln.reshape(-1, 1), lambda_init)
    yda = jnp.swapaxes(ydat, 1, 2).reshape(t, -1)

    assert s // SEL_BLOCK <= SEL_BLOCK, "selection bias rows hold at most 64 blocks"
    cmp4 = jnp.swapaxes(cmp_in.reshape(b, s, 2 * g, LANES), 1, 2)
    pe_kp, w1_kp = _pad_cmp_params(pe_k, w1k)
    pe_vp, w1_vp = _pad_cmp_params(pe_v, w1v)
    kca, vc = _compress(_cmp_blocks(cmp4[:, :g]), _cmp_blocks(cmp4[:, g:]), pe_kp, pe_vp,
                        w1_kp, w1_vp, w2k.astype(BF16), w2v.astype(BF16))
    ncp = s // CMP_STRIDE
    kca = kca.reshape(b, g, ncp, LANES)
    vct = jnp.swapaxes(vc.reshape(b, g, ncp, HEAD), 2, 3)

    qnt = tr(qn)
    ocmp_t, bias_t = _cmp_select(qnt, kca, vct, _overlap_table(s))
    onehot = (jnp.arange(s)[:, None] // SEL_BLOCK == jnp.arange(HEAD)[None, :]).astype(BF16)
    pad_hot = jnp.concatenate([jnp.zeros_like(onehot), onehot] * g, axis=1)
    ksa = ksx.reshape(b, s, -1) + pad_hot[None]
    ynt = _sel_win(qnt, bias_t, ksa, kwx.reshape(b, s, -1), tr(vsw), ocmp_t, tr(gn))
    yn = jnp.swapaxes(ynt, 1, 2).reshape(t, -1)

    h1_tiles, c_tiles, s1t, s2t = _out_proj(
        yda, yn, gm, x2, p_da.astype(BF16), p_nsa.astype(BF16), w_o.astype(BF16),
        ffn_norm.reshape(1, d), wq.astype(BF16), k1.astype(BF16), k2.astype(BF16))
    idx_t, gate_t = _peer_topk(s1t, s2t)
    idx = idx_t.reshape(PEER_E, t).T
    gate = gate_t.reshape(PEER_E, t).T
    out = _peer_ffn(idx, gate, c_tiles, h1_tiles, out_norm.reshape(1, SUBLANES, LANES),
                    _pack_expert_rows(pu, pv))
    return out.reshape(b, s, d)


def kernel(x, attn_norm, w_in, da_lambda_q1, da_lambda_k1, da_lambda_q2, da_lambda_k2, da_subln,
           cmp_pe_k, cmp_pe_v, cmp_w1_k, cmp_w1_v, cmp_w2_k, cmp_w2_v, p_da, p_nsa, w_o,
           ffn_norm, peer_wq, peer_k1, peer_k2, peer_u, peer_v, final_norm):
    depth = attn_norm.shape[0]
    assert depth == 1, "the final norm is fused into the last layer's PEER kernel"
    h = x
    for l in range(depth):
        h = _layer(h, l, attn_norm[l], w_in[l], da_lambda_q1[l], da_lambda_k1[l], da_lambda_q2[l],
                   da_lambda_k2[l], da_subln[l], cmp_pe_k[l], cmp_pe_v[l], cmp_w1_k[l],
                   cmp_w1_v[l], cmp_w2_k[l], cmp_w2_v[l], p_da[l], p_nsa[l], w_o[l], ffn_norm[l],
                   peer_wq[l], peer_k1[l], peer_k2[l], peer_u[l], peer_v[l], final_norm)
    return h
```

```python
import functools
import math

import jax
import jax.numpy as jnp
from jax import lax
from jax.experimental import pallas as pl
from jax.experimental.pallas import tpu as pltpu

F32 = jnp.float32
BF16 = jnp.bfloat16
I32 = jnp.int32

RMS_EPS = 1e-6
ROPE_THETA = 500000.0
ROPE_HALF = 8
HEAD = 64
DA_HEADS = 4
NSA_GROUPS = 2
NSA_HG = 4
CMP_STRIDE = 16
CMP_BLOCK = 32
SEL_BLOCK = 64
SEL_TOPK = 16
WINDOW = 512
FORCE_BONUS = 1e4
NEG_BIG = -1e30
SEL_MASK_BIAS = -32768.0
PEER_HEADS = 8
PEER_NKEYS = 128
PEER_TOPK = 16
LANES = 128
SUBLANES = 8
VMEM_LIMIT = 56 * 1024 * 1024

NT_DIMS = (((1,), (1,)), ((), ()))


def _rmsnorm(x, g):
    return x * lax.rsqrt(jnp.mean(x * x, axis=-1, keepdims=True) + RMS_EPS) * g


def _sigmoid(z):
    return 1.0 / (1.0 + jnp.exp(-z))


def _gelu(z):
    return 0.5 * z * (1.0 + lax.erf(z * (2.0 ** -0.5)))


def _params(*sem):
    return pltpu.CompilerParams(dimension_semantics=sem, vmem_limit_bytes=VMEM_LIMIT)


def _sublane_transpose(v):
    sub = lax.broadcasted_iota(I32, (SUBLANES, LANES), 0)
    v = list(v)
    for dist in (4, 2, 1):
        low = (sub & dist) == 0
        nxt = list(v)
        for i in range(SUBLANES):
            if i & dist == 0:
                nxt[i] = jnp.where(low, v[i], pltpu.roll(v[i + dist], dist, 0))
                nxt[i + dist] = jnp.where(low, pltpu.roll(v[i], SUBLANES - dist, 0), v[i + dist])
        v = nxt
    return v


def _rows_to_tiles(x, tile_ref):
    for g in range(x.shape[0] // SUBLANES):
        rows = slice(g * SUBLANES, (g + 1) * SUBLANES)
        tiles = _sublane_transpose([x[rows, c * LANES:(c + 1) * LANES] for c in range(SUBLANES)])
        for e in range(SUBLANES):
            tile_ref[g * SUBLANES + e] = tiles[e]


def _tiles_to_rows(t, row_ref):
    for g in range(t.shape[0] // SUBLANES):
        chunks = _sublane_transpose([t[g * SUBLANES + e] for e in range(SUBLANES)])
        for c in range(SUBLANES):
            row_ref[g * SUBLANES:(g + 1) * SUBLANES, c * LANES:(c + 1) * LANES] = chunks[c]


_QDA0, _KDA0, _VDA0, _QN0 = 0, 512, 1024, 1536
_KC0, _VC0, _KS0, _KW0, _VSW0, _GM0, _GN0, _WCOLS = 2048, 2304, 2560, 2816, 3072, 3328, 5376, 5632


def _inproj_body(x_ref, g_ref, w_ref, rc_ref, rs1_ref, rs2_ref,
                 qda_ref, kda_ref, vda_ref, qn_ref, cmp_ref, ks_ref, kw_ref, vsw_ref, gm_ref, gn_ref):
    a = _rmsnorm(x_ref[...], g_ref[...]).astype(BF16)
    rc, rs1, rs2 = rc_ref[...], rs1_ref[...], rs2_ref[...]

    def rope(z):
        return (z * rc + pltpu.roll(z, ROPE_HALF, 1) * rs1
                + pltpu.roll(z, LANES - ROPE_HALF, 1) * rs2)

    def proj(c0):
        return jnp.dot(a, w_ref[:, c0:c0 + 256], preferred_element_type=F32)

    def rope2(z):
        return jnp.concatenate([rope(z[:, :LANES]), rope(z[:, LANES:])], axis=1)

    for c in range(2):
        qda_ref[:, c * 256:(c + 1) * 256] = (rope2(proj(_QDA0 + c * 256)) * 0.125).astype(BF16)
        kda_ref[:, c * 256:(c + 1) * 256] = rope2(proj(_KDA0 + c * 256)).astype(BF16)
        vda_ref[:, c * 256:(c + 1) * 256] = proj(_VDA0 + c * 256).astype(BF16)
        qn_ref[:, c * 256:(c + 1) * 256] = (rope2(proj(_QN0 + c * 256)) * 0.125).astype(BF16)
    cmp_ref[:, 0:256] = rope2(proj(_KC0)).astype(BF16)
    cmp_ref[:, 256:512] = proj(_VC0).astype(BF16)
    ks_ref[...] = rope2(proj(_KS0)).astype(BF16)
    kw_ref[...] = rope2(proj(_KW0)).astype(BF16)
    vsw_ref[...] = proj(_VSW0).astype(BF16)
    for c in range(8):
        gm_ref[:, c * 256:(c + 1) * 256] = _sigmoid(proj(_GM0 + c * 256)).astype(BF16)
    gn_ref[...] = _sigmoid(proj(_GN0))


def _pack_w_in(w):
    d = w.shape[0]
    zeros = lambda n: jnp.zeros((d, n), w.dtype)

    def spread_groups(c0):
        return [w[:, c0:c0 + HEAD], zeros(HEAD), w[:, c0 + HEAD:c0 + 2 * HEAD], zeros(HEAD)]

    kc, vc, ks, vs, kw, vw = (2048 + 128 * i for i in range(6))
    gn = w[:, 2816:2840]
    cols = ([w[:, :2048]] + spread_groups(kc) + spread_groups(vc) + spread_groups(ks)
            + spread_groups(kw) + [w[:, vs:vs + 128], w[:, vw:vw + 128], w[:, 2840:],
                                   gn[:, :12], zeros(LANES - 12), gn[:, 12:], zeros(LANES - 12)])
    packed = jnp.concatenate(cols, axis=1).astype(BF16)
    assert packed.shape[1] == _WCOLS
    return packed


def _rope_tables(seq):
    inv = jnp.power(ROPE_THETA, -jnp.arange(ROPE_HALF, dtype=F32) * 2.0 / (2 * ROPE_HALF))
    ang = jnp.arange(seq, dtype=F32)[:, None] * inv[None, :]
    cos, sin = jnp.cos(ang), jnp.sin(ang)
    one = jnp.ones((seq, HEAD - 2 * ROPE_HALF), F32)
    zero8 = jnp.zeros((seq, ROPE_HALF), F32)
    zero48 = jnp.zeros_like(one)
    rc = jnp.concatenate([cos, cos, one], axis=1)
    rs1 = jnp.concatenate([zero8, sin, zero48], axis=1)
    rs2 = jnp.concatenate([-sin, zero8, zero48], axis=1)
    return tuple(jnp.concatenate([t, t], axis=1) for t in (rc, rs1, rs2))


def _in_proj(x2, norm_g, w_packed, rope_tabs, seq):
    t, d = x2.shape
    tm = 512
    nseq = seq // tm
    row = lambda i: (i, 0)
    fixed = lambda i: (0, 0)
    out_shapes = [
        jax.ShapeDtypeStruct((t, 512), BF16), jax.ShapeDtypeStruct((t, 512), BF16),
        jax.ShapeDtypeStruct((t, 512), BF16), jax.ShapeDtypeStruct((t, 512), BF16),
        jax.ShapeDtypeStruct((t, 512), BF16), jax.ShapeDtypeStruct((t, 256), BF16),
        jax.ShapeDtypeStruct((t, 256), BF16), jax.ShapeDtypeStruct((t, 256), BF16),
        jax.ShapeDtypeStruct((t, 2048), BF16), jax.ShapeDtypeStruct((t, 256), F32)]
    rope_spec = pl.BlockSpec((tm, LANES), lambda i: (i % nseq, 0))
    return pl.pallas_call(
        _inproj_body,
        grid=(t // tm,),
        in_specs=[pl.BlockSpec((tm, d), row), pl.BlockSpec((1, d), fixed),
                  pl.BlockSpec((d, _WCOLS), fixed), rope_spec, rope_spec, rope_spec],
        out_specs=[pl.BlockSpec((tm, s.shape[1]), row) for s in out_shapes],
        out_shape=out_shapes,
        compiler_params=_params("parallel"),
        name="in_proj",
    )(x2, norm_g, w_packed, *rope_tabs)


def _softmax_step(s, vt, m_ref, l_ref, acc_ref):
    m_prev = m_ref[...]
    m_new = jnp.maximum(m_prev, jnp.max(s, axis=0, keepdims=True))
    alpha = jnp.exp(m_prev - m_new)
    p = jnp.exp(s - m_new)
    l_ref[...] = alpha * l_ref[...] + jnp.sum(p, axis=0, keepdims=True)
    acc_ref[...] = alpha * acc_ref[...] + jnp.dot(vt, p.astype(BF16), preferred_element_type=F32)
    m_ref[...] = m_new


def _softmax_reset(m_ref, l_ref, acc_ref):
    m_ref[...] = jnp.full(m_ref.shape, NEG_BIG, F32)
    l_ref[...] = jnp.zeros(l_ref.shape, F32)
    acc_ref[...] = jnp.zeros(acc_ref.shape, F32)


def _attend_tiles(n_full, scores, values, mask_last, sa_ref, sb_ref, m_ref, l_ref, acc_ref):
    step = lambda s, t: _softmax_step(s, values(t), m_ref, l_ref, acc_ref)
    sa_ref[...] = scores(0)

    def two_tiles(i, carry):
        t = 2 * i
        sb_ref[...] = scores(t + 1)
        step(sa_ref[...], t)
        sa_ref[...] = scores(t + 2)
        step(sb_ref[...], t + 1)
        return carry

    lax.fori_loop(0, n_full // 2, two_tiles, 0)
    odd = (n_full & 1) == 1

    @pl.when(odd)
    def _():
        sb_ref[...] = scores(n_full)
        step(sa_ref[...], n_full - 1)
        step(mask_last(sb_ref[...]), n_full)

    @pl.when(jnp.logical_not(odd))
    def _():
        step(mask_last(sa_ref[...]), n_full)


def _key_tile(ref, kt, bk):
    return ref[(0,) * (len(ref.shape) - 2) + (pl.ds(pl.multiple_of(kt * bk, bk), bk), slice(None))]


def _value_tile(ref, kt, bk):
    return ref[(0,) * (len(ref.shape) - 2) + (slice(None), pl.ds(pl.multiple_of(kt * bk, bk), bk))]


def _diffattn_body(lambda_init, bq, q_ref, k_ref, v_ref, lq1_ref, lk1_ref, lq2_ref, lk2_ref,
                   sub_ref, y_ref, qbd_ref, m_ref, l_ref, acc_ref, sa_ref, sb_ref):
    qi = pl.program_id(2)
    bk = bq
    qt = q_ref[0]
    sub = lax.broadcasted_iota(I32, qt.shape, 0)
    zero = jnp.zeros_like(qt)
    qbd_ref[:, 0:bq] = jnp.where(sub < HEAD, qt, zero)
    qbd_ref[:, bq:2 * bq] = jnp.where(sub >= HEAD, qt, zero)
    _softmax_reset(m_ref, l_ref, acc_ref)

    def scores(kt):
        return jnp.dot(_key_tile(k_ref, kt, bk), qbd_ref[...], preferred_element_type=F32)

    def causal(s):
        r = lax.broadcasted_iota(I32, (bk, 2 * bq), 0)
        c = lax.broadcasted_iota(I32, (bk, 2 * bq), 1) & (bq - 1)
        return jnp.where(r <= c, s, NEG_BIG)

    _attend_tiles(qi, scores, lambda kt: _value_tile(v_ref, kt, bk), causal,
                  sa_ref, sb_ref, m_ref, l_ref, acc_ref)

    o = acc_ref[...] / l_ref[...]
    lam = (jnp.exp(jnp.sum(lq1_ref[...] * lk1_ref[...], axis=1, keepdims=True))
           - jnp.exp(jnp.sum(lq2_ref[...] * lk2_ref[...], axis=1, keepdims=True)) + lambda_init)
    d = o[:, 0:bq] - lam * o[:, bq:2 * bq]
    ms = jnp.mean(d * d, axis=0, keepdims=True)
    y = d * lax.rsqrt(ms + RMS_EPS) * sub_ref[...] * (1.0 - lambda_init)
    y_ref[0] = y.astype(BF16)


def _diff_attn(qdat, kda, vdat, lq1, lk1, lq2, lk2, subln_col, lambda_init):
    b, s, _ = kda.shape
    bq = 512
    vec = lambda n: pl.BlockSpec((1, n), lambda bi, h, qi: (0, 0))
    dv = 2 * HEAD
    qtile = pl.BlockSpec((1, dv, bq), lambda bi, h, qi: (bi, h, qi))
    return pl.pallas_call(
        functools.partial(_diffattn_body, lambda_init, bq),
        grid=(b, DA_HEADS, s // bq),
        in_specs=[qtile,
                  pl.BlockSpec((1, s, LANES), lambda bi, h, qi: (bi, 0, h)),
                  pl.BlockSpec((1, dv, s), lambda bi, h, qi: (bi, h, 0)),
                  vec(HEAD), vec(HEAD), vec(HEAD), vec(HEAD),
                  pl.BlockSpec((dv, 1), lambda bi, h, qi: (0, 0))],
        out_specs=qtile,
        out_shape=jax.ShapeDtypeStruct((b, DA_HEADS * dv, s), BF16),
        scratch_shapes=[pltpu.VMEM((LANES, 2 * bq), BF16), pltpu.VMEM((1, 2 * bq), F32),
                        pltpu.VMEM((1, 2 * bq), F32), pltpu.VMEM((dv, 2 * bq), F32),
                        pltpu.VMEM((bq, 2 * bq), F32), pltpu.VMEM((bq, 2 * bq), F32)],
        compiler_params=_params("parallel", "parallel", "parallel"),
        name="diff_attn",
    )(qdat, kda, vdat, lq1, lk1, lq2, lk2, subln_col)


def _compress_body(xk_ref, xv_ref, pek_ref, pev_ref, w1k_ref, w1v_ref, w2k_ref, w2v_ref,
                   kc_ref, vc_ref):
    def mlp(x_ref, pe_ref, w1_ref, w2_ref):
        blocks = (x_ref[0].astype(F32) + pe_ref[...]).astype(BF16)
        hid = _gelu(jnp.dot(blocks, w1_ref[...], preferred_element_type=F32))
        return jnp.dot(hid.astype(BF16), w2_ref[...], preferred_element_type=F32)

    kc = mlp(xk_ref, pek_ref, w1k_ref, w2k_ref)
    kc_ref[0] = jnp.concatenate([kc, jnp.zeros_like(kc)], axis=1).astype(BF16)
    vc_ref[0] = mlp(xv_ref, pev_ref, w1v_ref, w2v_ref).astype(BF16)


def _compress(xk, xv, pe_k, pe_v, w1k, w1v, w2k, w2v):
    n, ncp, width = xk.shape
    blk = pl.BlockSpec((1, ncp, width), lambda i: (i, 0, 0))
    fixed = lambda shape: pl.BlockSpec(shape, lambda i: (0, 0))
    return pl.pallas_call(
        _compress_body,
        grid=(n,),
        in_specs=[blk, blk, fixed((1, width)), fixed((1, width)), fixed((width, HEAD)),
                  fixed((width, HEAD)), fixed((HEAD, HEAD)), fixed((HEAD, HEAD))],
        out_specs=[pl.BlockSpec((1, ncp, LANES), lambda i: (i, 0, 0)),
                   pl.BlockSpec((1, ncp, HEAD), lambda i: (i, 0, 0))],
        out_shape=[jax.ShapeDtypeStruct((n, ncp, LANES), BF16),
                   jax.ShapeDtypeStruct((n, ncp, HEAD), BF16)],
        compiler_params=_params("parallel"),
        name="compress",
    )(xk, xv, pe_k, pe_v, w1k, w1v, w2k, w2v)


def _heads_on_lanes(qt, bq):
    return jnp.concatenate([qt[h * HEAD:(h + 1) * HEAD, :] for h in range(NSA_HG)], axis=1)


def _cmpsel_body(bq, q_ref, kc_ref, vc_ref, ovl_ref, ocmp_ref, bias_ref):
    qi = pl.program_id(2)
    ncp = kc_ref.shape[2]
    rows = NSA_HG * bq
    q2 = _heads_on_lanes(q_ref[0], bq)
    qz = jnp.concatenate([q2, jnp.zeros_like(q2)], axis=0)
    s = jnp.dot(kc_ref[0, 0], qz, preferred_element_type=F32)
    n = lax.broadcasted_iota(I32, (ncp, rows), 0)
    qpos = qi * bq + (lax.broadcasted_iota(I32, (ncp, rows), 1) & (bq - 1))
    cmask = n * CMP_STRIDE + (CMP_BLOCK - 1) <= qpos
    s = jnp.where(cmask, s, NEG_BIG)
    e = jnp.exp(s - jnp.max(s, axis=0, keepdims=True))
    p = jnp.where(cmask, e / jnp.sum(e, axis=0, keepdims=True), 0.0)
    o = jnp.dot(vc_ref[0, 0], p.astype(BF16), preferred_element_type=F32)
    for h in range(NSA_HG):
        ocmp_ref[0, 0, h * HEAD:(h + 1) * HEAD, :] = o[:, h * bq:(h + 1) * bq]

    psum = p[:, 0:bq] + p[:, bq:2 * bq] + p[:, 2 * bq:3 * bq] + p[:, 3 * bq:4 * bq]
    imp = jnp.dot(ovl_ref[...], psum, preferred_element_type=F32)
    blk = lax.broadcasted_iota(I32, (SEL_BLOCK, bq), 0)
    pos = qi * bq + lax.broadcasted_iota(I32, (SEL_BLOCK, bq), 1)
    cur = lax.shift_right_logical(pos, 6)
    valid = blk <= cur
    forced = (blk == 0) | (blk == cur) | (blk == cur - 1)
    score = jnp.where(valid, imp + jnp.where(forced, FORCE_BONUS, 0.0), -jnp.inf)
    rank = jnp.zeros((SEL_BLOCK, bq), I32)
    for i in range(SEL_BLOCK):
        other = score[i:i + 1, :]
        beats = (other > score) | ((other == score) & (blk > i))
        rank = rank + beats.astype(I32)
    keep = valid & (rank < SEL_TOPK)
    bias_ref[0, 0] = jnp.where(keep, 0.0, SEL_MASK_BIAS).astype(BF16)


def _cmp_select(qnt, kca, vct, overlap_t):
    b, g, ncp, _ = kca.shape
    s = qnt.shape[2]
    bq = 128
    dq = NSA_HG * HEAD
    return pl.pallas_call(
        functools.partial(_cmpsel_body, bq),
        grid=(b, g, s // bq),
        in_specs=[pl.BlockSpec((1, dq, bq), lambda bi, gi, qi: (bi, gi, qi)),
                  pl.BlockSpec((1, 1, ncp, LANES), lambda bi, gi, qi: (bi, gi, 0, 0)),
                  pl.BlockSpec((1, 1, HEAD, ncp), lambda bi, gi, qi: (bi, gi, 0, 0)),
                  pl.BlockSpec((SEL_BLOCK, ncp), lambda bi, gi, qi: (0, 0))],
        out_specs=[pl.BlockSpec((1, 1, dq, bq), lambda bi, gi, qi: (bi, gi, 0, qi)),
                   pl.BlockSpec((1, 1, SEL_BLOCK, bq), lambda bi, gi, qi: (bi, gi, 0, qi))],
        out_shape=[jax.ShapeDtypeStruct((b, g, dq, s), F32),
                   jax.ShapeDtypeStruct((b, g, SEL_BLOCK, s), BF16)],
        compiler_params=_params("parallel", "parallel", "parallel"),
        name="cmp_select",
    )(qnt, kca, vct, overlap_t)


def _selwin_body(bq, bks, q_ref, bias_ref, ks_ref, vs_ref, kw_ref, vw_ref, ocmp_ref, gate_ref,
                 y_ref, m_ref, l_ref, acc_ref, qa_ref, sa_ref, sb_ref):
    qi = pl.program_id(2)
    rows = NSA_HG * bq
    q2 = _heads_on_lanes(q_ref[0], bq)
    bias = bias_ref[0, 0]
    qa = jnp.concatenate([q2, jnp.concatenate([bias] * NSA_HG, axis=1)], axis=0)
    qw = jnp.concatenate([q2, jnp.zeros_like(q2)], axis=0)
    r = lax.broadcasted_iota(I32, (bq, rows), 0)
    c = lax.broadcasted_iota(I32, (bq, rows), 1) & (bq - 1)

    _softmax_reset(m_ref, l_ref, acc_ref)

    qa_ref[...] = qa
    last = (qi * bq) // bks

    def causal(s):
        kpos = last * bks + lax.broadcasted_iota(I32, (bks, rows), 0)
        qpos = qi * bq + (lax.broadcasted_iota(I32, (bks, rows), 1) & (bq - 1))
        return jnp.where(kpos <= qpos, s, NEG_BIG)

    _attend_tiles(last,
                  lambda kt: jnp.dot(_key_tile(ks_ref, kt, bks), qa_ref[...],
                                     preferred_element_type=F32),
                  lambda kt: _value_tile(vs_ref, kt, bks), causal,
                  sa_ref, sb_ref, m_ref, l_ref, acc_ref)
    o_sel = acc_ref[...] / l_ref[...]

    _softmax_reset(m_ref, l_ref, acc_ref)
    backs = list(range(WINDOW // bq, -1, -1))
    tiles = [jnp.maximum(qi - back, 0) for back in backs]
    raw = [jnp.dot(_key_tile(kw_ref, kt, bq), qw, preferred_element_type=F32) for kt in tiles]
    for back, kt, s in zip(backs, tiles, raw):
        dist = c + back * bq - r
        inside = (dist >= 0) & (dist < WINDOW) & (qi >= back)
        _softmax_step(jnp.where(inside, s, NEG_BIG), _value_tile(vw_ref, kt, bq),
                      m_ref, l_ref, acc_ref)

    o_win = acc_ref[...] / l_ref[...]
    gate = gate_ref[0]
    for h in range(NSA_HG):
        sl = slice(h * bq, (h + 1) * bq)
        y = (gate[3 * h:3 * h + 1, :] * ocmp_ref[0, 0, h * HEAD:(h + 1) * HEAD, :]
             + gate[3 * h + 1:3 * h + 2, :] * o_sel[:, sl]
             + gate[3 * h + 2:3 * h + 3, :] * o_win[:, sl])
        y_ref[0, h * HEAD:(h + 1) * HEAD, :] = y.astype(BF16)


def _sel_win(qnt, bias_t, ksa, kwa, vswt, ocmp_t, gnt):
    b, s, _ = ksa.shape
    g = NSA_GROUPS
    bq, bks = 256, 512
    dq = NSA_HG * HEAD
    rows = NSA_HG * bq
    keys = pl.BlockSpec((1, s, LANES), lambda bi, gi, qi: (bi, 0, gi))
    vals = lambda first: pl.BlockSpec((1, HEAD, s), lambda bi, gi, qi: (bi, first + gi, 0))
    qtile = pl.BlockSpec((1, dq, bq), lambda bi, gi, qi: (bi, gi, qi))
    return pl.pallas_call(
        functools.partial(_selwin_body, bq, bks),
        grid=(b, g, s // bq),
        in_specs=[qtile,
                  pl.BlockSpec((1, 1, SEL_BLOCK, bq), lambda bi, gi, qi: (bi, gi, 0, qi)),
                  keys, vals(0), keys, vals(g),
                  pl.BlockSpec((1, 1, dq, bq), lambda bi, gi, qi: (bi, gi, 0, qi)),
                  pl.BlockSpec((1, LANES, bq), lambda bi, gi, qi: (bi, gi, qi))],
        out_specs=qtile,
        out_shape=jax.ShapeDtypeStruct((b, g * dq, s), BF16),
        scratch_shapes=[pltpu.VMEM((1, rows), F32), pltpu.VMEM((1, rows), F32),
                        pltpu.VMEM((HEAD, rows), F32), pltpu.VMEM((LANES, rows), BF16),
                        pltpu.VMEM((bks, rows), F32), pltpu.VMEM((bks, rows), F32)],
        compiler_params=_params("parallel", "parallel", "parallel"),
        name="sel_win",
    )(qnt, bias_t, ksa, vswt, kwa, vswt, ocmp_t, gnt)


def _outproj_body(yda_ref, yn_ref, gm_ref, x_ref, pda_ref, pnsa_ref, wo_ref, fg_ref, wq_ref,
                  k1_ref, k2_ref, h1_ref, c_ref, s1_ref, s2_ref):
    d = x_ref.shape[1]
    a = jnp.dot(yda_ref[...], pda_ref[...], preferred_element_type=F32)
    bn = jnp.dot(yn_ref[...], pnsa_ref[...], preferred_element_type=F32)
    merged = gm_ref[:, :d].astype(F32) * a + gm_ref[:, d:].astype(F32) * bn
    h1 = x_ref[...] + jnp.dot(merged.astype(BF16), wo_ref[...], preferred_element_type=F32)
    c = _rmsnorm(h1, fg_ref[...])
    _rows_to_tiles(h1, h1_ref)
    _rows_to_tiles(c, c_ref)
    cb = c.astype(BF16)
    for h in range(PEER_HEADS):
        qh = jnp.dot(cb, wq_ref[:, h * 256:(h + 1) * 256], preferred_element_type=F32).astype(BF16)
        s1_ref[h] = lax.dot_general(k1_ref[...], qh[:, :LANES], NT_DIMS, preferred_element_type=F32)
        s2_ref[h] = lax.dot_general(k2_ref[...], qh[:, LANES:], NT_DIMS, preferred_element_type=F32)


def _out_proj(yda, yn, gm, x2, pda, pnsa, wo, ffn_g, wq, k1, k2):
    t, d = x2.shape
    tm = 512
    row = lambda w: pl.BlockSpec((tm, w), lambda i: (i, 0))
    fixed = lambda a: pl.BlockSpec(a.shape, lambda i: (0, 0))
    sspec = pl.BlockSpec((PEER_HEADS, PEER_NKEYS, tm), lambda i: (0, 0, i))
    assert d == SUBLANES * LANES
    tiles = pl.BlockSpec((tm, SUBLANES, LANES), lambda i: (i, 0, 0))
    return pl.pallas_call(
        _outproj_body,
        grid=(t // tm,),
        in_specs=[row(512), row(512), row(2 * d), row(d), fixed(pda), fixed(pnsa), fixed(wo),
                  fixed(ffn_g), fixed(wq), fixed(k1), fixed(k2)],
        out_specs=[tiles, tiles, sspec, sspec],
        out_shape=[jax.ShapeDtypeStruct((t, SUBLANES, LANES), F32),
                   jax.ShapeDtypeStruct((t, SUBLANES, LANES), F32),
                   jax.ShapeDtypeStruct((PEER_HEADS, PEER_NKEYS, t), F32),
                   jax.ShapeDtypeStruct((PEER_HEADS, PEER_NKEYS, t), F32)],
        compiler_params=_params("parallel"),
        name="out_proj",
    )(yda, yn, gm, x2, pda, pnsa, wo, ffn_g, wq, k1, k2)


def _batcher_pairs(n):
    pairs = []

    def merge(lo, hi, r):
        step = r * 2
        if step < hi - lo:
            merge(lo, hi, step)
            merge(lo + r, hi, step)
            pairs.extend((i, i + r) for i in range(lo + r, hi - r, step))
        else:
            pairs.append((lo, lo + r))

    def sort(lo, hi):
        if hi - lo >= 1:
            mid = lo + (hi - lo) // 2
            sort(lo, mid)
            sort(mid + 1, hi)
            merge(lo, hi, 1)

    sort(0, n - 1)
    return pairs


_NET16 = _batcher_pairs(PEER_TOPK)


def _cmpx(a, b):
    c = a[0] >= b[0]
    return ((jnp.where(c, a[0], b[0]), jnp.where(c, a[1], b[1])),
            (jnp.where(c, b[0], a[0]), jnp.where(c, b[1], a[1])))


def _sort_lists(lists, n_real):
    lists = list(lists)
    for i, j in _NET16:
        if j < n_real:
            lists[i], lists[j] = _cmpx(lists[i], lists[j])
    return lists


def _merge_top(a, b):
    k = PEER_TOPK
    lists = [_cmpx(a[i], b[k - 1 - i])[0] for i in range(k)]
    step = k // 2
    while step >= 1:
        for i in range(k):
            if i & step == 0:
                lists[i], lists[i + step] = _cmpx(lists[i], lists[i + step])
        step //= 2
    return lists


def _top16(pairs):
    k = PEER_TOPK
    filler = (jnp.full(pairs[0][0].shape, -jnp.inf, F32), jnp.zeros(pairs[0][1].shape, I32))
    groups = []
    for g0 in range(0, len(pairs), k):
        chunk = list(pairs[g0:g0 + k])
        groups.append(_sort_lists(chunk + [filler] * (k - len(chunk)), len(chunk)))
    while len(groups) > 1:
        groups = [_merge_top(groups[i], groups[i + 1]) if i + 1 < len(groups) else groups[i]
                  for i in range(0, len(groups), 2)]
    return groups[0]


def _product_key_select(s1, s2):
    k = PEER_TOPK
    ids = lambda n: [jnp.full(s1[0].shape, i, I32) for i in range(n)]
    l1 = _top16(list(zip(s1, ids(len(s1)))))
    l2 = _top16(list(zip(s2, ids(len(s2)))))
    pair = lambda a, b: (l1[a][0] + l2[b][0], l1[a][1] * PEER_NKEYS + l2[b][1])
    first_row = [pair(0, b) for b in range(k)]
    rest = [pair(a, b) for a in range(1, k) for b in range(k // (a + 1))]
    return _merge_top(first_row, _top16(rest))


def _peertopk_body(tt, s1_ref, s2_ref, idx_ref, gate_ref):
    nblk = tt // LANES
    assert nblk == SUBLANES

    def keys_major(ref, h):
        tiles = []
        for kg in range(PEER_NKEYS // SUBLANES):
            rows = slice(kg * SUBLANES, (kg + 1) * SUBLANES)
            tiles += _sublane_transpose([ref[h, rows, b * LANES:(b + 1) * LANES]
                                         for b in range(nblk)])
        return tiles

    def store(ref, h, tiles):
        for g in range(PEER_TOPK // SUBLANES):
            blocks = _sublane_transpose(tiles[g * SUBLANES:(g + 1) * SUBLANES])
            for b in range(nblk):
                ref[h, g * SUBLANES:(g + 1) * SUBLANES, b * LANES:(b + 1) * LANES] = blocks[b]

    def head(h, carry):
        best = _product_key_select(keys_major(s1_ref, h), keys_major(s2_ref, h))
        ex = [jnp.exp(v - best[0][0]) for v, _ in best]
        z = ex[0]
        for e in ex[1:]:
            z = z + e
        store(gate_ref, h, [e / z for e in ex])
        store(idx_ref, h, [ix for _, ix in best])
        return carry

    lax.fori_loop(0, PEER_HEADS, head, 0)


def _peer_topk(s1t, s2t):
    _, _, t = s1t.shape
    tt = SUBLANES * LANES
    spec_in = pl.BlockSpec((PEER_HEADS, PEER_NKEYS, tt), lambda i: (0, 0, i))
    spec_out = pl.BlockSpec((PEER_HEADS, PEER_TOPK, tt), lambda i: (0, 0, i))
    return pl.pallas_call(
        functools.partial(_peertopk_body, tt),
        grid=(t // tt,),
        in_specs=[spec_in, spec_in],
        out_specs=[spec_out, spec_out],
        out_shape=[jax.ShapeDtypeStruct((PEER_HEADS, PEER_TOPK, t), I32),
                   jax.ShapeDtypeStruct((PEER_HEADS, PEER_TOPK, t), F32)],
        compiler_params=_params("parallel"),
        name="peer_topk",
    )(s1t, s2t)


PEER_E = PEER_HEADS * PEER_TOPK
PEER_RING = 3
PEER_MID_ROWS = 48


def _pack_body(u_ref, v_ref, out_ref):
    bf16_bits = lambda a: lax.bitcast_convert_type(a.astype(BF16).astype(F32), I32)
    words = bf16_bits(u_ref[...]) | lax.shift_right_logical(bf16_bits(v_ref[...]), jnp.int32(16))
    _rows_to_tiles(words, out_ref)


def _pack_expert_rows(pu, pv):
    n, d = pu.shape
    assert d == SUBLANES * LANES
    te = 256
    blk = pl.BlockSpec((te, d), lambda i: (i, 0))
    return pl.pallas_call(
        _pack_body,
        grid=(n // te,),
        in_specs=[blk, blk],
        out_specs=pl.BlockSpec((te, SUBLANES, LANES), lambda i: (i, 0, 0)),
        out_shape=jax.ShapeDtypeStruct((n, SUBLANES, LANES), I32),
        compiler_params=_params("parallel"),
        name="pack_experts",
    )(pu, pv)


def _word_hi(w):
    return lax.bitcast_convert_type(w & jnp.int32(-65536), F32)


def _word_lo(w):
    return lax.bitcast_convert_type(lax.shift_left(w, jnp.int32(16)), F32)


def _sublane_sums(a, sub):
    for dist in (4, 2, 1):
        low = (sub & dist) == 0
        half = len(a) // 2
        a = [jnp.where(low, a[i], pltpu.roll(a[i + half], dist, 0))
             + jnp.where(low, pltpu.roll(a[i], SUBLANES - dist, 0), a[i + half])
             for i in range(half)]
    return a[0]


def _peer_pair_math(expert_u, expert_v, x8, gates, store, issue_some):
    sub = lax.broadcasted_iota(I32, (SUBLANES, LANES), 0)
    eye = (lax.broadcasted_iota(I32, (PEER_E, LANES), 0)
           == lax.broadcasted_iota(I32, (PEER_E, LANES), 1))
    ones_rows = jnp.ones((SUBLANES, LANES), BF16)
    ones_sq = jnp.ones((LANES, LANES), BF16)
    ngroup = PEER_E // SUBLANES

    def hidden(a):
        groups = []
        for g in range(ngroup):
            prods = [expert_u(a, g * SUBLANES + r) * x8[a] for r in range(SUBLANES)]
            groups.append(_sublane_sums(prods, sub))
            issue_some(3 - g % 2)
        return jnp.concatenate(groups, axis=0)

    def expert_weights(a, q):
        q_hi = q.astype(BF16)
        q_lo = (q - q_hi.astype(F32)).astype(BF16)
        hid = (lax.dot_general(ones_rows, q_hi, NT_DIMS, preferred_element_type=F32)
               + lax.dot_general(ones_rows, q_lo, NT_DIMS, preferred_element_type=F32))
        issue_some(PEER_MID_ROWS // 2)
        w = _gelu(hid[0:1]) * gates[a]
        wd = jnp.where(eye, jnp.broadcast_to(w, (PEER_E, LANES)), 0.0).astype(BF16)
        wcol = jnp.dot(wd, ones_sq, preferred_element_type=F32)
        issue_some(PEER_MID_ROWS // 2)
        return wcol

    def combine(a, wcol):
        out = jnp.zeros((SUBLANES, LANES), F32)
        for g in range(ngroup):
            for r in range(SUBLANES):
                j = g * SUBLANES + r
                out = out + wcol[j:j + 1, :] * expert_v(a, j)
            issue_some(3 - g % 2)
        store(a, out)

    wcol0 = expert_weights(0, hidden(0))
    wcol1 = expert_weights(1, hidden(1))
    combine(0, wcol0)
    combine(1, wcol1)


def _peer_finish(h1_ref, acc_ref, fg_ref, out_ref):
    hsum = h1_ref[...] + acc_ref[...]
    ms = jnp.mean(hsum * hsum, axis=(1, 2), keepdims=True)
    _tiles_to_rows(hsum * lax.rsqrt(ms + RMS_EPS) * fg_ref[...], out_ref)


def _peerffn_body(tb, idx_ref, gate_ref, c_ref, h1_ref, fg_ref, uv_ref, out_ref, buf_ref,
                  acc_ref, sem_ref):
    npairs = tb // 2
    ahead = PEER_RING - 1

    def row_copy(t, j, slot):
        return pltpu.make_async_copy(uv_ref.at[idx_ref[t, j]], buf_ref.at[slot, j],
                                     sem_ref.at[slot])

    def wait_all(slot):
        pltpu.make_async_copy(uv_ref.at[pl.ds(0, PEER_E)], buf_ref.at[slot],
                              sem_ref.at[slot]).wait()

    def ring(i):
        base = 2 * (i % PEER_RING)
        return (base, base + 1)

    def pair(i, prefetch):
        toks = (2 * i, 2 * i + 1)
        slots = ring(i)
        nslots = ring(i + ahead)
        todo = [(a, j) for a in range(2) for j in range(PEER_E)]

        def issue_some(n):
            for a, j in todo[:n]:
                if prefetch:
                    row_copy(toks[a] + 2 * ahead, j, nslots[a]).start(priority=j % 2)
            del todo[:n]

        wait_all(slots[0])
        wait_all(slots[1])

        def store(a, out):
            acc_ref[toks[a]] = out

        _peer_pair_math(
            lambda a, j: _word_hi(buf_ref[slots[a], j]),
            lambda a, j: _word_lo(buf_ref[slots[a], j]),
            [c_ref[t] for t in toks], [gate_ref[pl.ds(t, 1), :] for t in toks], store, issue_some)
        assert not todo, "every prefetch row DMA must be issued exactly once"

    for i in range(ahead):
        for a, slot in enumerate(ring(i)):
            for j in range(PEER_E):
                row_copy(2 * i + a, j, slot).start(priority=j % 2)

    def body(i, carry):
        pair(i, True)
        return carry

    lax.fori_loop(0, npairs - ahead, body, 0)
    for i in range(npairs - ahead, npairs):
        pair(i, False)
    _peer_finish(h1_ref, acc_ref, fg_ref, out_ref)


def _peer_ffn(idx, gate, c3, h13, final_g3, uv_tiles):
    t = h13.shape[0]
    tb = 256
    row3 = pl.BlockSpec((tb, SUBLANES, LANES), lambda i: (i, 0, 0))
    return pl.pallas_call(
        functools.partial(_peerffn_body, tb),
        grid=(t // tb,),
        in_specs=[pl.BlockSpec((tb, PEER_E), lambda i: (i, 0), memory_space=pltpu.SMEM),
                  pl.BlockSpec((tb, PEER_E), lambda i: (i, 0)), row3, row3,
                  pl.BlockSpec((1, SUBLANES, LANES), lambda i: (0, 0, 0)),
                  pl.BlockSpec(memory_space=pl.ANY)],
        out_specs=pl.BlockSpec((tb, SUBLANES * LANES), lambda i: (i, 0)),
        out_shape=jax.ShapeDtypeStruct((t, SUBLANES * LANES), F32),
        scratch_shapes=[pltpu.VMEM((2 * PEER_RING, PEER_E, SUBLANES, LANES), I32),
                        pltpu.VMEM((tb, SUBLANES, LANES), F32),
                        pltpu.SemaphoreType.DMA((2 * PEER_RING,))],
        compiler_params=_params("arbitrary"),
        name="peer_ffn",
    )(idx, gate, c3, h13, final_g3, uv_tiles)


def _overlap_table(seq):
    ci = jnp.arange(seq // CMP_STRIDE)[None, :] * CMP_STRIDE
    sj = jnp.arange(SEL_BLOCK)[:, None] * SEL_BLOCK
    return ((ci < sj + SEL_BLOCK) & (ci + CMP_BLOCK > sj)).astype(F32)


def _cmp_blocks(kv):
    b, g, s, dh = kv.shape
    r = kv.reshape(b * g, s // CMP_STRIDE, CMP_STRIDE * dh)
    return jnp.concatenate([r, jnp.roll(r, -1, axis=1)], axis=-1)


def _pad_cmp_params(pe, w1):
    pe_p = jnp.pad(pe, ((0, 0), (0, LANES - HEAD))).reshape(1, -1)
    w1_p = jnp.pad(w1.reshape(CMP_BLOCK, HEAD, -1), ((0, 0), (0, LANES - HEAD), (0, 0)))
    return pe_p, w1_p.reshape(CMP_BLOCK * LANES, -1).astype(BF16)


def _layer(h, lidx, attn_norm, w_in, lq1, lk1, lq2, lk2, subln, pe_k, pe_v, w1k, w1v, w2k, w2v,
           p_da, p_nsa, w_o, ffn_norm, wq, k1, k2, pu, pv, out_norm):
    b, s, d = h.shape
    t = b * s
    g, hg = NSA_GROUPS, NSA_HG
    lambda_init = 0.8 - 0.6 * math.exp(-0.3 * lidx)
    x2 = h.reshape(t, d)

    qda, kda, vda, qn, cmp_in, ksx, kwx, vsw, gm, gn = _in_proj(
        x2, attn_norm.reshape(1, d), _pack_w_in(w_in), _rope_tables(s), s)
    tr = lambda a2: jnp.swapaxes(a2.reshape(b, s, -1), 1, 2)
    ydat = _diff_attn(tr(qda), kda.reshape(b, s, -1), tr(vda),
                      lq1.reshape(1, -1), lk1.reshape(1, -1), lq2.reshape(1, -1),
                      lk2.reshape(1, -1), subln.reshape(-1, 1), lambda_init)
    yda = jnp.swapaxes(ydat, 1, 2).reshape(t, -1)

    assert s // SEL_BLOCK <= SEL_BLOCK, "selection bias rows hold at most 64 blocks"
    cmp4 = jnp.swapaxes(cmp_in.reshape(b, s, 2 * g, LANES), 1, 2)
    pe_kp, w1_kp = _pad_cmp_params(pe_k, w1k)
    pe_vp, w1_vp = _pad_cmp_params(pe_v, w1v)
    kca, vc = _compress(_cmp_blocks(cmp4[:, :g]), _cmp_blocks(cmp4[:, g:]), pe_kp, pe_vp,
                        w1_kp, w1_vp, w2k.astype(BF16), w2v.astype(BF16))
    ncp = s // CMP_STRIDE
    kca = kca.reshape(b, g, ncp, LANES)
    vct = jnp.swapaxes(vc.reshape(b, g, ncp, HEAD), 2, 3)

    qnt = tr(qn)
    ocmp_t, bias_t = _cmp_select(qnt, kca, vct, _overlap_table(s))
    onehot = (jnp.arange(s)[:, None] // SEL_BLOCK == jnp.arange(HEAD)[None, :]).astype(BF16)
    pad_hot = jnp.concatenate([jnp.zeros_like(onehot), onehot] * g, axis=1)
    ksa = ksx.reshape(b, s, -1) + pad_hot[None]
    ynt = _sel_win(qnt, bias_t, ksa, kwx.reshape(b, s, -1), tr(vsw), ocmp_t, tr(gn))
    yn = jnp.swapaxes(ynt, 1, 2).reshape(t, -1)

    h1_tiles, c_tiles, s1t, s2t = _out_proj(
        yda, yn, gm, x2, p_da.astype(BF16), p_nsa.astype(BF16), w_o.astype(BF16),
        ffn_norm.reshape(1, d), wq.astype(BF16), k1.astype(BF16), k2.astype(BF16))
    idx_t, gate_t = _peer_topk(s1t, s2t)
    idx = idx_t.reshape(PEER_E, t).T
    gate = gate_t.reshape(PEER_E, t).T
    out = _peer_ffn(idx, gate, c_tiles, h1_tiles, out_norm.reshape(1, SUBLANES, LANES),
                    _pack_expert_rows(pu, pv))
    return out.reshape(b, s, d)


def kernel(x, attn_norm, w_in, da_lambda_q1, da_lambda_k1, da_lambda_q2, da_lambda_k2, da_subln,
           cmp_pe_k, cmp_pe_v, cmp_w1_k, cmp_w1_v, cmp_w2_k, cmp_w2_v, p_da, p_nsa, w_o,
           ffn_norm, peer_wq, peer_k1, peer_k2, peer_u, peer_v, final_norm):
    depth = attn_norm.shape[0]
    assert depth == 1, "the final norm is fused into the last layer's PEER kernel"
    h = x
    for l in range(depth):
        h = _layer(h, l, attn_norm[l], w_in[l], da_lambda_q1[l], da_lambda_k1[l], da_lambda_q2[l],
                   da_lambda_k2[l], da_subln[l], cmp_pe_k[l], cmp_pe_v[l], cmp_w1_k[l],
                   cmp_w1_v[l], cmp_w2_k[l], cmp_w2_v[l], p_da[l], p_nsa[l], w_o[l], ffn_norm[l],
                   peer_wq[l], peer_k1[l], peer_k2[l], peer_u[l], peer_v[l], final_norm)
    return h
```

```python
import functools
import math

import jax
import jax.numpy as jnp
from jax import lax
from jax.experimental import pallas as pl
from jax.experimental.pallas import tpu as pltpu

F32 = jnp.float32
BF16 = jnp.bfloat16
I32 = jnp.int32

RMS_EPS = 1e-6
ROPE_THETA = 500000.0
ROPE_HALF = 8
HEAD = 64
DA_HEADS = 4
NSA_GROUPS = 2
NSA_HG = 4
CMP_STRIDE = 16
CMP_BLOCK = 32
SEL_BLOCK = 64
SEL_TOPK = 16
WINDOW = 512
FORCE_BONUS = 1e4
NEG_BIG = -1e30
SEL_MASK_BIAS = -32768.0
PEER_HEADS = 8
PEER_NKEYS = 128
PEER_TOPK = 16
LANES = 128
SUBLANES = 8
VMEM_LIMIT = 56 * 1024 * 1024

NT_DIMS = (((1,), (1,)), ((), ()))


def _rmsnorm(x, g):
    return x * lax.rsqrt(jnp.mean(x * x, axis=-1, keepdims=True) + RMS_EPS) * g


def _sigmoid(z):
    return 1.0 / (1.0 + jnp.exp(-z))


def _gelu(z):
    return 0.5 * z * (1.0 + lax.erf(z * (2.0 ** -0.5)))


def _params(*sem):
    return pltpu.CompilerParams(dimension_semantics=sem, vmem_limit_bytes=VMEM_LIMIT)


def _sublane_transpose(v):
    sub = lax.broadcasted_iota(I32, (SUBLANES, LANES), 0)
    v = list(v)
    for dist in (4, 2, 1):
        low = (sub & dist) == 0
        nxt = list(v)
        for i in range(SUBLANES):
            if i & dist == 0:
                nxt[i] = jnp.where(low, v[i], pltpu.roll(v[i + dist], dist, 0))
                nxt[i + dist] = jnp.where(low, pltpu.roll(v[i], SUBLANES - dist, 0), v[i + dist])
        v = nxt
    return v


def _rows_to_tiles(x, tile_ref):
    for g in range(x.shape[0] // SUBLANES):
        rows = slice(g * SUBLANES, (g + 1) * SUBLANES)
        tiles = _sublane_transpose([x[rows, c * LANES:(c + 1) * LANES] for c in range(SUBLANES)])
        for e in range(SUBLANES):
            tile_ref[g * SUBLANES + e] = tiles[e]


def _tiles_to_rows(t, row_ref):
    for g in range(t.shape[0] // SUBLANES):
        chunks = _sublane_transpose([t[g * SUBLANES + e] for e in range(SUBLANES)])
        for c in range(SUBLANES):
            row_ref[g * SUBLANES:(g + 1) * SUBLANES, c * LANES:(c + 1) * LANES] = chunks[c]


_QDA0, _KDA0, _VDA0, _QN0 = 0, 512, 1024, 1536
_KC0, _VC0, _KS0, _KW0, _VSW0, _GM0, _GN0, _WCOLS = 2048, 2304, 2560, 2816, 3072, 3328, 5376, 5632


def _inproj_body(x_ref, g_ref, w_ref, rc_ref, rs1_ref, rs2_ref,
                 qda_ref, kda_ref, vda_ref, qn_ref, cmp_ref, ks_ref, kw_ref, vsw_ref, gm_ref, gn_ref):
    a = _rmsnorm(x_ref[...], g_ref[...]).astype(BF16)
    rc, rs1, rs2 = rc_ref[...], rs1_ref[...], rs2_ref[...]

    def rope(z):
        return (z * rc + pltpu.roll(z, ROPE_HALF, 1) * rs1
                + pltpu.roll(z, LANES - ROPE_HALF, 1) * rs2)

    def proj(c0):
        return jnp.dot(a, w_ref[:, c0:c0 + 256], preferred_element_type=F32)

    def rope2(z):
        return jnp.concatenate([rope(z[:, :LANES]), rope(z[:, LANES:])], axis=1)

    for c in range(2):
        qda_ref[:, c * 256:(c + 1) * 256] = (rope2(proj(_QDA0 + c * 256)) * 0.125).astype(BF16)
        kda_ref[:, c * 256:(c + 1) * 256] = rope2(proj(_KDA0 + c * 256)).astype(BF16)
        vda_ref[:, c * 256:(c + 1) * 256] = proj(_VDA0 + c * 256).astype(BF16)
        qn_ref[:, c * 256:(c + 1) * 256] = (rope2(proj(_QN0 + c * 256)) * 0.125).astype(BF16)
    cmp_ref[:, 0:256] = rope2(proj(_KC0)).astype(BF16)
    cmp_ref[:, 256:512] = proj(_VC0).astype(BF16)
    ks_ref[...] = rope2(proj(_KS0)).astype(BF16)
    kw_ref[...] = rope2(proj(_KW0)).astype(BF16)
    vsw_ref[...] = proj(_VSW0).astype(BF16)
    for c in range(8):
        gm_ref[:, c * 256:(c + 1) * 256] = _sigmoid(proj(_GM0 + c * 256)).astype(BF16)
    gn_ref[...] = _sigmoid(proj(_GN0))


def _pack_w_in(w):
    d = w.shape[0]
    zeros = lambda n: jnp.zeros((d, n), w.dtype)

    def spread_groups(c0):
        return [w[:, c0:c0 + HEAD], zeros(HEAD), w[:, c0 + HEAD:c0 + 2 * HEAD], zeros(HEAD)]

    kc, vc, ks, vs, kw, vw = (2048 + 128 * i for i in range(6))
    gn = w[:, 2816:2840]
    cols = ([w[:, :2048]] + spread_groups(kc) + spread_groups(vc) + spread_groups(ks)
            + spread_groups(kw) + [w[:, vs:vs + 128], w[:, vw:vw + 128], w[:, 2840:],
                                   gn[:, :12], zeros(LANES - 12), gn[:, 12:], zeros(LANES - 12)])
    packed = jnp.concatenate(cols, axis=1).astype(BF16)
    assert packed.shape[1] == _WCOLS
    return packed


def _rope_tables(seq):
    inv = jnp.power(ROPE_THETA, -jnp.arange(ROPE_HALF, dtype=F32) * 2.0 / (2 * ROPE_HALF))
    ang = jnp.arange(seq, dtype=F32)[:, None] * inv[None, :]
    cos, sin = jnp.cos(ang), jnp.sin(ang)
    one = jnp.ones((seq, HEAD - 2 * ROPE_HALF), F32)
    zero8 = jnp.zeros((seq, ROPE_HALF), F32)
    zero48 = jnp.zeros_like(one)
    rc = jnp.concatenate([cos, cos, one], axis=1)
    rs1 = jnp.concatenate([zero8, sin, zero48], axis=1)
    rs2 = jnp.concatenate([-sin, zero8, zero48], axis=1)
    return tuple(jnp.concatenate([t, t], axis=1) for t in (rc, rs1, rs2))


def _in_proj(x2, norm_g, w_packed, rope_tabs, seq):
    t, d = x2.shape
    tm = 512
    nseq = seq // tm
    row = lambda i: (i, 0)
    fixed = lambda i: (0, 0)
    out_shapes = [
        jax.ShapeDtypeStruct((t, 512), BF16), jax.ShapeDtypeStruct((t, 512), BF16),
        jax.ShapeDtypeStruct((t, 512), BF16), jax.ShapeDtypeStruct((t, 512), BF16),
        jax.ShapeDtypeStruct((t, 512), BF16), jax.ShapeDtypeStruct((t, 256), BF16),
        jax.ShapeDtypeStruct((t, 256), BF16), jax.ShapeDtypeStruct((t, 256), BF16),
        jax.ShapeDtypeStruct((t, 2048), BF16), jax.ShapeDtypeStruct((t, 256), F32)]
    rope_spec = pl.BlockSpec((tm, LANES), lambda i: (i % nseq, 0))
    return pl.pallas_call(
        _inproj_body,
        grid=(t // tm,),
        in_specs=[pl.BlockSpec((tm, d), row), pl.BlockSpec((1, d), fixed),
                  pl.BlockSpec((d, _WCOLS), fixed), rope_spec, rope_spec, rope_spec],
        out_specs=[pl.BlockSpec((tm, s.shape[1]), row) for s in out_shapes],
        out_shape=out_shapes,
        compiler_params=_params("parallel"),
        name="in_proj",
    )(x2, norm_g, w_packed, *rope_tabs)


def _softmax_step(s, vt, m_ref, l_ref, acc_ref):
    m_prev = m_ref[...]
    m_new = jnp.maximum(m_prev, jnp.max(s, axis=0, keepdims=True))
    alpha = jnp.exp(m_prev - m_new)
    p = jnp.exp(s - m_new)
    l_ref[...] = alpha * l_ref[...] + jnp.sum(p, axis=0, keepdims=True)
    acc_ref[...] = alpha * acc_ref[...] + jnp.dot(vt, p.astype(BF16), preferred_element_type=F32)
    m_ref[...] = m_new


def _softmax_reset(m_ref, l_ref, acc_ref):
    m_ref[...] = jnp.full(m_ref.shape, NEG_BIG, F32)
    l_ref[...] = jnp.zeros(l_ref.shape, F32)
    acc_ref[...] = jnp.zeros(acc_ref.shape, F32)


def _attend_tiles(n_full, scores, values, mask_last, sa_ref, sb_ref, m_ref, l_ref, acc_ref):
    step = lambda s, t: _softmax_step(s, values(t), m_ref, l_ref, acc_ref)
    sa_ref[...] = scores(0)

    def two_tiles(i, carry):
        t = 2 * i
        sb_ref[...] = scores(t + 1)
        step(sa_ref[...], t)
        sa_ref[...] = scores(t + 2)
        step(sb_ref[...], t + 1)
        return carry

    lax.fori_loop(0, n_full // 2, two_tiles, 0)
    odd = (n_full & 1) == 1

    @pl.when(odd)
    def _():
        sb_ref[...] = scores(n_full)
        step(sa_ref[...], n_full - 1)
        step(mask_last(sb_ref[...]), n_full)

    @pl.when(jnp.logical_not(odd))
    def _():
        step(mask_last(sa_ref[...]), n_full)


def _key_tile(ref, kt, bk):
    return ref[(0,) * (len(ref.shape) - 2) + (pl.ds(pl.multiple_of(kt * bk, bk), bk), slice(None))]


def _value_tile(ref, kt, bk):
    return ref[(0,) * (len(ref.shape) - 2) + (slice(None), pl.ds(pl.multiple_of(kt * bk, bk), bk))]


def _diffattn_body(lambda_init, bq, q_ref, k_ref, v_ref, lq1_ref, lk1_ref, lq2_ref, lk2_ref,
                   sub_ref, y_ref, qbd_ref, m_ref, l_ref, acc_ref, sa_ref, sb_ref):
    qi = pl.program_id(2)
    bk = bq
    qt = q_ref[0]
    sub = lax.broadcasted_iota(I32, qt.shape, 0)
    zero = jnp.zeros_like(qt)
    qbd_ref[:, 0:bq] = jnp.where(sub < HEAD, qt, zero)
    qbd_ref[:, bq:2 * bq] = jnp.where(sub >= HEAD, qt, zero)
    _softmax_reset(m_ref, l_ref, acc_ref)

    def scores(kt):
        return jnp.dot(_key_tile(k_ref, kt, bk), qbd_ref[...], preferred_element_type=F32)

    def causal(s):
        r = lax.broadcasted_iota(I32, (bk, 2 * bq), 0)
        c = lax.broadcasted_iota(I32, (bk, 2 * bq), 1) & (bq - 1)
        return jnp.where(r <= c, s, NEG_BIG)

    _attend_tiles(qi, scores, lambda kt: _value_tile(v_ref, kt, bk), causal,
                  sa_ref, sb_ref, m_ref, l_ref, acc_ref)

    o = acc_ref[...] / l_ref[...]
    lam = (jnp.exp(jnp.sum(lq1_ref[...] * lk1_ref[...], axis=1, keepdims=True))
           - jnp.exp(jnp.sum(lq2_ref[...] * lk2_ref[...], axis=1, keepdims=True)) + lambda_init)
    d = o[:, 0:bq] - lam * o[:, bq:2 * bq]
    ms = jnp.mean(d * d, axis=0, keepdims=True)
    y = d * lax.rsqrt(ms + RMS_EPS) * sub_ref[...] * (1.0 - lambda_init)
    y_ref[0] = y.astype(BF16)


def _diff_attn(qdat, kda, vdat, lq1, lk1, lq2, lk2, subln_col, lambda_init):
    b, s, _ = kda.shape
    bq = 512
    vec = lambda n: pl.BlockSpec((1, n), lambda bi, h, qi: (0, 0))
    dv = 2 * HEAD
    qtile = pl.BlockSpec((1, dv, bq), lambda bi, h, qi: (bi, h, qi))
    return pl.pallas_call(
        functools.partial(_diffattn_body, lambda_init, bq),
        grid=(b, DA_HEADS, s // bq),
        in_specs=[qtile,
                  pl.BlockSpec((1, s, LANES), lambda bi, h, qi: (bi, 0, h)),
                  pl.BlockSpec((1, dv, s), lambda bi, h, qi: (bi, h, 0)),
                  vec(HEAD), vec(HEAD), vec(HEAD), vec(HEAD),
                  pl.BlockSpec((dv, 1), lambda bi, h, qi: (0, 0))],
        out_specs=qtile,
        out_shape=jax.ShapeDtypeStruct((b, DA_HEADS * dv, s), BF16),
        scratch_shapes=[pltpu.VMEM((LANES, 2 * bq), BF16), pltpu.VMEM((1, 2 * bq), F32),
                        pltpu.VMEM((1, 2 * bq), F32), pltpu.VMEM((dv, 2 * bq), F32),
                        pltpu.VMEM((bq, 2 * bq), F32), pltpu.VMEM((bq, 2 * bq), F32)],
        compiler_params=_params("parallel", "parallel", "parallel"),
        name="diff_attn",
    )(qdat, kda, vdat, lq1, lk1, lq2, lk2, subln_col)


def _compress_body(xk_ref, xv_ref, pek_ref, pev_ref, w1k_ref, w1v_ref, w2k_ref, w2v_ref,
                   kc_ref, vc_ref):
    def mlp(x_ref, pe_ref, w1_ref, w2_ref):
        blocks = (x_ref[0].astype(F32) + pe_ref[...]).astype(BF16)
        hid = _gelu(jnp.dot(blocks, w1_ref[...], preferred_element_type=F32))
        return jnp.dot(hid.astype(BF16), w2_ref[...], preferred_element_type=F32)

    kc = mlp(xk_ref, pek_ref, w1k_ref, w2k_ref)
    kc_ref[0] = jnp.concatenate([kc, jnp.zeros_like(kc)], axis=1).astype(BF16)
    vc_ref[0] = mlp(xv_ref, pev_ref, w1v_ref, w2v_ref).astype(BF16)


def _compress(xk, xv, pe_k, pe_v, w1k, w1v, w2k, w2v):
    n, ncp, width = xk.shape
    blk = pl.BlockSpec((1, ncp, width), lambda i: (i, 0, 0))
    fixed = lambda shape: pl.BlockSpec(shape, lambda i: (0, 0))
    return pl.pallas_call(
        _compress_body,
        grid=(n,),
        in_specs=[blk, blk, fixed((1, width)), fixed((1, width)), fixed((width, HEAD)),
                  fixed((width, HEAD)), fixed((HEAD, HEAD)), fixed((HEAD, HEAD))],
        out_specs=[pl.BlockSpec((1, ncp, LANES), lambda i: (i, 0, 0)),
                   pl.BlockSpec((1, ncp, HEAD), lambda i: (i, 0, 0))],
        out_shape=[jax.ShapeDtypeStruct((n, ncp, LANES), BF16),
                   jax.ShapeDtypeStruct((n, ncp, HEAD), BF16)],
        compiler_params=_params("parallel"),
        name="compress",
    )(xk, xv, pe_k, pe_v, w1k, w1v, w2k, w2v)


def _heads_on_lanes(qt, bq):
    return jnp.concatenate([qt[h * HEAD:(h + 1) * HEAD, :] for h in range(NSA_HG)], axis=1)


def _cmpsel_body(bq, q_ref, kc_ref, vc_ref, ovl_ref, ocmp_ref, bias_ref):
    qi = pl.program_id(2)
    ncp = kc_ref.shape[2]
    rows = NSA_HG * bq
    q2 = _heads_on_lanes(q_ref[0], bq)
    qz = jnp.concatenate([q2, jnp.zeros_like(q2)], axis=0)
    s = jnp.dot(kc_ref[0, 0], qz, preferred_element_type=F32)
    n = lax.broadcasted_iota(I32, (ncp, rows), 0)
    qpos = qi * bq + (lax.broadcasted_iota(I32, (ncp, rows), 1) & (bq - 1))
    cmask = n * CMP_STRIDE + (CMP_BLOCK - 1) <= qpos
    s = jnp.where(cmask, s, NEG_BIG)
    e = jnp.exp(s - jnp.max(s, axis=0, keepdims=True))
    p = jnp.where(cmask, e / jnp.sum(e, axis=0, keepdims=True), 0.0)
    o = jnp.dot(vc_ref[0, 0], p.astype(BF16), preferred_element_type=F32)
    for h in range(NSA_HG):
        ocmp_ref[0, 0, h * HEAD:(h + 1) * HEAD, :] = o[:, h * bq:(h + 1) * bq]

    psum = p[:, 0:bq] + p[:, bq:2 * bq] + p[:, 2 * bq:3 * bq] + p[:, 3 * bq:4 * bq]
    imp = jnp.dot(ovl_ref[...], psum, preferred_element_type=F32)
    blk = lax.broadcasted_iota(I32, (SEL_BLOCK, bq), 0)
    pos = qi * bq + lax.broadcasted_iota(I32, (SEL_BLOCK, bq), 1)
    cur = lax.shift_right_logical(pos, 6)
    valid = blk <= cur
    forced = (blk == 0) | (blk == cur) | (blk == cur - 1)
    score = jnp.where(valid, imp + jnp.where(forced, FORCE_BONUS, 0.0), -jnp.inf)
    rank = jnp.zeros((SEL_BLOCK, bq), I32)
    for i in range(SEL_BLOCK):
        other = score[i:i + 1, :]
        beats = (other > score) | ((other == score) & (blk > i))
        rank = rank + beats.astype(I32)
    keep = valid & (rank < SEL_TOPK)
    bias_ref[0, 0] = jnp.where(keep, 0.0, SEL_MASK_BIAS).astype(BF16)


def _cmp_select(qnt, kca, vct, overlap_t):
    b, g, ncp, _ = kca.shape
    s = qnt.shape[2]
    bq = 128
    dq = NSA_HG * HEAD
    return pl.pallas_call(
        functools.partial(_cmpsel_body, bq),
        grid=(b, g, s // bq),
        in_specs=[pl.BlockSpec((1, dq, bq), lambda bi, gi, qi: (bi, gi, qi)),
                  pl.BlockSpec((1, 1, ncp, LANES), lambda bi, gi, qi: (bi, gi, 0, 0)),
                  pl.BlockSpec((1, 1, HEAD, ncp), lambda bi, gi, qi: (bi, gi, 0, 0)),
                  pl.BlockSpec((SEL_BLOCK, ncp), lambda bi, gi, qi: (0, 0))],
        out_specs=[pl.BlockSpec((1, 1, dq, bq), lambda bi, gi, qi: (bi, gi, 0, qi)),
                   pl.BlockSpec((1, 1, SEL_BLOCK, bq), lambda bi, gi, qi: (bi, gi, 0, qi))],
        out_shape=[jax.ShapeDtypeStruct((b, g, dq, s), F32),
                   jax.ShapeDtypeStruct((b, g, SEL_BLOCK, s), BF16)],
        compiler_params=_params("parallel", "parallel", "parallel"),
        name="cmp_select",
    )(qnt, kca, vct, overlap_t)


def _selwin_body(bq, bks, q_ref, bias_ref, ks_ref, vs_ref, kw_ref, vw_ref, ocmp_ref, gate_ref,
                 y_ref, m_ref, l_ref, acc_ref, qa_ref, sa_ref, sb_ref):
    qi = pl.program_id(2)
    rows = NSA_HG * bq
    q2 = _heads_on_lanes(q_ref[0], bq)
    bias = bias_ref[0, 0]
    qa = jnp.concatenate([q2, jnp.concatenate([bias] * NSA_HG, axis=1)], axis=0)
    qw = jnp.concatenate([q2, jnp.zeros_like(q2)], axis=0)
    r = lax.broadcasted_iota(I32, (bq, rows), 0)
    c = lax.broadcasted_iota(I32, (bq, rows), 1) & (bq - 1)

    _softmax_reset(m_ref, l_ref, acc_ref)

    qa_ref[...] = qa
    last = (qi * bq) // bks

    def causal(s):
        kpos = last * bks + lax.broadcasted_iota(I32, (bks, rows), 0)
        qpos = qi * bq + (lax.broadcasted_iota(I32, (bks, rows), 1) & (bq - 1))
        return jnp.where(kpos <= qpos, s, NEG_BIG)

    _attend_tiles(last,
                  lambda kt: jnp.dot(_key_tile(ks_ref, kt, bks), qa_ref[...],
                                     preferred_element_type=F32),
                  lambda kt: _value_tile(vs_ref, kt, bks), causal,
                  sa_ref, sb_ref, m_ref, l_ref, acc_ref)
    o_sel = acc_ref[...] / l_ref[...]

    _softmax_reset(m_ref, l_ref, acc_ref)
    backs = list(range(WINDOW // bq, -1, -1))
    tiles = [jnp.maximum(qi - back, 0) for back in backs]
    raw = [jnp.dot(_key_tile(kw_ref, kt, bq), qw, preferred_element_type=F32) for kt in tiles]
    for back, kt, s in zip(backs, tiles, raw):
        dist = c + back * bq - r
        inside = (dist >= 0) & (dist < WINDOW) & (qi >= back)
        _softmax_step(jnp.where(inside, s, NEG_BIG), _value_tile(vw_ref, kt, bq),
                      m_ref, l_ref, acc_ref)

    o_win = acc_ref[...] / l_ref[...]
    gate = gate_ref[0]
    for h in range(NSA_HG):
        sl = slice(h * bq, (h + 1) * bq)
        y = (gate[3 * h:3 * h + 1, :] * ocmp_ref[0, 0, h * HEAD:(h + 1) * HEAD, :]
             + gate[3 * h + 1:3 * h + 2, :] * o_sel[:, sl]
             + gate[3 * h + 2:3 * h + 3, :] * o_win[:, sl])
        y_ref[0, h * HEAD:(h + 1) * HEAD, :] = y.astype(BF16)


def _sel_win(qnt, bias_t, ksa, kwa, vswt, ocmp_t, gnt):
    b, s, _ = ksa.shape
    g = NSA_GROUPS
    bq, bks = 256, 512
    dq = NSA_HG * HEAD
    rows = NSA_HG * bq
    keys = pl.BlockSpec((1, s, LANES), lambda bi, gi, qi: (bi, 0, gi))
    vals = lambda first: pl.BlockSpec((1, HEAD, s), lambda bi, gi, qi: (bi, first + gi, 0))
    qtile = pl.BlockSpec((1, dq, bq), lambda bi, gi, qi: (bi, gi, qi))
    return pl.pallas_call(
        functools.partial(_selwin_body, bq, bks),
        grid=(b, g, s // bq),
        in_specs=[qtile,
                  pl.BlockSpec((1, 1, SEL_BLOCK, bq), lambda bi, gi, qi: (bi, gi, 0, qi)),
                  keys, vals(0), keys, vals(g),
                  pl.BlockSpec((1, 1, dq, bq), lambda bi, gi, qi: (bi, gi, 0, qi)),
                  pl.BlockSpec((1, LANES, bq), lambda bi, gi, qi: (bi, gi, qi))],
        out_specs=qtile,
        out_shape=jax.ShapeDtypeStruct((b, g * dq, s), BF16),
        scratch_shapes=[pltpu.VMEM((1, rows), F32), pltpu.VMEM((1, rows), F32),
                        pltpu.VMEM((HEAD, rows), F32), pltpu.VMEM((LANES, rows), BF16),
                        pltpu.VMEM((bks, rows), F32), pltpu.VMEM((bks, rows), F32)],
        compiler_params=_params("parallel", "parallel", "parallel"),
        name="sel_win",
    )(qnt, bias_t, ksa, vswt, kwa, vswt, ocmp_t, gnt)


def _outproj_body(yda_ref, yn_ref, gm_ref, x_ref, pda_ref, pnsa_ref, wo_ref, fg_ref, wq_ref,
                  k1_ref, k2_ref, h1_ref, c_ref, s1_ref, s2_ref):
    d = x_ref.shape[1]
    a = jnp.dot(yda_ref[...], pda_ref[...], preferred_element_type=F32)
    bn = jnp.dot(yn_ref[...], pnsa_ref[...], preferred_element_type=F32)
    merged = gm_ref[:, :d].astype(F32) * a + gm_ref[:, d:].astype(F32) * bn
    h1 = x_ref[...] + jnp.dot(merged.astype(BF16), wo_ref[...], preferred_element_type=F32)
    c = _rmsnorm(h1, fg_ref[...])
    _rows_to_tiles(h1, h1_ref)
    _rows_to_tiles(c, c_ref)
    cb = c.astype(BF16)
    for h in range(PEER_HEADS):
        qh = jnp.dot(cb, wq_ref[:, h * 256:(h + 1) * 256], preferred_element_type=F32).astype(BF16)
        s1_ref[h] = lax.dot_general(k1_ref[...], qh[:, :LANES], NT_DIMS, preferred_element_type=F32)
        s2_ref[h] = lax.dot_general(k2_ref[...], qh[:, LANES:], NT_DIMS, preferred_element_type=F32)


def _out_proj(yda, yn, gm, x2, pda, pnsa, wo, ffn_g, wq, k1, k2):
    t, d = x2.shape
    tm = 512
    row = lambda w: pl.BlockSpec((tm, w), lambda i: (i, 0))
    fixed = lambda a: pl.BlockSpec(a.shape, lambda i: (0, 0))
    sspec = pl.BlockSpec((PEER_HEADS, PEER_NKEYS, tm), lambda i: (0, 0, i))
    assert d == SUBLANES * LANES
    tiles = pl.BlockSpec((tm, SUBLANES, LANES), lambda i: (i, 0, 0))
    return pl.pallas_call(
        _outproj_body,
        grid=(t // tm,),
        in_specs=[row(512), row(512), row(2 * d), row(d), fixed(pda), fixed(pnsa), fixed(wo),
                  fixed(ffn_g), fixed(wq), fixed(k1), fixed(k2)],
        out_specs=[tiles, tiles, sspec, sspec],
        out_shape=[jax.ShapeDtypeStruct((t, SUBLANES, LANES), F32),
                   jax.ShapeDtypeStruct((t, SUBLANES, LANES), F32),
                   jax.ShapeDtypeStruct((PEER_HEADS, PEER_NKEYS, t), F32),
                   jax.ShapeDtypeStruct((PEER_HEADS, PEER_NKEYS, t), F32)],
        compiler_params=_params("parallel"),
        name="out_proj",
    )(yda, yn, gm, x2, pda, pnsa, wo, ffn_g, wq, k1, k2)


def _batcher_pairs(n):
    pairs = []

    def merge(lo, hi, r):
        step = r * 2
        if step < hi - lo:
            merge(lo, hi, step)
            merge(lo + r, hi, step)
            pairs.extend((i, i + r) for i in range(lo + r, hi - r, step))
        else:
            pairs.append((lo, lo + r))

    def sort(lo, hi):
        if hi - lo >= 1:
            mid = lo + (hi - lo) // 2
            sort(lo, mid)
            sort(mid + 1, hi)
            merge(lo, hi, 1)

    sort(0, n - 1)
    return pairs


_NET16 = _batcher_pairs(PEER_TOPK)


def _cmpx(a, b):
    c = (a[0] > b[0]) | ((a[0] == b[0]) & (a[1] < b[1]))
    return ((jnp.where(c, a[0], b[0]), jnp.where(c, a[1], b[1])),
            (jnp.where(c, b[0], a[0]), jnp.where(c, b[1], a[1])))


def _sort_lists(lists, n_real):
    lists = list(lists)
    for i, j in _NET16:
        if j < n_real:
            lists[i], lists[j] = _cmpx(lists[i], lists[j])
    return lists


def _merge_top(a, b):
    k = PEER_TOPK
    lists = [_cmpx(a[i], b[k - 1 - i])[0] for i in range(k)]
    step = k // 2
    while step >= 1:
        for i in range(k):
            if i & step == 0:
                lists[i], lists[i + step] = _cmpx(lists[i], lists[i + step])
        step //= 2
    return lists


def _top16(pairs):
    k = PEER_TOPK
    filler = (jnp.full(pairs[0][0].shape, -jnp.inf, F32), jnp.zeros(pairs[0][1].shape, I32))
    groups = []
    for g0 in range(0, len(pairs), k):
        chunk = list(pairs[g0:g0 + k])
        groups.append(_sort_lists(chunk + [filler] * (k - len(chunk)), len(chunk)))
    while len(groups) > 1:
        groups = [_merge_top(groups[i], groups[i + 1]) if i + 1 < len(groups) else groups[i]
                  for i in range(0, len(groups), 2)]
    return groups[0]


def _product_key_select(s1, s2):
    k = PEER_TOPK
    n_exp = PEER_NKEYS * PEER_NKEYS
    ids = lambda n: [jnp.full(s1[0].shape, i, I32) for i in range(n)]
    l1 = _top16(list(zip(s1, ids(len(s1)))))
    l2 = _top16(list(zip(s2, ids(len(s2)))))
    pair = lambda a, b: (l1[a][0] + l2[b][0],
                         (a * k + b) * n_exp + l1[a][1] * PEER_NKEYS + l2[b][1])
    first_row = [pair(0, b) for b in range(k)]
    rest = [pair(a, b) for a in range(1, k) for b in range(k // (a + 1))]
    best = _merge_top(first_row, _top16(rest))
    return [(v, p & (n_exp - 1)) for v, p in best]


def _peertopk_body(tt, s1_ref, s2_ref, idx_ref, gate_ref):
    nblk = tt // LANES
    assert nblk == SUBLANES

    def keys_major(ref, h):
        tiles = []
        for kg in range(PEER_NKEYS // SUBLANES):
            rows = slice(kg * SUBLANES, (kg + 1) * SUBLANES)
            tiles += _sublane_transpose([ref[h, rows, b * LANES:(b + 1) * LANES]
                                         for b in range(nblk)])
        return tiles

    def store(ref, h, tiles):
        for g in range(PEER_TOPK // SUBLANES):
            blocks = _sublane_transpose(tiles[g * SUBLANES:(g + 1) * SUBLANES])
            for b in range(nblk):
                ref[h, g * SUBLANES:(g + 1) * SUBLANES, b * LANES:(b + 1) * LANES] = blocks[b]

    def head(h, carry):
        best = _product_key_select(keys_major(s1_ref, h), keys_major(s2_ref, h))
        ex = [jnp.exp(v - best[0][0]) for v, _ in best]
        z = ex[0]
        for e in ex[1:]:
            z = z + e
        store(gate_ref, h, [e / z for e in ex])
        store(idx_ref, h, [ix for _, ix in best])
        return carry

    lax.fori_loop(0, PEER_HEADS, head, 0)


def _peer_topk(s1t, s2t):
    _, _, t = s1t.shape
    tt = SUBLANES * LANES
    spec_in = pl.BlockSpec((PEER_HEADS, PEER_NKEYS, tt), lambda i: (0, 0, i))
    spec_out = pl.BlockSpec((PEER_HEADS, PEER_TOPK, tt), lambda i: (0, 0, i))
    return pl.pallas_call(
        functools.partial(_peertopk_body, tt),
        grid=(t // tt,),
        in_specs=[spec_in, spec_in],
        out_specs=[spec_out, spec_out],
        out_shape=[jax.ShapeDtypeStruct((PEER_HEADS, PEER_TOPK, t), I32),
                   jax.ShapeDtypeStruct((PEER_HEADS, PEER_TOPK, t), F32)],
        compiler_params=_params("parallel"),
        name="peer_topk",
    )(s1t, s2t)


PEER_E = PEER_HEADS * PEER_TOPK
PEER_RING = 3
PEER_MID_ROWS = 48


def _pack_body(u_ref, v_ref, out_ref):
    bf16_bits = lambda a: lax.bitcast_convert_type(a.astype(BF16).astype(F32), I32)
    words = bf16_bits(u_ref[...]) | lax.shift_right_logical(bf16_bits(v_ref[...]), jnp.int32(16))
    _rows_to_tiles(words, out_ref)


def _pack_expert_rows(pu, pv):
    n, d = pu.shape
    assert d == SUBLANES * LANES
    te = 256
    blk = pl.BlockSpec((te, d), lambda i: (i, 0))
    return pl.pallas_call(
        _pack_body,
        grid=(n // te,),
        in_specs=[blk, blk],
        out_specs=pl.BlockSpec((te, SUBLANES, LANES), lambda i: (i, 0, 0)),
        out_shape=jax.ShapeDtypeStruct((n, SUBLANES, LANES), I32),
        compiler_params=_params("parallel"),
        name="pack_experts",
    )(pu, pv)


def _word_hi(w):
    return lax.bitcast_convert_type(w & jnp.int32(-65536), F32)


def _word_lo(w):
    return lax.bitcast_convert_type(lax.shift_left(w, jnp.int32(16)), F32)


def _sublane_sums(a, sub):
    for dist in (4, 2, 1):
        low = (sub & dist) == 0
        half = len(a) // 2
        a = [jnp.where(low, a[i], pltpu.roll(a[i + half], dist, 0))
             + jnp.where(low, pltpu.roll(a[i], SUBLANES - dist, 0), a[i + half])
             for i in range(half)]
    return a[0]


def _peer_pair_math(expert_u, expert_v, x8, gates, store, issue_some):
    sub = lax.broadcasted_iota(I32, (SUBLANES, LANES), 0)
    eye = (lax.broadcasted_iota(I32, (PEER_E, LANES), 0)
           == lax.broadcasted_iota(I32, (PEER_E, LANES), 1))
    ones_rows = jnp.ones((SUBLANES, LANES), BF16)
    ones_sq = jnp.ones((LANES, LANES), BF16)
    ngroup = PEER_E // SUBLANES

    def hidden(a):
        groups = []
        for g in range(ngroup):
            prods = [expert_u(a, g * SUBLANES + r) * x8[a] for r in range(SUBLANES)]
            groups.append(_sublane_sums(prods, sub))
            issue_some(3 - g % 2)
        return jnp.concatenate(groups, axis=0)

    def expert_weights(a, q):
        q_hi = q.astype(BF16)
        q_lo = (q - q_hi.astype(F32)).astype(BF16)
        hid = (lax.dot_general(ones_rows, q_hi, NT_DIMS, preferred_element_type=F32)
               + lax.dot_general(ones_rows, q_lo, NT_DIMS, preferred_element_type=F32))
        issue_some(PEER_MID_ROWS // 2)
        w = _gelu(hid[0:1]) * gates[a]
        wd = jnp.where(eye, jnp.broadcast_to(w, (PEER_E, LANES)), 0.0).astype(BF16)
        wcol = jnp.dot(wd, ones_sq, preferred_element_type=F32)
        issue_some(PEER_MID_ROWS // 2)
        return wcol

    def combine(a, wcol):
        out = jnp.zeros((SUBLANES, LANES), F32)
        for g in range(ngroup):
            for r in range(SUBLANES):
                j = g * SUBLANES + r
                out = out + wcol[j:j + 1, :] * expert_v(a, j)
            issue_some(3 - g % 2)
        store(a, out)

    wcol0 = expert_weights(0, hidden(0))
    wcol1 = expert_weights(1, hidden(1))
    combine(0, wcol0)
    combine(1, wcol1)


def _peer_finish(h1_ref, acc_ref, fg_ref, out_ref):
    hsum = h1_ref[...] + acc_ref[...]
    ms = jnp.mean(hsum * hsum, axis=(1, 2), keepdims=True)
    _tiles_to_rows(hsum * lax.rsqrt(ms + RMS_EPS) * fg_ref[...], out_ref)


def _peerffn_body(tb, idx_ref, gate_ref, c_ref, h1_ref, fg_ref, uv_ref, out_ref, buf_ref,
                  acc_ref, sem_ref):
    npairs = tb // 2
    ahead = PEER_RING - 1

    def row_copy(t, j, slot):
        return pltpu.make_async_copy(uv_ref.at[idx_ref[t, j]], buf_ref.at[slot, j],
                                     sem_ref.at[slot])

    def wait_all(slot):
        pltpu.make_async_copy(uv_ref.at[pl.ds(0, PEER_E)], buf_ref.at[slot],
                              sem_ref.at[slot]).wait()

    def ring(i):
        base = 2 * (i % PEER_RING)
        return (base, base + 1)

    def pair(i, prefetch):
        toks = (2 * i, 2 * i + 1)
        slots = ring(i)
        nslots = ring(i + ahead)
        todo = [(a, j) for a in range(2) for j in range(PEER_E)]

        def issue_some(n):
            for a, j in todo[:n]:
                if prefetch:
                    row_copy(toks[a] + 2 * ahead, j, nslots[a]).start(priority=j % 2)
            del todo[:n]

        wait_all(slots[0])
        wait_all(slots[1])

        def store(a, out):
            acc_ref[toks[a]] = out

        _peer_pair_math(
            lambda a, j: _word_hi(buf_ref[slots[a], j]),
            lambda a, j: _word_lo(buf_ref[slots[a], j]),
            [c_ref[t] for t in toks], [gate_ref[pl.ds(t, 1), :] for t in toks], store, issue_some)
        assert not todo, "every prefetch row DMA must be issued exactly once"

    for i in range(ahead):
        for a, slot in enumerate(ring(i)):
            for j in range(PEER_E):
                row_copy(2 * i + a, j, slot).start(priority=j % 2)

    def body(i, carry):
        pair(i, True)
        return carry

    lax.fori_loop(0, npairs - ahead, body, 0)
    for i in range(npairs - ahead, npairs):
        pair(i, False)
    _peer_finish(h1_ref, acc_ref, fg_ref, out_ref)


def _peer_ffn(idx, gate, c3, h13, final_g3, uv_tiles):
    t = h13.shape[0]
    tb = 256
    row3 = pl.BlockSpec((tb, SUBLANES, LANES), lambda i: (i, 0, 0))
    return pl.pallas_call(
        functools.partial(_peerffn_body, tb),
        grid=(t // tb,),
        in_specs=[pl.BlockSpec((tb, PEER_E), lambda i: (i, 0), memory_space=pltpu.SMEM),
                  pl.BlockSpec((tb, PEER_E), lambda i: (i, 0)), row3, row3,
                  pl.BlockSpec((1, SUBLANES, LANES), lambda i: (0, 0, 0)),
                  pl.BlockSpec(memory_space=pl.ANY)],
        out_specs=pl.BlockSpec((tb, SUBLANES * LANES), lambda i: (i, 0)),
        out_shape=jax.ShapeDtypeStruct((t, SUBLANES * LANES), F32),
        scratch_shapes=[pltpu.VMEM((2 * PEER_RING, PEER_E, SUBLANES, LANES), I32),
                        pltpu.VMEM((tb, SUBLANES, LANES), F32),
                        pltpu.SemaphoreType.DMA((2 * PEER_RING,))],
        compiler_params=_params("arbitrary"),
        name="peer_ffn",
    )(idx, gate, c3, h13, final_g3, uv_tiles)


def _overlap_table(seq):
    ci = jnp.arange(seq // CMP_STRIDE)[None, :] * CMP_STRIDE
    sj = jnp.arange(SEL_BLOCK)[:, None] * SEL_BLOCK
    return ((ci < sj + SEL_BLOCK) & (ci + CMP_BLOCK > sj)).astype(F32)


def _cmp_blocks(kv):
    b, g, s, dh = kv.shape
    r = kv.reshape(b * g, s // CMP_STRIDE, CMP_STRIDE * dh)
    return jnp.concatenate([r, jnp.roll(r, -1, axis=1)], axis=-1)


def _pad_cmp_params(pe, w1):
    pe_p = jnp.pad(pe, ((0, 0), (0, LANES - HEAD))).reshape(1, -1)
    w1_p = jnp.pad(w1.reshape(CMP_BLOCK, HEAD, -1), ((0, 0), (0, LANES - HEAD), (0, 0)))
    return pe_p, w1_p.reshape(CMP_BLOCK * LANES, -1).astype(BF16)


def _layer(h, lidx, attn_norm, w_in, lq1, lk1, lq2, lk2, subln, pe_k, pe_v, w1k, w1v, w2k, w2v,
           p_da, p_nsa, w_o, ffn_norm, wq, k1, k2, pu, pv, out_norm):
    b, s, d = h.shape
    t = b * s
    g, hg = NSA_GROUPS, NSA_HG
    lambda_init = 0.8 - 0.6 * math.exp(-0.3 * lidx)
    x2 = h.reshape(t, d)

    qda, kda, vda, qn, cmp_in, ksx, kwx, vsw, gm, gn = _in_proj(
        x2, attn_norm.reshape(1, d), _pack_w_in(w_in), _rope_tables(s), s)
    tr = lambda a2: jnp.swapaxes(a2.reshape(b, s, -1), 1, 2)
    ydat = _diff_attn(tr(qda), kda.reshape(b, s, -1), tr(vda),
                      lq1.reshape(1, -1), lk1.reshape(1, -1), lq2.reshape(1, -1),
                      lk2.reshape(1, -1), subln.reshape(-1, 1), lambda_init)
    yda = jnp.swapaxes(ydat, 1, 2).reshape(t, -1)

    assert s // SEL_BLOCK <= SEL_BLOCK, "selection bias rows hold at most 64 blocks"
    cmp4 = jnp.swapaxes(cmp_in.reshape(b, s, 2 * g, LANES), 1, 2)
    pe_kp, w1_kp = _pad_cmp_params(pe_k, w1k)
    pe_vp, w1_vp = _pad_cmp_params(pe_v, w1v)
    kca, vc = _compress(_cmp_blocks(cmp4[:, :g]), _cmp_blocks(cmp4[:, g:]), pe_kp, pe_vp,
                        w1_kp, w1_vp, w2k.astype(BF16), w2v.astype(BF16))
    ncp = s // CMP_STRIDE
    kca = kca.reshape(b, g, ncp, LANES)
    vct = jnp.swapaxes(vc.reshape(b, g, ncp, HEAD), 2, 3)

    qnt = tr(qn)
    ocmp_t, bias_t = _cmp_select(qnt, kca, vct, _overlap_table(s))
    onehot = (jnp.arange(s)[:, None] // SEL_BLOCK == jnp.arange(HEAD)[None, :]).astype(BF16)
    pad_hot = jnp.concatenate([jnp.zeros_like(onehot), onehot] * g, axis=1)
    ksa = ksx.reshape(b, s, -1) + pad_hot[None]
    ynt = _sel_win(qnt, bias_t, ksa, kwx.reshape(b, s, -1), tr(vsw), ocmp_t, tr(gn))
    yn = jnp.swapaxes(ynt, 1, 2).reshape(t, -1)

    h1_tiles, c_tiles, s1t, s2t = _out_proj(
        yda, yn, gm, x2, p_da.astype(BF16), p_nsa.astype(BF16), w_o.astype(BF16),
        ffn_norm.reshape(1, d), wq.astype(BF16), k1.astype(BF16), k2.astype(BF16))
    idx_t, gate_t = _peer_topk(s1t, s2t)
    idx = idx_t.reshape(PEER_E, t).T
    gate = gate_t.reshape(PEER_E, t).T
    out = _peer_ffn(idx, gate, c_tiles, h1_tiles, out_norm.reshape(1, SUBLANES, LANES),
                    _pack_expert_rows(pu, pv))
    return out.reshape(b, s, d)


def kernel(x, attn_norm, w_in, da_lambda_q1, da_lambda_k1, da_lambda_q2, da_lambda_k2, da_subln,
           cmp_pe_k, cmp_pe_v, cmp_w1_k, cmp_w1_v, cmp_w2_k, cmp_w2_v, p_da, p_nsa, w_o,
           ffn_norm, peer_wq, peer_k1, peer_k2, peer_u, peer_v, final_norm):
    depth = attn_norm.shape[0]
    assert depth == 1, "the final norm is fused into the last layer's PEER kernel"
    h = x
    for l in range(depth):
        h = _layer(h, l, attn_norm[l], w_in[l], da_lambda_q1[l], da_lambda_k1[l], da_lambda_q2[l],
                   da_lambda_k2[l], da_subln[l], cmp_pe_k[l], cmp_pe_v[l], cmp_w1_k[l],
                   cmp_w1_v[l], cmp_w2_k[l], cmp_w2_v[l], p_da[l], p_nsa[l], w_o[l], ffn_norm[l],
                   peer_wq[l], peer_k1[l], peer_k2[l], peer_u[l], peer_v[l], final_norm)
    return h
```

```python
import functools
import math

import jax
import jax.numpy as jnp
from jax import lax
from jax.experimental import pallas as pl
from jax.experimental.pallas import tpu as pltpu

F32 = jnp.float32
BF16 = jnp.bfloat16
I32 = jnp.int32

RMS_EPS = 1e-6
ROPE_THETA = 500000.0
ROPE_HALF = 8
HEAD = 64
DA_HEADS = 4
NSA_GROUPS = 2
NSA_HG = 4
CMP_STRIDE = 16
CMP_BLOCK = 32
SEL_BLOCK = 64
SEL_TOPK = 16
WINDOW = 512
FORCE_BONUS = 1e4
NEG_BIG = -1e30
SEL_MASK_BIAS = -2.0 ** 100
ATTN_SCALE = HEAD ** -0.5
PEER_HEADS = 8
PEER_NKEYS = 128
PEER_TOPK = 16
LANES = 128
SUBLANES = 8
VMEM_LIMIT = 56 * 1024 * 1024

NT_DIMS = (((1,), (1,)), ((), ()))


def _rmsnorm(x, g):
    return x * lax.rsqrt(jnp.mean(x * x, axis=-1, keepdims=True) + RMS_EPS) * g


def _sigmoid(z):
    return 1.0 / (1.0 + jnp.exp(-z))


def _gelu(z):
    return 0.5 * z * (1.0 + lax.erf(z * (2.0 ** -0.5)))


def _params(*sem):
    return pltpu.CompilerParams(dimension_semantics=sem, vmem_limit_bytes=VMEM_LIMIT)


def _sublane_transpose(v):
    sub = lax.broadcasted_iota(I32, (SUBLANES, LANES), 0)
    v = list(v)
    for dist in (4, 2, 1):
        low = (sub & dist) == 0
        nxt = list(v)
        for i in range(SUBLANES):
            if i & dist == 0:
                nxt[i] = jnp.where(low, v[i], pltpu.roll(v[i + dist], dist, 0))
                nxt[i + dist] = jnp.where(low, pltpu.roll(v[i], SUBLANES - dist, 0), v[i + dist])
        v = nxt
    return v


def _rows_to_tiles(x, tile_ref):
    for g in range(x.shape[0] // SUBLANES):
        rows = slice(g * SUBLANES, (g + 1) * SUBLANES)
        tiles = _sublane_transpose([x[rows, c * LANES:(c + 1) * LANES] for c in range(SUBLANES)])
        for e in range(SUBLANES):
            tile_ref[g * SUBLANES + e] = tiles[e]


def _tiles_to_rows(t, row_ref):
    for g in range(t.shape[0] // SUBLANES):
        chunks = _sublane_transpose([t[g * SUBLANES + e] for e in range(SUBLANES)])
        for c in range(SUBLANES):
            row_ref[g * SUBLANES:(g + 1) * SUBLANES, c * LANES:(c + 1) * LANES] = chunks[c]


_QDA0, _KDA0, _VDA0, _QN0 = 0, 512, 1024, 1536
_KC0, _VC0, _KS0, _KW0, _VSW0, _GM0, _GN0, _WCOLS = 2048, 2304, 2560, 2816, 3072, 3328, 5376, 5632


def _inproj_body(x_ref, g_ref, w_ref, rc_ref, rs1_ref, rs2_ref,
                 qda_ref, kda_ref, vda_ref, qn_ref, cmp_ref, ks_ref, kw_ref, vsw_ref, gm_ref, gn_ref):
    a = _rmsnorm(x_ref[...], g_ref[...]).astype(BF16)
    rc, rs1, rs2 = rc_ref[...], rs1_ref[...], rs2_ref[...]

    def rope(z):
        return (z * rc + pltpu.roll(z, ROPE_HALF, 1) * rs1
                + pltpu.roll(z, LANES - ROPE_HALF, 1) * rs2)

    def proj(c0):
        return jnp.dot(a, w_ref[:, c0:c0 + 256], preferred_element_type=F32)

    def rope2(z):
        return jnp.concatenate([rope(z[:, :LANES]), rope(z[:, LANES:])], axis=1)

    for c in range(2):
        qda_ref[:, c * 256:(c + 1) * 256] = (rope2(proj(_QDA0 + c * 256)) * ATTN_SCALE).astype(BF16)
        kda_ref[:, c * 256:(c + 1) * 256] = rope2(proj(_KDA0 + c * 256)).astype(BF16)
        vda_ref[:, c * 256:(c + 1) * 256] = proj(_VDA0 + c * 256).astype(BF16)
        qn_ref[:, c * 256:(c + 1) * 256] = (rope2(proj(_QN0 + c * 256)) * ATTN_SCALE).astype(BF16)
    cmp_ref[:, 0:256] = rope2(proj(_KC0)).astype(BF16)
    cmp_ref[:, 256:512] = proj(_VC0).astype(BF16)
    ks_ref[...] = rope2(proj(_KS0)).astype(BF16)
    kw_ref[...] = rope2(proj(_KW0)).astype(BF16)
    vsw_ref[...] = proj(_VSW0).astype(BF16)
    for c in range(8):
        gm_ref[:, c * 256:(c + 1) * 256] = _sigmoid(proj(_GM0 + c * 256)).astype(BF16)
    gn_ref[...] = _sigmoid(proj(_GN0))


def _pack_w_in(w):
    d = w.shape[0]
    zeros = lambda n: jnp.zeros((d, n), w.dtype)

    def spread_groups(c0):
        return [w[:, c0:c0 + HEAD], zeros(HEAD), w[:, c0 + HEAD:c0 + 2 * HEAD], zeros(HEAD)]

    kv_w = NSA_GROUPS * HEAD
    kv0 = 4 * 512
    kc, vc, ks, vs, kw, vw = (kv0 + kv_w * i for i in range(6))
    gn0 = kv0 + 6 * kv_w
    per_group = NSA_HG * 3
    gm0 = gn0 + NSA_GROUPS * per_group
    assert w.shape[1] == gm0 + 2 * d
    gn = w[:, gn0:gm0]
    cols = ([w[:, :kv0]] + spread_groups(kc) + spread_groups(vc) + spread_groups(ks)
            + spread_groups(kw) + [w[:, vs:vs + kv_w], w[:, vw:vw + kv_w], w[:, gm0:],
                                   gn[:, :per_group], zeros(LANES - per_group),
                                   gn[:, per_group:], zeros(LANES - per_group)])
    packed = jnp.concatenate(cols, axis=1).astype(BF16)
    assert packed.shape[1] == _WCOLS
    return packed


def _rope_tables(seq):
    inv = jnp.power(ROPE_THETA, -jnp.arange(ROPE_HALF, dtype=F32) * 2.0 / (2 * ROPE_HALF))
    ang = jnp.arange(seq, dtype=F32)[:, None] * inv[None, :]
    cos, sin = jnp.cos(ang), jnp.sin(ang)
    one = jnp.ones((seq, HEAD - 2 * ROPE_HALF), F32)
    zero8 = jnp.zeros((seq, ROPE_HALF), F32)
    zero48 = jnp.zeros_like(one)
    rc = jnp.concatenate([cos, cos, one], axis=1)
    rs1 = jnp.concatenate([zero8, sin, zero48], axis=1)
    rs2 = jnp.concatenate([-sin, zero8, zero48], axis=1)
    return tuple(jnp.concatenate([t, t], axis=1) for t in (rc, rs1, rs2))


def _in_proj(x2, norm_g, w_packed, rope_tabs, seq):
    t, d = x2.shape
    tm = 512
    nseq = seq // tm
    row = lambda i: (i, 0)
    fixed = lambda i: (0, 0)
    out_shapes = [
        jax.ShapeDtypeStruct((t, 512), BF16), jax.ShapeDtypeStruct((t, 512), BF16),
        jax.ShapeDtypeStruct((t, 512), BF16), jax.ShapeDtypeStruct((t, 512), BF16),
        jax.ShapeDtypeStruct((t, 512), BF16), jax.ShapeDtypeStruct((t, 256), BF16),
        jax.ShapeDtypeStruct((t, 256), BF16), jax.ShapeDtypeStruct((t, 256), BF16),
        jax.ShapeDtypeStruct((t, 2048), BF16), jax.ShapeDtypeStruct((t, 256), F32)]
    rope_spec = pl.BlockSpec((tm, LANES), lambda i: (i % nseq, 0))
    return pl.pallas_call(
        _inproj_body,
        grid=(t // tm,),
        in_specs=[pl.BlockSpec((tm, d), row), pl.BlockSpec((1, d), fixed),
                  pl.BlockSpec((d, _WCOLS), fixed), rope_spec, rope_spec, rope_spec],
        out_specs=[pl.BlockSpec((tm, s.shape[1]), row) for s in out_shapes],
        out_shape=out_shapes,
        compiler_params=_params("parallel"),
        name="in_proj",
    )(x2, norm_g, w_packed, *rope_tabs)


def _softmax_step(s, vt, m_ref, l_ref, acc_ref):
    m_prev = m_ref[...]
    m_new = jnp.maximum(m_prev, jnp.max(s, axis=0, keepdims=True))
    alpha = jnp.exp(m_prev - m_new)
    p = jnp.exp(s - m_new)
    l_ref[...] = alpha * l_ref[...] + jnp.sum(p, axis=0, keepdims=True)
    acc_ref[...] = alpha * acc_ref[...] + jnp.dot(vt, p.astype(BF16), preferred_element_type=F32)
    m_ref[...] = m_new


def _softmax_reset(m_ref, l_ref, acc_ref):
    m_ref[...] = jnp.full(m_ref.shape, NEG_BIG, F32)
    l_ref[...] = jnp.zeros(l_ref.shape, F32)
    acc_ref[...] = jnp.zeros(acc_ref.shape, F32)


def _attend_tiles(n_full, scores, values, mask_last, sa_ref, sb_ref, m_ref, l_ref, acc_ref):
    step = lambda s, t: _softmax_step(s, values(t), m_ref, l_ref, acc_ref)
    sa_ref[...] = scores(0)

    def two_tiles(i, carry):
        t = 2 * i
        sb_ref[...] = scores(t + 1)
        step(sa_ref[...], t)
        sa_ref[...] = scores(t + 2)
        step(sb_ref[...], t + 1)
        return carry

    lax.fori_loop(0, n_full // 2, two_tiles, 0)
    odd = (n_full & 1) == 1

    @pl.when(odd)
    def _():
        sb_ref[...] = scores(n_full)
        step(sa_ref[...], n_full - 1)
        step(mask_last(sb_ref[...]), n_full)

    @pl.when(jnp.logical_not(odd))
    def _():
        step(mask_last(sa_ref[...]), n_full)


def _key_tile(ref, kt, bk):
    return ref[(0,) * (len(ref.shape) - 2) + (pl.ds(pl.multiple_of(kt * bk, bk), bk), slice(None))]


def _value_tile(ref, kt, bk):
    return ref[(0,) * (len(ref.shape) - 2) + (slice(None), pl.ds(pl.multiple_of(kt * bk, bk), bk))]


def _diffattn_body(lambda_init, bq, q_ref, k_ref, v_ref, lq1_ref, lk1_ref, lq2_ref, lk2_ref,
                   sub_ref, y_ref, qbd_ref, m_ref, l_ref, acc_ref, sa_ref, sb_ref):
    qi = pl.program_id(2)
    bk = bq
    qt = q_ref[0]
    sub = lax.broadcasted_iota(I32, qt.shape, 0)
    zero = jnp.zeros_like(qt)
    qbd_ref[:, 0:bq] = jnp.where(sub < HEAD, qt, zero)
    qbd_ref[:, bq:2 * bq] = jnp.where(sub >= HEAD, qt, zero)
    _softmax_reset(m_ref, l_ref, acc_ref)

    def scores(kt):
        return jnp.dot(_key_tile(k_ref, kt, bk), qbd_ref[...], preferred_element_type=F32)

    def causal(s):
        r = lax.broadcasted_iota(I32, (bk, 2 * bq), 0)
        c = lax.broadcasted_iota(I32, (bk, 2 * bq), 1) & (bq - 1)
        return jnp.where(r <= c, s, NEG_BIG)

    _attend_tiles(qi, scores, lambda kt: _value_tile(v_ref, kt, bk), causal,
                  sa_ref, sb_ref, m_ref, l_ref, acc_ref)

    o = acc_ref[...] / l_ref[...]
    lam = (jnp.exp(jnp.sum(lq1_ref[...] * lk1_ref[...], axis=1, keepdims=True))
           - jnp.exp(jnp.sum(lq2_ref[...] * lk2_ref[...], axis=1, keepdims=True)) + lambda_init)
    d = o[:, 0:bq] - lam * o[:, bq:2 * bq]
    ms = jnp.mean(d * d, axis=0, keepdims=True)
    y = d * lax.rsqrt(ms + RMS_EPS) * sub_ref[...] * (1.0 - lambda_init)
    y_ref[0] = y.astype(BF16)


def _diff_attn(qdat, kda, vdat, lq1, lk1, lq2, lk2, subln_col, lambda_init):
    b, s, _ = kda.shape
    bq = 512
    vec = lambda n: pl.BlockSpec((1, n), lambda bi, h, qi: (0, 0))
    dv = 2 * HEAD
    qtile = pl.BlockSpec((1, dv, bq), lambda bi, h, qi: (bi, h, qi))
    return pl.pallas_call(
        functools.partial(_diffattn_body, lambda_init, bq),
        grid=(b, DA_HEADS, s // bq),
        in_specs=[qtile,
                  pl.BlockSpec((1, s, LANES), lambda bi, h, qi: (bi, 0, h)),
                  pl.BlockSpec((1, dv, s), lambda bi, h, qi: (bi, h, 0)),
                  vec(HEAD), vec(HEAD), vec(HEAD), vec(HEAD),
                  pl.BlockSpec((dv, 1), lambda bi, h, qi: (0, 0))],
        out_specs=qtile,
        out_shape=jax.ShapeDtypeStruct((b, DA_HEADS * dv, s), BF16),
        scratch_shapes=[pltpu.VMEM((LANES, 2 * bq), BF16), pltpu.VMEM((1, 2 * bq), F32),
                        pltpu.VMEM((1, 2 * bq), F32), pltpu.VMEM((dv, 2 * bq), F32),
                        pltpu.VMEM((bq, 2 * bq), F32), pltpu.VMEM((bq, 2 * bq), F32)],
        compiler_params=_params("parallel", "parallel", "parallel"),
        name="diff_attn",
    )(qdat, kda, vdat, lq1, lk1, lq2, lk2, subln_col)


def _compress_body(xk_ref, xv_ref, pek_ref, pev_ref, w1k_ref, w1v_ref, w2k_ref, w2v_ref,
                   kc_ref, vc_ref):
    def mlp(x_ref, pe_ref, w1_ref, w2_ref):
        blocks = (x_ref[0].astype(F32) + pe_ref[...]).astype(BF16)
        hid = _gelu(jnp.dot(blocks, w1_ref[...], preferred_element_type=F32))
        return jnp.dot(hid.astype(BF16), w2_ref[...], preferred_element_type=F32)

    kc = mlp(xk_ref, pek_ref, w1k_ref, w2k_ref)
    kc_ref[0] = jnp.concatenate([kc, jnp.zeros_like(kc)], axis=1).astype(BF16)
    vc_ref[0] = mlp(xv_ref, pev_ref, w1v_ref, w2v_ref).astype(BF16)


def _compress(xk, xv, pe_k, pe_v, w1k, w1v, w2k, w2v):
    n, ncp, width = xk.shape
    blk = pl.BlockSpec((1, ncp, width), lambda i: (i, 0, 0))
    fixed = lambda shape: pl.BlockSpec(shape, lambda i: (0, 0))
    return pl.pallas_call(
        _compress_body,
        grid=(n,),
        in_specs=[blk, blk, fixed((1, width)), fixed((1, width)), fixed((width, HEAD)),
                  fixed((width, HEAD)), fixed((HEAD, HEAD)), fixed((HEAD, HEAD))],
        out_specs=[pl.BlockSpec((1, ncp, LANES), lambda i: (i, 0, 0)),
                   pl.BlockSpec((1, ncp, HEAD), lambda i: (i, 0, 0))],
        out_shape=[jax.ShapeDtypeStruct((n, ncp, LANES), BF16),
                   jax.ShapeDtypeStruct((n, ncp, HEAD), BF16)],
        compiler_params=_params("parallel"),
        name="compress",
    )(xk, xv, pe_k, pe_v, w1k, w1v, w2k, w2v)


def _heads_on_lanes(qt, bq):
    return jnp.concatenate([qt[h * HEAD:(h + 1) * HEAD, :] for h in range(NSA_HG)], axis=1)


def _cmpsel_body(bq, q_ref, kc_ref, vc_ref, ovl_ref, ocmp_ref, bias_ref):
    qi = pl.program_id(2)
    ncp = kc_ref.shape[2]
    rows = NSA_HG * bq
    q2 = _heads_on_lanes(q_ref[0], bq)
    qz = jnp.concatenate([q2, jnp.zeros_like(q2)], axis=0)
    s = jnp.dot(kc_ref[0, 0], qz, preferred_element_type=F32)
    n = lax.broadcasted_iota(I32, (ncp, rows), 0)
    qpos = qi * bq + (lax.broadcasted_iota(I32, (ncp, rows), 1) & (bq - 1))
    cmask = n * CMP_STRIDE + (CMP_BLOCK - 1) <= qpos
    s = jnp.where(cmask, s, NEG_BIG)
    e = jnp.exp(s - jnp.max(s, axis=0, keepdims=True))
    p = jnp.where(cmask, e / jnp.sum(e, axis=0, keepdims=True), 0.0)
    o = jnp.dot(vc_ref[0, 0], p.astype(BF16), preferred_element_type=F32)
    for h in range(NSA_HG):
        ocmp_ref[0, 0, h * HEAD:(h + 1) * HEAD, :] = o[:, h * bq:(h + 1) * bq]

    psum = p[:, 0:bq] + p[:, bq:2 * bq] + p[:, 2 * bq:3 * bq] + p[:, 3 * bq:4 * bq]
    imp = jnp.dot(ovl_ref[...], psum, preferred_element_type=F32)
    blk = lax.broadcasted_iota(I32, (SEL_BLOCK, bq), 0)
    pos = qi * bq + lax.broadcasted_iota(I32, (SEL_BLOCK, bq), 1)
    cur = lax.shift_right_logical(pos, SEL_BLOCK.bit_length() - 1)
    valid = blk <= cur
    forced = (blk == 0) | (blk == cur) | (blk == cur - 1)
    score = jnp.where(valid, imp + jnp.where(forced, FORCE_BONUS, 0.0), -jnp.inf)
    rank = jnp.zeros((SEL_BLOCK, bq), I32)
    for i in range(SEL_BLOCK):
        other = score[i:i + 1, :]
        beats = (other > score) | ((other == score) & (blk > i))
        rank = rank + beats.astype(I32)
    keep = valid & (rank < SEL_TOPK)
    bias_ref[0, 0] = jnp.where(keep, 0.0, SEL_MASK_BIAS).astype(BF16)


def _cmp_select(qnt, kca, vct, overlap_t):
    b, g, ncp, _ = kca.shape
    s = qnt.shape[2]
    bq = 128
    dq = NSA_HG * HEAD
    return pl.pallas_call(
        functools.partial(_cmpsel_body, bq),
        grid=(b, g, s // bq),
        in_specs=[pl.BlockSpec((1, dq, bq), lambda bi, gi, qi: (bi, gi, qi)),
                  pl.BlockSpec((1, 1, ncp, LANES), lambda bi, gi, qi: (bi, gi, 0, 0)),
                  pl.BlockSpec((1, 1, HEAD, ncp), lambda bi, gi, qi: (bi, gi, 0, 0)),
                  pl.BlockSpec((SEL_BLOCK, ncp), lambda bi, gi, qi: (0, 0))],
        out_specs=[pl.BlockSpec((1, 1, dq, bq), lambda bi, gi, qi: (bi, gi, 0, qi)),
                   pl.BlockSpec((1, 1, SEL_BLOCK, bq), lambda bi, gi, qi: (bi, gi, 0, qi))],
        out_shape=[jax.ShapeDtypeStruct((b, g, dq, s), F32),
                   jax.ShapeDtypeStruct((b, g, SEL_BLOCK, s), BF16)],
        compiler_params=_params("parallel", "parallel", "parallel"),
        name="cmp_select",
    )(qnt, kca, vct, overlap_t)


def _selwin_body(bq, bks, q_ref, bias_ref, ks_ref, vs_ref, kw_ref, vw_ref, ocmp_ref, gate_ref,
                 y_ref, m_ref, l_ref, acc_ref, qa_ref, sa_ref, sb_ref):
    qi = pl.program_id(2)
    rows = NSA_HG * bq
    q2 = _heads_on_lanes(q_ref[0], bq)
    bias = bias_ref[0, 0]
    qa = jnp.concatenate([q2, jnp.concatenate([bias] * NSA_HG, axis=1)], axis=0)
    qw = jnp.concatenate([q2, jnp.zeros_like(q2)], axis=0)
    r = lax.broadcasted_iota(I32, (bq, rows), 0)
    c = lax.broadcasted_iota(I32, (bq, rows), 1) & (bq - 1)

    _softmax_reset(m_ref, l_ref, acc_ref)

    qa_ref[...] = qa
    last = (qi * bq) // bks

    def causal(s):
        kpos = last * bks + lax.broadcasted_iota(I32, (bks, rows), 0)
        qpos = qi * bq + (lax.broadcasted_iota(I32, (bks, rows), 1) & (bq - 1))
        return jnp.where(kpos <= qpos, s, NEG_BIG)

    _attend_tiles(last,
                  lambda kt: jnp.dot(_key_tile(ks_ref, kt, bks), qa_ref[...],
                                     preferred_element_type=F32),
                  lambda kt: _value_tile(vs_ref, kt, bks), causal,
                  sa_ref, sb_ref, m_ref, l_ref, acc_ref)
    o_sel = acc_ref[...] / l_ref[...]

    _softmax_reset(m_ref, l_ref, acc_ref)
    backs = list(range(WINDOW // bq, -1, -1))
    tiles = [jnp.maximum(qi - back, 0) for back in backs]
    raw = [jnp.dot(_key_tile(kw_ref, kt, bq), qw, preferred_element_type=F32) for kt in tiles]
    for back, kt, s in zip(backs, tiles, raw):
        if back >= 1 and (back + 1) * bq <= WINDOW:
            inside = qi >= back
        else:
            dist = c + back * bq - r
            inside = (dist >= 0) & (dist < WINDOW) & (qi >= back)
        _softmax_step(jnp.where(inside, s, NEG_BIG), _value_tile(vw_ref, kt, bq),
                      m_ref, l_ref, acc_ref)

    o_win = acc_ref[...] / l_ref[...]
    gate = gate_ref[0]
    for h in range(NSA_HG):
        sl = slice(h * bq, (h + 1) * bq)
        y = (gate[3 * h:3 * h + 1, :] * ocmp_ref[0, 0, h * HEAD:(h + 1) * HEAD, :]
             + gate[3 * h + 1:3 * h + 2, :] * o_sel[:, sl]
             + gate[3 * h + 2:3 * h + 3, :] * o_win[:, sl])
        y_ref[0, h * HEAD:(h + 1) * HEAD, :] = y.astype(BF16)


def _sel_win(qnt, bias_t, ksa, kwa, vswt, ocmp_t, gnt):
    b, s, _ = ksa.shape
    g = NSA_GROUPS
    bq, bks = 256, 512
    dq = NSA_HG * HEAD
    rows = NSA_HG * bq
    keys = pl.BlockSpec((1, s, LANES), lambda bi, gi, qi: (bi, 0, gi))
    vals = lambda first: pl.BlockSpec((1, HEAD, s), lambda bi, gi, qi: (bi, first + gi, 0))
    qtile = pl.BlockSpec((1, dq, bq), lambda bi, gi, qi: (bi, gi, qi))
    return pl.pallas_call(
        functools.partial(_selwin_body, bq, bks),
        grid=(b, g, s // bq),
        in_specs=[qtile,
                  pl.BlockSpec((1, 1, SEL_BLOCK, bq), lambda bi, gi, qi: (bi, gi, 0, qi)),
                  keys, vals(0), keys, vals(g),
                  pl.BlockSpec((1, 1, dq, bq), lambda bi, gi, qi: (bi, gi, 0, qi)),
                  pl.BlockSpec((1, LANES, bq), lambda bi, gi, qi: (bi, gi, qi))],
        out_specs=qtile,
        out_shape=jax.ShapeDtypeStruct((b, g * dq, s), BF16),
        scratch_shapes=[pltpu.VMEM((1, rows), F32), pltpu.VMEM((1, rows), F32),
                        pltpu.VMEM((HEAD, rows), F32), pltpu.VMEM((LANES, rows), BF16),
                        pltpu.VMEM((bks, rows), F32), pltpu.VMEM((bks, rows), F32)],
        compiler_params=_params("parallel", "parallel", "parallel"),
        name="sel_win",
    )(qnt, bias_t, ksa, vswt, kwa, vswt, ocmp_t, gnt)


def _outproj_body(yda_ref, yn_ref, gm_ref, x_ref, pda_ref, pnsa_ref, wo_ref, fg_ref, wq_ref,
                  k1_ref, k2_ref, h1_ref, c_ref, s1_ref, s2_ref):
    d = x_ref.shape[1]
    a = jnp.dot(yda_ref[...], pda_ref[...], preferred_element_type=F32)
    bn = jnp.dot(yn_ref[...], pnsa_ref[...], preferred_element_type=F32)
    merged = gm_ref[:, :d].astype(F32) * a + gm_ref[:, d:].astype(F32) * bn
    h1 = x_ref[...] + jnp.dot(merged.astype(BF16), wo_ref[...], preferred_element_type=F32)
    c = _rmsnorm(h1, fg_ref[...])
    _rows_to_tiles(h1, h1_ref)
    _rows_to_tiles(c, c_ref)
    cb = c.astype(BF16)
    for h in range(PEER_HEADS):
        qh = jnp.dot(cb, wq_ref[:, h * 256:(h + 1) * 256], preferred_element_type=F32).astype(BF16)
        s1_ref[h] = lax.dot_general(k1_ref[...], qh[:, :LANES], NT_DIMS, preferred_element_type=F32)
        s2_ref[h] = lax.dot_general(k2_ref[...], qh[:, LANES:], NT_DIMS, preferred_element_type=F32)


def _out_proj(yda, yn, gm, x2, pda, pnsa, wo, ffn_g, wq, k1, k2):
    t, d = x2.shape
    tm = 512
    row = lambda w: pl.BlockSpec((tm, w), lambda i: (i, 0))
    fixed = lambda a: pl.BlockSpec(a.shape, lambda i: (0, 0))
    sspec = pl.BlockSpec((PEER_HEADS, PEER_NKEYS, tm), lambda i: (0, 0, i))
    assert d == SUBLANES * LANES
    tiles = pl.BlockSpec((tm, SUBLANES, LANES), lambda i: (i, 0, 0))
    return pl.pallas_call(
        _outproj_body,
        grid=(t // tm,),
        in_specs=[row(512), row(512), row(2 * d), row(d), fixed(pda), fixed(pnsa), fixed(wo),
                  fixed(ffn_g), fixed(wq), fixed(k1), fixed(k2)],
        out_specs=[tiles, tiles, sspec, sspec],
        out_shape=[jax.ShapeDtypeStruct((t, SUBLANES, LANES), F32),
                   jax.ShapeDtypeStruct((t, SUBLANES, LANES), F32),
                   jax.ShapeDtypeStruct((PEER_HEADS, PEER_NKEYS, t), F32),
                   jax.ShapeDtypeStruct((PEER_HEADS, PEER_NKEYS, t), F32)],
        compiler_params=_params("parallel"),
        name="out_proj",
    )(yda, yn, gm, x2, pda, pnsa, wo, ffn_g, wq, k1, k2)


def _batcher_pairs(n):
    pairs = []

    def merge(lo, hi, r):
        step = r * 2
        if step < hi - lo:
            merge(lo, hi, step)
            merge(lo + r, hi, step)
            pairs.extend((i, i + r) for i in range(lo + r, hi - r, step))
        else:
            pairs.append((lo, lo + r))

    def sort(lo, hi):
        if hi - lo >= 1:
            mid = lo + (hi - lo) // 2
            sort(lo, mid)
            sort(mid + 1, hi)
            merge(lo, hi, 1)

    sort(0, n - 1)
    return pairs


_NET16 = _batcher_pairs(PEER_TOPK)


def _cmpx(a, b):
    c = (a[0] > b[0]) | ((a[0] == b[0]) & (a[1] < b[1]))
    return ((jnp.where(c, a[0], b[0]), jnp.where(c, a[1], b[1])),
            (jnp.where(c, b[0], a[0]), jnp.where(c, b[1], a[1])))


def _sort_lists(lists, n_real):
    lists = list(lists)
    for i, j in _NET16:
        if j < n_real:
            lists[i], lists[j] = _cmpx(lists[i], lists[j])
    return lists


def _merge_top(a, b):
    k = PEER_TOPK
    lists = [_cmpx(a[i], b[k - 1 - i])[0] for i in range(k)]
    step = k // 2
    while step >= 1:
        for i in range(k):
            if i & step == 0:
                lists[i], lists[i + step] = _cmpx(lists[i], lists[i + step])
        step //= 2
    return lists


def _top16(pairs):
    k = PEER_TOPK
    filler = (jnp.full(pairs[0][0].shape, -jnp.inf, F32), jnp.zeros(pairs[0][1].shape, I32))
    groups = []
    for g0 in range(0, len(pairs), k):
        chunk = list(pairs[g0:g0 + k])
        groups.append(_sort_lists(chunk + [filler] * (k - len(chunk)), len(chunk)))
    while len(groups) > 1:
        groups = [_merge_top(groups[i], groups[i + 1]) if i + 1 < len(groups) else groups[i]
                  for i in range(0, len(groups), 2)]
    return groups[0]


def _product_key_select(s1, s2):
    k = PEER_TOPK
    n_exp = PEER_NKEYS * PEER_NKEYS
    ids = lambda n: [jnp.full(s1[0].shape, i, I32) for i in range(n)]
    l1 = _top16(list(zip(s1, ids(len(s1)))))
    l2 = _top16(list(zip(s2, ids(len(s2)))))
    pair = lambda a, b: (l1[a][0] + l2[b][0],
                         (a * k + b) * n_exp + l1[a][1] * PEER_NKEYS + l2[b][1])
    first_row = [pair(0, b) for b in range(k)]
    rest = [pair(a, b) for a in range(1, k) for b in range(k // (a + 1))]
    best = _merge_top(first_row, _top16(rest))
    return [(v, p & (n_exp - 1)) for v, p in best]


def _peertopk_body(tt, s1_ref, s2_ref, idx_ref, gate_ref):
    nblk = tt // LANES
    assert nblk == SUBLANES

    def keys_major(ref, h):
        tiles = []
        for kg in range(PEER_NKEYS // SUBLANES):
            rows = slice(kg * SUBLANES, (kg + 1) * SUBLANES)
            tiles += _sublane_transpose([ref[h, rows, b * LANES:(b + 1) * LANES]
                                         for b in range(nblk)])
        return tiles

    def store(ref, h, tiles):
        for g in range(PEER_TOPK // SUBLANES):
            blocks = _sublane_transpose(tiles[g * SUBLANES:(g + 1) * SUBLANES])
            for b in range(nblk):
                ref[h, g * SUBLANES:(g + 1) * SUBLANES, b * LANES:(b + 1) * LANES] = blocks[b]

    def head(h, carry):
        best = _product_key_select(keys_major(s1_ref, h), keys_major(s2_ref, h))
        ex = [jnp.exp(v - best[0][0]) for v, _ in best]
        z = ex[0]
        for e in ex[1:]:
            z = z + e
        store(gate_ref, h, [e / z for e in ex])
        store(idx_ref, h, [ix for _, ix in best])
        return carry

    lax.fori_loop(0, PEER_HEADS, head, 0)


def _peer_topk(s1t, s2t):
    _, _, t = s1t.shape
    tt = SUBLANES * LANES
    spec_in = pl.BlockSpec((PEER_HEADS, PEER_NKEYS, tt), lambda i: (0, 0, i))
    spec_out = pl.BlockSpec((PEER_HEADS, PEER_TOPK, tt), lambda i: (0, 0, i))
    return pl.pallas_call(
        functools.partial(_peertopk_body, tt),
        grid=(t // tt,),
        in_specs=[spec_in, spec_in],
        out_specs=[spec_out, spec_out],
        out_shape=[jax.ShapeDtypeStruct((PEER_HEADS, PEER_TOPK, t), I32),
                   jax.ShapeDtypeStruct((PEER_HEADS, PEER_TOPK, t), F32)],
        compiler_params=_params("parallel"),
        name="peer_topk",
    )(s1t, s2t)


PEER_E = PEER_HEADS * PEER_TOPK
PEER_RING = 3
PEER_MID_ROWS = 48


def _pack_body(u_ref, v_ref, out_ref):
    bf16_bits = lambda a: lax.bitcast_convert_type(a.astype(BF16).astype(F32), I32)
    words = bf16_bits(u_ref[...]) | lax.shift_right_logical(bf16_bits(v_ref[...]), jnp.int32(16))
    _rows_to_tiles(words, out_ref)


def _pack_expert_rows(pu, pv):
    n, d = pu.shape
    assert d == SUBLANES * LANES
    te = 256
    blk = pl.BlockSpec((te, d), lambda i: (i, 0))
    return pl.pallas_call(
        _pack_body,
        grid=(n // te,),
        in_specs=[blk, blk],
        out_specs=pl.BlockSpec((te, SUBLANES, LANES), lambda i: (i, 0, 0)),
        out_shape=jax.ShapeDtypeStruct((n, SUBLANES, LANES), I32),
        compiler_params=_params("parallel"),
        name="pack_experts",
    )(pu, pv)


def _word_hi(w):
    return lax.bitcast_convert_type(w & jnp.int32(-65536), F32)


def _word_lo(w):
    return lax.bitcast_convert_type(lax.shift_left(w, jnp.int32(16)), F32)


def _sublane_sums(a, sub):
    for dist in (4, 2, 1):
        low = (sub & dist) == 0
        half = len(a) // 2
        a = [jnp.where(low, a[i], pltpu.roll(a[i + half], dist, 0))
             + jnp.where(low, pltpu.roll(a[i], SUBLANES - dist, 0), a[i + half])
             for i in range(half)]
    return a[0]


def _peer_pair_math(expert_u, expert_v, x8, gates, store, issue_some):
    sub = lax.broadcasted_iota(I32, (SUBLANES, LANES), 0)
    eye = (lax.broadcasted_iota(I32, (PEER_E, LANES), 0)
           == lax.broadcasted_iota(I32, (PEER_E, LANES), 1))
    ones_rows = jnp.ones((SUBLANES, LANES), BF16)
    ones_sq = jnp.ones((LANES, LANES), BF16)
    ngroup = PEER_E // SUBLANES

    def hidden(a):
        groups = []
        for g in range(ngroup):
            prods = [expert_u(a, g * SUBLANES + r) * x8[a] for r in range(SUBLANES)]
            groups.append(_sublane_sums(prods, sub))
            issue_some(3 - g % 2)
        return jnp.concatenate(groups, axis=0)

    def expert_weights(a, q):
        q_hi = q.astype(BF16)
        q_lo = (q - q_hi.astype(F32)).astype(BF16)
        hid = (lax.dot_general(ones_rows, q_hi, NT_DIMS, preferred_element_type=F32)
               + lax.dot_general(ones_rows, q_lo, NT_DIMS, preferred_element_type=F32))
        issue_some(PEER_MID_ROWS // 2)
        w = _gelu(hid[0:1]) * gates[a]
        wd = jnp.where(eye, jnp.broadcast_to(w, (PEER_E, LANES)), 0.0).astype(BF16)
        wcol = jnp.dot(wd, ones_sq, preferred_element_type=F32)
        issue_some(PEER_MID_ROWS // 2)
        return wcol

    def combine(a, wcol):
        out = jnp.zeros((SUBLANES, LANES), F32)
        for g in range(ngroup):
            for r in range(SUBLANES):
                j = g * SUBLANES + r
                out = out + wcol[j:j + 1, :] * expert_v(a, j)
            issue_some(3 - g % 2)
        store(a, out)

    wcol0 = expert_weights(0, hidden(0))
    wcol1 = expert_weights(1, hidden(1))
    combine(0, wcol0)
    combine(1, wcol1)


def _peer_finish(h1_ref, acc_ref, fg_ref, out_ref):
    hsum = h1_ref[...] + acc_ref[...]
    ms = jnp.mean(hsum * hsum, axis=(1, 2), keepdims=True)
    _tiles_to_rows(hsum * lax.rsqrt(ms + RMS_EPS) * fg_ref[...], out_ref)


def _peerffn_body(tb, idx_ref, gate_ref, c_ref, h1_ref, fg_ref, uv_ref, out_ref, buf_ref,
                  acc_ref, sem_ref):
    npairs = tb // 2
    ahead = PEER_RING - 1

    def row_copy(t, j, slot):
        return pltpu.make_async_copy(uv_ref.at[idx_ref[t, j]], buf_ref.at[slot, j],
                                     sem_ref.at[slot])

    def wait_all(slot):
        pltpu.make_async_copy(uv_ref.at[pl.ds(0, PEER_E)], buf_ref.at[slot],
                              sem_ref.at[slot]).wait()

    def ring(i):
        base = 2 * (i % PEER_RING)
        return (base, base + 1)

    def pair(i, prefetch):
        toks = (2 * i, 2 * i + 1)
        slots = ring(i)
        nslots = ring(i + ahead)
        todo = [(a, j) for a in range(2) for j in range(PEER_E)]

        def issue_some(n):
            for a, j in todo[:n]:
                if prefetch:
                    row_copy(toks[a] + 2 * ahead, j, nslots[a]).start(priority=j % 2)
            del todo[:n]

        wait_all(slots[0])
        wait_all(slots[1])

        def store(a, out):
            acc_ref[toks[a]] = out

        _peer_pair_math(
            lambda a, j: _word_hi(buf_ref[slots[a], j]),
            lambda a, j: _word_lo(buf_ref[slots[a], j]),
            [c_ref[t] for t in toks], [gate_ref[pl.ds(t, 1), :] for t in toks], store, issue_some)
        assert not todo, "every prefetch row DMA must be issued exactly once"

    for i in range(ahead):
        for a, slot in enumerate(ring(i)):
            for j in range(PEER_E):
                row_copy(2 * i + a, j, slot).start(priority=j % 2)

    def body(i, carry):
        pair(i, True)
        return carry

    lax.fori_loop(0, npairs - ahead, body, 0)
    for i in range(npairs - ahead, npairs):
        pair(i, False)
    _peer_finish(h1_ref, acc_ref, fg_ref, out_ref)


def _peer_ffn(idx, gate, c3, h13, final_g3, uv_tiles):
    t = h13.shape[0]
    tb = 256
    row3 = pl.BlockSpec((tb, SUBLANES, LANES), lambda i: (i, 0, 0))
    return pl.pallas_call(
        functools.partial(_peerffn_body, tb),
        grid=(t // tb,),
        in_specs=[pl.BlockSpec((tb, PEER_E), lambda i: (i, 0), memory_space=pltpu.SMEM),
                  pl.BlockSpec((tb, PEER_E), lambda i: (i, 0)), row3, row3,
                  pl.BlockSpec((1, SUBLANES, LANES), lambda i: (0, 0, 0)),
                  pl.BlockSpec(memory_space=pl.ANY)],
        out_specs=pl.BlockSpec((tb, SUBLANES * LANES), lambda i: (i, 0)),
        out_shape=jax.ShapeDtypeStruct((t, SUBLANES * LANES), F32),
        scratch_shapes=[pltpu.VMEM((2 * PEER_RING, PEER_E, SUBLANES, LANES), I32),
                        pltpu.VMEM((tb, SUBLANES, LANES), F32),
                        pltpu.SemaphoreType.DMA((2 * PEER_RING,))],
        compiler_params=_params("arbitrary"),
        name="peer_ffn",
    )(idx, gate, c3, h13, final_g3, uv_tiles)


def _overlap_table(seq):
    ci = jnp.arange(seq // CMP_STRIDE)[None, :] * CMP_STRIDE
    sj = jnp.arange(SEL_BLOCK)[:, None] * SEL_BLOCK
    return ((ci < sj + SEL_BLOCK) & (ci + CMP_BLOCK > sj)).astype(F32)


def _cmp_blocks(kv):
    b, g, s, dh = kv.shape
    r = kv.reshape(b * g, s // CMP_STRIDE, CMP_STRIDE * dh)
    return jnp.concatenate([r, jnp.roll(r, -1, axis=1)], axis=-1)


def _pad_cmp_params(pe, w1):
    pe_p = jnp.pad(pe, ((0, 0), (0, LANES - HEAD))).reshape(1, -1)
    w1_p = jnp.pad(w1.reshape(CMP_BLOCK, HEAD, -1), ((0, 0), (0, LANES - HEAD), (0, 0)))
    return pe_p, w1_p.reshape(CMP_BLOCK * LANES, -1).astype(BF16)


def _layer(h, lidx, attn_norm, w_in, lq1, lk1, lq2, lk2, subln, pe_k, pe_v, w1k, w1v, w2k, w2v,
           p_da, p_nsa, w_o, ffn_norm, wq, k1, k2, pu, pv, out_norm):
    b, s, d = h.shape
    t = b * s
    g, hg = NSA_GROUPS, NSA_HG
    lambda_init = 0.8 - 0.6 * math.exp(-0.3 * lidx)
    x2 = h.reshape(t, d)

    qda, kda, vda, qn, cmp_in, ksx, kwx, vsw, gm, gn = _in_proj(
        x2, attn_norm.reshape(1, d), _pack_w_in(w_in), _rope_tables(s), s)
    tr = lambda a2: jnp.swapaxes(a2.reshape(b, s, -1), 1, 2)
    ydat = _diff_attn(tr(qda), kda.reshape(b, s, -1), tr(vda),
                      lq1.reshape(1, -1), lk1.reshape(1, -1), lq2.reshape(1, -1),
                      lk2.reshape(1, -1), subln.reshape(-1, 1), lambda_init)
    yda = jnp.swapaxes(ydat, 1, 2).reshape(t, -1)

    assert s // SEL_BLOCK <= SEL_BLOCK, "selection bias rows hold at most 64 blocks"
    cmp4 = jnp.swapaxes(cmp_in.reshape(b, s, 2 * g, LANES), 1, 2)
    pe_kp, w1_kp = _pad_cmp_params(pe_k, w1k)
    pe_vp, w1_vp = _pad_cmp_params(pe_v, w1v)
    kca, vc = _compress(_cmp_blocks(cmp4[:, :g]), _cmp_blocks(cmp4[:, g:]), pe_kp, pe_vp,
                        w1_kp, w1_vp, w2k.astype(BF16), w2v.astype(BF16))
    ncp = s // CMP_STRIDE
    kca = kca.reshape(b, g, ncp, LANES)
    vct = jnp.swapaxes(vc.reshape(b, g, ncp, HEAD), 2, 3)

    qnt = tr(qn)
    ocmp_t, bias_t = _cmp_select(qnt, kca, vct, _overlap_table(s))
    onehot = (jnp.arange(s)[:, None] // SEL_BLOCK == jnp.arange(HEAD)[None, :]).astype(BF16)
    pad_hot = jnp.concatenate([jnp.zeros_like(onehot), onehot] * g, axis=1)
    ksa = ksx.reshape(b, s, -1) + pad_hot[None]
    ynt = _sel_win(qnt, bias_t, ksa, kwx.reshape(b, s, -1), tr(vsw), ocmp_t, tr(gn))
    yn = jnp.swapaxes(ynt, 1, 2).reshape(t, -1)

    h1_tiles, c_tiles, s1t, s2t = _out_proj(
        yda, yn, gm, x2, p_da.astype(BF16), p_nsa.astype(BF16), w_o.astype(BF16),
        ffn_norm.reshape(1, d), wq.astype(BF16), k1.astype(BF16), k2.astype(BF16))
    idx_t, gate_t = _peer_topk(s1t, s2t)
    idx = idx_t.reshape(PEER_E, t).T
    gate = gate_t.reshape(PEER_E, t).T
    out = _peer_ffn(idx, gate, c_tiles, h1_tiles, out_norm.reshape(1, SUBLANES, LANES),
                    _pack_expert_rows(pu, pv))
    return out.reshape(b, s, d)


def kernel(x, attn_norm, w_in, da_lambda_q1, da_lambda_k1, da_lambda_q2, da_lambda_k2, da_subln,
           cmp_pe_k, cmp_pe_v, cmp_w1_k, cmp_w1_v, cmp_w2_k, cmp_w2_v, p_da, p_nsa, w_o,
           ffn_norm, peer_wq, peer_k1, peer_k2, peer_u, peer_v, final_norm):
    depth = attn_norm.shape[0]
    assert depth == 1, "the final norm is fused into the last layer's PEER kernel"
    h = x
    for l in range(depth):
        h = _layer(h, l, attn_norm[l], w_in[l], da_lambda_q1[l], da_lambda_k1[l], da_lambda_q2[l],
                   da_lambda_k2[l], da_subln[l], cmp_pe_k[l], cmp_pe_v[l], cmp_w1_k[l],
                   cmp_w1_v[l], cmp_w2_k[l], cmp_w2_v[l], p_da[l], p_nsa[l], w_o[l], ffn_norm[l],
                   peer_wq[l], peer_k1[l], peer_k2[l], peer_u[l], peer_v[l], final_norm)
    return h
```

```python
import functools
import math

import jax
import jax.numpy as jnp
from jax import lax
from jax.experimental import pallas as pl
from jax.experimental.pallas import tpu as pltpu

F32 = jnp.float32
BF16 = jnp.bfloat16
I32 = jnp.int32

RMS_EPS = 1e-6
ROPE_THETA = 500000.0
ROPE_HALF = 8
HEAD = 64
DA_HEADS = 4
NSA_GROUPS = 2
NSA_HG = 4
CMP_STRIDE = 16
CMP_BLOCK = 32
SEL_BLOCK = 64
SEL_TOPK = 16
WINDOW = 512
FORCE_BONUS = 1e4
NEG_BIG = -1e30
SEL_MASK_BIAS = -2.0 ** 100
ATTN_SCALE = HEAD ** -0.5 * math.log2(math.e)
PEER_HEADS = 8
PEER_NKEYS = 128
PEER_TOPK = 16
LANES = 128
SUBLANES = 8
VMEM_LIMIT = 56 * 1024 * 1024

NT_DIMS = (((1,), (1,)), ((), ()))


def _rmsnorm(x, g):
    return x * lax.rsqrt(jnp.mean(x * x, axis=-1, keepdims=True) + RMS_EPS) * g


def _sigmoid(z):
    return 1.0 / (1.0 + jnp.exp(-z))


def _gelu(z):
    return 0.5 * z * (1.0 + lax.erf(z * (2.0 ** -0.5)))


def _params(*sem):
    return pltpu.CompilerParams(dimension_semantics=sem, vmem_limit_bytes=VMEM_LIMIT)


def _sublane_transpose(v):
    sub = lax.broadcasted_iota(I32, (SUBLANES, LANES), 0)
    v = list(v)
    for dist in (4, 2, 1):
        low = (sub & dist) == 0
        nxt = list(v)
        for i in range(SUBLANES):
            if i & dist == 0:
                nxt[i] = jnp.where(low, v[i], pltpu.roll(v[i + dist], dist, 0))
                nxt[i + dist] = jnp.where(low, pltpu.roll(v[i], SUBLANES - dist, 0), v[i + dist])
        v = nxt
    return v


def _rows_to_tiles(x, tile_ref):
    for g in range(x.shape[0] // SUBLANES):
        rows = slice(g * SUBLANES, (g + 1) * SUBLANES)
        tiles = _sublane_transpose([x[rows, c * LANES:(c + 1) * LANES] for c in range(SUBLANES)])
        for e in range(SUBLANES):
            tile_ref[g * SUBLANES + e] = tiles[e]


def _tiles_to_rows(t, row_ref):
    for g in range(t.shape[0] // SUBLANES):
        chunks = _sublane_transpose([t[g * SUBLANES + e] for e in range(SUBLANES)])
        for c in range(SUBLANES):
            row_ref[g * SUBLANES:(g + 1) * SUBLANES, c * LANES:(c + 1) * LANES] = chunks[c]


_QDA0, _KDA0, _VDA0, _QN0 = 0, 512, 1024, 1536
_KC0, _VC0, _KS0, _KW0, _VSW0, _GM0, _GN0, _WCOLS = 2048, 2304, 2560, 2816, 3072, 3328, 5376, 5632


def _inproj_body(x_ref, g_ref, w_ref, rc_ref, rs1_ref, rs2_ref,
                 qda_ref, kda_ref, vda_ref, qn_ref, cmp_ref, ks_ref, kw_ref, vsw_ref, gm_ref, gn_ref):
    a = _rmsnorm(x_ref[...], g_ref[...]).astype(BF16)
    rc, rs1, rs2 = rc_ref[...], rs1_ref[...], rs2_ref[...]

    def rope(z):
        return (z * rc + pltpu.roll(z, ROPE_HALF, 1) * rs1
                + pltpu.roll(z, LANES - ROPE_HALF, 1) * rs2)

    def proj(c0):
        return jnp.dot(a, w_ref[:, c0:c0 + 256], preferred_element_type=F32)

    def rope2(z):
        return jnp.concatenate([rope(z[:, :LANES]), rope(z[:, LANES:])], axis=1)

    for c in range(2):
        qda_ref[:, c * 256:(c + 1) * 256] = (rope2(proj(_QDA0 + c * 256)) * ATTN_SCALE).astype(BF16)
        kda_ref[:, c * 256:(c + 1) * 256] = rope2(proj(_KDA0 + c * 256)).astype(BF16)
        vda_ref[:, c * 256:(c + 1) * 256] = proj(_VDA0 + c * 256).astype(BF16)
        qn_ref[:, c * 256:(c + 1) * 256] = (rope2(proj(_QN0 + c * 256)) * ATTN_SCALE).astype(BF16)
    cmp_ref[:, 0:256] = rope2(proj(_KC0)).astype(BF16)
    cmp_ref[:, 256:512] = proj(_VC0).astype(BF16)
    ks_ref[...] = rope2(proj(_KS0)).astype(BF16)
    kw_ref[...] = rope2(proj(_KW0)).astype(BF16)
    vsw_ref[...] = proj(_VSW0).astype(BF16)
    for c in range(8):
        gm_ref[:, c * 256:(c + 1) * 256] = _sigmoid(proj(_GM0 + c * 256)).astype(BF16)
    gn_ref[...] = _sigmoid(proj(_GN0))


def _pack_w_in(w):
    d = w.shape[0]
    zeros = lambda n: jnp.zeros((d, n), w.dtype)

    def spread_groups(c0):
        return [w[:, c0:c0 + HEAD], zeros(HEAD), w[:, c0 + HEAD:c0 + 2 * HEAD], zeros(HEAD)]

    kv_w = NSA_GROUPS * HEAD
    kv0 = 4 * 512
    kc, vc, ks, vs, kw, vw = (kv0 + kv_w * i for i in range(6))
    gn0 = kv0 + 6 * kv_w
    per_group = NSA_HG * 3
    gm0 = gn0 + NSA_GROUPS * per_group
    assert w.shape[1] == gm0 + 2 * d
    gn = w[:, gn0:gm0]
    cols = ([w[:, :kv0]] + spread_groups(kc) + spread_groups(vc) + spread_groups(ks)
            + spread_groups(kw) + [w[:, vs:vs + kv_w], w[:, vw:vw + kv_w], w[:, gm0:],
                                   gn[:, :per_group], zeros(LANES - per_group),
                                   gn[:, per_group:], zeros(LANES - per_group)])
    packed = jnp.concatenate(cols, axis=1).astype(BF16)
    assert packed.shape[1] == _WCOLS
    return packed


def _rope_tables(seq):
    inv = jnp.power(ROPE_THETA, -jnp.arange(ROPE_HALF, dtype=F32) * 2.0 / (2 * ROPE_HALF))
    ang = jnp.arange(seq, dtype=F32)[:, None] * inv[None, :]
    cos, sin = jnp.cos(ang), jnp.sin(ang)
    one = jnp.ones((seq, HEAD - 2 * ROPE_HALF), F32)
    zero8 = jnp.zeros((seq, ROPE_HALF), F32)
    zero48 = jnp.zeros_like(one)
    rc = jnp.concatenate([cos, cos, one], axis=1)
    rs1 = jnp.concatenate([zero8, sin, zero48], axis=1)
    rs2 = jnp.concatenate([-sin, zero8, zero48], axis=1)
    return tuple(jnp.concatenate([t, t], axis=1) for t in (rc, rs1, rs2))


def _in_proj(x2, norm_g, w_packed, rope_tabs, seq):
    t, d = x2.shape
    tm = 512
    nseq = seq // tm
    row = lambda i: (i, 0)
    fixed = lambda i: (0, 0)
    out_shapes = [
        jax.ShapeDtypeStruct((t, 512), BF16), jax.ShapeDtypeStruct((t, 512), BF16),
        jax.ShapeDtypeStruct((t, 512), BF16), jax.ShapeDtypeStruct((t, 512), BF16),
        jax.ShapeDtypeStruct((t, 512), BF16), jax.ShapeDtypeStruct((t, 256), BF16),
        jax.ShapeDtypeStruct((t, 256), BF16), jax.ShapeDtypeStruct((t, 256), BF16),
        jax.ShapeDtypeStruct((t, 2048), BF16), jax.ShapeDtypeStruct((t, 256), F32)]
    rope_spec = pl.BlockSpec((tm, LANES), lambda i: (i % nseq, 0))
    return pl.pallas_call(
        _inproj_body,
        grid=(t // tm,),
        in_specs=[pl.BlockSpec((tm, d), row), pl.BlockSpec((1, d), fixed),
                  pl.BlockSpec((d, _WCOLS), fixed), rope_spec, rope_spec, rope_spec],
        out_specs=[pl.BlockSpec((tm, s.shape[1]), row) for s in out_shapes],
        out_shape=out_shapes,
        compiler_params=_params("parallel"),
        name="in_proj",
    )(x2, norm_g, w_packed, *rope_tabs)


def _softmax_step(s, vt, m_ref, l_ref, acc_ref):
    m_prev = m_ref[...]
    m_new = jnp.maximum(m_prev, jnp.max(s, axis=0, keepdims=True))
    alpha = jnp.exp2(m_prev - m_new)
    p = jnp.exp2(s - m_new)
    l_ref[...] = alpha * l_ref[...] + jnp.sum(p, axis=0, keepdims=True)
    acc_ref[...] = alpha * acc_ref[...] + jnp.dot(vt, p.astype(BF16), preferred_element_type=F32)
    m_ref[...] = m_new


def _softmax_reset(m_ref, l_ref, acc_ref):
    m_ref[...] = jnp.full(m_ref.shape, NEG_BIG, F32)
    l_ref[...] = jnp.zeros(l_ref.shape, F32)
    acc_ref[...] = jnp.zeros(acc_ref.shape, F32)


def _attend_tiles(n_full, scores, values, mask_last, sa_ref, sb_ref, m_ref, l_ref, acc_ref):
    step = lambda s, t: _softmax_step(s, values(t), m_ref, l_ref, acc_ref)
    sa_ref[...] = scores(0)

    def two_tiles(i, carry):
        t = 2 * i
        sb_ref[...] = scores(t + 1)
        step(sa_ref[...], t)
        sa_ref[...] = scores(t + 2)
        step(sb_ref[...], t + 1)
        return carry

    lax.fori_loop(0, n_full // 2, two_tiles, 0)
    odd = (n_full & 1) == 1

    @pl.when(odd)
    def _():
        sb_ref[...] = scores(n_full)
        step(sa_ref[...], n_full - 1)
        step(mask_last(sb_ref[...]), n_full)

    @pl.when(jnp.logical_not(odd))
    def _():
        step(mask_last(sa_ref[...]), n_full)


def _key_tile(ref, kt, bk):
    return ref[(0,) * (len(ref.shape) - 2) + (pl.ds(pl.multiple_of(kt * bk, bk), bk), slice(None))]


def _value_tile(ref, kt, bk):
    return ref[(0,) * (len(ref.shape) - 2) + (slice(None), pl.ds(pl.multiple_of(kt * bk, bk), bk))]


def _diffattn_body(lambda_init, bq, q_ref, k_ref, v_ref, lq1_ref, lk1_ref, lq2_ref, lk2_ref,
                   sub_ref, y_ref, qbd_ref, m_ref, l_ref, acc_ref, sa_ref, sb_ref):
    qi = pl.program_id(2)
    bk = bq
    qt = q_ref[0]
    sub = lax.broadcasted_iota(I32, qt.shape, 0)
    zero = jnp.zeros_like(qt)
    qbd_ref[:, 0:bq] = jnp.where(sub < HEAD, qt, zero)
    qbd_ref[:, bq:2 * bq] = jnp.where(sub >= HEAD, qt, zero)
    _softmax_reset(m_ref, l_ref, acc_ref)

    def scores(kt):
        return jnp.dot(_key_tile(k_ref, kt, bk), qbd_ref[...], preferred_element_type=F32)

    def causal(s):
        r = lax.broadcasted_iota(I32, (bk, 2 * bq), 0)
        c = lax.broadcasted_iota(I32, (bk, 2 * bq), 1) & (bq - 1)
        return jnp.where(r <= c, s, NEG_BIG)

    _attend_tiles(qi, scores, lambda kt: _value_tile(v_ref, kt, bk), causal,
                  sa_ref, sb_ref, m_ref, l_ref, acc_ref)

    o = acc_ref[...] / l_ref[...]
    lam = (jnp.exp(jnp.sum(lq1_ref[...] * lk1_ref[...], axis=1, keepdims=True))
           - jnp.exp(jnp.sum(lq2_ref[...] * lk2_ref[...], axis=1, keepdims=True)) + lambda_init)
    d = o[:, 0:bq] - lam * o[:, bq:2 * bq]
    ms = jnp.mean(d * d, axis=0, keepdims=True)
    y = d * lax.rsqrt(ms + RMS_EPS) * sub_ref[...] * (1.0 - lambda_init)
    y_ref[0] = y.astype(BF16)


def _diff_attn(qdat, kda, vdat, lq1, lk1, lq2, lk2, subln_col, lambda_init):
    b, s, _ = kda.shape
    bq = 512
    vec = lambda n: pl.BlockSpec((1, n), lambda bi, h, qi: (0, 0))
    dv = 2 * HEAD
    qtile = pl.BlockSpec((1, dv, bq), lambda bi, h, qi: (bi, h, qi))
    return pl.pallas_call(
        functools.partial(_diffattn_body, lambda_init, bq),
        grid=(b, DA_HEADS, s // bq),
        in_specs=[qtile,
                  pl.BlockSpec((1, s, LANES), lambda bi, h, qi: (bi, 0, h)),
                  pl.BlockSpec((1, dv, s), lambda bi, h, qi: (bi, h, 0)),
                  vec(HEAD), vec(HEAD), vec(HEAD), vec(HEAD),
                  pl.BlockSpec((dv, 1), lambda bi, h, qi: (0, 0))],
        out_specs=qtile,
        out_shape=jax.ShapeDtypeStruct((b, DA_HEADS * dv, s), BF16),
        scratch_shapes=[pltpu.VMEM((LANES, 2 * bq), BF16), pltpu.VMEM((1, 2 * bq), F32),
                        pltpu.VMEM((1, 2 * bq), F32), pltpu.VMEM((dv, 2 * bq), F32),
                        pltpu.VMEM((bq, 2 * bq), F32), pltpu.VMEM((bq, 2 * bq), F32)],
        compiler_params=_params("parallel", "parallel", "parallel"),
        name="diff_attn",
    )(qdat, kda, vdat, lq1, lk1, lq2, lk2, subln_col)


def _compress_body(xk_ref, xv_ref, pek_ref, pev_ref, w1k_ref, w1v_ref, w2k_ref, w2v_ref,
                   kc_ref, vc_ref):
    def mlp(x_ref, pe_ref, w1_ref, w2_ref):
        blocks = (x_ref[0].astype(F32) + pe_ref[...]).astype(BF16)
        hid = _gelu(jnp.dot(blocks, w1_ref[...], preferred_element_type=F32))
        return jnp.dot(hid.astype(BF16), w2_ref[...], preferred_element_type=F32)

    kc = mlp(xk_ref, pek_ref, w1k_ref, w2k_ref)
    kc_ref[0] = jnp.concatenate([kc, jnp.zeros_like(kc)], axis=1).astype(BF16)
    vc_ref[0] = mlp(xv_ref, pev_ref, w1v_ref, w2v_ref).astype(BF16)


def _compress(xk, xv, pe_k, pe_v, w1k, w1v, w2k, w2v):
    n, ncp, width = xk.shape
    blk = pl.BlockSpec((1, ncp, width), lambda i: (i, 0, 0))
    fixed = lambda shape: pl.BlockSpec(shape, lambda i: (0, 0))
    return pl.pallas_call(
        _compress_body,
        grid=(n,),
        in_specs=[blk, blk, fixed((1, width)), fixed((1, width)), fixed((width, HEAD)),
                  fixed((width, HEAD)), fixed((HEAD, HEAD)), fixed((HEAD, HEAD))],
        out_specs=[pl.BlockSpec((1, ncp, LANES), lambda i: (i, 0, 0)),
                   pl.BlockSpec((1, ncp, HEAD), lambda i: (i, 0, 0))],
        out_shape=[jax.ShapeDtypeStruct((n, ncp, LANES), BF16),
                   jax.ShapeDtypeStruct((n, ncp, HEAD), BF16)],
        compiler_params=_params("parallel"),
        name="compress",
    )(xk, xv, pe_k, pe_v, w1k, w1v, w2k, w2v)


def _heads_on_lanes(qt, bq):
    return jnp.concatenate([qt[h * HEAD:(h + 1) * HEAD, :] for h in range(NSA_HG)], axis=1)


def _cmpsel_body(bq, q_ref, kc_ref, vc_ref, ovl_ref, ocmp_ref, bias_ref):
    qi = pl.program_id(2)
    ncp = kc_ref.shape[2]
    rows = NSA_HG * bq
    q2 = _heads_on_lanes(q_ref[0], bq)
    qz = jnp.concatenate([q2, jnp.zeros_like(q2)], axis=0)
    s = jnp.dot(kc_ref[0, 0], qz, preferred_element_type=F32)
    n = lax.broadcasted_iota(I32, (ncp, rows), 0)
    qpos = qi * bq + (lax.broadcasted_iota(I32, (ncp, rows), 1) & (bq - 1))
    cmask = n * CMP_STRIDE + (CMP_BLOCK - 1) <= qpos
    s = jnp.where(cmask, s, NEG_BIG)
    e = jnp.exp2(s - jnp.max(s, axis=0, keepdims=True))
    p = jnp.where(cmask, e / jnp.sum(e, axis=0, keepdims=True), 0.0)
    o = jnp.dot(vc_ref[0, 0], p.astype(BF16), preferred_element_type=F32)
    for h in range(NSA_HG):
        ocmp_ref[0, 0, h * HEAD:(h + 1) * HEAD, :] = o[:, h * bq:(h + 1) * bq]

    psum = p[:, 0:bq] + p[:, bq:2 * bq] + p[:, 2 * bq:3 * bq] + p[:, 3 * bq:4 * bq]
    imp = jnp.dot(ovl_ref[...], psum, preferred_element_type=F32)
    blk = lax.broadcasted_iota(I32, (SEL_BLOCK, bq), 0)
    pos = qi * bq + lax.broadcasted_iota(I32, (SEL_BLOCK, bq), 1)
    cur = lax.shift_right_logical(pos, SEL_BLOCK.bit_length() - 1)
    valid = blk <= cur
    forced = (blk == 0) | (blk == cur) | (blk == cur - 1)
    score = jnp.where(valid, imp + jnp.where(forced, FORCE_BONUS, 0.0), -jnp.inf)
    rank = jnp.zeros((SEL_BLOCK, bq), I32)
    for i in range(SEL_BLOCK):
        other = score[i:i + 1, :]
        beats = (other > score) | ((other == score) & (blk > i))
        rank = rank + beats.astype(I32)
    keep = valid & (rank < SEL_TOPK)
    bias_ref[0, 0] = jnp.where(keep, 0.0, SEL_MASK_BIAS).astype(BF16)


def _cmp_select(qnt, kca, vct, overlap_t):
    b, g, ncp, _ = kca.shape
    s = qnt.shape[2]
    bq = 128
    dq = NSA_HG * HEAD
    return pl.pallas_call(
        functools.partial(_cmpsel_body, bq),
        grid=(b, g, s // bq),
        in_specs=[pl.BlockSpec((1, dq, bq), lambda bi, gi, qi: (bi, gi, qi)),
                  pl.BlockSpec((1, 1, ncp, LANES), lambda bi, gi, qi: (bi, gi, 0, 0)),
                  pl.BlockSpec((1, 1, HEAD, ncp), lambda bi, gi, qi: (bi, gi, 0, 0)),
                  pl.BlockSpec((SEL_BLOCK, ncp), lambda bi, gi, qi: (0, 0))],
        out_specs=[pl.BlockSpec((1, 1, dq, bq), lambda bi, gi, qi: (bi, gi, 0, qi)),
                   pl.BlockSpec((1, 1, SEL_BLOCK, bq), lambda bi, gi, qi: (bi, gi, 0, qi))],
        out_shape=[jax.ShapeDtypeStruct((b, g, dq, s), F32),
                   jax.ShapeDtypeStruct((b, g, SEL_BLOCK, s), BF16)],
        compiler_params=_params("parallel", "parallel", "parallel"),
        name="cmp_select",
    )(qnt, kca, vct, overlap_t)


def _selwin_body(bq, bks, q_ref, bias_ref, ks_ref, vs_ref, kw_ref, vw_ref, ocmp_ref, gate_ref,
                 y_ref, m_ref, l_ref, acc_ref, qa_ref, sa_ref, sb_ref):
    qi = pl.program_id(2)
    rows = NSA_HG * bq
    q2 = _heads_on_lanes(q_ref[0], bq)
    bias = bias_ref[0, 0]
    qa = jnp.concatenate([q2, jnp.concatenate([bias] * NSA_HG, axis=1)], axis=0)
    qw = jnp.concatenate([q2, jnp.zeros_like(q2)], axis=0)
    r = lax.broadcasted_iota(I32, (bq, rows), 0)
    c = lax.broadcasted_iota(I32, (bq, rows), 1) & (bq - 1)

    _softmax_reset(m_ref, l_ref, acc_ref)

    qa_ref[...] = qa
    last = (qi * bq) // bks

    def causal(s):
        kpos = last * bks + lax.broadcasted_iota(I32, (bks, rows), 0)
        qpos = qi * bq + (lax.broadcasted_iota(I32, (bks, rows), 1) & (bq - 1))
        return jnp.where(kpos <= qpos, s, NEG_BIG)

    _attend_tiles(last,
                  lambda kt: jnp.dot(_key_tile(ks_ref, kt, bks), qa_ref[...],
                                     preferred_element_type=F32),
                  lambda kt: _value_tile(vs_ref, kt, bks), causal,
                  sa_ref, sb_ref, m_ref, l_ref, acc_ref)
    o_sel = acc_ref[...] / l_ref[...]

    _softmax_reset(m_ref, l_ref, acc_ref)
    backs = list(range(WINDOW // bq, -1, -1))
    tiles = [jnp.maximum(qi - back, 0) for back in backs]
    raw = [jnp.dot(_key_tile(kw_ref, kt, bq), qw, preferred_element_type=F32) for kt in tiles]
    for back, kt, s in zip(backs, tiles, raw):
        if back >= 1 and (back + 1) * bq <= WINDOW:
            inside = qi >= back
        else:
            dist = c + back * bq - r
            inside = (dist >= 0) & (dist < WINDOW) & (qi >= back)
        _softmax_step(jnp.where(inside, s, NEG_BIG), _value_tile(vw_ref, kt, bq),
                      m_ref, l_ref, acc_ref)

    o_win = acc_ref[...] / l_ref[...]
    gate = gate_ref[0]
    for h in range(NSA_HG):
        sl = slice(h * bq, (h + 1) * bq)
        y = (gate[3 * h:3 * h + 1, :] * ocmp_ref[0, 0, h * HEAD:(h + 1) * HEAD, :]
             + gate[3 * h + 1:3 * h + 2, :] * o_sel[:, sl]
             + gate[3 * h + 2:3 * h + 3, :] * o_win[:, sl])
        y_ref[0, h * HEAD:(h + 1) * HEAD, :] = y.astype(BF16)


def _sel_win(qnt, bias_t, ksa, kwa, vswt, ocmp_t, gnt):
    b, s, _ = ksa.shape
    g = NSA_GROUPS
    bq, bks = 256, 512
    dq = NSA_HG * HEAD
    rows = NSA_HG * bq
    keys = pl.BlockSpec((1, s, LANES), lambda bi, gi, qi: (bi, 0, gi))
    vals = lambda first: pl.BlockSpec((1, HEAD, s), lambda bi, gi, qi: (bi, first + gi, 0))
    qtile = pl.BlockSpec((1, dq, bq), lambda bi, gi, qi: (bi, gi, qi))
    return pl.pallas_call(
        functools.partial(_selwin_body, bq, bks),
        grid=(b, g, s // bq),
        in_specs=[qtile,
                  pl.BlockSpec((1, 1, SEL_BLOCK, bq), lambda bi, gi, qi: (bi, gi, 0, qi)),
                  keys, vals(0), keys, vals(g),
                  pl.BlockSpec((1, 1, dq, bq), lambda bi, gi, qi: (bi, gi, 0, qi)),
                  pl.BlockSpec((1, LANES, bq), lambda bi, gi, qi: (bi, gi, qi))],
        out_specs=qtile,
        out_shape=jax.ShapeDtypeStruct((b, g * dq, s), BF16),
        scratch_shapes=[pltpu.VMEM((1, rows), F32), pltpu.VMEM((1, rows), F32),
                        pltpu.VMEM((HEAD, rows), F32), pltpu.VMEM((LANES, rows), BF16),
                        pltpu.VMEM((bks, rows), F32), pltpu.VMEM((bks, rows), F32)],
        compiler_params=_params("parallel", "parallel", "parallel"),
        name="sel_win",
    )(qnt, bias_t, ksa, vswt, kwa, vswt, ocmp_t, gnt)


def _outproj_body(yda_ref, yn_ref, gm_ref, x_ref, pda_ref, pnsa_ref, wo_ref, fg_ref, wq_ref,
                  k1_ref, k2_ref, h1_ref, c_ref, s1_ref, s2_ref):
    d = x_ref.shape[1]
    a = jnp.dot(yda_ref[...], pda_ref[...], preferred_element_type=F32)
    bn = jnp.dot(yn_ref[...], pnsa_ref[...], preferred_element_type=F32)
    merged = gm_ref[:, :d].astype(F32) * a + gm_ref[:, d:].astype(F32) * bn
    h1 = x_ref[...] + jnp.dot(merged.astype(BF16), wo_ref[...], preferred_element_type=F32)
    c = _rmsnorm(h1, fg_ref[...])
    _rows_to_tiles(h1, h1_ref)
    _rows_to_tiles(c, c_ref)
    cb = c.astype(BF16)
    for h in range(PEER_HEADS):
        qh = jnp.dot(cb, wq_ref[:, h * 256:(h + 1) * 256], preferred_element_type=F32).astype(BF16)
        s1_ref[h] = lax.dot_general(k1_ref[...], qh[:, :LANES], NT_DIMS, preferred_element_type=F32)
        s2_ref[h] = lax.dot_general(k2_ref[...], qh[:, LANES:], NT_DIMS, preferred_element_type=F32)


def _out_proj(yda, yn, gm, x2, pda, pnsa, wo, ffn_g, wq, k1, k2):
    t, d = x2.shape
    tm = 512
    row = lambda w: pl.BlockSpec((tm, w), lambda i: (i, 0))
    fixed = lambda a: pl.BlockSpec(a.shape, lambda i: (0, 0))
    sspec = pl.BlockSpec((PEER_HEADS, PEER_NKEYS, tm), lambda i: (0, 0, i))
    assert d == SUBLANES * LANES
    tiles = pl.BlockSpec((tm, SUBLANES, LANES), lambda i: (i, 0, 0))
    return pl.pallas_call(
        _outproj_body,
        grid=(t // tm,),
        in_specs=[row(512), row(512), row(2 * d), row(d), fixed(pda), fixed(pnsa), fixed(wo),
                  fixed(ffn_g), fixed(wq), fixed(k1), fixed(k2)],
        out_specs=[tiles, tiles, sspec, sspec],
        out_shape=[jax.ShapeDtypeStruct((t, SUBLANES, LANES), F32),
                   jax.ShapeDtypeStruct((t, SUBLANES, LANES), F32),
                   jax.ShapeDtypeStruct((PEER_HEADS, PEER_NKEYS, t), F32),
                   jax.ShapeDtypeStruct((PEER_HEADS, PEER_NKEYS, t), F32)],
        compiler_params=_params("parallel"),
        name="out_proj",
    )(yda, yn, gm, x2, pda, pnsa, wo, ffn_g, wq, k1, k2)


def _batcher_pairs(n):
    pairs = []

    def merge(lo, hi, r):
        step = r * 2
        if step < hi - lo:
            merge(lo, hi, step)
            merge(lo + r, hi, step)
            pairs.extend((i, i + r) for i in range(lo + r, hi - r, step))
        else:
            pairs.append((lo, lo + r))

    def sort(lo, hi):
        if hi - lo >= 1:
            mid = lo + (hi - lo) // 2
            sort(lo, mid)
            sort(mid + 1, hi)
            merge(lo, hi, 1)

    sort(0, n - 1)
    return pairs


_NET16 = _batcher_pairs(PEER_TOPK)


def _cmpx(a, b):
    c = (a[0] > b[0]) | ((a[0] == b[0]) & (a[1] < b[1]))
    return ((jnp.where(c, a[0], b[0]), jnp.where(c, a[1], b[1])),
            (jnp.where(c, b[0], a[0]), jnp.where(c, b[1], a[1])))


def _sort_lists(lists, n_real):
    lists = list(lists)
    for i, j in _NET16:
        if j < n_real:
            lists[i], lists[j] = _cmpx(lists[i], lists[j])
    return lists


def _merge_top(a, b):
    k = PEER_TOPK
    lists = [_cmpx(a[i], b[k - 1 - i])[0] for i in range(k)]
    step = k // 2
    while step >= 1:
        for i in range(k):
            if i & step == 0:
                lists[i], lists[i + step] = _cmpx(lists[i], lists[i + step])
        step //= 2
    return lists


def _top16(pairs):
    k = PEER_TOPK
    filler = (jnp.full(pairs[0][0].shape, -jnp.inf, F32), jnp.zeros(pairs[0][1].shape, I32))
    groups = []
    for g0 in range(0, len(pairs), k):
        chunk = list(pairs[g0:g0 + k])
        groups.append(_sort_lists(chunk + [filler] * (k - len(chunk)), len(chunk)))
    while len(groups) > 1:
        groups = [_merge_top(groups[i], groups[i + 1]) if i + 1 < len(groups) else groups[i]
                  for i in range(0, len(groups), 2)]
    return groups[0]


def _product_key_select(s1, s2):
    k = PEER_TOPK
    n_exp = PEER_NKEYS * PEER_NKEYS
    ids = lambda n: [jnp.full(s1[0].shape, i, I32) for i in range(n)]
    l1 = _top16(list(zip(s1, ids(len(s1)))))
    l2 = _top16(list(zip(s2, ids(len(s2)))))
    pair = lambda a, b: (l1[a][0] + l2[b][0],
                         (a * k + b) * n_exp + l1[a][1] * PEER_NKEYS + l2[b][1])
    first_row = [pair(0, b) for b in range(k)]
    rest = [pair(a, b) for a in range(1, k) for b in range(k // (a + 1))]
    best = _merge_top(first_row, _top16(rest))
    return [(v, p & (n_exp - 1)) for v, p in best]


def _peertopk_body(tt, s1_ref, s2_ref, idx_ref, gate_ref):
    nblk = tt // LANES
    assert nblk == SUBLANES

    def keys_major(ref, h):
        tiles = []
        for kg in range(PEER_NKEYS // SUBLANES):
            rows = slice(kg * SUBLANES, (kg + 1) * SUBLANES)
            tiles += _sublane_transpose([ref[h, rows, b * LANES:(b + 1) * LANES]
                                         for b in range(nblk)])
        return tiles

    def store(ref, h, tiles):
        for g in range(PEER_TOPK // SUBLANES):
            blocks = _sublane_transpose(tiles[g * SUBLANES:(g + 1) * SUBLANES])
            for b in range(nblk):
                ref[h, g * SUBLANES:(g + 1) * SUBLANES, b * LANES:(b + 1) * LANES] = blocks[b]

    def head(h, carry):
        best = _product_key_select(keys_major(s1_ref, h), keys_major(s2_ref, h))
        ex = [jnp.exp(v - best[0][0]) for v, _ in best]
        z = ex[0]
        for e in ex[1:]:
            z = z + e
        store(gate_ref, h, [e / z for e in ex])
        store(idx_ref, h, [ix for _, ix in best])
        return carry

    lax.fori_loop(0, PEER_HEADS, head, 0)


def _peer_topk(s1t, s2t):
    _, _, t = s1t.shape
    tt = SUBLANES * LANES
    spec_in = pl.BlockSpec((PEER_HEADS, PEER_NKEYS, tt), lambda i: (0, 0, i))
    spec_out = pl.BlockSpec((PEER_HEADS, PEER_TOPK, tt), lambda i: (0, 0, i))
    return pl.pallas_call(
        functools.partial(_peertopk_body, tt),
        grid=(t // tt,),
        in_specs=[spec_in, spec_in],
        out_specs=[spec_out, spec_out],
        out_shape=[jax.ShapeDtypeStruct((PEER_HEADS, PEER_TOPK, t), I32),
                   jax.ShapeDtypeStruct((PEER_HEADS, PEER_TOPK, t), F32)],
        compiler_params=_params("parallel"),
        name="peer_topk",
    )(s1t, s2t)


PEER_E = PEER_HEADS * PEER_TOPK
PEER_RING = 3
PEER_MID_ROWS = 48


def _pack_body(u_ref, v_ref, out_ref):
    bf16_bits = lambda a: lax.bitcast_convert_type(a.astype(BF16).astype(F32), I32)
    words = bf16_bits(u_ref[...]) | lax.shift_right_logical(bf16_bits(v_ref[...]), jnp.int32(16))
    _rows_to_tiles(words, out_ref)


def _pack_expert_rows(pu, pv):
    n, d = pu.shape
    assert d == SUBLANES * LANES
    te = 256
    blk = pl.BlockSpec((te, d), lambda i: (i, 0))
    return pl.pallas_call(
        _pack_body,
        grid=(n // te,),
        in_specs=[blk, blk],
        out_specs=pl.BlockSpec((te, SUBLANES, LANES), lambda i: (i, 0, 0)),
        out_shape=jax.ShapeDtypeStruct((n, SUBLANES, LANES), I32),
        compiler_params=_params("parallel"),
        name="pack_experts",
    )(pu, pv)


def _word_hi(w):
    return lax.bitcast_convert_type(w & jnp.int32(-65536), F32)


def _word_lo(w):
    return lax.bitcast_convert_type(lax.shift_left(w, jnp.int32(16)), F32)


def _sublane_sums(a, sub):
    for dist in (4, 2, 1):
        low = (sub & dist) == 0
        half = len(a) // 2
        a = [jnp.where(low, a[i], pltpu.roll(a[i + half], dist, 0))
             + jnp.where(low, pltpu.roll(a[i], SUBLANES - dist, 0), a[i + half])
             for i in range(half)]
    return a[0]


def _peer_pair_math(expert_u, expert_v, x8, gates, store, issue_some):
    sub = lax.broadcasted_iota(I32, (SUBLANES, LANES), 0)
    eye = (lax.broadcasted_iota(I32, (PEER_E, LANES), 0)
           == lax.broadcasted_iota(I32, (PEER_E, LANES), 1))
    ones_rows = jnp.ones((SUBLANES, LANES), BF16)
    ones_sq = jnp.ones((LANES, LANES), BF16)
    ngroup = PEER_E // SUBLANES

    def hidden(a):
        groups = []
        for g in range(ngroup):
            prods = [expert_u(a, g * SUBLANES + r) * x8[a] for r in range(SUBLANES)]
            groups.append(_sublane_sums(prods, sub))
            issue_some(3 - g % 2)
        return jnp.concatenate(groups, axis=0)

    def expert_weights(a, q):
        q_hi = q.astype(BF16)
        q_lo = (q - q_hi.astype(F32)).astype(BF16)
        hid = (lax.dot_general(ones_rows, q_hi, NT_DIMS, preferred_element_type=F32)
               + lax.dot_general(ones_rows, q_lo, NT_DIMS, preferred_element_type=F32))
        issue_some(PEER_MID_ROWS // 2)
        w = _gelu(hid[0:1]) * gates[a]
        wd = jnp.where(eye, jnp.broadcast_to(w, (PEER_E, LANES)), 0.0).astype(BF16)
        wcol = jnp.dot(wd, ones_sq, preferred_element_type=F32)
        issue_some(PEER_MID_ROWS // 2)
        return wcol

    def combine(a, wcol):
        out = jnp.zeros((SUBLANES, LANES), F32)
        for g in range(ngroup):
            for r in range(SUBLANES):
                j = g * SUBLANES + r
                out = out + wcol[j:j + 1, :] * expert_v(a, j)
            issue_some(3 - g % 2)
        store(a, out)

    wcol0 = expert_weights(0, hidden(0))
    wcol1 = expert_weights(1, hidden(1))
    combine(0, wcol0)
    combine(1, wcol1)


def _peer_finish(h1_ref, acc_ref, fg_ref, out_ref):
    hsum = h1_ref[...] + acc_ref[...]
    ms = jnp.mean(hsum * hsum, axis=(1, 2), keepdims=True)
    _tiles_to_rows(hsum * lax.rsqrt(ms + RMS_EPS) * fg_ref[...], out_ref)


def _peerffn_body(tb, idx_ref, gate_ref, c_ref, h1_ref, fg_ref, uv_ref, out_ref, buf_ref,
                  acc_ref, sem_ref):
    npairs = tb // 2
    ahead = PEER_RING - 1

    def row_copy(t, j, slot):
        return pltpu.make_async_copy(uv_ref.at[idx_ref[t, j]], buf_ref.at[slot, j],
                                     sem_ref.at[slot])

    def wait_all(slot):
        pltpu.make_async_copy(uv_ref.at[pl.ds(0, PEER_E)], buf_ref.at[slot],
                              sem_ref.at[slot]).wait()

    def ring(i):
        base = 2 * (i % PEER_RING)
        return (base, base + 1)

    def pair(i, prefetch):
        toks = (2 * i, 2 * i + 1)
        slots = ring(i)
        nslots = ring(i + ahead)
        todo = [(a, j) for a in range(2) for j in range(PEER_E)]

        def issue_some(n):
            for a, j in todo[:n]:
                if prefetch:
                    row_copy(toks[a] + 2 * ahead, j, nslots[a]).start(priority=j % 2)
            del todo[:n]

        wait_all(slots[0])
        wait_all(slots[1])

        def store(a, out):
            acc_ref[toks[a]] = out

        _peer_pair_math(
            lambda a, j: _word_hi(buf_ref[slots[a], j]),
            lambda a, j: _word_lo(buf_ref[slots[a], j]),
            [c_ref[t] for t in toks], [gate_ref[pl.ds(t, 1), :] for t in toks], store, issue_some)
        assert not todo, "every prefetch row DMA must be issued exactly once"

    for i in range(ahead):
        for a, slot in enumerate(ring(i)):
            for j in range(PEER_E):
                row_copy(2 * i + a, j, slot).start(priority=j % 2)

    def body(i, carry):
        pair(i, True)
        return carry

    lax.fori_loop(0, npairs - ahead, body, 0)
    for i in range(npairs - ahead, npairs):
        pair(i, False)
    _peer_finish(h1_ref, acc_ref, fg_ref, out_ref)


def _peer_ffn(idx, gate, c3, h13, final_g3, uv_tiles):
    t = h13.shape[0]
    tb = 256
    row3 = pl.BlockSpec((tb, SUBLANES, LANES), lambda i: (i, 0, 0))
    return pl.pallas_call(
        functools.partial(_peerffn_body, tb),
        grid=(t // tb,),
        in_specs=[pl.BlockSpec((tb, PEER_E), lambda i: (i, 0), memory_space=pltpu.SMEM),
                  pl.BlockSpec((tb, PEER_E), lambda i: (i, 0)), row3, row3,
                  pl.BlockSpec((1, SUBLANES, LANES), lambda i: (0, 0, 0)),
                  pl.BlockSpec(memory_space=pl.ANY)],
        out_specs=pl.BlockSpec((tb, SUBLANES * LANES), lambda i: (i, 0)),
        out_shape=jax.ShapeDtypeStruct((t, SUBLANES * LANES), F32),
        scratch_shapes=[pltpu.VMEM((2 * PEER_RING, PEER_E, SUBLANES, LANES), I32),
                        pltpu.VMEM((tb, SUBLANES, LANES), F32),
                        pltpu.SemaphoreType.DMA((2 * PEER_RING,))],
        compiler_params=_params("arbitrary"),
        name="peer_ffn",
    )(idx, gate, c3, h13, final_g3, uv_tiles)


def _overlap_table(seq):
    ci = jnp.arange(seq // CMP_STRIDE)[None, :] * CMP_STRIDE
    sj = jnp.arange(SEL_BLOCK)[:, None] * SEL_BLOCK
    return ((ci < sj + SEL_BLOCK) & (ci + CMP_BLOCK > sj)).astype(F32)


def _cmp_blocks(kv):
    b, g, s, dh = kv.shape
    r = kv.reshape(b * g, s // CMP_STRIDE, CMP_STRIDE * dh)
    return jnp.concatenate([r, jnp.roll(r, -1, axis=1)], axis=-1)


def _pad_cmp_params(pe, w1):
    pe_p = jnp.pad(pe, ((0, 0), (0, LANES - HEAD))).reshape(1, -1)
    w1_p = jnp.pad(w1.reshape(CMP_BLOCK, HEAD, -1), ((0, 0), (0, LANES - HEAD), (0, 0)))
    return pe_p, w1_p.reshape(CMP_BLOCK * LANES, -1).astype(BF16)


def _layer(h, lidx, attn_norm, w_in, lq1, lk1, lq2, lk2, subln, pe_k, pe_v, w1k, w1v, w2k, w2v,
           p_da, p_nsa, w_o, ffn_norm, wq, k1, k2, pu, pv, out_norm):
    b, s, d = h.shape
    t = b * s
    g, hg = NSA_GROUPS, NSA_HG
    lambda_init = 0.8 - 0.6 * math.exp(-0.3 * lidx)
    x2 = h.reshape(t, d)

    qda, kda, vda, qn, cmp_in, ksx, kwx, vsw, gm, gn = _in_proj(
        x2, attn_norm.reshape(1, d), _pack_w_in(w_in), _rope_tables(s), s)
    tr = lambda a2: jnp.swapaxes(a2.reshape(b, s, -1), 1, 2)
    ydat = _diff_attn(tr(qda), kda.reshape(b, s, -1), tr(vda),
                      lq1.reshape(1, -1), lk1.reshape(1, -1), lq2.reshape(1, -1),
                      lk2.reshape(1, -1), subln.reshape(-1, 1), lambda_init)
    yda = jnp.swapaxes(ydat, 1, 2).reshape(t, -1)

    assert s // SEL_BLOCK <= SEL_BLOCK, "selection bias rows hold at most 64 blocks"
    cmp4 = jnp.swapaxes(cmp_in.reshape(b, s, 2 * g, LANES), 1, 2)
    pe_kp, w1_kp = _pad_cmp_params(pe_k, w1k)
    pe_vp, w1_vp = _pad_cmp_params(pe_v, w1v)
    kca, vc = _compress(_cmp_blocks(cmp4[:, :g]), _cmp_blocks(cmp4[:, g:]), pe_kp, pe_vp,
                        w1_kp, w1_vp, w2k.astype(BF16), w2v.astype(BF16))
    ncp = s // CMP_STRIDE
    kca = kca.reshape(b, g, ncp, LANES)
    vct = jnp.swapaxes(vc.reshape(b, g, ncp, HEAD), 2, 3)

    qnt = tr(qn)
    ocmp_t, bias_t = _cmp_select(qnt, kca, vct, _overlap_table(s))
    onehot = (jnp.arange(s)[:, None] // SEL_BLOCK == jnp.arange(HEAD)[None, :]).astype(BF16)
    pad_hot = jnp.concatenate([jnp.zeros_like(onehot), onehot] * g, axis=1)
    ksa = ksx.reshape(b, s, -1) + pad_hot[None]
    ynt = _sel_win(qnt, bias_t, ksa, kwx.reshape(b, s, -1), tr(vsw), ocmp_t, tr(gn))
    yn = jnp.swapaxes(ynt, 1, 2).reshape(t, -1)

    h1_tiles, c_tiles, s1t, s2t = _out_proj(
        yda, yn, gm, x2, p_da.astype(BF16), p_nsa.astype(BF16), w_o.astype(BF16),
        ffn_norm.reshape(1, d), wq.astype(BF16), k1.astype(BF16), k2.astype(BF16))
    idx_t, gate_t = _peer_topk(s1t, s2t)
    idx = idx_t.reshape(PEER_E, t).T
    gate = gate_t.reshape(PEER_E, t).T
    out = _peer_ffn(idx, gate, c_tiles, h1_tiles, out_norm.reshape(1, SUBLANES, LANES),
                    _pack_expert_rows(pu, pv))
    return out.reshape(b, s, d)


def kernel(x, attn_norm, w_in, da_lambda_q1, da_lambda_k1, da_lambda_q2, da_lambda_k2, da_subln,
           cmp_pe_k, cmp_pe_v, cmp_w1_k, cmp_w1_v, cmp_w2_k, cmp_w2_v, p_da, p_nsa, w_o,
           ffn_norm, peer_wq, peer_k1, peer_k2, peer_u, peer_v, final_norm):
    depth = attn_norm.shape[0]
    assert depth == 1, "the final norm is fused into the last layer's PEER kernel"
    h = x
    for l in range(depth):
        h = _layer(h, l, attn_norm[l], w_in[l], da_lambda_q1[l], da_lambda_k1[l], da_lambda_q2[l],
                   da_lambda_k2[l], da_subln[l], cmp_pe_k[l], cmp_pe_v[l], cmp_w1_k[l],
                   cmp_w1_v[l], cmp_w2_k[l], cmp_w2_v[l], p_da[l], p_nsa[l], w_o[l], ffn_norm[l],
                   peer_wq[l], peer_k1[l], peer_k2[l], peer_u[l], peer_v[l], final_norm)
    return h
```

```python
import functools
import math

import jax
import jax.numpy as jnp
from jax import lax
from jax.experimental import pallas as pl
from jax.experimental.pallas import tpu as pltpu

F32 = jnp.float32
BF16 = jnp.bfloat16
I32 = jnp.int32

RMS_EPS = 1e-6
ROPE_THETA = 500000.0
ROPE_HALF = 8
HEAD = 64
DA_HEADS = 4
NSA_GROUPS = 2
NSA_HG = 4
CMP_STRIDE = 16
CMP_BLOCK = 32
SEL_BLOCK = 64
SEL_TOPK = 16
WINDOW = 512
FORCE_BONUS = 1e4
NEG_BIG = -1e30
SEL_MASK_BIAS = -2.0 ** 100
ATTN_SCALE = HEAD ** -0.5 * math.log2(math.e)
PEER_HEADS = 8
PEER_NKEYS = 128
PEER_TOPK = 16
LANES = 128
SUBLANES = 8
VMEM_LIMIT = 56 * 1024 * 1024

NT_DIMS = (((1,), (1,)), ((), ()))


def _rmsnorm(x, g):
    return x * lax.rsqrt(jnp.mean(x * x, axis=-1, keepdims=True) + RMS_EPS) * g


def _sigmoid(z):
    return 1.0 / (1.0 + jnp.exp(-z))


def _gelu(z):
    return 0.5 * z * (1.0 + lax.erf(z * (2.0 ** -0.5)))


def _params(*sem):
    return pltpu.CompilerParams(dimension_semantics=sem, vmem_limit_bytes=VMEM_LIMIT)


def _sublane_transpose(v):
    sub = lax.broadcasted_iota(I32, (SUBLANES, LANES), 0)
    v = list(v)
    for dist in (4, 2, 1):
        low = (sub & dist) == 0
        nxt = list(v)
        for i in range(SUBLANES):
            if i & dist == 0:
                nxt[i] = jnp.where(low, v[i], pltpu.roll(v[i + dist], dist, 0))
                nxt[i + dist] = jnp.where(low, pltpu.roll(v[i], SUBLANES - dist, 0), v[i + dist])
        v = nxt
    return v


def _rows_to_tiles(x, tile_ref):
    for g in range(x.shape[0] // SUBLANES):
        rows = slice(g * SUBLANES, (g + 1) * SUBLANES)
        tiles = _sublane_transpose([x[rows, c * LANES:(c + 1) * LANES] for c in range(SUBLANES)])
        for e in range(SUBLANES):
            tile_ref[g * SUBLANES + e] = tiles[e]


def _tiles_to_rows(t, row_ref):
    for g in range(t.shape[0] // SUBLANES):
        chunks = _sublane_transpose([t[g * SUBLANES + e] for e in range(SUBLANES)])
        for c in range(SUBLANES):
            row_ref[g * SUBLANES:(g + 1) * SUBLANES, c * LANES:(c + 1) * LANES] = chunks[c]


_QDA0, _KDA0, _VDA0, _QN0 = 0, 512, 1024, 1536
_KC0, _VC0, _KS0, _KW0, _VSW0, _GM0, _GN0, _WCOLS = 2048, 2304, 2560, 2816, 3072, 3328, 5376, 5632


def _inproj_body(x_ref, g_ref, w_ref, rc_ref, rs1_ref, rs2_ref,
                 qda_ref, kda_ref, vda_ref, qn_ref, cmp_ref, ks_ref, kw_ref, vsw_ref, gm_ref, gn_ref):
    a = _rmsnorm(x_ref[...], g_ref[...]).astype(BF16)
    rc, rs1, rs2 = rc_ref[...], rs1_ref[...], rs2_ref[...]

    def rope(z):
        return (z * rc + pltpu.roll(z, ROPE_HALF, 1) * rs1
                + pltpu.roll(z, LANES - ROPE_HALF, 1) * rs2)

    def proj(c0):
        return jnp.dot(a, w_ref[:, c0:c0 + 256], preferred_element_type=F32)

    def rope2(z):
        return jnp.concatenate([rope(z[:, :LANES]), rope(z[:, LANES:])], axis=1)

    for c in range(2):
        qda_ref[:, c * 256:(c + 1) * 256] = (rope2(proj(_QDA0 + c * 256)) * ATTN_SCALE).astype(BF16)
        kda_ref[:, c * 256:(c + 1) * 256] = rope2(proj(_KDA0 + c * 256)).astype(BF16)
        vda_ref[:, c * 256:(c + 1) * 256] = proj(_VDA0 + c * 256).astype(BF16)
        qn_ref[:, c * 256:(c + 1) * 256] = (rope2(proj(_QN0 + c * 256)) * ATTN_SCALE).astype(BF16)
    cmp_ref[:, 0:256] = rope2(proj(_KC0)).astype(BF16)
    cmp_ref[:, 256:512] = proj(_VC0).astype(BF16)
    ks_ref[...] = rope2(proj(_KS0)).astype(BF16)
    kw_ref[...] = rope2(proj(_KW0)).astype(BF16)
    vsw_ref[...] = proj(_VSW0).astype(BF16)
    for c in range(8):
        gm_ref[:, c * 256:(c + 1) * 256] = _sigmoid(proj(_GM0 + c * 256)).astype(BF16)
    gn_ref[...] = _sigmoid(proj(_GN0))


def _pack_w_in(w):
    d = w.shape[0]
    zeros = lambda n: jnp.zeros((d, n), w.dtype)

    def spread_groups(c0):
        return [w[:, c0:c0 + HEAD], zeros(HEAD), w[:, c0 + HEAD:c0 + 2 * HEAD], zeros(HEAD)]

    kv_w = NSA_GROUPS * HEAD
    kv0 = 4 * 512
    kc, vc, ks, vs, kw, vw = (kv0 + kv_w * i for i in range(6))
    gn0 = kv0 + 6 * kv_w
    per_group = NSA_HG * 3
    gm0 = gn0 + NSA_GROUPS * per_group
    assert w.shape[1] == gm0 + 2 * d
    gn = w[:, gn0:gm0]
    cols = ([w[:, :kv0]] + spread_groups(kc) + spread_groups(vc) + spread_groups(ks)
            + spread_groups(kw) + [w[:, vs:vs + kv_w], w[:, vw:vw + kv_w], w[:, gm0:],
                                   gn[:, :per_group], zeros(LANES - per_group),
                                   gn[:, per_group:], zeros(LANES - per_group)])
    packed = jnp.concatenate(cols, axis=1).astype(BF16)
    assert packed.shape[1] == _WCOLS
    return packed


def _rope_tables(seq):
    inv = jnp.power(ROPE_THETA, -jnp.arange(ROPE_HALF, dtype=F32) * 2.0 / (2 * ROPE_HALF))
    ang = jnp.arange(seq, dtype=F32)[:, None] * inv[None, :]
    cos, sin = jnp.cos(ang), jnp.sin(ang)
    one = jnp.ones((seq, HEAD - 2 * ROPE_HALF), F32)
    zero8 = jnp.zeros((seq, ROPE_HALF), F32)
    zero48 = jnp.zeros_like(one)
    rc = jnp.concatenate([cos, cos, one], axis=1)
    rs1 = jnp.concatenate([zero8, sin, zero48], axis=1)
    rs2 = jnp.concatenate([-sin, zero8, zero48], axis=1)
    return tuple(jnp.concatenate([t, t], axis=1) for t in (rc, rs1, rs2))


def _in_proj(x2, norm_g, w_packed, rope_tabs, seq):
    t, d = x2.shape
    tm = 512
    nseq = seq // tm
    row = lambda i: (i, 0)
    fixed = lambda i: (0, 0)
    out_shapes = [
        jax.ShapeDtypeStruct((t, 512), BF16), jax.ShapeDtypeStruct((t, 512), BF16),
        jax.ShapeDtypeStruct((t, 512), BF16), jax.ShapeDtypeStruct((t, 512), BF16),
        jax.ShapeDtypeStruct((t, 512), BF16), jax.ShapeDtypeStruct((t, 256), BF16),
        jax.ShapeDtypeStruct((t, 256), BF16), jax.ShapeDtypeStruct((t, 256), BF16),
        jax.ShapeDtypeStruct((t, 2048), BF16), jax.ShapeDtypeStruct((t, 256), F32)]
    rope_spec = pl.BlockSpec((tm, LANES), lambda i: (i % nseq, 0))
    return pl.pallas_call(
        _inproj_body,
        grid=(t // tm,),
        in_specs=[pl.BlockSpec((tm, d), row), pl.BlockSpec((1, d), fixed),
                  pl.BlockSpec((d, _WCOLS), fixed), rope_spec, rope_spec, rope_spec],
        out_specs=[pl.BlockSpec((tm, s.shape[1]), row) for s in out_shapes],
        out_shape=out_shapes,
        compiler_params=_params("parallel"),
        name="in_proj",
    )(x2, norm_g, w_packed, *rope_tabs)


def _softmax_step(s, vt, m_ref, l_ref, acc_ref):
    m_prev = m_ref[...]
    m_new = jnp.maximum(m_prev, jnp.max(s, axis=0, keepdims=True))
    alpha = jnp.exp2(m_prev - m_new)
    p = jnp.exp2(s - m_new)
    l_ref[...] = alpha * l_ref[...] + jnp.sum(p, axis=0, keepdims=True)
    acc_ref[...] = alpha * acc_ref[...] + jnp.dot(vt, p.astype(BF16), preferred_element_type=F32)
    m_ref[...] = m_new


def _softmax_reset(m_ref, l_ref, acc_ref):
    m_ref[...] = jnp.full(m_ref.shape, NEG_BIG, F32)
    l_ref[...] = jnp.zeros(l_ref.shape, F32)
    acc_ref[...] = jnp.zeros(acc_ref.shape, F32)


def _attend_tiles(n_full, scores, values, mask_last, sa_ref, sb_ref, m_ref, l_ref, acc_ref):
    step = lambda s, t: _softmax_step(s, values(t), m_ref, l_ref, acc_ref)
    sa_ref[...] = scores(0)

    def two_tiles(i, carry):
        t = 2 * i
        sb_ref[...] = scores(t + 1)
        step(sa_ref[...], t)
        sa_ref[...] = scores(t + 2)
        step(sb_ref[...], t + 1)
        return carry

    lax.fori_loop(0, n_full // 2, two_tiles, 0)
    odd = (n_full & 1) == 1

    @pl.when(odd)
    def _():
        sb_ref[...] = scores(n_full)
        step(sa_ref[...], n_full - 1)
        step(mask_last(sb_ref[...]), n_full)

    @pl.when(jnp.logical_not(odd))
    def _():
        step(mask_last(sa_ref[...]), n_full)


def _key_tile(ref, kt, bk):
    return ref[(0,) * (len(ref.shape) - 2) + (pl.ds(pl.multiple_of(kt * bk, bk), bk), slice(None))]


def _value_tile(ref, kt, bk):
    return ref[(0,) * (len(ref.shape) - 2) + (slice(None), pl.ds(pl.multiple_of(kt * bk, bk), bk))]


def _diffattn_body(lambda_init, bq, q_ref, k_ref, v_ref, lq1_ref, lk1_ref, lq2_ref, lk2_ref,
                   sub_ref, y_ref, qbd_ref, m_ref, l_ref, acc_ref, sa_ref, sb_ref):
    qi = pl.program_id(2)
    bk = bq
    qt = q_ref[0]
    sub = lax.broadcasted_iota(I32, qt.shape, 0)
    zero = jnp.zeros_like(qt)
    qbd_ref[:, 0:bq] = jnp.where(sub < HEAD, qt, zero)
    qbd_ref[:, bq:2 * bq] = jnp.where(sub >= HEAD, qt, zero)
    _softmax_reset(m_ref, l_ref, acc_ref)

    def scores(kt):
        return jnp.dot(_key_tile(k_ref, kt, bk), qbd_ref[...], preferred_element_type=F32)

    def causal(s):
        r = lax.broadcasted_iota(I32, (bk, 2 * bq), 0)
        c = lax.broadcasted_iota(I32, (bk, 2 * bq), 1) & (bq - 1)
        return jnp.where(r <= c, s, NEG_BIG)

    _attend_tiles(qi, scores, lambda kt: _value_tile(v_ref, kt, bk), causal,
                  sa_ref, sb_ref, m_ref, l_ref, acc_ref)

    o = acc_ref[...] / l_ref[...]
    lam = (jnp.exp(jnp.sum(lq1_ref[...] * lk1_ref[...], axis=1, keepdims=True))
           - jnp.exp(jnp.sum(lq2_ref[...] * lk2_ref[...], axis=1, keepdims=True)) + lambda_init)
    d = o[:, 0:bq] - lam * o[:, bq:2 * bq]
    ms = jnp.mean(d * d, axis=0, keepdims=True)
    y = d * lax.rsqrt(ms + RMS_EPS) * sub_ref[...] * (1.0 - lambda_init)
    y_ref[0] = y.astype(BF16)


def _diff_attn(qdat, kda, vdat, lq1, lk1, lq2, lk2, subln_col, lambda_init):
    b, s, _ = kda.shape
    bq = 512
    vec = lambda n: pl.BlockSpec((1, n), lambda bi, h, qi: (0, 0))
    dv = 2 * HEAD
    qtile = pl.BlockSpec((1, dv, bq), lambda bi, h, qi: (bi, h, qi))
    return pl.pallas_call(
        functools.partial(_diffattn_body, lambda_init, bq),
        grid=(b, DA_HEADS, s // bq),
        in_specs=[qtile,
                  pl.BlockSpec((1, s, LANES), lambda bi, h, qi: (bi, 0, h)),
                  pl.BlockSpec((1, dv, s), lambda bi, h, qi: (bi, h, 0)),
                  vec(HEAD), vec(HEAD), vec(HEAD), vec(HEAD),
                  pl.BlockSpec((dv, 1), lambda bi, h, qi: (0, 0))],
        out_specs=qtile,
        out_shape=jax.ShapeDtypeStruct((b, DA_HEADS * dv, s), BF16),
        scratch_shapes=[pltpu.VMEM((LANES, 2 * bq), BF16), pltpu.VMEM((1, 2 * bq), F32),
                        pltpu.VMEM((1, 2 * bq), F32), pltpu.VMEM((dv, 2 * bq), F32),
                        pltpu.VMEM((bq, 2 * bq), F32), pltpu.VMEM((bq, 2 * bq), F32)],
        compiler_params=_params("parallel", "parallel", "parallel"),
        name="diff_attn",
    )(qdat, kda, vdat, lq1, lk1, lq2, lk2, subln_col)


def _compress_body(xk_ref, xv_ref, pek_ref, pev_ref, w1k_ref, w1v_ref, w2k_ref, w2v_ref,
                   kc_ref, vc_ref):
    def mlp(x_ref, pe_ref, w1_ref, w2_ref):
        blocks = (x_ref[0].astype(F32) + pe_ref[...]).astype(BF16)
        hid = _gelu(jnp.dot(blocks, w1_ref[...], preferred_element_type=F32))
        return jnp.dot(hid.astype(BF16), w2_ref[...], preferred_element_type=F32)

    kc = mlp(xk_ref, pek_ref, w1k_ref, w2k_ref)
    kc_ref[0] = jnp.concatenate([kc, jnp.zeros_like(kc)], axis=1).astype(BF16)
    vc_ref[0] = mlp(xv_ref, pev_ref, w1v_ref, w2v_ref).astype(BF16)


def _compress(xk, xv, pe_k, pe_v, w1k, w1v, w2k, w2v):
    n, ncp, width = xk.shape
    blk = pl.BlockSpec((1, ncp, width), lambda i: (i, 0, 0))
    fixed = lambda shape: pl.BlockSpec(shape, lambda i: (0, 0))
    return pl.pallas_call(
        _compress_body,
        grid=(n,),
        in_specs=[blk, blk, fixed((1, width)), fixed((1, width)), fixed((width, HEAD)),
                  fixed((width, HEAD)), fixed((HEAD, HEAD)), fixed((HEAD, HEAD))],
        out_specs=[pl.BlockSpec((1, ncp, LANES), lambda i: (i, 0, 0)),
                   pl.BlockSpec((1, ncp, HEAD), lambda i: (i, 0, 0))],
        out_shape=[jax.ShapeDtypeStruct((n, ncp, LANES), BF16),
                   jax.ShapeDtypeStruct((n, ncp, HEAD), BF16)],
        compiler_params=_params("parallel"),
        name="compress",
    )(xk, xv, pe_k, pe_v, w1k, w1v, w2k, w2v)


def _heads_on_lanes(qt, bq):
    return jnp.concatenate([qt[h * HEAD:(h + 1) * HEAD, :] for h in range(NSA_HG)], axis=1)


def _cmpsel_body(bq, q_ref, kc_ref, vc_ref, ovl_ref, ocmp_ref, bias_ref):
    qi = pl.program_id(2)
    ncp = kc_ref.shape[2]
    rows = NSA_HG * bq
    q2 = _heads_on_lanes(q_ref[0], bq)
    qz = jnp.concatenate([q2, jnp.zeros_like(q2)], axis=0)
    s = jnp.dot(kc_ref[0, 0], qz, preferred_element_type=F32)
    n = lax.broadcasted_iota(I32, (ncp, rows), 0)
    qpos = qi * bq + (lax.broadcasted_iota(I32, (ncp, rows), 1) & (bq - 1))
    cmask = n * CMP_STRIDE + (CMP_BLOCK - 1) <= qpos
    s = jnp.where(cmask, s, NEG_BIG)
    e = jnp.exp2(s - jnp.max(s, axis=0, keepdims=True))
    p = jnp.where(cmask, e / jnp.sum(e, axis=0, keepdims=True), 0.0)
    o = jnp.dot(vc_ref[0, 0], p.astype(BF16), preferred_element_type=F32)
    for h in range(NSA_HG):
        ocmp_ref[0, 0, h * HEAD:(h + 1) * HEAD, :] = o[:, h * bq:(h + 1) * bq]

    psum = p[:, 0:bq] + p[:, bq:2 * bq] + p[:, 2 * bq:3 * bq] + p[:, 3 * bq:4 * bq]
    imp = jnp.dot(ovl_ref[...], psum, preferred_element_type=F32)
    blk = lax.broadcasted_iota(I32, (SEL_BLOCK, bq), 0)
    pos = qi * bq + lax.broadcasted_iota(I32, (SEL_BLOCK, bq), 1)
    cur = lax.shift_right_logical(pos, SEL_BLOCK.bit_length() - 1)
    valid = blk <= cur
    forced = (blk == 0) | (blk == cur) | (blk == cur - 1)
    score = jnp.where(valid, imp + jnp.where(forced, FORCE_BONUS, 0.0), -jnp.inf)
    rank = jnp.zeros((SEL_BLOCK, bq), I32)
    for i in range(SEL_BLOCK):
        other = score[i:i + 1, :]
        beats = (other > score) | ((other == score) & (blk > i))
        rank = rank + beats.astype(I32)
    keep = valid & (rank < SEL_TOPK)
    bias_ref[0, 0] = jnp.where(keep, 0.0, SEL_MASK_BIAS).astype(BF16)


def _cmp_select(qnt, kca, vct, overlap_t):
    b, g, ncp, _ = kca.shape
    s = qnt.shape[2]
    bq = 128
    dq = NSA_HG * HEAD
    return pl.pallas_call(
        functools.partial(_cmpsel_body, bq),
        grid=(b, g, s // bq),
        in_specs=[pl.BlockSpec((1, dq, bq), lambda bi, gi, qi: (bi, gi, qi)),
                  pl.BlockSpec((1, 1, ncp, LANES), lambda bi, gi, qi: (bi, gi, 0, 0)),
                  pl.BlockSpec((1, 1, HEAD, ncp), lambda bi, gi, qi: (bi, gi, 0, 0)),
                  pl.BlockSpec((SEL_BLOCK, ncp), lambda bi, gi, qi: (0, 0))],
        out_specs=[pl.BlockSpec((1, 1, dq, bq), lambda bi, gi, qi: (bi, gi, 0, qi)),
                   pl.BlockSpec((1, 1, SEL_BLOCK, bq), lambda bi, gi, qi: (bi, gi, 0, qi))],
        out_shape=[jax.ShapeDtypeStruct((b, g, dq, s), F32),
                   jax.ShapeDtypeStruct((b, g, SEL_BLOCK, s), BF16)],
        compiler_params=_params("parallel", "parallel", "parallel"),
        name="cmp_select",
    )(qnt, kca, vct, overlap_t)


def _selwin_body(bq, bks, q_ref, bias_ref, ks_ref, vs_ref, kw_ref, vw_ref, ocmp_ref, gate_ref,
                 y_ref, m_ref, l_ref, acc_ref, qa_ref, sa_ref, sb_ref):
    qi = pl.program_id(2)
    rows = NSA_HG * bq
    q2 = _heads_on_lanes(q_ref[0], bq)
    bias = bias_ref[0, 0]
    qa = jnp.concatenate([q2, jnp.concatenate([bias] * NSA_HG, axis=1)], axis=0)
    qw = jnp.concatenate([q2, jnp.zeros_like(q2)], axis=0)
    r = lax.broadcasted_iota(I32, (bq, rows), 0)
    c = lax.broadcasted_iota(I32, (bq, rows), 1) & (bq - 1)

    _softmax_reset(m_ref, l_ref, acc_ref)

    qa_ref[...] = qa
    last = (qi * bq) // bks

    def causal(s):
        kpos = last * bks + lax.broadcasted_iota(I32, (bks, rows), 0)
        qpos = qi * bq + (lax.broadcasted_iota(I32, (bks, rows), 1) & (bq - 1))
        return jnp.where(kpos <= qpos, s, NEG_BIG)

    _attend_tiles(last,
                  lambda kt: jnp.dot(_key_tile(ks_ref, kt, bks), qa_ref[...],
                                     preferred_element_type=F32),
                  lambda kt: _value_tile(vs_ref, kt, bks), causal,
                  sa_ref, sb_ref, m_ref, l_ref, acc_ref)
    o_sel = acc_ref[...] / l_ref[...]

    _softmax_reset(m_ref, l_ref, acc_ref)
    backs = list(range(WINDOW // bq, -1, -1))
    tiles = [jnp.maximum(qi - back, 0) for back in backs]
    raw = [jnp.dot(_key_tile(kw_ref, kt, bq), qw, preferred_element_type=F32) for kt in tiles]
    for back, kt, s in zip(backs, tiles, raw):
        if back >= 1 and (back + 1) * bq <= WINDOW:
            inside = qi >= back
        else:
            dist = c + back * bq - r
            inside = (dist >= 0) & (dist < WINDOW) & (qi >= back)
        _softmax_step(jnp.where(inside, s, NEG_BIG), _value_tile(vw_ref, kt, bq),
                      m_ref, l_ref, acc_ref)

    o_win = acc_ref[...] / l_ref[...]
    gate = gate_ref[0]
    for h in range(NSA_HG):
        sl = slice(h * bq, (h + 1) * bq)
        y = (gate[3 * h:3 * h + 1, :] * ocmp_ref[0, 0, h * HEAD:(h + 1) * HEAD, :]
             + gate[3 * h + 1:3 * h + 2, :] * o_sel[:, sl]
             + gate[3 * h + 2:3 * h + 3, :] * o_win[:, sl])
        y_ref[0, h * HEAD:(h + 1) * HEAD, :] = y.astype(BF16)


def _sel_win(qnt, bias_t, ksa, kwa, vswt, ocmp_t, gnt):
    b, s, _ = ksa.shape
    g = NSA_GROUPS
    bq, bks = 256, 512
    dq = NSA_HG * HEAD
    rows = NSA_HG * bq
    keys = pl.BlockSpec((1, s, LANES), lambda bi, gi, qi: (bi, 0, gi))
    vals = lambda first: pl.BlockSpec((1, HEAD, s), lambda bi, gi, qi: (bi, first + gi, 0))
    qtile = pl.BlockSpec((1, dq, bq), lambda bi, gi, qi: (bi, gi, qi))
    return pl.pallas_call(
        functools.partial(_selwin_body, bq, bks),
        grid=(b, g, s // bq),
        in_specs=[qtile,
                  pl.BlockSpec((1, 1, SEL_BLOCK, bq), lambda bi, gi, qi: (bi, gi, 0, qi)),
                  keys, vals(0), keys, vals(g),
                  pl.BlockSpec((1, 1, dq, bq), lambda bi, gi, qi: (bi, gi, 0, qi)),
                  pl.BlockSpec((1, LANES, bq), lambda bi, gi, qi: (bi, gi, qi))],
        out_specs=qtile,
        out_shape=jax.ShapeDtypeStruct((b, g * dq, s), BF16),
        scratch_shapes=[pltpu.VMEM((1, rows), F32), pltpu.VMEM((1, rows), F32),
                        pltpu.VMEM((HEAD, rows), F32), pltpu.VMEM((LANES, rows), BF16),
                        pltpu.VMEM((bks, rows), F32), pltpu.VMEM((bks, rows), F32)],
        compiler_params=_params("parallel", "parallel", "parallel"),
        name="sel_win",
    )(qnt, bias_t, ksa, vswt, kwa, vswt, ocmp_t, gnt)


def _outproj_body(yda_ref, yn_ref, gm_ref, x_ref, pda_ref, pnsa_ref, wo_ref, fg_ref, wq_ref,
                  k1_ref, k2_ref, h1_ref, c_ref, s1_ref, s2_ref):
    d = x_ref.shape[1]
    a = jnp.dot(yda_ref[...], pda_ref[...], preferred_element_type=F32)
    bn = jnp.dot(yn_ref[...], pnsa_ref[...], preferred_element_type=F32)
    merged = gm_ref[:, :d].astype(F32) * a + gm_ref[:, d:].astype(F32) * bn
    h1 = x_ref[...] + jnp.dot(merged.astype(BF16), wo_ref[...], preferred_element_type=F32)
    c = _rmsnorm(h1, fg_ref[...])
    _rows_to_tiles(h1, h1_ref)
    _rows_to_tiles(c, c_ref)
    cb = c.astype(BF16)
    for h in range(PEER_HEADS):
        qh = jnp.dot(cb, wq_ref[:, h * 256:(h + 1) * 256], preferred_element_type=F32).astype(BF16)
        s1_ref[h] = lax.dot_general(k1_ref[...], qh[:, :LANES], NT_DIMS, preferred_element_type=F32)
        s2_ref[h] = lax.dot_general(k2_ref[...], qh[:, LANES:], NT_DIMS, preferred_element_type=F32)


def _out_proj(yda, yn, gm, x2, pda, pnsa, wo, ffn_g, wq, k1, k2):
    t, d = x2.shape
    tm = 512
    row = lambda w: pl.BlockSpec((tm, w), lambda i: (i, 0))
    fixed = lambda a: pl.BlockSpec(a.shape, lambda i: (0, 0))
    sspec = pl.BlockSpec((PEER_HEADS, PEER_NKEYS, tm), lambda i: (0, 0, i))
    assert d == SUBLANES * LANES
    tiles = pl.BlockSpec((tm, SUBLANES, LANES), lambda i: (i, 0, 0))
    return pl.pallas_call(
        _outproj_body,
        grid=(t // tm,),
        in_specs=[row(512), row(512), row(2 * d), row(d), fixed(pda), fixed(pnsa), fixed(wo),
                  fixed(ffn_g), fixed(wq), fixed(k1), fixed(k2)],
        out_specs=[tiles, tiles, sspec, sspec],
        out_shape=[jax.ShapeDtypeStruct((t, SUBLANES, LANES), F32),
                   jax.ShapeDtypeStruct((t, SUBLANES, LANES), F32),
                   jax.ShapeDtypeStruct((PEER_HEADS, PEER_NKEYS, t), F32),
                   jax.ShapeDtypeStruct((PEER_HEADS, PEER_NKEYS, t), F32)],
        compiler_params=_params("parallel"),
        name="out_proj",
    )(yda, yn, gm, x2, pda, pnsa, wo, ffn_g, wq, k1, k2)


def _batcher_pairs(n):
    pairs = []

    def merge(lo, hi, r):
        step = r * 2
        if step < hi - lo:
            merge(lo, hi, step)
            merge(lo + r, hi, step)
            pairs.extend((i, i + r) for i in range(lo + r, hi - r, step))
        else:
            pairs.append((lo, lo + r))

    def sort(lo, hi):
        if hi - lo >= 1:
            mid = lo + (hi - lo) // 2
            sort(lo, mid)
            sort(mid + 1, hi)
            merge(lo, hi, 1)

    sort(0, n - 1)
    return pairs


_NET16 = _batcher_pairs(PEER_TOPK)


def _cmpx(a, b):
    c = (a[0] > b[0]) | ((a[0] == b[0]) & (a[1] < b[1]))
    return ((jnp.where(c, a[0], b[0]), jnp.where(c, a[1], b[1])),
            (jnp.where(c, b[0], a[0]), jnp.where(c, b[1], a[1])))


def _sort_lists(lists, n_real):
    lists = list(lists)
    for i, j in _NET16:
        if j < n_real:
            lists[i], lists[j] = _cmpx(lists[i], lists[j])
    return lists


def _merge_top(a, b):
    k = PEER_TOPK
    lists = [_cmpx(a[i], b[k - 1 - i])[0] for i in range(k)]
    step = k // 2
    while step >= 1:
        for i in range(k):
            if i & step == 0:
                lists[i], lists[i + step] = _cmpx(lists[i], lists[i + step])
        step //= 2
    return lists


def _top16(pairs):
    k = PEER_TOPK
    filler = (jnp.full(pairs[0][0].shape, -jnp.inf, F32), jnp.zeros(pairs[0][1].shape, I32))
    groups = []
    for g0 in range(0, len(pairs), k):
        chunk = list(pairs[g0:g0 + k])
        groups.append(_sort_lists(chunk + [filler] * (k - len(chunk)), len(chunk)))
    while len(groups) > 1:
        groups = [_merge_top(groups[i], groups[i + 1]) if i + 1 < len(groups) else groups[i]
                  for i in range(0, len(groups), 2)]
    return groups[0]


def _product_key_select(s1, s2):
    k = PEER_TOPK
    n_exp = PEER_NKEYS * PEER_NKEYS
    ids = lambda n: [jnp.full(s1[0].shape, i, I32) for i in range(n)]
    l1 = _top16(list(zip(s1, ids(len(s1)))))
    l2 = _top16(list(zip(s2, ids(len(s2)))))
    pair = lambda a, b: (l1[a][0] + l2[b][0],
                         (a * k + b) * n_exp + l1[a][1] * PEER_NKEYS + l2[b][1])
    first_row = [pair(0, b) for b in range(k)]
    rest = [pair(a, b) for a in range(1, k) for b in range(k // (a + 1))]
    best = _merge_top(first_row, _top16(rest))
    return [(v, p & (n_exp - 1)) for v, p in best]


def _peertopk_body(tt, s1_ref, s2_ref, idx_ref, gate_ref):
    nblk = tt // LANES
    assert nblk == SUBLANES

    def keys_major(ref, h):
        tiles = []
        for kg in range(PEER_NKEYS // SUBLANES):
            rows = slice(kg * SUBLANES, (kg + 1) * SUBLANES)
            tiles += _sublane_transpose([ref[h, rows, b * LANES:(b + 1) * LANES]
                                         for b in range(nblk)])
        return tiles

    def store(ref, h, tiles):
        for g in range(PEER_TOPK // SUBLANES):
            blocks = _sublane_transpose(tiles[g * SUBLANES:(g + 1) * SUBLANES])
            for b in range(nblk):
                ref[h, g * SUBLANES:(g + 1) * SUBLANES, b * LANES:(b + 1) * LANES] = blocks[b]

    def head(h, carry):
        best = _product_key_select(keys_major(s1_ref, h), keys_major(s2_ref, h))
        ex = [jnp.exp(v - best[0][0]) for v, _ in best]
        z = ex[0]
        for e in ex[1:]:
            z = z + e
        store(gate_ref, h, [e / z for e in ex])
        store(idx_ref, h, [ix for _, ix in best])
        return carry

    lax.fori_loop(0, PEER_HEADS, head, 0)


def _peer_topk(s1t, s2t):
    _, _, t = s1t.shape
    tt = SUBLANES * LANES
    spec_in = pl.BlockSpec((PEER_HEADS, PEER_NKEYS, tt), lambda i: (0, 0, i))
    spec_out = pl.BlockSpec((PEER_HEADS, PEER_TOPK, tt), lambda i: (0, 0, i))
    return pl.pallas_call(
        functools.partial(_peertopk_body, tt),
        grid=(t // tt,),
        in_specs=[spec_in, spec_in],
        out_specs=[spec_out, spec_out],
        out_shape=[jax.ShapeDtypeStruct((PEER_HEADS, PEER_TOPK, t), I32),
                   jax.ShapeDtypeStruct((PEER_HEADS, PEER_TOPK, t), F32)],
        compiler_params=_params("parallel"),
        name="peer_topk",
    )(s1t, s2t)


PEER_E = PEER_HEADS * PEER_TOPK
PEER_RING = 4
PEER_MID_ROWS = 48


def _pack_body(u_ref, v_ref, out_ref):
    bf16_bits = lambda a: lax.bitcast_convert_type(a.astype(BF16).astype(F32), I32)
    words = bf16_bits(u_ref[...]) | lax.shift_right_logical(bf16_bits(v_ref[...]), jnp.int32(16))
    _rows_to_tiles(words, out_ref)


def _pack_expert_rows(pu, pv):
    n, d = pu.shape
    assert d == SUBLANES * LANES
    te = 256
    blk = pl.BlockSpec((te, d), lambda i: (i, 0))
    return pl.pallas_call(
        _pack_body,
        grid=(n // te,),
        in_specs=[blk, blk],
        out_specs=pl.BlockSpec((te, SUBLANES, LANES), lambda i: (i, 0, 0)),
        out_shape=jax.ShapeDtypeStruct((n, SUBLANES, LANES), I32),
        compiler_params=_params("parallel"),
        name="pack_experts",
    )(pu, pv)


def _word_hi(w):
    return lax.bitcast_convert_type(w & jnp.int32(-65536), F32)


def _word_lo(w):
    return lax.bitcast_convert_type(lax.shift_left(w, jnp.int32(16)), F32)


def _sublane_sums(a, sub):
    for dist in (4, 2, 1):
        low = (sub & dist) == 0
        half = len(a) // 2
        a = [jnp.where(low, a[i], pltpu.roll(a[i + half], dist, 0))
             + jnp.where(low, pltpu.roll(a[i], SUBLANES - dist, 0), a[i + half])
             for i in range(half)]
    return a[0]


def _peer_pair_math(expert_u, expert_v, x8, gates, store, issue_some):
    sub = lax.broadcasted_iota(I32, (SUBLANES, LANES), 0)
    eye = (lax.broadcasted_iota(I32, (PEER_E, LANES), 0)
           == lax.broadcasted_iota(I32, (PEER_E, LANES), 1))
    ones_rows = jnp.ones((SUBLANES, LANES), BF16)
    ones_sq = jnp.ones((LANES, LANES), BF16)
    ngroup = PEER_E // SUBLANES

    def hidden(a):
        groups = []
        for g in range(ngroup):
            prods = [expert_u(a, g * SUBLANES + r) * x8[a] for r in range(SUBLANES)]
            groups.append(_sublane_sums(prods, sub))
            issue_some(3 - g % 2)
        return jnp.concatenate(groups, axis=0)

    def expert_weights(a, q):
        q_hi = q.astype(BF16)
        q_lo = (q - q_hi.astype(F32)).astype(BF16)
        hid = (lax.dot_general(ones_rows, q_hi, NT_DIMS, preferred_element_type=F32)
               + lax.dot_general(ones_rows, q_lo, NT_DIMS, preferred_element_type=F32))
        issue_some(PEER_MID_ROWS // 2)
        w = _gelu(hid[0:1]) * gates[a]
        wd = jnp.where(eye, jnp.broadcast_to(w, (PEER_E, LANES)), 0.0).astype(BF16)
        wcol = jnp.dot(wd, ones_sq, preferred_element_type=F32)
        issue_some(PEER_MID_ROWS // 2)
        return wcol

    def combine(a, wcol):
        out = jnp.zeros((SUBLANES, LANES), F32)
        for g in range(ngroup):
            for r in range(SUBLANES):
                j = g * SUBLANES + r
                out = out + wcol[j:j + 1, :] * expert_v(a, j)
            issue_some(3 - g % 2)
        store(a, out)

    wcol0 = expert_weights(0, hidden(0))
    wcol1 = expert_weights(1, hidden(1))
    combine(0, wcol0)
    combine(1, wcol1)


def _peer_finish(h1_ref, acc_ref, fg_ref, out_ref):
    hsum = h1_ref[...] + acc_ref[...]
    ms = jnp.mean(hsum * hsum, axis=(1, 2), keepdims=True)
    _tiles_to_rows(hsum * lax.rsqrt(ms + RMS_EPS) * fg_ref[...], out_ref)


def _peerffn_body(tb, idx_ref, gate_ref, c_ref, h1_ref, fg_ref, uv_ref, out_ref, buf_ref,
                  acc_ref, sem_ref):
    npairs = tb // 2
    ahead = PEER_RING - 1

    def row_copy(t, j, slot):
        return pltpu.make_async_copy(uv_ref.at[idx_ref[t, j]], buf_ref.at[slot, j],
                                     sem_ref.at[slot])

    def wait_all(slot):
        pltpu.make_async_copy(uv_ref.at[pl.ds(0, PEER_E)], buf_ref.at[slot],
                              sem_ref.at[slot]).wait()

    def ring(phase):
        base = 2 * (phase % PEER_RING)
        return (base, base + 1)

    def pair(i, phase, prefetch):
        toks = (2 * i, 2 * i + 1)
        slots = ring(phase)
        nslots = ring(phase + ahead)
        todo = [(a, j) for a in range(2) for j in range(PEER_E)]

        def issue_some(n):
            for a, j in todo[:n]:
                if prefetch:
                    row_copy(toks[a] + 2 * ahead, j, nslots[a]).start(priority=j % 2)
            del todo[:n]

        wait_all(slots[0])
        wait_all(slots[1])

        def store(a, out):
            acc_ref[toks[a]] = out

        _peer_pair_math(
            lambda a, j: _word_hi(buf_ref[slots[a], j]),
            lambda a, j: _word_lo(buf_ref[slots[a], j]),
            [c_ref[t] for t in toks], [gate_ref[pl.ds(t, 1), :] for t in toks], store, issue_some)
        assert not todo, "every prefetch row DMA must be issued exactly once"

    assert npairs % PEER_RING == 0
    for i in range(ahead):
        for a, slot in enumerate(ring(i)):
            for j in range(PEER_E):
                row_copy(2 * i + a, j, slot).start(priority=j % 2)

    def body(g, carry):
        for phase in range(PEER_RING):
            pair(PEER_RING * g + phase, phase, True)
        return carry

    ngroups = npairs // PEER_RING
    lax.fori_loop(0, ngroups - 1, body, 0)
    for phase in range(PEER_RING):
        i = PEER_RING * (ngroups - 1) + phase
        pair(i, phase, i + ahead < npairs)
    _peer_finish(h1_ref, acc_ref, fg_ref, out_ref)


def _peer_ffn(idx, gate, c3, h13, final_g3, uv_tiles):
    t = h13.shape[0]
    tb = 256
    row3 = pl.BlockSpec((tb, SUBLANES, LANES), lambda i: (i, 0, 0))
    return pl.pallas_call(
        functools.partial(_peerffn_body, tb),
        grid=(t // tb,),
        in_specs=[pl.BlockSpec((tb, PEER_E), lambda i: (i, 0), memory_space=pltpu.SMEM),
                  pl.BlockSpec((tb, PEER_E), lambda i: (i, 0)), row3, row3,
                  pl.BlockSpec((1, SUBLANES, LANES), lambda i: (0, 0, 0)),
                  pl.BlockSpec(memory_space=pl.ANY)],
        out_specs=pl.BlockSpec((tb, SUBLANES * LANES), lambda i: (i, 0)),
        out_shape=jax.ShapeDtypeStruct((t, SUBLANES * LANES), F32),
        scratch_shapes=[pltpu.VMEM((2 * PEER_RING, PEER_E, SUBLANES, LANES), I32),
                        pltpu.VMEM((tb, SUBLANES, LANES), F32),
                        pltpu.SemaphoreType.DMA((2 * PEER_RING,))],
        compiler_params=_params("arbitrary"),
        name="peer_ffn",
    )(idx, gate, c3, h13, final_g3, uv_tiles)


def _overlap_table(seq):
    ci = jnp.arange(seq // CMP_STRIDE)[None, :] * CMP_STRIDE
    sj = jnp.arange(SEL_BLOCK)[:, None] * SEL_BLOCK
    return ((ci < sj + SEL_BLOCK) & (ci + CMP_BLOCK > sj)).astype(F32)


def _cmp_blocks(kv):
    b, g, s, dh = kv.shape
    r = kv.reshape(b * g, s // CMP_STRIDE, CMP_STRIDE * dh)
    return jnp.concatenate([r, jnp.roll(r, -1, axis=1)], axis=-1)


def _pad_cmp_params(pe, w1):
    pe_p = jnp.pad(pe, ((0, 0), (0, LANES - HEAD))).reshape(1, -1)
    w1_p = jnp.pad(w1.reshape(CMP_BLOCK, HEAD, -1), ((0, 0), (0, LANES - HEAD), (0, 0)))
    return pe_p, w1_p.reshape(CMP_BLOCK * LANES, -1).astype(BF16)


def _layer(h, lidx, attn_norm, w_in, lq1, lk1, lq2, lk2, subln, pe_k, pe_v, w1k, w1v, w2k, w2v,
           p_da, p_nsa, w_o, ffn_norm, wq, k1, k2, pu, pv, out_norm):
    b, s, d = h.shape
    t = b * s
    g, hg = NSA_GROUPS, NSA_HG
    lambda_init = 0.8 - 0.6 * math.exp(-0.3 * lidx)
    x2 = h.reshape(t, d)

    qda, kda, vda, qn, cmp_in, ksx, kwx, vsw, gm, gn = _in_proj(
        x2, attn_norm.reshape(1, d), _pack_w_in(w_in), _rope_tables(s), s)
    tr = lambda a2: jnp.swapaxes(a2.reshape(b, s, -1), 1, 2)
    ydat = _diff_attn(tr(qda), kda.reshape(b, s, -1), tr(vda),
                      lq1.reshape(1, -1), lk1.reshape(1, -1), lq2.reshape(1, -1),
                      lk2.reshape(1, -1), subln.reshape(-1, 1), lambda_init)
    yda = jnp.swapaxes(ydat, 1, 2).reshape(t, -1)

    assert s // SEL_BLOCK <= SEL_BLOCK, "selection bias rows hold at most 64 blocks"
    cmp4 = jnp.swapaxes(cmp_in.reshape(b, s, 2 * g, LANES), 1, 2)
    pe_kp, w1_kp = _pad_cmp_params(pe_k, w1k)
    pe_vp, w1_vp = _pad_cmp_params(pe_v, w1v)
    kca, vc = _compress(_cmp_blocks(cmp4[:, :g]), _cmp_blocks(cmp4[:, g:]), pe_kp, pe_vp,
                        w1_kp, w1_vp, w2k.astype(BF16), w2v.astype(BF16))
    ncp = s // CMP_STRIDE
    kca = kca.reshape(b, g, ncp, LANES)
    vct = jnp.swapaxes(vc.reshape(b, g, ncp, HEAD), 2, 3)

    qnt = tr(qn)
    ocmp_t, bias_t = _cmp_select(qnt, kca, vct, _overlap_table(s))
    onehot = (jnp.arange(s)[:, None] // SEL_BLOCK == jnp.arange(HEAD)[None, :]).astype(BF16)
    pad_hot = jnp.concatenate([jnp.zeros_like(onehot), onehot] * g, axis=1)
    ksa = ksx.reshape(b, s, -1) + pad_hot[None]
    ynt = _sel_win(qnt, bias_t, ksa, kwx.reshape(b, s, -1), tr(vsw), ocmp_t, tr(gn))
    yn = jnp.swapaxes(ynt, 1, 2).reshape(t, -1)

    h1_tiles, c_tiles, s1t, s2t = _out_proj(
        yda, yn, gm, x2, p_da.astype(BF16), p_nsa.astype(BF16), w_o.astype(BF16),
        ffn_norm.reshape(1, d), wq.astype(BF16), k1.astype(BF16), k2.astype(BF16))
    idx_t, gate_t = _peer_topk(s1t, s2t)
    idx = idx_t.reshape(PEER_E, t).T
    gate = gate_t.reshape(PEER_E, t).T
    out = _peer_ffn(idx, gate, c_tiles, h1_tiles, out_norm.reshape(1, SUBLANES, LANES),
                    _pack_expert_rows(pu, pv))
    return out.reshape(b, s, d)


def kernel(x, attn_norm, w_in, da_lambda_q1, da_lambda_k1, da_lambda_q2, da_lambda_k2, da_subln,
           cmp_pe_k, cmp_pe_v, cmp_w1_k, cmp_w1_v, cmp_w2_k, cmp_w2_v, p_da, p_nsa, w_o,
           ffn_norm, peer_wq, peer_k1, peer_k2, peer_u, peer_v, final_norm):
    depth = attn_norm.shape[0]
    assert depth == 1, "the final norm is fused into the last layer's PEER kernel"
    h = x
    for l in range(depth):
        h = _layer(h, l, attn_norm[l], w_in[l], da_lambda_q1[l], da_lambda_k1[l], da_lambda_q2[l],
                   da_lambda_k2[l], da_subln[l], cmp_pe_k[l], cmp_pe_v[l], cmp_w1_k[l],
                   cmp_w1_v[l], cmp_w2_k[l], cmp_w2_v[l], p_da[l], p_nsa[l], w_o[l], ffn_norm[l],
                   peer_wq[l], peer_k1[l], peer_k2[l], peer_u[l], peer_v[l], final_norm)
    return h
```

```python
import functools
import math

import jax
import jax.numpy as jnp
from jax import lax
from jax.experimental import pallas as pl
from jax.experimental.pallas import tpu as pltpu

F32 = jnp.float32
BF16 = jnp.bfloat16
I32 = jnp.int32

RMS_EPS = 1e-6
ROPE_THETA = 500000.0
ROPE_HALF = 8
HEAD = 64
DA_HEADS = 4
NSA_GROUPS = 2
NSA_HG = 4
CMP_STRIDE = 16
CMP_BLOCK = 32
SEL_BLOCK = 64
SEL_TOPK = 16
WINDOW = 512
FORCE_BONUS = 1e4
NEG_BIG = -1e30
SEL_MASK_BIAS = -2.0 ** 100
ATTN_SCALE = HEAD ** -0.5 * math.log2(math.e)
PEER_HEADS = 8
PEER_NKEYS = 128
PEER_TOPK = 16
LANES = 128
SUBLANES = 8
VMEM_LIMIT = 56 * 1024 * 1024

NT_DIMS = (((1,), (1,)), ((), ()))


def _rmsnorm(x, g):
    return x * lax.rsqrt(jnp.mean(x * x, axis=-1, keepdims=True) + RMS_EPS) * g


def _sigmoid(z):
    return 1.0 / (1.0 + jnp.exp(-z))


def _gelu(z):
    return 0.5 * z * (1.0 + lax.erf(z * (2.0 ** -0.5)))


def _params(*sem):
    return pltpu.CompilerParams(dimension_semantics=sem, vmem_limit_bytes=VMEM_LIMIT)


def _sublane_transpose(v):
    sub = lax.broadcasted_iota(I32, (SUBLANES, LANES), 0)
    v = list(v)
    for dist in (4, 2, 1):
        low = (sub & dist) == 0
        nxt = list(v)
        for i in range(SUBLANES):
            if i & dist == 0:
                nxt[i] = jnp.where(low, v[i], pltpu.roll(v[i + dist], dist, 0))
                nxt[i + dist] = jnp.where(low, pltpu.roll(v[i], SUBLANES - dist, 0), v[i + dist])
        v = nxt
    return v


def _rows_to_tiles(x, tile_ref):
    for g in range(x.shape[0] // SUBLANES):
        rows = slice(g * SUBLANES, (g + 1) * SUBLANES)
        tiles = _sublane_transpose([x[rows, c * LANES:(c + 1) * LANES] for c in range(SUBLANES)])
        for e in range(SUBLANES):
            tile_ref[g * SUBLANES + e] = tiles[e]


def _tiles_to_rows(t, row_ref):
    for g in range(t.shape[0] // SUBLANES):
        chunks = _sublane_transpose([t[g * SUBLANES + e] for e in range(SUBLANES)])
        for c in range(SUBLANES):
            row_ref[g * SUBLANES:(g + 1) * SUBLANES, c * LANES:(c + 1) * LANES] = chunks[c]


_QDA0, _KDA0, _VDA0, _QN0 = 0, 512, 1024, 1536
_KC0, _VC0, _KS0, _KW0, _VSW0, _GM0, _GN0, _WCOLS = 2048, 2304, 2560, 2816, 3072, 3328, 5376, 5632


def _inproj_body(x_ref, g_ref, w_ref, rc_ref, rs1_ref, rs2_ref,
                 qda_ref, kda_ref, vda_ref, qn_ref, cmp_ref, ks_ref, kw_ref, vsw_ref, gm_ref, gn_ref):
    a = _rmsnorm(x_ref[...], g_ref[...]).astype(BF16)
    rc, rs1, rs2 = rc_ref[...], rs1_ref[...], rs2_ref[...]

    def rope(z):
        return (z * rc + pltpu.roll(z, ROPE_HALF, 1) * rs1
                + pltpu.roll(z, LANES - ROPE_HALF, 1) * rs2)

    def proj(c0):
        return jnp.dot(a, w_ref[:, c0:c0 + 256], preferred_element_type=F32)

    def rope2(z):
        return jnp.concatenate([rope(z[:, :LANES]), rope(z[:, LANES:])], axis=1)

    for c in range(2):
        qda_ref[:, c * 256:(c + 1) * 256] = (rope2(proj(_QDA0 + c * 256)) * ATTN_SCALE).astype(BF16)
        kda_ref[:, c * 256:(c + 1) * 256] = rope2(proj(_KDA0 + c * 256)).astype(BF16)
        vda_ref[:, c * 256:(c + 1) * 256] = proj(_VDA0 + c * 256).astype(BF16)
        qn_ref[:, c * 256:(c + 1) * 256] = (rope2(proj(_QN0 + c * 256)) * ATTN_SCALE).astype(BF16)
    cmp_ref[:, 0:256] = rope2(proj(_KC0)).astype(BF16)
    cmp_ref[:, 256:512] = proj(_VC0).astype(BF16)
    ks_ref[...] = rope2(proj(_KS0)).astype(BF16)
    kw_ref[...] = rope2(proj(_KW0)).astype(BF16)
    vsw_ref[...] = proj(_VSW0).astype(BF16)
    for c in range(8):
        gm_ref[:, c * 256:(c + 1) * 256] = _sigmoid(proj(_GM0 + c * 256)).astype(BF16)
    gn_ref[...] = _sigmoid(proj(_GN0))


def _pack_w_in(w):
    d = w.shape[0]
    zeros = lambda n: jnp.zeros((d, n), w.dtype)

    def spread_groups(c0):
        return [w[:, c0:c0 + HEAD], zeros(HEAD), w[:, c0 + HEAD:c0 + 2 * HEAD], zeros(HEAD)]

    kv_w = NSA_GROUPS * HEAD
    kv0 = 4 * 512
    kc, vc, ks, vs, kw, vw = (kv0 + kv_w * i for i in range(6))
    gn0 = kv0 + 6 * kv_w
    per_group = NSA_HG * 3
    gm0 = gn0 + NSA_GROUPS * per_group
    assert w.shape[1] == gm0 + 2 * d
    gn = w[:, gn0:gm0]
    cols = ([w[:, :kv0]] + spread_groups(kc) + spread_groups(vc) + spread_groups(ks)
            + spread_groups(kw) + [w[:, vs:vs + kv_w], w[:, vw:vw + kv_w], w[:, gm0:],
                                   gn[:, :per_group], zeros(LANES - per_group),
                                   gn[:, per_group:], zeros(LANES - per_group)])
    packed = jnp.concatenate(cols, axis=1).astype(BF16)
    assert packed.shape[1] == _WCOLS
    return packed


def _rope_tables(seq):
    inv = jnp.power(ROPE_THETA, -jnp.arange(ROPE_HALF, dtype=F32) * 2.0 / (2 * ROPE_HALF))
    ang = jnp.arange(seq, dtype=F32)[:, None] * inv[None, :]
    cos, sin = jnp.cos(ang), jnp.sin(ang)
    one = jnp.ones((seq, HEAD - 2 * ROPE_HALF), F32)
    zero8 = jnp.zeros((seq, ROPE_HALF), F32)
    zero48 = jnp.zeros_like(one)
    rc = jnp.concatenate([cos, cos, one], axis=1)
    rs1 = jnp.concatenate([zero8, sin, zero48], axis=1)
    rs2 = jnp.concatenate([-sin, zero8, zero48], axis=1)
    return tuple(jnp.concatenate([t, t], axis=1) for t in (rc, rs1, rs2))


def _in_proj(x2, norm_g, w_packed, rope_tabs, seq):
    t, d = x2.shape
    tm = 512
    nseq = seq // tm
    row = lambda i: (i, 0)
    fixed = lambda i: (0, 0)
    out_shapes = [
        jax.ShapeDtypeStruct((t, 512), BF16), jax.ShapeDtypeStruct((t, 512), BF16),
        jax.ShapeDtypeStruct((t, 512), BF16), jax.ShapeDtypeStruct((t, 512), BF16),
        jax.ShapeDtypeStruct((t, 512), BF16), jax.ShapeDtypeStruct((t, 256), BF16),
        jax.ShapeDtypeStruct((t, 256), BF16), jax.ShapeDtypeStruct((t, 256), BF16),
        jax.ShapeDtypeStruct((t, 2048), BF16), jax.ShapeDtypeStruct((t, 256), F32)]
    rope_spec = pl.BlockSpec((tm, LANES), lambda i: (i % nseq, 0))
    return pl.pallas_call(
        _inproj_body,
        grid=(t // tm,),
        in_specs=[pl.BlockSpec((tm, d), row), pl.BlockSpec((1, d), fixed),
                  pl.BlockSpec((d, _WCOLS), fixed), rope_spec, rope_spec, rope_spec],
        out_specs=[pl.BlockSpec((tm, s.shape[1]), row) for s in out_shapes],
        out_shape=out_shapes,
        compiler_params=_params("parallel"),
        name="in_proj",
    )(x2, norm_g, w_packed, *rope_tabs)


def _softmax_step(s, vt, m_ref, l_ref, acc_ref):
    m_prev = m_ref[...]
    m_new = jnp.maximum(m_prev, jnp.max(s, axis=0, keepdims=True))
    alpha = jnp.exp2(m_prev - m_new)
    p = jnp.exp2(s - m_new)
    l_ref[...] = alpha * l_ref[...] + jnp.sum(p, axis=0, keepdims=True)
    acc_ref[...] = alpha * acc_ref[...] + jnp.dot(vt, p.astype(BF16), preferred_element_type=F32)
    m_ref[...] = m_new


def _softmax_reset(m_ref, l_ref, acc_ref):
    m_ref[...] = jnp.full(m_ref.shape, NEG_BIG, F32)
    l_ref[...] = jnp.zeros(l_ref.shape, F32)
    acc_ref[...] = jnp.zeros(acc_ref.shape, F32)


def _attend_tiles(n_full, scores, values, mask_last, sa_ref, sb_ref, m_ref, l_ref, acc_ref):
    step = lambda s, t: _softmax_step(s, values(t), m_ref, l_ref, acc_ref)
    sa_ref[...] = scores(0)

    def two_tiles(i, carry):
        t = 2 * i
        sb_ref[...] = scores(t + 1)
        step(sa_ref[...], t)
        sa_ref[...] = scores(t + 2)
        step(sb_ref[...], t + 1)
        return carry

    lax.fori_loop(0, n_full // 2, two_tiles, 0)
    odd = (n_full & 1) == 1

    @pl.when(odd)
    def _():
        sb_ref[...] = scores(n_full)
        step(sa_ref[...], n_full - 1)
        step(mask_last(sb_ref[...]), n_full)

    @pl.when(jnp.logical_not(odd))
    def _():
        step(mask_last(sa_ref[...]), n_full)


def _key_tile(ref, kt, bk):
    return ref[(0,) * (len(ref.shape) - 2) + (pl.ds(pl.multiple_of(kt * bk, bk), bk), slice(None))]


def _value_tile(ref, kt, bk):
    return ref[(0,) * (len(ref.shape) - 2) + (slice(None), pl.ds(pl.multiple_of(kt * bk, bk), bk))]


def _diffattn_body(lambda_init, bq, q_ref, k_ref, v_ref, lq1_ref, lk1_ref, lq2_ref, lk2_ref,
                   sub_ref, y_ref, qbd_ref, m_ref, l_ref, acc_ref, sa_ref, sb_ref):
    qi = pl.program_id(2)
    bk = bq
    qt = q_ref[0]
    sub = lax.broadcasted_iota(I32, qt.shape, 0)
    zero = jnp.zeros_like(qt)
    qbd_ref[:, 0:bq] = jnp.where(sub < HEAD, qt, zero)
    qbd_ref[:, bq:2 * bq] = jnp.where(sub >= HEAD, qt, zero)
    _softmax_reset(m_ref, l_ref, acc_ref)

    def scores(kt):
        return jnp.dot(_key_tile(k_ref, kt, bk), qbd_ref[...], preferred_element_type=F32)

    def causal(s):
        r = lax.broadcasted_iota(I32, (bk, 2 * bq), 0)
        c = lax.broadcasted_iota(I32, (bk, 2 * bq), 1) & (bq - 1)
        return jnp.where(r <= c, s, NEG_BIG)

    _attend_tiles(qi, scores, lambda kt: _value_tile(v_ref, kt, bk), causal,
                  sa_ref, sb_ref, m_ref, l_ref, acc_ref)

    o = acc_ref[...] / l_ref[...]
    lam = (jnp.exp(jnp.sum(lq1_ref[...] * lk1_ref[...], axis=1, keepdims=True))
           - jnp.exp(jnp.sum(lq2_ref[...] * lk2_ref[...], axis=1, keepdims=True)) + lambda_init)
    d = o[:, 0:bq] - lam * o[:, bq:2 * bq]
    ms = jnp.mean(d * d, axis=0, keepdims=True)
    y = d * lax.rsqrt(ms + RMS_EPS) * sub_ref[...] * (1.0 - lambda_init)
    y_ref[0] = y.astype(BF16)


def _diff_attn(qdat, kda, vdat, lq1, lk1, lq2, lk2, subln_col, lambda_init):
    b, s, _ = kda.shape
    bq = 512
    vec = lambda n: pl.BlockSpec((1, n), lambda bi, h, qi: (0, 0))
    dv = 2 * HEAD
    qtile = pl.BlockSpec((1, dv, bq), lambda bi, h, qi: (bi, h, qi))
    return pl.pallas_call(
        functools.partial(_diffattn_body, lambda_init, bq),
        grid=(b, DA_HEADS, s // bq),
        in_specs=[qtile,
                  pl.BlockSpec((1, s, LANES), lambda bi, h, qi: (bi, 0, h)),
                  pl.BlockSpec((1, dv, s), lambda bi, h, qi: (bi, h, 0)),
                  vec(HEAD), vec(HEAD), vec(HEAD), vec(HEAD),
                  pl.BlockSpec((dv, 1), lambda bi, h, qi: (0, 0))],
        out_specs=qtile,
        out_shape=jax.ShapeDtypeStruct((b, DA_HEADS * dv, s), BF16),
        scratch_shapes=[pltpu.VMEM((LANES, 2 * bq), BF16), pltpu.VMEM((1, 2 * bq), F32),
                        pltpu.VMEM((1, 2 * bq), F32), pltpu.VMEM((dv, 2 * bq), F32),
                        pltpu.VMEM((bq, 2 * bq), F32), pltpu.VMEM((bq, 2 * bq), F32)],
        compiler_params=_params("parallel", "parallel", "parallel"),
        name="diff_attn",
    )(qdat, kda, vdat, lq1, lk1, lq2, lk2, subln_col)


def _compress_body(xk_ref, xv_ref, pek_ref, pev_ref, w1k_ref, w1v_ref, w2k_ref, w2v_ref,
                   kc_ref, vc_ref):
    def mlp(x_ref, pe_ref, w1_ref, w2_ref):
        blocks = (x_ref[0].astype(F32) + pe_ref[...]).astype(BF16)
        hid = _gelu(jnp.dot(blocks, w1_ref[...], preferred_element_type=F32))
        return jnp.dot(hid.astype(BF16), w2_ref[...], preferred_element_type=F32)

    kc = mlp(xk_ref, pek_ref, w1k_ref, w2k_ref)
    kc_ref[0] = jnp.concatenate([kc, jnp.zeros_like(kc)], axis=1).astype(BF16)
    vc_ref[0] = mlp(xv_ref, pev_ref, w1v_ref, w2v_ref).astype(BF16)


def _compress(xk, xv, pe_k, pe_v, w1k, w1v, w2k, w2v):
    n, ncp, width = xk.shape
    blk = pl.BlockSpec((1, ncp, width), lambda i: (i, 0, 0))
    fixed = lambda shape: pl.BlockSpec(shape, lambda i: (0, 0))
    return pl.pallas_call(
        _compress_body,
        grid=(n,),
        in_specs=[blk, blk, fixed((1, width)), fixed((1, width)), fixed((width, HEAD)),
                  fixed((width, HEAD)), fixed((HEAD, HEAD)), fixed((HEAD, HEAD))],
        out_specs=[pl.BlockSpec((1, ncp, LANES), lambda i: (i, 0, 0)),
                   pl.BlockSpec((1, ncp, HEAD), lambda i: (i, 0, 0))],
        out_shape=[jax.ShapeDtypeStruct((n, ncp, LANES), BF16),
                   jax.ShapeDtypeStruct((n, ncp, HEAD), BF16)],
        compiler_params=_params("parallel"),
        name="compress",
    )(xk, xv, pe_k, pe_v, w1k, w1v, w2k, w2v)


def _heads_on_lanes(qt, bq):
    return jnp.concatenate([qt[h * HEAD:(h + 1) * HEAD, :] for h in range(NSA_HG)], axis=1)


def _cmpsel_body(bq, q_ref, kc_ref, vc_ref, ovl_ref, ocmp_ref, bias_ref):
    qi = pl.program_id(2)
    ncp = kc_ref.shape[2]
    rows = NSA_HG * bq
    q2 = _heads_on_lanes(q_ref[0], bq)
    qz = jnp.concatenate([q2, jnp.zeros_like(q2)], axis=0)
    s = jnp.dot(kc_ref[0, 0], qz, preferred_element_type=F32)
    n = lax.broadcasted_iota(I32, (ncp, rows), 0)
    qpos = qi * bq + (lax.broadcasted_iota(I32, (ncp, rows), 1) & (bq - 1))
    cmask = n * CMP_STRIDE + (CMP_BLOCK - 1) <= qpos
    s = jnp.where(cmask, s, NEG_BIG)
    e = jnp.exp2(s - jnp.max(s, axis=0, keepdims=True))
    p = jnp.where(cmask, e / jnp.sum(e, axis=0, keepdims=True), 0.0)
    o = jnp.dot(vc_ref[0, 0], p.astype(BF16), preferred_element_type=F32)
    for h in range(NSA_HG):
        ocmp_ref[0, 0, h * HEAD:(h + 1) * HEAD, :] = o[:, h * bq:(h + 1) * bq]

    psum = p[:, 0:bq] + p[:, bq:2 * bq] + p[:, 2 * bq:3 * bq] + p[:, 3 * bq:4 * bq]
    imp = jnp.dot(ovl_ref[...], psum, preferred_element_type=F32)
    blk = lax.broadcasted_iota(I32, (SEL_BLOCK, bq), 0)
    pos = qi * bq + lax.broadcasted_iota(I32, (SEL_BLOCK, bq), 1)
    cur = lax.shift_right_logical(pos, SEL_BLOCK.bit_length() - 1)
    valid = blk <= cur
    forced = (blk == 0) | (blk == cur) | (blk == cur - 1)
    score = jnp.where(valid, imp + jnp.where(forced, FORCE_BONUS, 0.0), -jnp.inf)
    rank = jnp.zeros((SEL_BLOCK, bq), I32)
    for i in range(SEL_BLOCK):
        other = score[i:i + 1, :]
        beats = (other > score) | ((other == score) & (blk > i))
        rank = rank + beats.astype(I32)
    keep = valid & (rank < SEL_TOPK)
    bias_ref[0, 0] = jnp.where(keep, 0.0, SEL_MASK_BIAS).astype(BF16)


def _cmp_select(qnt, kca, vct, overlap_t):
    b, g, ncp, _ = kca.shape
    s = qnt.shape[2]
    bq = 128
    dq = NSA_HG * HEAD
    return pl.pallas_call(
        functools.partial(_cmpsel_body, bq),
        grid=(b, g, s // bq),
        in_specs=[pl.BlockSpec((1, dq, bq), lambda bi, gi, qi: (bi, gi, qi)),
                  pl.BlockSpec((1, 1, ncp, LANES), lambda bi, gi, qi: (bi, gi, 0, 0)),
                  pl.BlockSpec((1, 1, HEAD, ncp), lambda bi, gi, qi: (bi, gi, 0, 0)),
                  pl.BlockSpec((SEL_BLOCK, ncp), lambda bi, gi, qi: (0, 0))],
        out_specs=[pl.BlockSpec((1, 1, dq, bq), lambda bi, gi, qi: (bi, gi, 0, qi)),
                   pl.BlockSpec((1, 1, SEL_BLOCK, bq), lambda bi, gi, qi: (bi, gi, 0, qi))],
        out_shape=[jax.ShapeDtypeStruct((b, g, dq, s), F32),
                   jax.ShapeDtypeStruct((b, g, SEL_BLOCK, s), BF16)],
        compiler_params=_params("parallel", "parallel", "parallel"),
        name="cmp_select",
    )(qnt, kca, vct, overlap_t)


def _selwin_body(bq, bks, q_ref, bias_ref, ks_ref, vs_ref, kw_ref, vw_ref, ocmp_ref, gate_ref,
                 y_ref, m_ref, l_ref, acc_ref, qa_ref, sa_ref, sb_ref):
    qi = pl.program_id(2)
    rows = NSA_HG * bq
    q2 = _heads_on_lanes(q_ref[0], bq)
    bias = bias_ref[0, 0]
    qa = jnp.concatenate([q2, jnp.concatenate([bias] * NSA_HG, axis=1)], axis=0)
    qw = jnp.concatenate([q2, jnp.zeros_like(q2)], axis=0)
    r = lax.broadcasted_iota(I32, (bq, rows), 0)
    c = lax.broadcasted_iota(I32, (bq, rows), 1) & (bq - 1)

    _softmax_reset(m_ref, l_ref, acc_ref)

    qa_ref[...] = qa
    last = (qi * bq) // bks

    def causal(s):
        kpos = last * bks + lax.broadcasted_iota(I32, (bks, rows), 0)
        qpos = qi * bq + (lax.broadcasted_iota(I32, (bks, rows), 1) & (bq - 1))
        return jnp.where(kpos <= qpos, s, NEG_BIG)

    _attend_tiles(last,
                  lambda kt: jnp.dot(_key_tile(ks_ref, kt, bks), qa_ref[...],
                                     preferred_element_type=F32),
                  lambda kt: _value_tile(vs_ref, kt, bks), causal,
                  sa_ref, sb_ref, m_ref, l_ref, acc_ref)
    o_sel = acc_ref[...] / l_ref[...]

    _softmax_reset(m_ref, l_ref, acc_ref)
    backs = list(range(WINDOW // bq, -1, -1))
    tiles = [jnp.maximum(qi - back, 0) for back in backs]
    raw = [jnp.dot(_key_tile(kw_ref, kt, bq), qw, preferred_element_type=F32) for kt in tiles]
    for back, kt, s in zip(backs, tiles, raw):
        if back >= 1 and (back + 1) * bq <= WINDOW:
            inside = qi >= back
        else:
            dist = c + back * bq - r
            inside = (dist >= 0) & (dist < WINDOW) & (qi >= back)
        _softmax_step(jnp.where(inside, s, NEG_BIG), _value_tile(vw_ref, kt, bq),
                      m_ref, l_ref, acc_ref)

    o_win = acc_ref[...] / l_ref[...]
    gate = gate_ref[0]
    for h in range(NSA_HG):
        sl = slice(h * bq, (h + 1) * bq)
        y = (gate[3 * h:3 * h + 1, :] * ocmp_ref[0, 0, h * HEAD:(h + 1) * HEAD, :]
             + gate[3 * h + 1:3 * h + 2, :] * o_sel[:, sl]
             + gate[3 * h + 2:3 * h + 3, :] * o_win[:, sl])
        y_ref[0, h * HEAD:(h + 1) * HEAD, :] = y.astype(BF16)


def _sel_win(qnt, bias_t, ksa, kwa, vswt, ocmp_t, gnt):
    b, s, _ = ksa.shape
    g = NSA_GROUPS
    bq, bks = 256, 512
    dq = NSA_HG * HEAD
    rows = NSA_HG * bq
    keys = pl.BlockSpec((1, s, LANES), lambda bi, gi, qi: (bi, 0, gi))
    vals = lambda first: pl.BlockSpec((1, HEAD, s), lambda bi, gi, qi: (bi, first + gi, 0))
    qtile = pl.BlockSpec((1, dq, bq), lambda bi, gi, qi: (bi, gi, qi))
    return pl.pallas_call(
        functools.partial(_selwin_body, bq, bks),
        grid=(b, g, s // bq),
        in_specs=[qtile,
                  pl.BlockSpec((1, 1, SEL_BLOCK, bq), lambda bi, gi, qi: (bi, gi, 0, qi)),
                  keys, vals(0), keys, vals(g),
                  pl.BlockSpec((1, 1, dq, bq), lambda bi, gi, qi: (bi, gi, 0, qi)),
                  pl.BlockSpec((1, LANES, bq), lambda bi, gi, qi: (bi, gi, qi))],
        out_specs=qtile,
        out_shape=jax.ShapeDtypeStruct((b, g * dq, s), BF16),
        scratch_shapes=[pltpu.VMEM((1, rows), F32), pltpu.VMEM((1, rows), F32),
                        pltpu.VMEM((HEAD, rows), F32), pltpu.VMEM((LANES, rows), BF16),
                        pltpu.VMEM((bks, rows), F32), pltpu.VMEM((bks, rows), F32)],
        compiler_params=_params("parallel", "parallel", "parallel"),
        name="sel_win",
    )(qnt, bias_t, ksa, vswt, kwa, vswt, ocmp_t, gnt)


def _outproj_body(yda_ref, yn_ref, gm_ref, x_ref, pda_ref, pnsa_ref, wo_ref, fg_ref, wq_ref,
                  k1_ref, k2_ref, h1_ref, c_ref, s1_ref, s2_ref):
    d = x_ref.shape[1]
    a = jnp.dot(yda_ref[...], pda_ref[...], preferred_element_type=F32)
    bn = jnp.dot(yn_ref[...], pnsa_ref[...], preferred_element_type=F32)
    merged = gm_ref[:, :d].astype(F32) * a + gm_ref[:, d:].astype(F32) * bn
    h1 = x_ref[...] + jnp.dot(merged.astype(BF16), wo_ref[...], preferred_element_type=F32)
    c = _rmsnorm(h1, fg_ref[...])
    _rows_to_tiles(h1, h1_ref)
    _rows_to_tiles(c, c_ref)
    cb = c.astype(BF16)
    for h in range(PEER_HEADS):
        qh = jnp.dot(cb, wq_ref[:, h * 256:(h + 1) * 256], preferred_element_type=F32).astype(BF16)
        s1_ref[h] = lax.dot_general(k1_ref[...], qh[:, :LANES], NT_DIMS, preferred_element_type=F32)
        s2_ref[h] = lax.dot_general(k2_ref[...], qh[:, LANES:], NT_DIMS, preferred_element_type=F32)


def _out_proj(yda, yn, gm, x2, pda, pnsa, wo, ffn_g, wq, k1, k2):
    t, d = x2.shape
    tm = 512
    row = lambda w: pl.BlockSpec((tm, w), lambda i: (i, 0))
    fixed = lambda a: pl.BlockSpec(a.shape, lambda i: (0, 0))
    sspec = pl.BlockSpec((PEER_HEADS, PEER_NKEYS, tm), lambda i: (0, 0, i))
    assert d == SUBLANES * LANES
    tiles = pl.BlockSpec((tm, SUBLANES, LANES), lambda i: (i, 0, 0))
    return pl.pallas_call(
        _outproj_body,
        grid=(t // tm,),
        in_specs=[row(512), row(512), row(2 * d), row(d), fixed(pda), fixed(pnsa), fixed(wo),
                  fixed(ffn_g), fixed(wq), fixed(k1), fixed(k2)],
        out_specs=[tiles, tiles, sspec, sspec],
        out_shape=[jax.ShapeDtypeStruct((t, SUBLANES, LANES), F32),
                   jax.ShapeDtypeStruct((t, SUBLANES, LANES), F32),
                   jax.ShapeDtypeStruct((PEER_HEADS, PEER_NKEYS, t), F32),
                   jax.ShapeDtypeStruct((PEER_HEADS, PEER_NKEYS, t), F32)],
        compiler_params=_params("parallel"),
        name="out_proj",
    )(yda, yn, gm, x2, pda, pnsa, wo, ffn_g, wq, k1, k2)


def _batcher_pairs(n):
    pairs = []

    def merge(lo, hi, r):
        step = r * 2
        if step < hi - lo:
            merge(lo, hi, step)
            merge(lo + r, hi, step)
            pairs.extend((i, i + r) for i in range(lo + r, hi - r, step))
        else:
            pairs.append((lo, lo + r))

    def sort(lo, hi):
        if hi - lo >= 1:
            mid = lo + (hi - lo) // 2
            sort(lo, mid)
            sort(mid + 1, hi)
            merge(lo, hi, 1)

    sort(0, n - 1)
    return pairs


_NET16 = _batcher_pairs(PEER_TOPK)


def _cmpx(a, b):
    c = (a[0] > b[0]) | ((a[0] == b[0]) & (a[1] < b[1]))
    return ((jnp.where(c, a[0], b[0]), jnp.where(c, a[1], b[1])),
            (jnp.where(c, b[0], a[0]), jnp.where(c, b[1], a[1])))


def _sort_lists(lists, n_real):
    lists = list(lists)
    for i, j in _NET16:
        if j < n_real:
            lists[i], lists[j] = _cmpx(lists[i], lists[j])
    return lists


def _merge_top(a, b):
    k = PEER_TOPK
    lists = [_cmpx(a[i], b[k - 1 - i])[0] for i in range(k)]
    step = k // 2
    while step >= 1:
        for i in range(k):
            if i & step == 0:
                lists[i], lists[i + step] = _cmpx(lists[i], lists[i + step])
        step //= 2
    return lists


def _top16(pairs):
    k = PEER_TOPK
    filler = (jnp.full(pairs[0][0].shape, -jnp.inf, F32), jnp.zeros(pairs[0][1].shape, I32))
    groups = []
    for g0 in range(0, len(pairs), k):
        chunk = list(pairs[g0:g0 + k])
        groups.append(_sort_lists(chunk + [filler] * (k - len(chunk)), len(chunk)))
    while len(groups) > 1:
        groups = [_merge_top(groups[i], groups[i + 1]) if i + 1 < len(groups) else groups[i]
                  for i in range(0, len(groups), 2)]
    return groups[0]


def _product_key_select(s1, s2):
    k = PEER_TOPK
    n_exp = PEER_NKEYS * PEER_NKEYS
    ids = lambda n: [jnp.full(s1[0].shape, i, I32) for i in range(n)]
    l1 = _top16(list(zip(s1, ids(len(s1)))))
    l2 = _top16(list(zip(s2, ids(len(s2)))))
    pair = lambda a, b: (l1[a][0] + l2[b][0],
                         (a * k + b) * n_exp + l1[a][1] * PEER_NKEYS + l2[b][1])
    first_row = [pair(0, b) for b in range(k)]
    rest = [pair(a, b) for a in range(1, k) for b in range(k // (a + 1))]
    best = _merge_top(first_row, _top16(rest))
    return [(v, p & (n_exp - 1)) for v, p in best]


def _peertopk_body(tt, s1_ref, s2_ref, idx_ref, gate_ref):
    nblk = tt // LANES
    assert nblk == SUBLANES

    def keys_major(ref, h):
        tiles = []
        for kg in range(PEER_NKEYS // SUBLANES):
            rows = slice(kg * SUBLANES, (kg + 1) * SUBLANES)
            tiles += _sublane_transpose([ref[h, rows, b * LANES:(b + 1) * LANES]
                                         for b in range(nblk)])
        return tiles

    def store(ref, h, tiles):
        for g in range(PEER_TOPK // SUBLANES):
            blocks = _sublane_transpose(tiles[g * SUBLANES:(g + 1) * SUBLANES])
            for b in range(nblk):
                ref[h, g * SUBLANES:(g + 1) * SUBLANES, b * LANES:(b + 1) * LANES] = blocks[b]

    def head(h, carry):
        best = _product_key_select(keys_major(s1_ref, h), keys_major(s2_ref, h))
        ex = [jnp.exp(v - best[0][0]) for v, _ in best]
        z = ex[0]
        for e in ex[1:]:
            z = z + e
        store(gate_ref, h, [e / z for e in ex])
        store(idx_ref, h, [ix for _, ix in best])
        return carry

    lax.fori_loop(0, PEER_HEADS, head, 0)


def _peer_topk(s1t, s2t):
    _, _, t = s1t.shape
    tt = SUBLANES * LANES
    spec_in = pl.BlockSpec((PEER_HEADS, PEER_NKEYS, tt), lambda i: (0, 0, i))
    spec_out = pl.BlockSpec((PEER_HEADS, PEER_TOPK, tt), lambda i: (0, 0, i))
    return pl.pallas_call(
        functools.partial(_peertopk_body, tt),
        grid=(t // tt,),
        in_specs=[spec_in, spec_in],
        out_specs=[spec_out, spec_out],
        out_shape=[jax.ShapeDtypeStruct((PEER_HEADS, PEER_TOPK, t), I32),
                   jax.ShapeDtypeStruct((PEER_HEADS, PEER_TOPK, t), F32)],
        compiler_params=_params("parallel"),
        name="peer_topk",
    )(s1t, s2t)


PEER_E = PEER_HEADS * PEER_TOPK
PEER_RING = 4
PEER_MID_ROWS = 48


def _pack_body(u_ref, v_ref, out_ref):
    bf16_bits = lambda a: lax.bitcast_convert_type(a.astype(BF16).astype(F32), I32)
    words = bf16_bits(u_ref[...]) | lax.shift_right_logical(bf16_bits(v_ref[...]), jnp.int32(16))
    _rows_to_tiles(words, out_ref)


def _pack_expert_rows(pu, pv):
    n, d = pu.shape
    assert d == SUBLANES * LANES
    te = 256
    blk = pl.BlockSpec((te, d), lambda i: (i, 0))
    return pl.pallas_call(
        _pack_body,
        grid=(n // te,),
        in_specs=[blk, blk],
        out_specs=pl.BlockSpec((te, SUBLANES, LANES), lambda i: (i, 0, 0)),
        out_shape=jax.ShapeDtypeStruct((n, SUBLANES, LANES), I32),
        compiler_params=_params("parallel"),
        name="pack_experts",
    )(pu, pv)


def _word_hi(w):
    return lax.bitcast_convert_type(w & jnp.int32(-65536), F32)


def _word_lo(w):
    return lax.bitcast_convert_type(lax.shift_left(w, jnp.int32(16)), F32)


def _sublane_sums(a, sub):
    for dist in (4, 2, 1):
        low = (sub & dist) == 0
        half = len(a) // 2
        a = [jnp.where(low, a[i], pltpu.roll(a[i + half], dist, 0))
             + jnp.where(low, pltpu.roll(a[i], SUBLANES - dist, 0), a[i + half])
             for i in range(half)]
    return a[0]


def _peer_pair_math(expert_u, expert_v, x8, gates, store, issue_some, wcol_ref):
    sub = lax.broadcasted_iota(I32, (SUBLANES, LANES), 0)
    eye = (lax.broadcasted_iota(I32, (PEER_E, LANES), 0)
           == lax.broadcasted_iota(I32, (PEER_E, LANES), 1))
    ones_rows = jnp.ones((SUBLANES, LANES), BF16)
    ones_sq = jnp.ones((LANES, LANES), BF16)
    ngroup = PEER_E // SUBLANES

    def hidden(a):
        groups = []
        for g in range(ngroup):
            prods = [expert_u(a, g * SUBLANES + r) * x8[a] for r in range(SUBLANES)]
            groups.append(_sublane_sums(prods, sub))
            issue_some(3 - g % 2)
        return jnp.concatenate(groups, axis=0)

    def expert_weights(a, q):
        q_hi = q.astype(BF16)
        q_lo = (q - q_hi.astype(F32)).astype(BF16)
        hid = (lax.dot_general(ones_rows, q_hi, NT_DIMS, preferred_element_type=F32)
               + lax.dot_general(ones_rows, q_lo, NT_DIMS, preferred_element_type=F32))
        issue_some(PEER_MID_ROWS // 2)
        w = _gelu(hid[0:1]) * gates[a]
        wd = jnp.where(eye, jnp.broadcast_to(w, (PEER_E, LANES)), 0.0).astype(BF16)
        wcol_ref[a] = jnp.dot(wd, ones_sq, preferred_element_type=F32)
        issue_some(PEER_MID_ROWS // 2)

    def combine(a):
        out = jnp.zeros((SUBLANES, LANES), F32)
        for g in range(ngroup):
            for r in range(SUBLANES):
                j = g * SUBLANES + r
                out = out + wcol_ref[a, j:j + 1, :] * expert_v(a, j)
            issue_some(3 - g % 2)
        store(a, out)

    expert_weights(0, hidden(0))
    expert_weights(1, hidden(1))
    combine(0)
    combine(1)


def _peer_finish(h1_ref, acc_ref, fg_ref, out_ref):
    hsum = h1_ref[...] + acc_ref[...]
    ms = jnp.mean(hsum * hsum, axis=(1, 2), keepdims=True)
    _tiles_to_rows(hsum * lax.rsqrt(ms + RMS_EPS) * fg_ref[...], out_ref)


def _peerffn_body(tb, idx_ref, gate_ref, c_ref, h1_ref, fg_ref, uv_ref, out_ref, buf_ref,
                  acc_ref, wcol_ref, sem_ref):
    npairs = tb // 2
    ahead = PEER_RING - 1

    def row_copy(t, j, slot):
        return pltpu.make_async_copy(uv_ref.at[idx_ref[t, j]], buf_ref.at[slot, j],
                                     sem_ref.at[slot])

    def wait_all(slot):
        pltpu.make_async_copy(uv_ref.at[pl.ds(0, PEER_E)], buf_ref.at[slot],
                              sem_ref.at[slot]).wait()

    def ring(phase):
        base = 2 * (phase % PEER_RING)
        return (base, base + 1)

    def pair(i, phase, prefetch):
        toks = (2 * i, 2 * i + 1)
        slots = ring(phase)
        nslots = ring(phase + ahead)
        todo = [(a, j) for a in range(2) for j in range(PEER_E)]

        def issue_some(n):
            for a, j in todo[:n]:
                if prefetch:
                    row_copy(toks[a] + 2 * ahead, j, nslots[a]).start(priority=j % 2)
            del todo[:n]

        wait_all(slots[0])
        wait_all(slots[1])

        def store(a, out):
            acc_ref[toks[a]] = out

        _peer_pair_math(
            lambda a, j: _word_hi(buf_ref[slots[a], j]),
            lambda a, j: _word_lo(buf_ref[slots[a], j]),
            [c_ref[t] for t in toks], [gate_ref[pl.ds(t, 1), :] for t in toks], store, issue_some,
            wcol_ref)
        assert not todo, "every prefetch row DMA must be issued exactly once"

    assert npairs % PEER_RING == 0
    for i in range(ahead):
        for a, slot in enumerate(ring(i)):
            for j in range(PEER_E):
                row_copy(2 * i + a, j, slot).start(priority=j % 2)

    def body(g, carry):
        for phase in range(PEER_RING):
            pair(PEER_RING * g + phase, phase, True)
        return carry

    ngroups = npairs // PEER_RING
    lax.fori_loop(0, ngroups - 1, body, 0)
    for phase in range(PEER_RING):
        i = PEER_RING * (ngroups - 1) + phase
        pair(i, phase, i + ahead < npairs)
    _peer_finish(h1_ref, acc_ref, fg_ref, out_ref)


def _peer_ffn(idx, gate, c3, h13, final_g3, uv_tiles):
    t = h13.shape[0]
    tb = 256
    row3 = pl.BlockSpec((tb, SUBLANES, LANES), lambda i: (i, 0, 0))
    return pl.pallas_call(
        functools.partial(_peerffn_body, tb),
        grid=(t // tb,),
        in_specs=[pl.BlockSpec((tb, PEER_E), lambda i: (i, 0), memory_space=pltpu.SMEM),
                  pl.BlockSpec((tb, PEER_E), lambda i: (i, 0)), row3, row3,
                  pl.BlockSpec((1, SUBLANES, LANES), lambda i: (0, 0, 0)),
                  pl.BlockSpec(memory_space=pl.ANY)],
        out_specs=pl.BlockSpec((tb, SUBLANES * LANES), lambda i: (i, 0)),
        out_shape=jax.ShapeDtypeStruct((t, SUBLANES * LANES), F32),
        scratch_shapes=[pltpu.VMEM((2 * PEER_RING, PEER_E, SUBLANES, LANES), I32),
                        pltpu.VMEM((tb, SUBLANES, LANES), F32),
                        pltpu.VMEM((2, PEER_E, LANES), F32),
                        pltpu.SemaphoreType.DMA((2 * PEER_RING,))],
        compiler_params=_params("arbitrary"),
        name="peer_ffn",
    )(idx, gate, c3, h13, final_g3, uv_tiles)


def _overlap_table(seq):
    ci = jnp.arange(seq // CMP_STRIDE)[None, :] * CMP_STRIDE
    sj = jnp.arange(SEL_BLOCK)[:, None] * SEL_BLOCK
    return ((ci < sj + SEL_BLOCK) & (ci + CMP_BLOCK > sj)).astype(F32)


def _cmp_blocks(kv):
    b, g, s, dh = kv.shape
    r = kv.reshape(b * g, s // CMP_STRIDE, CMP_STRIDE * dh)
    return jnp.concatenate([r, jnp.roll(r, -1, axis=1)], axis=-1)


def _pad_cmp_params(pe, w1):
    pe_p = jnp.pad(pe, ((0, 0), (0, LANES - HEAD))).reshape(1, -1)
    w1_p = jnp.pad(w1.reshape(CMP_BLOCK, HEAD, -1), ((0, 0), (0, LANES - HEAD), (0, 0)))
    return pe_p, w1_p.reshape(CMP_BLOCK * LANES, -1).astype(BF16)


def _layer(h, lidx, attn_norm, w_in, lq1, lk1, lq2, lk2, subln, pe_k, pe_v, w1k, w1v, w2k, w2v,
           p_da, p_nsa, w_o, ffn_norm, wq, k1, k2, pu, pv, out_norm):
    b, s, d = h.shape
    t = b * s
    g, hg = NSA_GROUPS, NSA_HG
    lambda_init = 0.8 - 0.6 * math.exp(-0.3 * lidx)
    x2 = h.reshape(t, d)

    qda, kda, vda, qn, cmp_in, ksx, kwx, vsw, gm, gn = _in_proj(
        x2, attn_norm.reshape(1, d), _pack_w_in(w_in), _rope_tables(s), s)
    tr = lambda a2: jnp.swapaxes(a2.reshape(b, s, -1), 1, 2)
    ydat = _diff_attn(tr(qda), kda.reshape(b, s, -1), tr(vda),
                      lq1.reshape(1, -1), lk1.reshape(1, -1), lq2.reshape(1, -1),
                      lk2.reshape(1, -1), subln.reshape(-1, 1), lambda_init)
    yda = jnp.swapaxes(ydat, 1, 2).reshape(t, -1)

    assert s // SEL_BLOCK <= SEL_BLOCK, "selection bias rows hold at most 64 blocks"
    cmp4 = jnp.swapaxes(cmp_in.reshape(b, s, 2 * g, LANES), 1, 2)
    pe_kp, w1_kp = _pad_cmp_params(pe_k, w1k)
    pe_vp, w1_vp = _pad_cmp_params(pe_v, w1v)
    kca, vc = _compress(_cmp_blocks(cmp4[:, :g]), _cmp_blocks(cmp4[:, g:]), pe_kp, pe_vp,
                        w1_kp, w1_vp, w2k.astype(BF16), w2v.astype(BF16))
    ncp = s // CMP_STRIDE
    kca = kca.reshape(b, g, ncp, LANES)
    vct = jnp.swapaxes(vc.reshape(b, g, ncp, HEAD), 2, 3)

    qnt = tr(qn)
    ocmp_t, bias_t = _cmp_select(qnt, kca, vct, _overlap_table(s))
    onehot = (jnp.arange(s)[:, None] // SEL_BLOCK == jnp.arange(HEAD)[None, :]).astype(BF16)
    pad_hot = jnp.concatenate([jnp.zeros_like(onehot), onehot] * g, axis=1)
    ksa = ksx.reshape(b, s, -1) + pad_hot[None]
    ynt = _sel_win(qnt, bias_t, ksa, kwx.reshape(b, s, -1), tr(vsw), ocmp_t, tr(gn))
    yn = jnp.swapaxes(ynt, 1, 2).reshape(t, -1)

    h1_tiles, c_tiles, s1t, s2t = _out_proj(
        yda, yn, gm, x2, p_da.astype(BF16), p_nsa.astype(BF16), w_o.astype(BF16),
        ffn_norm.reshape(1, d), wq.astype(BF16), k1.astype(BF16), k2.astype(BF16))
    idx_t, gate_t = _peer_topk(s1t, s2t)
    idx = idx_t.reshape(PEER_E, t).T
    gate = gate_t.reshape(PEER_E, t).T
    out = _peer_ffn(idx, gate, c_tiles, h1_tiles, out_norm.reshape(1, SUBLANES, LANES),
                    _pack_expert_rows(pu, pv))
    return out.reshape(b, s, d)


def kernel(x, attn_norm, w_in, da_lambda_q1, da_lambda_k1, da_lambda_q2, da_lambda_k2, da_subln,
           cmp_pe_k, cmp_pe_v, cmp_w1_k, cmp_w1_v, cmp_w2_k, cmp_w2_v, p_da, p_nsa, w_o,
           ffn_norm, peer_wq, peer_k1, peer_k2, peer_u, peer_v, final_norm):
    depth = attn_norm.shape[0]
    assert depth == 1, "the final norm is fused into the last layer's PEER kernel"
    h = x
    for l in range(depth):
        h = _layer(h, l, attn_norm[l], w_in[l], da_lambda_q1[l], da_lambda_k1[l], da_lambda_q2[l],
                   da_lambda_k2[l], da_subln[l], cmp_pe_k[l], cmp_pe_v[l], cmp_w1_k[l],
                   cmp_w1_v[l], cmp_w2_k[l], cmp_w2_v[l], p_da[l], p_nsa[l], w_o[l], ffn_norm[l],
                   peer_wq[l], peer_k1[l], peer_k2[l], peer_u[l], peer_v[l], final_norm)
    return h
```

```python
import functools
import math

import jax
import jax.numpy as jnp
from jax import lax
from jax.experimental import pallas as pl
from jax.experimental.pallas import tpu as pltpu

F32 = jnp.float32
BF16 = jnp.bfloat16
I32 = jnp.int32

RMS_EPS = 1e-6
ROPE_THETA = 500000.0
ROPE_HALF = 8
HEAD = 64
DA_HEADS = 4
NSA_GROUPS = 2
NSA_HG = 4
CMP_STRIDE = 16
CMP_BLOCK = 32
SEL_BLOCK = 64
SEL_TOPK = 16
WINDOW = 512
FORCE_BONUS = 1e4
NEG_BIG = -1e30
SEL_MASK_BIAS = -2.0 ** 100
ATTN_SCALE = HEAD ** -0.5 * math.log2(math.e)
PEER_HEADS = 8
PEER_NKEYS = 128
PEER_TOPK = 16
LANES = 128
SUBLANES = 8
VMEM_LIMIT = 56 * 1024 * 1024

NT_DIMS = (((1,), (1,)), ((), ()))


def _rmsnorm(x, g):
    return x * lax.rsqrt(jnp.mean(x * x, axis=-1, keepdims=True) + RMS_EPS) * g


def _sigmoid(z):
    return 1.0 / (1.0 + jnp.exp(-z))


def _gelu(z):
    return 0.5 * z * (1.0 + lax.erf(z * (2.0 ** -0.5)))


def _params(*sem):
    return pltpu.CompilerParams(dimension_semantics=sem, vmem_limit_bytes=VMEM_LIMIT)


def _sublane_transpose(v):
    sub = lax.broadcasted_iota(I32, (SUBLANES, LANES), 0)
    v = list(v)
    for dist in (4, 2, 1):
        low = (sub & dist) == 0
        nxt = list(v)
        for i in range(SUBLANES):
            if i & dist == 0:
                nxt[i] = jnp.where(low, v[i], pltpu.roll(v[i + dist], dist, 0))
                nxt[i + dist] = jnp.where(low, pltpu.roll(v[i], SUBLANES - dist, 0), v[i + dist])
        v = nxt
    return v


def _rows_to_tiles(x, tile_ref):
    for g in range(x.shape[0] // SUBLANES):
        rows = slice(g * SUBLANES, (g + 1) * SUBLANES)
        tiles = _sublane_transpose([x[rows, c * LANES:(c + 1) * LANES] for c in range(SUBLANES)])
        for e in range(SUBLANES):
            tile_ref[g * SUBLANES + e] = tiles[e]


def _tiles_to_rows(t, row_ref):
    for g in range(t.shape[0] // SUBLANES):
        chunks = _sublane_transpose([t[g * SUBLANES + e] for e in range(SUBLANES)])
        for c in range(SUBLANES):
            row_ref[g * SUBLANES:(g + 1) * SUBLANES, c * LANES:(c + 1) * LANES] = chunks[c]


_QDA0, _KDA0, _VDA0, _QN0 = 0, 512, 1024, 1536
_KC0, _VC0, _KS0, _KW0, _VSW0, _GM0, _GN0, _WCOLS = 2048, 2304, 2560, 2816, 3072, 3328, 5376, 5632


def _inproj_body(x_ref, g_ref, w_ref, rc_ref, rs1_ref, rs2_ref,
                 qda_ref, kda_ref, vda_ref, qn_ref, cmp_ref, ks_ref, kw_ref, vsw_ref, gm_ref, gn_ref):
    a = _rmsnorm(x_ref[...], g_ref[...]).astype(BF16)
    rc, rs1, rs2 = rc_ref[...], rs1_ref[...], rs2_ref[...]

    def rope(z):
        return (z * rc + pltpu.roll(z, ROPE_HALF, 1) * rs1
                + pltpu.roll(z, LANES - ROPE_HALF, 1) * rs2)

    def proj(c0):
        return jnp.dot(a, w_ref[:, c0:c0 + 256], preferred_element_type=F32)

    def rope2(z):
        return jnp.concatenate([rope(z[:, :LANES]), rope(z[:, LANES:])], axis=1)

    for c in range(2):
        qda_ref[:, c * 256:(c + 1) * 256] = (rope2(proj(_QDA0 + c * 256)) * ATTN_SCALE).astype(BF16)
        kda_ref[:, c * 256:(c + 1) * 256] = rope2(proj(_KDA0 + c * 256)).astype(BF16)
        vda_ref[:, c * 256:(c + 1) * 256] = proj(_VDA0 + c * 256).astype(BF16)
        qn_ref[:, c * 256:(c + 1) * 256] = (rope2(proj(_QN0 + c * 256)) * ATTN_SCALE).astype(BF16)
    cmp_ref[:, 0:256] = rope2(proj(_KC0)).astype(BF16)
    cmp_ref[:, 256:512] = proj(_VC0).astype(BF16)
    ks_ref[...] = rope2(proj(_KS0)).astype(BF16)
    kw_ref[...] = rope2(proj(_KW0)).astype(BF16)
    vsw_ref[...] = proj(_VSW0).astype(BF16)
    for c in range(8):
        gm_ref[:, c * 256:(c + 1) * 256] = _sigmoid(proj(_GM0 + c * 256)).astype(BF16)
    gn_ref[...] = _sigmoid(proj(_GN0))


def _pack_w_in(w):
    d = w.shape[0]
    zeros = lambda n: jnp.zeros((d, n), w.dtype)

    def spread_groups(c0):
        return [w[:, c0:c0 + HEAD], zeros(HEAD), w[:, c0 + HEAD:c0 + 2 * HEAD], zeros(HEAD)]

    kv_w = NSA_GROUPS * HEAD
    kv0 = 4 * 512
    kc, vc, ks, vs, kw, vw = (kv0 + kv_w * i for i in range(6))
    gn0 = kv0 + 6 * kv_w
    per_group = NSA_HG * 3
    gm0 = gn0 + NSA_GROUPS * per_group
    assert w.shape[1] == gm0 + 2 * d
    gn = w[:, gn0:gm0]
    cols = ([w[:, :kv0]] + spread_groups(kc) + spread_groups(vc) + spread_groups(ks)
            + spread_groups(kw) + [w[:, vs:vs + kv_w], w[:, vw:vw + kv_w], w[:, gm0:],
                                   gn[:, :per_group], zeros(LANES - per_group),
                                   gn[:, per_group:], zeros(LANES - per_group)])
    packed = jnp.concatenate(cols, axis=1).astype(BF16)
    assert packed.shape[1] == _WCOLS
    return packed


def _rope_tables(seq):
    inv = jnp.power(ROPE_THETA, -jnp.arange(ROPE_HALF, dtype=F32) * 2.0 / (2 * ROPE_HALF))
    ang = jnp.arange(seq, dtype=F32)[:, None] * inv[None, :]
    cos, sin = jnp.cos(ang), jnp.sin(ang)
    one = jnp.ones((seq, HEAD - 2 * ROPE_HALF), F32)
    zero8 = jnp.zeros((seq, ROPE_HALF), F32)
    zero48 = jnp.zeros_like(one)
    rc = jnp.concatenate([cos, cos, one], axis=1)
    rs1 = jnp.concatenate([zero8, sin, zero48], axis=1)
    rs2 = jnp.concatenate([-sin, zero8, zero48], axis=1)
    return tuple(jnp.concatenate([t, t], axis=1) for t in (rc, rs1, rs2))


def _in_proj(x2, norm_g, w_packed, rope_tabs, seq):
    t, d = x2.shape
    tm = 512
    nseq = seq // tm
    row = lambda i: (i, 0)
    fixed = lambda i: (0, 0)
    out_shapes = [
        jax.ShapeDtypeStruct((t, 512), BF16), jax.ShapeDtypeStruct((t, 512), BF16),
        jax.ShapeDtypeStruct((t, 512), BF16), jax.ShapeDtypeStruct((t, 512), BF16),
        jax.ShapeDtypeStruct((t, 512), BF16), jax.ShapeDtypeStruct((t, 256), BF16),
        jax.ShapeDtypeStruct((t, 256), BF16), jax.ShapeDtypeStruct((t, 256), BF16),
        jax.ShapeDtypeStruct((t, 2048), BF16), jax.ShapeDtypeStruct((t, 256), F32)]
    rope_spec = pl.BlockSpec((tm, LANES), lambda i: (i % nseq, 0))
    return pl.pallas_call(
        _inproj_body,
        grid=(t // tm,),
        in_specs=[pl.BlockSpec((tm, d), row), pl.BlockSpec((1, d), fixed),
                  pl.BlockSpec((d, _WCOLS), fixed), rope_spec, rope_spec, rope_spec],
        out_specs=[pl.BlockSpec((tm, s.shape[1]), row) for s in out_shapes],
        out_shape=out_shapes,
        compiler_params=_params("parallel"),
        name="in_proj",
    )(x2, norm_g, w_packed, *rope_tabs)


def _softmax_step(s, vt, m_ref, l_ref, acc_ref):
    m_prev = m_ref[...]
    m_new = jnp.maximum(m_prev, jnp.max(s, axis=0, keepdims=True))
    alpha = jnp.exp2(m_prev - m_new)
    p = jnp.exp2(s - m_new)
    l_ref[...] = alpha * l_ref[...] + jnp.sum(p, axis=0, keepdims=True)
    acc_ref[...] = alpha * acc_ref[...] + jnp.dot(vt, p.astype(BF16), preferred_element_type=F32)
    m_ref[...] = m_new


def _softmax_reset(m_ref, l_ref, acc_ref):
    m_ref[...] = jnp.full(m_ref.shape, NEG_BIG, F32)
    l_ref[...] = jnp.zeros(l_ref.shape, F32)
    acc_ref[...] = jnp.zeros(acc_ref.shape, F32)


def _attend_tiles(n_full, scores, values, mask_last, sa_ref, sb_ref, m_ref, l_ref, acc_ref):
    step = lambda s, t: _softmax_step(s, values(t), m_ref, l_ref, acc_ref)
    sa_ref[...] = scores(0)

    def two_tiles(i, carry):
        t = 2 * i
        sb_ref[...] = scores(t + 1)
        step(sa_ref[...], t)
        sa_ref[...] = scores(t + 2)
        step(sb_ref[...], t + 1)
        return carry

    lax.fori_loop(0, n_full // 2, two_tiles, 0)
    odd = (n_full & 1) == 1

    @pl.when(odd)
    def _():
        sb_ref[...] = scores(n_full)
        step(sa_ref[...], n_full - 1)
        step(mask_last(sb_ref[...]), n_full)

    @pl.when(jnp.logical_not(odd))
    def _():
        step(mask_last(sa_ref[...]), n_full)


def _key_tile(ref, kt, bk):
    return ref[(0,) * (len(ref.shape) - 2) + (pl.ds(pl.multiple_of(kt * bk, bk), bk), slice(None))]


def _value_tile(ref, kt, bk):
    return ref[(0,) * (len(ref.shape) - 2) + (slice(None), pl.ds(pl.multiple_of(kt * bk, bk), bk))]


def _diffattn_body(lambda_init, bq, q_ref, k_ref, v_ref, lq1_ref, lk1_ref, lq2_ref, lk2_ref,
                   sub_ref, y_ref, qbd_ref, m_ref, l_ref, acc_ref, sa_ref, sb_ref):
    qi = pl.program_id(2)
    bk = bq
    qt = q_ref[0]
    sub = lax.broadcasted_iota(I32, qt.shape, 0)
    zero = jnp.zeros_like(qt)
    qbd_ref[:, 0:bq] = jnp.where(sub < HEAD, qt, zero)
    qbd_ref[:, bq:2 * bq] = jnp.where(sub >= HEAD, qt, zero)
    _softmax_reset(m_ref, l_ref, acc_ref)

    def scores(kt):
        return jnp.dot(_key_tile(k_ref, kt, bk), qbd_ref[...], preferred_element_type=F32)

    def causal(s):
        r = lax.broadcasted_iota(I32, (bk, 2 * bq), 0)
        c = lax.broadcasted_iota(I32, (bk, 2 * bq), 1) & (bq - 1)
        return jnp.where(r <= c, s, NEG_BIG)

    _attend_tiles(qi, scores, lambda kt: _value_tile(v_ref, kt, bk), causal,
                  sa_ref, sb_ref, m_ref, l_ref, acc_ref)

    o = acc_ref[...] / l_ref[...]
    lam = (jnp.exp(jnp.sum(lq1_ref[...] * lk1_ref[...], axis=1, keepdims=True))
           - jnp.exp(jnp.sum(lq2_ref[...] * lk2_ref[...], axis=1, keepdims=True)) + lambda_init)
    d = o[:, 0:bq] - lam * o[:, bq:2 * bq]
    ms = jnp.mean(d * d, axis=0, keepdims=True)
    y = d * lax.rsqrt(ms + RMS_EPS) * sub_ref[...] * (1.0 - lambda_init)
    y_ref[0] = y.astype(BF16)


def _diff_attn(qdat, kda, vdat, lq1, lk1, lq2, lk2, subln_col, lambda_init):
    b, s, _ = kda.shape
    bq = 512
    vec = lambda n: pl.BlockSpec((1, n), lambda bi, h, qi: (0, 0))
    dv = 2 * HEAD
    qtile = pl.BlockSpec((1, dv, bq), lambda bi, h, qi: (bi, h, qi))
    return pl.pallas_call(
        functools.partial(_diffattn_body, lambda_init, bq),
        grid=(b, DA_HEADS, s // bq),
        in_specs=[qtile,
                  pl.BlockSpec((1, s, LANES), lambda bi, h, qi: (bi, 0, h)),
                  pl.BlockSpec((1, dv, s), lambda bi, h, qi: (bi, h, 0)),
                  vec(HEAD), vec(HEAD), vec(HEAD), vec(HEAD),
                  pl.BlockSpec((dv, 1), lambda bi, h, qi: (0, 0))],
        out_specs=qtile,
        out_shape=jax.ShapeDtypeStruct((b, DA_HEADS * dv, s), BF16),
        scratch_shapes=[pltpu.VMEM((LANES, 2 * bq), BF16), pltpu.VMEM((1, 2 * bq), F32),
                        pltpu.VMEM((1, 2 * bq), F32), pltpu.VMEM((dv, 2 * bq), F32),
                        pltpu.VMEM((bq, 2 * bq), F32), pltpu.VMEM((bq, 2 * bq), F32)],
        compiler_params=_params("parallel", "parallel", "parallel"),
        name="diff_attn",
    )(qdat, kda, vdat, lq1, lk1, lq2, lk2, subln_col)


def _compress_body(xk_ref, xv_ref, pek_ref, pev_ref, w1k_ref, w1v_ref, w2k_ref, w2v_ref,
                   kc_ref, vc_ref):
    def mlp(x_ref, pe_ref, w1_ref, w2_ref):
        blocks = (x_ref[0].astype(F32) + pe_ref[...]).astype(BF16)
        hid = _gelu(jnp.dot(blocks, w1_ref[...], preferred_element_type=F32))
        return jnp.dot(hid.astype(BF16), w2_ref[...], preferred_element_type=F32)

    kc = mlp(xk_ref, pek_ref, w1k_ref, w2k_ref)
    kc_ref[0] = jnp.concatenate([kc, jnp.zeros_like(kc)], axis=1).astype(BF16)
    vc_ref[0] = mlp(xv_ref, pev_ref, w1v_ref, w2v_ref).astype(BF16)


def _compress(xk, xv, pe_k, pe_v, w1k, w1v, w2k, w2v):
    n, ncp, width = xk.shape
    blk = pl.BlockSpec((1, ncp, width), lambda i: (i, 0, 0))
    fixed = lambda shape: pl.BlockSpec(shape, lambda i: (0, 0))
    return pl.pallas_call(
        _compress_body,
        grid=(n,),
        in_specs=[blk, blk, fixed((1, width)), fixed((1, width)), fixed((width, HEAD)),
                  fixed((width, HEAD)), fixed((HEAD, HEAD)), fixed((HEAD, HEAD))],
        out_specs=[pl.BlockSpec((1, ncp, LANES), lambda i: (i, 0, 0)),
                   pl.BlockSpec((1, ncp, HEAD), lambda i: (i, 0, 0))],
        out_shape=[jax.ShapeDtypeStruct((n, ncp, LANES), BF16),
                   jax.ShapeDtypeStruct((n, ncp, HEAD), BF16)],
        compiler_params=_params("parallel"),
        name="compress",
    )(xk, xv, pe_k, pe_v, w1k, w1v, w2k, w2v)


def _heads_on_lanes(qt, bq):
    return jnp.concatenate([qt[h * HEAD:(h + 1) * HEAD, :] for h in range(NSA_HG)], axis=1)


def _cmpsel_body(bq, q_ref, kc_ref, vc_ref, ovl_ref, ocmp_ref, bias_ref):
    qi = pl.program_id(2)
    ncp = kc_ref.shape[2]
    rows = NSA_HG * bq
    q2 = _heads_on_lanes(q_ref[0], bq)
    qz = jnp.concatenate([q2, jnp.zeros_like(q2)], axis=0)
    s = jnp.dot(kc_ref[0, 0], qz, preferred_element_type=F32)
    n = lax.broadcasted_iota(I32, (ncp, rows), 0)
    qpos = qi * bq + (lax.broadcasted_iota(I32, (ncp, rows), 1) & (bq - 1))
    cmask = n * CMP_STRIDE + (CMP_BLOCK - 1) <= qpos
    s = jnp.where(cmask, s, NEG_BIG)
    e = jnp.exp2(s - jnp.max(s, axis=0, keepdims=True))
    p = jnp.where(cmask, e / jnp.sum(e, axis=0, keepdims=True), 0.0)
    o = jnp.dot(vc_ref[0, 0], p.astype(BF16), preferred_element_type=F32)
    for h in range(NSA_HG):
        ocmp_ref[0, 0, h * HEAD:(h + 1) * HEAD, :] = o[:, h * bq:(h + 1) * bq]

    psum = p[:, 0:bq] + p[:, bq:2 * bq] + p[:, 2 * bq:3 * bq] + p[:, 3 * bq:4 * bq]
    imp = jnp.dot(ovl_ref[...], psum, preferred_element_type=F32)
    blk = lax.broadcasted_iota(I32, (SEL_BLOCK, bq), 0)
    pos = qi * bq + lax.broadcasted_iota(I32, (SEL_BLOCK, bq), 1)
    cur = lax.shift_right_logical(pos, SEL_BLOCK.bit_length() - 1)
    valid = blk <= cur
    forced = (blk == 0) | (blk == cur) | (blk == cur - 1)
    score = jnp.where(valid, imp + jnp.where(forced, FORCE_BONUS, 0.0), -jnp.inf)
    rank = jnp.zeros((SEL_BLOCK, bq), I32)
    for i in range(SEL_BLOCK):
        other = score[i:i + 1, :]
        beats = (other > score) | ((other == score) & (blk > i))
        rank = rank + beats.astype(I32)
    keep = valid & (rank < SEL_TOPK)
    bias_ref[0, 0] = jnp.where(keep, 0.0, SEL_MASK_BIAS).astype(BF16)


def _cmp_select(qnt, kca, vct, overlap_t):
    b, g, ncp, _ = kca.shape
    s = qnt.shape[2]
    bq = 128
    dq = NSA_HG * HEAD
    return pl.pallas_call(
        functools.partial(_cmpsel_body, bq),
        grid=(b, g, s // bq),
        in_specs=[pl.BlockSpec((1, dq, bq), lambda bi, gi, qi: (bi, gi, qi)),
                  pl.BlockSpec((1, 1, ncp, LANES), lambda bi, gi, qi: (bi, gi, 0, 0)),
                  pl.BlockSpec((1, 1, HEAD, ncp), lambda bi, gi, qi: (bi, gi, 0, 0)),
                  pl.BlockSpec((SEL_BLOCK, ncp), lambda bi, gi, qi: (0, 0))],
        out_specs=[pl.BlockSpec((1, 1, dq, bq), lambda bi, gi, qi: (bi, gi, 0, qi)),
                   pl.BlockSpec((1, 1, SEL_BLOCK, bq), lambda bi, gi, qi: (bi, gi, 0, qi))],
        out_shape=[jax.ShapeDtypeStruct((b, g, dq, s), F32),
                   jax.ShapeDtypeStruct((b, g, SEL_BLOCK, s), BF16)],
        compiler_params=_params("parallel", "parallel", "parallel"),
        name="cmp_select",
    )(qnt, kca, vct, overlap_t)


def _selwin_body(bq, bks, q_ref, bias_ref, ks_ref, vs_ref, kw_ref, vw_ref, ocmp_ref, gate_ref,
                 y_ref, m_ref, l_ref, acc_ref, qa_ref, sa_ref, sb_ref):
    qi = pl.program_id(2)
    rows = NSA_HG * bq
    q2 = _heads_on_lanes(q_ref[0], bq)
    bias = bias_ref[0, 0]
    qa = jnp.concatenate([q2, jnp.concatenate([bias] * NSA_HG, axis=1)], axis=0)
    qw = jnp.concatenate([q2, jnp.zeros_like(q2)], axis=0)
    r = lax.broadcasted_iota(I32, (bq, rows), 0)
    c = lax.broadcasted_iota(I32, (bq, rows), 1) & (bq - 1)

    _softmax_reset(m_ref, l_ref, acc_ref)

    qa_ref[...] = qa
    last = (qi * bq) // bks

    def causal(s):
        kpos = last * bks + lax.broadcasted_iota(I32, (bks, rows), 0)
        qpos = qi * bq + (lax.broadcasted_iota(I32, (bks, rows), 1) & (bq - 1))
        return jnp.where(kpos <= qpos, s, NEG_BIG)

    _attend_tiles(last,
                  lambda kt: jnp.dot(_key_tile(ks_ref, kt, bks), qa_ref[...],
                                     preferred_element_type=F32),
                  lambda kt: _value_tile(vs_ref, kt, bks), causal,
                  sa_ref, sb_ref, m_ref, l_ref, acc_ref)
    o_sel = acc_ref[...] / l_ref[...]

    _softmax_reset(m_ref, l_ref, acc_ref)
    backs = list(range(WINDOW // bq, -1, -1))
    tiles = [jnp.maximum(qi - back, 0) for back in backs]
    raw = [jnp.dot(_key_tile(kw_ref, kt, bq), qw, preferred_element_type=F32) for kt in tiles]
    for back, kt, s in zip(backs, tiles, raw):
        if back >= 1 and (back + 1) * bq <= WINDOW:
            inside = qi >= back
        else:
            dist = c + back * bq - r
            inside = (dist >= 0) & (dist < WINDOW) & (qi >= back)
        _softmax_step(jnp.where(inside, s, NEG_BIG), _value_tile(vw_ref, kt, bq),
                      m_ref, l_ref, acc_ref)

    o_win = acc_ref[...] / l_ref[...]
    gate = gate_ref[0]
    for h in range(NSA_HG):
        sl = slice(h * bq, (h + 1) * bq)
        y = (gate[3 * h:3 * h + 1, :] * ocmp_ref[0, 0, h * HEAD:(h + 1) * HEAD, :]
             + gate[3 * h + 1:3 * h + 2, :] * o_sel[:, sl]
             + gate[3 * h + 2:3 * h + 3, :] * o_win[:, sl])
        y_ref[0, h * HEAD:(h + 1) * HEAD, :] = y.astype(BF16)


def _sel_win(qnt, bias_t, ksa, kwa, vswt, ocmp_t, gnt):
    b, s, _ = ksa.shape
    g = NSA_GROUPS
    bq, bks = 256, 512
    dq = NSA_HG * HEAD
    rows = NSA_HG * bq
    keys = pl.BlockSpec((1, s, LANES), lambda bi, gi, qi: (bi, 0, gi))
    vals = lambda first: pl.BlockSpec((1, HEAD, s), lambda bi, gi, qi: (bi, first + gi, 0))
    qtile = pl.BlockSpec((1, dq, bq), lambda bi, gi, qi: (bi, gi, qi))
    return pl.pallas_call(
        functools.partial(_selwin_body, bq, bks),
        grid=(b, g, s // bq),
        in_specs=[qtile,
                  pl.BlockSpec((1, 1, SEL_BLOCK, bq), lambda bi, gi, qi: (bi, gi, 0, qi)),
                  keys, vals(0), keys, vals(g),
                  pl.BlockSpec((1, 1, dq, bq), lambda bi, gi, qi: (bi, gi, 0, qi)),
                  pl.BlockSpec((1, LANES, bq), lambda bi, gi, qi: (bi, gi, qi))],
        out_specs=qtile,
        out_shape=jax.ShapeDtypeStruct((b, g * dq, s), BF16),
        scratch_shapes=[pltpu.VMEM((1, rows), F32), pltpu.VMEM((1, rows), F32),
                        pltpu.VMEM((HEAD, rows), F32), pltpu.VMEM((LANES, rows), BF16),
                        pltpu.VMEM((bks, rows), F32), pltpu.VMEM((bks, rows), F32)],
        compiler_params=_params("parallel", "parallel", "parallel"),
        name="sel_win",
    )(qnt, bias_t, ksa, vswt, kwa, vswt, ocmp_t, gnt)


def _outproj_body(yda_ref, yn_ref, gm_ref, x_ref, pda_ref, pnsa_ref, wo_ref, fg_ref, wq_ref,
                  k1_ref, k2_ref, h1_ref, c_ref, s1_ref, s2_ref):
    d = x_ref.shape[1]
    a = jnp.dot(yda_ref[...], pda_ref[...], preferred_element_type=F32)
    bn = jnp.dot(yn_ref[...], pnsa_ref[...], preferred_element_type=F32)
    merged = gm_ref[:, :d].astype(F32) * a + gm_ref[:, d:].astype(F32) * bn
    h1 = x_ref[...] + jnp.dot(merged.astype(BF16), wo_ref[...], preferred_element_type=F32)
    c = _rmsnorm(h1, fg_ref[...])
    _rows_to_tiles(h1, h1_ref)
    _rows_to_tiles(c, c_ref)
    cb = c.astype(BF16)
    for h in range(PEER_HEADS):
        qh = jnp.dot(cb, wq_ref[:, h * 256:(h + 1) * 256], preferred_element_type=F32).astype(BF16)
        s1_ref[h] = lax.dot_general(k1_ref[...], qh[:, :LANES], NT_DIMS, preferred_element_type=F32)
        s2_ref[h] = lax.dot_general(k2_ref[...], qh[:, LANES:], NT_DIMS, preferred_element_type=F32)


def _out_proj(yda, yn, gm, x2, pda, pnsa, wo, ffn_g, wq, k1, k2):
    t, d = x2.shape
    tm = 512
    row = lambda w: pl.BlockSpec((tm, w), lambda i: (i, 0))
    fixed = lambda a: pl.BlockSpec(a.shape, lambda i: (0, 0))
    sspec = pl.BlockSpec((PEER_HEADS, PEER_NKEYS, tm), lambda i: (0, 0, i))
    assert d == SUBLANES * LANES
    tiles = pl.BlockSpec((tm, SUBLANES, LANES), lambda i: (i, 0, 0))
    return pl.pallas_call(
        _outproj_body,
        grid=(t // tm,),
        in_specs=[row(512), row(512), row(2 * d), row(d), fixed(pda), fixed(pnsa), fixed(wo),
                  fixed(ffn_g), fixed(wq), fixed(k1), fixed(k2)],
        out_specs=[tiles, tiles, sspec, sspec],
        out_shape=[jax.ShapeDtypeStruct((t, SUBLANES, LANES), F32),
                   jax.ShapeDtypeStruct((t, SUBLANES, LANES), F32),
                   jax.ShapeDtypeStruct((PEER_HEADS, PEER_NKEYS, t), F32),
                   jax.ShapeDtypeStruct((PEER_HEADS, PEER_NKEYS, t), F32)],
        compiler_params=_params("parallel"),
        name="out_proj",
    )(yda, yn, gm, x2, pda, pnsa, wo, ffn_g, wq, k1, k2)


def _batcher_pairs(n):
    pairs = []

    def merge(lo, hi, r):
        step = r * 2
        if step < hi - lo:
            merge(lo, hi, step)
            merge(lo + r, hi, step)
            pairs.extend((i, i + r) for i in range(lo + r, hi - r, step))
        else:
            pairs.append((lo, lo + r))

    def sort(lo, hi):
        if hi - lo >= 1:
            mid = lo + (hi - lo) // 2
            sort(lo, mid)
            sort(mid + 1, hi)
            merge(lo, hi, 1)

    sort(0, n - 1)
    return pairs


_NET16 = _batcher_pairs(PEER_TOPK)


def _cmpx(a, b):
    c = (a[0] > b[0]) | ((a[0] == b[0]) & (a[1] < b[1]))
    return ((jnp.where(c, a[0], b[0]), jnp.where(c, a[1], b[1])),
            (jnp.where(c, b[0], a[0]), jnp.where(c, b[1], a[1])))


def _sort_lists(lists, n_real):
    lists = list(lists)
    for i, j in _NET16:
        if j < n_real:
            lists[i], lists[j] = _cmpx(lists[i], lists[j])
    return lists


def _merge_top(a, b):
    k = PEER_TOPK
    lists = [_cmpx(a[i], b[k - 1 - i])[0] for i in range(k)]
    step = k // 2
    while step >= 1:
        for i in range(k):
            if i & step == 0:
                lists[i], lists[i + step] = _cmpx(lists[i], lists[i + step])
        step //= 2
    return lists


def _top16(pairs):
    k = PEER_TOPK
    filler = (jnp.full(pairs[0][0].shape, -jnp.inf, F32), jnp.zeros(pairs[0][1].shape, I32))
    groups = []
    for g0 in range(0, len(pairs), k):
        chunk = list(pairs[g0:g0 + k])
        groups.append(_sort_lists(chunk + [filler] * (k - len(chunk)), len(chunk)))
    while len(groups) > 1:
        groups = [_merge_top(groups[i], groups[i + 1]) if i + 1 < len(groups) else groups[i]
                  for i in range(0, len(groups), 2)]
    return groups[0]


def _product_key_select(s1, s2):
    k = PEER_TOPK
    n_exp = PEER_NKEYS * PEER_NKEYS
    ids = lambda n: [jnp.full(s1[0].shape, i, I32) for i in range(n)]
    l1 = _top16(list(zip(s1, ids(len(s1)))))
    l2 = _top16(list(zip(s2, ids(len(s2)))))
    pair = lambda a, b: (l1[a][0] + l2[b][0],
                         (a * k + b) * n_exp + l1[a][1] * PEER_NKEYS + l2[b][1])
    first_row = [pair(0, b) for b in range(k)]
    rest = [pair(a, b) for a in range(1, k) for b in range(k // (a + 1))]
    best = _merge_top(first_row, _top16(rest))
    return [(v, p & (n_exp - 1)) for v, p in best]


def _peertopk_body(tt, s1_ref, s2_ref, idx_ref, gate_ref):
    nblk = tt // LANES
    assert nblk == SUBLANES

    def keys_major(ref, h):
        tiles = []
        for kg in range(PEER_NKEYS // SUBLANES):
            rows = slice(kg * SUBLANES, (kg + 1) * SUBLANES)
            tiles += _sublane_transpose([ref[h, rows, b * LANES:(b + 1) * LANES]
                                         for b in range(nblk)])
        return tiles

    def store(ref, h, tiles):
        for g in range(PEER_TOPK // SUBLANES):
            blocks = _sublane_transpose(tiles[g * SUBLANES:(g + 1) * SUBLANES])
            for b in range(nblk):
                ref[h, g * SUBLANES:(g + 1) * SUBLANES, b * LANES:(b + 1) * LANES] = blocks[b]

    def head(h, carry):
        best = _product_key_select(keys_major(s1_ref, h), keys_major(s2_ref, h))
        ex = [jnp.exp(v - best[0][0]) for v, _ in best]
        z = ex[0]
        for e in ex[1:]:
            z = z + e
        store(gate_ref, h, [e / z for e in ex])
        store(idx_ref, h, [ix for _, ix in best])
        return carry

    lax.fori_loop(0, PEER_HEADS, head, 0)


def _peer_topk(s1t, s2t):
    _, _, t = s1t.shape
    tt = SUBLANES * LANES
    spec_in = pl.BlockSpec((PEER_HEADS, PEER_NKEYS, tt), lambda i: (0, 0, i))
    spec_out = pl.BlockSpec((PEER_HEADS, PEER_TOPK, tt), lambda i: (0, 0, i))
    return pl.pallas_call(
        functools.partial(_peertopk_body, tt),
        grid=(t // tt,),
        in_specs=[spec_in, spec_in],
        out_specs=[spec_out, spec_out],
        out_shape=[jax.ShapeDtypeStruct((PEER_HEADS, PEER_TOPK, t), I32),
                   jax.ShapeDtypeStruct((PEER_HEADS, PEER_TOPK, t), F32)],
        compiler_params=_params("parallel"),
        name="peer_topk",
    )(s1t, s2t)


PEER_E = PEER_HEADS * PEER_TOPK
PEER_RING = 4
PEER_MID_ROWS = 48


def _pack_body(u_ref, v_ref, out_ref):
    bf16_bits = lambda a: lax.bitcast_convert_type(a.astype(BF16).astype(F32), I32)
    words = bf16_bits(u_ref[...]) | lax.shift_right_logical(bf16_bits(v_ref[...]), jnp.int32(16))
    _rows_to_tiles(words, out_ref)


def _pack_expert_rows(pu, pv):
    n, d = pu.shape
    assert d == SUBLANES * LANES
    te = 256
    blk = pl.BlockSpec((te, d), lambda i: (i, 0))
    return pl.pallas_call(
        _pack_body,
        grid=(n // te,),
        in_specs=[blk, blk],
        out_specs=pl.BlockSpec((te, SUBLANES, LANES), lambda i: (i, 0, 0)),
        out_shape=jax.ShapeDtypeStruct((n, SUBLANES, LANES), I32),
        compiler_params=_params("parallel"),
        name="pack_experts",
    )(pu, pv)


def _word_hi(w):
    return lax.bitcast_convert_type(w & jnp.int32(-65536), F32)


def _word_lo(w):
    return lax.bitcast_convert_type(lax.shift_left(w, jnp.int32(16)), F32)


def _sublane_sums(a, sub):
    for dist in (4, 2, 1):
        low = (sub & dist) == 0
        half = len(a) // 2
        a = [jnp.where(low, a[i], pltpu.roll(a[i + half], dist, 0))
             + jnp.where(low, pltpu.roll(a[i], SUBLANES - dist, 0), a[i + half])
             for i in range(half)]
    return a[0]


def _peer_pair_math(expert_u, expert_v, x8, gates, store, issue_some, wcol_ref):
    sub = lax.broadcasted_iota(I32, (SUBLANES, LANES), 0)
    eye = (lax.broadcasted_iota(I32, (PEER_E, LANES), 0)
           == lax.broadcasted_iota(I32, (PEER_E, LANES), 1))
    ones_rows = jnp.ones((SUBLANES, LANES), BF16)
    ones_sq = jnp.ones((LANES, LANES), BF16)
    ngroup = PEER_E // SUBLANES

    def hidden(a):
        groups = []
        for g in range(ngroup):
            prods = [expert_u(a, g * SUBLANES + r) * x8[a] for r in range(SUBLANES)]
            groups.append(_sublane_sums(prods, sub))
            issue_some(3 - g % 2)
        return jnp.concatenate(groups, axis=0)

    def expert_weights(a, q):
        q_hi = q.astype(BF16)
        q_lo = (q - q_hi.astype(F32)).astype(BF16)
        hid = (lax.dot_general(ones_rows, q_hi, NT_DIMS, preferred_element_type=F32)
               + lax.dot_general(ones_rows, q_lo, NT_DIMS, preferred_element_type=F32))
        issue_some(PEER_MID_ROWS // 2)
        w = _gelu(hid[0:1]) * gates[a]
        wd = jnp.where(eye, jnp.broadcast_to(w, (PEER_E, LANES)), 0.0).astype(BF16)
        wcol_ref[a] = jnp.dot(wd, ones_sq, preferred_element_type=F32)
        issue_some(PEER_MID_ROWS // 2)

    def combine(a):
        out = jnp.zeros((SUBLANES, LANES), F32)
        for g in range(ngroup):
            for r in range(SUBLANES):
                j = g * SUBLANES + r
                out = out + wcol_ref[a, j:j + 1, :] * expert_v(a, j)
            issue_some(3 - g % 2)
        store(a, out)

    expert_weights(0, hidden(0))
    expert_weights(1, hidden(1))
    combine(0)
    combine(1)


def _peer_finish(h1_ref, acc_ref, fg_ref, out_ref):
    hsum = h1_ref[...] + acc_ref[...]
    ms = jnp.mean(hsum * hsum, axis=(1, 2), keepdims=True)
    _tiles_to_rows(hsum * lax.rsqrt(ms + RMS_EPS) * fg_ref[...], out_ref)


def _peerffn_body(tb, idx_ref, gate_ref, c_ref, h1_ref, fg_ref, uv_ref, out_ref, buf_ref,
                  acc_ref, wcol_ref, sem_ref):
    npairs = tb // 2
    ahead = PEER_RING - 1

    def row_copy(t, j, slot):
        return pltpu.make_async_copy(uv_ref.at[idx_ref[t, j]], buf_ref.at[slot, j],
                                     sem_ref.at[slot])

    def wait_all(slot):
        pltpu.make_async_copy(uv_ref.at[pl.ds(0, PEER_E)], buf_ref.at[slot],
                              sem_ref.at[slot]).wait()

    def ring(phase):
        base = 2 * (phase % PEER_RING)
        return (base, base + 1)

    def pair(i, phase, prefetch):
        toks = (2 * i, 2 * i + 1)
        slots = ring(phase)
        nslots = ring(phase + ahead)
        todo = [(a, j) for a in range(2) for j in range(PEER_E)]

        def issue_some(n):
            for a, j in todo[:n]:
                if prefetch:
                    row_copy(toks[a] + 2 * ahead, j, nslots[a]).start(priority=j % 2)
            del todo[:n]

        wait_all(slots[0])
        wait_all(slots[1])

        def store(a, out):
            acc_ref[toks[a]] = out

        _peer_pair_math(
            lambda a, j: _word_hi(buf_ref[slots[a], j]),
            lambda a, j: _word_lo(buf_ref[slots[a], j]),
            [c_ref[t] for t in toks], [gate_ref[pl.ds(t, 1), :] for t in toks], store, issue_some,
            wcol_ref)
        assert not todo, "every prefetch row DMA must be issued exactly once"

    assert npairs % PEER_RING == 0
    for i in range(ahead):
        for a, slot in enumerate(ring(i)):
            for j in range(PEER_E):
                row_copy(2 * i + a, j, slot).start(priority=j % 2)

    def body(g, carry):
        for phase in range(PEER_RING):
            pair(PEER_RING * g + phase, phase, True)
        return carry

    ngroups = npairs // PEER_RING
    lax.fori_loop(0, ngroups - 1, body, 0)
    for phase in range(PEER_RING):
        i = PEER_RING * (ngroups - 1) + phase
        pair(i, phase, i + ahead < npairs)
    _peer_finish(h1_ref, acc_ref, fg_ref, out_ref)


def _peer_ffn(idx, gate, c3, h13, final_g3, uv_tiles):
    t = h13.shape[0]
    tb = 512
    row3 = pl.BlockSpec((tb, SUBLANES, LANES), lambda i: (i, 0, 0))
    return pl.pallas_call(
        functools.partial(_peerffn_body, tb),
        grid=(t // tb,),
        in_specs=[pl.BlockSpec((tb, PEER_E), lambda i: (i, 0), memory_space=pltpu.SMEM),
                  pl.BlockSpec((tb, PEER_E), lambda i: (i, 0)), row3, row3,
                  pl.BlockSpec((1, SUBLANES, LANES), lambda i: (0, 0, 0)),
                  pl.BlockSpec(memory_space=pl.ANY)],
        out_specs=pl.BlockSpec((tb, SUBLANES * LANES), lambda i: (i, 0)),
        out_shape=jax.ShapeDtypeStruct((t, SUBLANES * LANES), F32),
        scratch_shapes=[pltpu.VMEM((2 * PEER_RING, PEER_E, SUBLANES, LANES), I32),
                        pltpu.VMEM((tb, SUBLANES, LANES), F32),
                        pltpu.VMEM((2, PEER_E, LANES), F32),
                        pltpu.SemaphoreType.DMA((2 * PEER_RING,))],
        compiler_params=_params("arbitrary"),
        name="peer_ffn",
    )(idx, gate, c3, h13, final_g3, uv_tiles)


def _overlap_table(seq):
    ci = jnp.arange(seq // CMP_STRIDE)[None, :] * CMP_STRIDE
    sj = jnp.arange(SEL_BLOCK)[:, None] * SEL_BLOCK
    return ((ci < sj + SEL_BLOCK) & (ci + CMP_BLOCK > sj)).astype(F32)


def _cmp_blocks(kv):
    b, g, s, dh = kv.shape
    r = kv.reshape(b * g, s // CMP_STRIDE, CMP_STRIDE * dh)
    return jnp.concatenate([r, jnp.roll(r, -1, axis=1)], axis=-1)


def _pad_cmp_params(pe, w1):
    pe_p = jnp.pad(pe, ((0, 0), (0, LANES - HEAD))).reshape(1, -1)
    w1_p = jnp.pad(w1.reshape(CMP_BLOCK, HEAD, -1), ((0, 0), (0, LANES - HEAD), (0, 0)))
    return pe_p, w1_p.reshape(CMP_BLOCK * LANES, -1).astype(BF16)


def _layer(h, lidx, attn_norm, w_in, lq1, lk1, lq2, lk2, subln, pe_k, pe_v, w1k, w1v, w2k, w2v,
           p_da, p_nsa, w_o, ffn_norm, wq, k1, k2, pu, pv, out_norm):
    b, s, d = h.shape
    t = b * s
    g, hg = NSA_GROUPS, NSA_HG
    lambda_init = 0.8 - 0.6 * math.exp(-0.3 * lidx)
    x2 = h.reshape(t, d)

    qda, kda, vda, qn, cmp_in, ksx, kwx, vsw, gm, gn = _in_proj(
        x2, attn_norm.reshape(1, d), _pack_w_in(w_in), _rope_tables(s), s)
    tr = lambda a2: jnp.swapaxes(a2.reshape(b, s, -1), 1, 2)
    ydat = _diff_attn(tr(qda), kda.reshape(b, s, -1), tr(vda),
                      lq1.reshape(1, -1), lk1.reshape(1, -1), lq2.reshape(1, -1),
                      lk2.reshape(1, -1), subln.reshape(-1, 1), lambda_init)
    yda = jnp.swapaxes(ydat, 1, 2).reshape(t, -1)

    assert s // SEL_BLOCK <= SEL_BLOCK, "selection bias rows hold at most 64 blocks"
    cmp4 = jnp.swapaxes(cmp_in.reshape(b, s, 2 * g, LANES), 1, 2)
    pe_kp, w1_kp = _pad_cmp_params(pe_k, w1k)
    pe_vp, w1_vp = _pad_cmp_params(pe_v, w1v)
    kca, vc = _compress(_cmp_blocks(cmp4[:, :g]), _cmp_blocks(cmp4[:, g:]), pe_kp, pe_vp,
                        w1_kp, w1_vp, w2k.astype(BF16), w2v.astype(BF16))
    ncp = s // CMP_STRIDE
    kca = kca.reshape(b, g, ncp, LANES)
    vct = jnp.swapaxes(vc.reshape(b, g, ncp, HEAD), 2, 3)

    qnt = tr(qn)
    ocmp_t, bias_t = _cmp_select(qnt, kca, vct, _overlap_table(s))
    onehot = (jnp.arange(s)[:, None] // SEL_BLOCK == jnp.arange(HEAD)[None, :]).astype(BF16)
    pad_hot = jnp.concatenate([jnp.zeros_like(onehot), onehot] * g, axis=1)
    ksa = ksx.reshape(b, s, -1) + pad_hot[None]
    ynt = _sel_win(qnt, bias_t, ksa, kwx.reshape(b, s, -1), tr(vsw), ocmp_t, tr(gn))
    yn = jnp.swapaxes(ynt, 1, 2).reshape(t, -1)

    h1_tiles, c_tiles, s1t, s2t = _out_proj(
        yda, yn, gm, x2, p_da.astype(BF16), p_nsa.astype(BF16), w_o.astype(BF16),
        ffn_norm.reshape(1, d), wq.astype(BF16), k1.astype(BF16), k2.astype(BF16))
    idx_t, gate_t = _peer_topk(s1t, s2t)
    idx = idx_t.reshape(PEER_E, t).T
    gate = gate_t.reshape(PEER_E, t).T
    out = _peer_ffn(idx, gate, c_tiles, h1_tiles, out_norm.reshape(1, SUBLANES, LANES),
                    _pack_expert_rows(pu, pv))
    return out.reshape(b, s, d)


def kernel(x, attn_norm, w_in, da_lambda_q1, da_lambda_k1, da_lambda_q2, da_lambda_k2, da_subln,
           cmp_pe_k, cmp_pe_v, cmp_w1_k, cmp_w1_v, cmp_w2_k, cmp_w2_v, p_da, p_nsa, w_o,
           ffn_norm, peer_wq, peer_k1, peer_k2, peer_u, peer_v, final_norm):
    depth = attn_norm.shape[0]
    assert depth == 1, "the final norm is fused into the last layer's PEER kernel"
    h = x
    for l in range(depth):
        h = _layer(h, l, attn_norm[l], w_in[l], da_lambda_q1[l], da_lambda_k1[l], da_lambda_q2[l],
                   da_lambda_k2[l], da_subln[l], cmp_pe_k[l], cmp_pe_v[l], cmp_w1_k[l],
                   cmp_w1_v[l], cmp_w2_k[l], cmp_w2_v[l], p_da[l], p_nsa[l], w_o[l], ffn_norm[l],
                   peer_wq[l], peer_k1[l], peer_k2[l], peer_u[l], peer_v[l], final_norm)
    return h
```

```python
import functools
import math

import jax
import jax.numpy as jnp
from jax import lax
from jax.experimental import pallas as pl
from jax.experimental.pallas import tpu as pltpu

F32 = jnp.float32
BF16 = jnp.bfloat16
I32 = jnp.int32

RMS_EPS = 1e-6
ROPE_THETA = 500000.0
ROPE_HALF = 8
HEAD = 64
DA_HEADS = 4
NSA_GROUPS = 2
NSA_HG = 4
CMP_STRIDE = 16
CMP_BLOCK = 32
SEL_BLOCK = 64
SEL_TOPK = 16
WINDOW = 512
FORCE_BONUS = 1e4
NEG_BIG = -1e30
SEL_MASK_BIAS = -2.0 ** 100
ATTN_SCALE = HEAD ** -0.5 * math.log2(math.e)
PEER_HEADS = 8
PEER_NKEYS = 128
PEER_TOPK = 16
LANES = 128
SUBLANES = 8
VMEM_LIMIT = 56 * 1024 * 1024

NT_DIMS = (((1,), (1,)), ((), ()))


def _rmsnorm(x, g):
    return x * lax.rsqrt(jnp.mean(x * x, axis=-1, keepdims=True) + RMS_EPS) * g


def _sigmoid(z):
    return 1.0 / (1.0 + jnp.exp(-z))


def _gelu(z):
    return 0.5 * z * (1.0 + lax.erf(z * (2.0 ** -0.5)))


def _params(*sem):
    return pltpu.CompilerParams(dimension_semantics=sem, vmem_limit_bytes=VMEM_LIMIT)


def _sublane_transpose(v):
    sub = lax.broadcasted_iota(I32, (SUBLANES, LANES), 0)
    v = list(v)
    for dist in (4, 2, 1):
        low = (sub & dist) == 0
        nxt = list(v)
        for i in range(SUBLANES):
            if i & dist == 0:
                nxt[i] = jnp.where(low, v[i], pltpu.roll(v[i + dist], dist, 0))
                nxt[i + dist] = jnp.where(low, pltpu.roll(v[i], SUBLANES - dist, 0), v[i + dist])
        v = nxt
    return v


def _rows_to_tiles(x, tile_ref):
    for g in range(x.shape[0] // SUBLANES):
        rows = slice(g * SUBLANES, (g + 1) * SUBLANES)
        tiles = _sublane_transpose([x[rows, c * LANES:(c + 1) * LANES] for c in range(SUBLANES)])
        for e in range(SUBLANES):
            tile_ref[g * SUBLANES + e] = tiles[e]


def _tiles_to_rows(t, row_ref):
    for g in range(t.shape[0] // SUBLANES):
        chunks = _sublane_transpose([t[g * SUBLANES + e] for e in range(SUBLANES)])
        for c in range(SUBLANES):
            row_ref[g * SUBLANES:(g + 1) * SUBLANES, c * LANES:(c + 1) * LANES] = chunks[c]


_QDA0, _KDA0, _VDA0, _QN0 = 0, 512, 1024, 1536
_KC0, _VC0, _KS0, _KW0, _VSW0, _GM0, _GN0, _WCOLS = 2048, 2304, 2560, 2816, 3072, 3328, 5376, 5632


def _inproj_body(x_ref, g_ref, w_ref, rc_ref, rs1_ref, rs2_ref,
                 qda_ref, kda_ref, vda_ref, qn_ref, cmp_ref, ks_ref, kw_ref, vsw_ref, gm_ref, gn_ref):
    a = _rmsnorm(x_ref[...], g_ref[...]).astype(BF16)
    rc, rs1, rs2 = rc_ref[...], rs1_ref[...], rs2_ref[...]

    def rope(z):
        return (z * rc + pltpu.roll(z, ROPE_HALF, 1) * rs1
                + pltpu.roll(z, LANES - ROPE_HALF, 1) * rs2)

    def proj(c0):
        return jnp.dot(a, w_ref[:, c0:c0 + 256], preferred_element_type=F32)

    def rope2(z):
        return jnp.concatenate([rope(z[:, :LANES]), rope(z[:, LANES:])], axis=1)

    for c in range(2):
        cols = slice(c * 256, (c + 1) * 256)
        qda_ref[0, cols, :] = (rope2(proj(_QDA0 + c * 256)) * ATTN_SCALE).T.astype(BF16)
        kda_ref[:, cols] = rope2(proj(_KDA0 + c * 256)).astype(BF16)
        vda_ref[0, cols, :] = proj(_VDA0 + c * 256).T.astype(BF16)
        qn_ref[0, cols, :] = (rope2(proj(_QN0 + c * 256)) * ATTN_SCALE).T.astype(BF16)
    cmp_ref[:, 0:256] = rope2(proj(_KC0)).astype(BF16)
    cmp_ref[:, 256:512] = proj(_VC0).astype(BF16)
    ks_ref[...] = rope2(proj(_KS0)).astype(BF16)
    kw_ref[...] = rope2(proj(_KW0)).astype(BF16)
    vsw_ref[0] = proj(_VSW0).T.astype(BF16)
    for c in range(8):
        gm_ref[:, c * 256:(c + 1) * 256] = _sigmoid(proj(_GM0 + c * 256)).astype(BF16)
    gn_ref[0] = _sigmoid(proj(_GN0)).T


def _pack_w_in(w):
    d = w.shape[0]
    zeros = lambda n: jnp.zeros((d, n), w.dtype)

    def spread_groups(c0):
        return [w[:, c0:c0 + HEAD], zeros(HEAD), w[:, c0 + HEAD:c0 + 2 * HEAD], zeros(HEAD)]

    kv_w = NSA_GROUPS * HEAD
    kv0 = 4 * 512
    kc, vc, ks, vs, kw, vw = (kv0 + kv_w * i for i in range(6))
    gn0 = kv0 + 6 * kv_w
    per_group = NSA_HG * 3
    gm0 = gn0 + NSA_GROUPS * per_group
    assert w.shape[1] == gm0 + 2 * d
    gn = w[:, gn0:gm0]
    cols = ([w[:, :kv0]] + spread_groups(kc) + spread_groups(vc) + spread_groups(ks)
            + spread_groups(kw) + [w[:, vs:vs + kv_w], w[:, vw:vw + kv_w], w[:, gm0:],
                                   gn[:, :per_group], zeros(LANES - per_group),
                                   gn[:, per_group:], zeros(LANES - per_group)])
    packed = jnp.concatenate(cols, axis=1).astype(BF16)
    assert packed.shape[1] == _WCOLS
    return packed


def _rope_tables(seq):
    inv = jnp.power(ROPE_THETA, -jnp.arange(ROPE_HALF, dtype=F32) * 2.0 / (2 * ROPE_HALF))
    ang = jnp.arange(seq, dtype=F32)[:, None] * inv[None, :]
    cos, sin = jnp.cos(ang), jnp.sin(ang)
    one = jnp.ones((seq, HEAD - 2 * ROPE_HALF), F32)
    zero8 = jnp.zeros((seq, ROPE_HALF), F32)
    zero48 = jnp.zeros_like(one)
    rc = jnp.concatenate([cos, cos, one], axis=1)
    rs1 = jnp.concatenate([zero8, sin, zero48], axis=1)
    rs2 = jnp.concatenate([-sin, zero8, zero48], axis=1)
    return tuple(jnp.concatenate([t, t], axis=1) for t in (rc, rs1, rs2))


def _in_proj(x2, norm_g, w_packed, rope_tabs, seq):
    t, d = x2.shape
    tm = 512
    nseq = seq // tm
    row = lambda i: (i, 0)
    fixed = lambda i: (0, 0)
    b = t // seq
    outs = [(512, BF16, True), (512, BF16, False), (512, BF16, True), (512, BF16, True),
            (512, BF16, False), (256, BF16, False), (256, BF16, False), (256, BF16, True),
            (2048, BF16, False), (256, F32, True)]
    out_shapes = [jax.ShapeDtypeStruct((b, w, seq) if tr else (t, w), dt) for w, dt, tr in outs]
    out_specs = [pl.BlockSpec((1, w, tm), lambda i: (i // nseq, 0, i % nseq)) if tr
                 else pl.BlockSpec((tm, w), row) for w, dt, tr in outs]
    rope_spec = pl.BlockSpec((tm, LANES), lambda i: (i % nseq, 0))
    return pl.pallas_call(
        _inproj_body,
        grid=(t // tm,),
        in_specs=[pl.BlockSpec((tm, d), row), pl.BlockSpec((1, d), fixed),
                  pl.BlockSpec((d, _WCOLS), fixed), rope_spec, rope_spec, rope_spec],
        out_specs=out_specs,
        out_shape=out_shapes,
        compiler_params=_params("parallel"),
        name="in_proj",
    )(x2, norm_g, w_packed, *rope_tabs)


def _softmax_step(s, vt, m_ref, l_ref, acc_ref):
    m_prev = m_ref[...]
    m_new = jnp.maximum(m_prev, jnp.max(s, axis=0, keepdims=True))
    alpha = jnp.exp2(m_prev - m_new)
    p = jnp.exp2(s - m_new)
    l_ref[...] = alpha * l_ref[...] + jnp.sum(p, axis=0, keepdims=True)
    acc_ref[...] = alpha * acc_ref[...] + jnp.dot(vt, p.astype(BF16), preferred_element_type=F32)
    m_ref[...] = m_new


def _softmax_reset(m_ref, l_ref, acc_ref):
    m_ref[...] = jnp.full(m_ref.shape, NEG_BIG, F32)
    l_ref[...] = jnp.zeros(l_ref.shape, F32)
    acc_ref[...] = jnp.zeros(acc_ref.shape, F32)


def _attend_tiles(n_full, scores, values, mask_last, sa_ref, sb_ref, m_ref, l_ref, acc_ref):
    step = lambda s, t: _softmax_step(s, values(t), m_ref, l_ref, acc_ref)
    sa_ref[...] = scores(0)

    def two_tiles(i, carry):
        t = 2 * i
        sb_ref[...] = scores(t + 1)
        step(sa_ref[...], t)
        sa_ref[...] = scores(t + 2)
        step(sb_ref[...], t + 1)
        return carry

    lax.fori_loop(0, n_full // 2, two_tiles, 0)
    odd = (n_full & 1) == 1

    @pl.when(odd)
    def _():
        sb_ref[...] = scores(n_full)
        step(sa_ref[...], n_full - 1)
        step(mask_last(sb_ref[...]), n_full)

    @pl.when(jnp.logical_not(odd))
    def _():
        step(mask_last(sa_ref[...]), n_full)


def _key_tile(ref, kt, bk):
    return ref[(0,) * (len(ref.shape) - 2) + (pl.ds(pl.multiple_of(kt * bk, bk), bk), slice(None))]


def _value_tile(ref, kt, bk):
    return ref[(0,) * (len(ref.shape) - 2) + (slice(None), pl.ds(pl.multiple_of(kt * bk, bk), bk))]


def _diffattn_body(lambda_init, bq, q_ref, k_ref, v_ref, lq1_ref, lk1_ref, lq2_ref, lk2_ref,
                   sub_ref, y_ref, qbd_ref, m_ref, l_ref, acc_ref, sa_ref, sb_ref):
    qi = pl.program_id(2)
    bk = bq
    qt = q_ref[0]
    sub = lax.broadcasted_iota(I32, qt.shape, 0)
    zero = jnp.zeros_like(qt)
    qbd_ref[:, 0:bq] = jnp.where(sub < HEAD, qt, zero)
    qbd_ref[:, bq:2 * bq] = jnp.where(sub >= HEAD, qt, zero)
    _softmax_reset(m_ref, l_ref, acc_ref)

    def scores(kt):
        return jnp.dot(_key_tile(k_ref, kt, bk), qbd_ref[...], preferred_element_type=F32)

    def causal(s):
        r = lax.broadcasted_iota(I32, (bk, 2 * bq), 0)
        c = lax.broadcasted_iota(I32, (bk, 2 * bq), 1) & (bq - 1)
        return jnp.where(r <= c, s, NEG_BIG)

    _attend_tiles(qi, scores, lambda kt: _value_tile(v_ref, kt, bk), causal,
                  sa_ref, sb_ref, m_ref, l_ref, acc_ref)

    o = acc_ref[...] / l_ref[...]
    lam = (jnp.exp(jnp.sum(lq1_ref[...] * lk1_ref[...], axis=1, keepdims=True))
           - jnp.exp(jnp.sum(lq2_ref[...] * lk2_ref[...], axis=1, keepdims=True)) + lambda_init)
    d = o[:, 0:bq] - lam * o[:, bq:2 * bq]
    ms = jnp.mean(d * d, axis=0, keepdims=True)
    y = d * lax.rsqrt(ms + RMS_EPS) * sub_ref[...] * (1.0 - lambda_init)
    y_ref[0] = y.astype(BF16)


def _diff_attn(qdat, kda, vdat, lq1, lk1, lq2, lk2, subln_col, lambda_init):
    b, s, _ = kda.shape
    bq = 512
    vec = lambda n: pl.BlockSpec((1, n), lambda bi, h, qi: (0, 0))
    dv = 2 * HEAD
    qtile = pl.BlockSpec((1, dv, bq), lambda bi, h, qi: (bi, h, qi))
    return pl.pallas_call(
        functools.partial(_diffattn_body, lambda_init, bq),
        grid=(b, DA_HEADS, s // bq),
        in_specs=[qtile,
                  pl.BlockSpec((1, s, LANES), lambda bi, h, qi: (bi, 0, h)),
                  pl.BlockSpec((1, dv, s), lambda bi, h, qi: (bi, h, 0)),
                  vec(HEAD), vec(HEAD), vec(HEAD), vec(HEAD),
                  pl.BlockSpec((dv, 1), lambda bi, h, qi: (0, 0))],
        out_specs=qtile,
        out_shape=jax.ShapeDtypeStruct((b, DA_HEADS * dv, s), BF16),
        scratch_shapes=[pltpu.VMEM((LANES, 2 * bq), BF16), pltpu.VMEM((1, 2 * bq), F32),
                        pltpu.VMEM((1, 2 * bq), F32), pltpu.VMEM((dv, 2 * bq), F32),
                        pltpu.VMEM((bq, 2 * bq), F32), pltpu.VMEM((bq, 2 * bq), F32)],
        compiler_params=_params("parallel", "parallel", "parallel"),
        name="diff_attn",
    )(qdat, kda, vdat, lq1, lk1, lq2, lk2, subln_col)


def _compress_body(xk_ref, xv_ref, pek_ref, pev_ref, w1k_ref, w1v_ref, w2k_ref, w2v_ref,
                   kc_ref, vc_ref):
    def mlp(x_ref, pe_ref, w1_ref, w2_ref):
        blocks = (x_ref[0].astype(F32) + pe_ref[...]).astype(BF16)
        hid = _gelu(jnp.dot(blocks, w1_ref[...], preferred_element_type=F32))
        return jnp.dot(hid.astype(BF16), w2_ref[...], preferred_element_type=F32)

    kc = mlp(xk_ref, pek_ref, w1k_ref, w2k_ref)
    kc_ref[0] = jnp.concatenate([kc, jnp.zeros_like(kc)], axis=1).astype(BF16)
    vc_ref[0] = mlp(xv_ref, pev_ref, w1v_ref, w2v_ref).astype(BF16)


def _compress(xk, xv, pe_k, pe_v, w1k, w1v, w2k, w2v):
    n, ncp, width = xk.shape
    blk = pl.BlockSpec((1, ncp, width), lambda i: (i, 0, 0))
    fixed = lambda shape: pl.BlockSpec(shape, lambda i: (0, 0))
    return pl.pallas_call(
        _compress_body,
        grid=(n,),
        in_specs=[blk, blk, fixed((1, width)), fixed((1, width)), fixed((width, HEAD)),
                  fixed((width, HEAD)), fixed((HEAD, HEAD)), fixed((HEAD, HEAD))],
        out_specs=[pl.BlockSpec((1, ncp, LANES), lambda i: (i, 0, 0)),
                   pl.BlockSpec((1, ncp, HEAD), lambda i: (i, 0, 0))],
        out_shape=[jax.ShapeDtypeStruct((n, ncp, LANES), BF16),
                   jax.ShapeDtypeStruct((n, ncp, HEAD), BF16)],
        compiler_params=_params("parallel"),
        name="compress",
    )(xk, xv, pe_k, pe_v, w1k, w1v, w2k, w2v)


def _heads_on_lanes(qt, bq):
    return jnp.concatenate([qt[h * HEAD:(h + 1) * HEAD, :] for h in range(NSA_HG)], axis=1)


def _cmpsel_body(bq, q_ref, kc_ref, vc_ref, ovl_ref, ocmp_ref, bias_ref):
    qi = pl.program_id(2)
    ncp = kc_ref.shape[2]
    rows = NSA_HG * bq
    q2 = _heads_on_lanes(q_ref[0], bq)
    qz = jnp.concatenate([q2, jnp.zeros_like(q2)], axis=0)
    s = jnp.dot(kc_ref[0, 0], qz, preferred_element_type=F32)
    n = lax.broadcasted_iota(I32, (ncp, rows), 0)
    qpos = qi * bq + (lax.broadcasted_iota(I32, (ncp, rows), 1) & (bq - 1))
    cmask = n * CMP_STRIDE + (CMP_BLOCK - 1) <= qpos
    s = jnp.where(cmask, s, NEG_BIG)
    e = jnp.exp2(s - jnp.max(s, axis=0, keepdims=True))
    p = jnp.where(cmask, e / jnp.sum(e, axis=0, keepdims=True), 0.0)
    o = jnp.dot(vc_ref[0, 0], p.astype(BF16), preferred_element_type=F32)
    for h in range(NSA_HG):
        ocmp_ref[0, 0, h * HEAD:(h + 1) * HEAD, :] = o[:, h * bq:(h + 1) * bq]

    psum = p[:, 0:bq] + p[:, bq:2 * bq] + p[:, 2 * bq:3 * bq] + p[:, 3 * bq:4 * bq]
    imp = jnp.dot(ovl_ref[...], psum, preferred_element_type=F32)
    blk = lax.broadcasted_iota(I32, (SEL_BLOCK, bq), 0)
    pos = qi * bq + lax.broadcasted_iota(I32, (SEL_BLOCK, bq), 1)
    cur = lax.shift_right_logical(pos, SEL_BLOCK.bit_length() - 1)
    valid = blk <= cur
    forced = (blk == 0) | (blk == cur) | (blk == cur - 1)
    score = jnp.where(valid, imp + jnp.where(forced, FORCE_BONUS, 0.0), -jnp.inf)
    rank = jnp.zeros((SEL_BLOCK, bq), I32)
    for i in range(SEL_BLOCK):
        other = score[i:i + 1, :]
        beats = (other > score) | ((other == score) & (blk > i))
        rank = rank + beats.astype(I32)
    keep = valid & (rank < SEL_TOPK)
    bias_ref[0, 0] = jnp.where(keep, 0.0, SEL_MASK_BIAS).astype(BF16)


def _cmp_select(qnt, kca, vct, overlap_t):
    b, g, ncp, _ = kca.shape
    s = qnt.shape[2]
    bq = 128
    dq = NSA_HG * HEAD
    return pl.pallas_call(
        functools.partial(_cmpsel_body, bq),
        grid=(b, g, s // bq),
        in_specs=[pl.BlockSpec((1, dq, bq), lambda bi, gi, qi: (bi, gi, qi)),
                  pl.BlockSpec((1, 1, ncp, LANES), lambda bi, gi, qi: (bi, gi, 0, 0)),
                  pl.BlockSpec((1, 1, HEAD, ncp), lambda bi, gi, qi: (bi, gi, 0, 0)),
                  pl.BlockSpec((SEL_BLOCK, ncp), lambda bi, gi, qi: (0, 0))],
        out_specs=[pl.BlockSpec((1, 1, dq, bq), lambda bi, gi, qi: (bi, gi, 0, qi)),
                   pl.BlockSpec((1, 1, SEL_BLOCK, bq), lambda bi, gi, qi: (bi, gi, 0, qi))],
        out_shape=[jax.ShapeDtypeStruct((b, g, dq, s), F32),
                   jax.ShapeDtypeStruct((b, g, SEL_BLOCK, s), BF16)],
        compiler_params=_params("parallel", "parallel", "parallel"),
        name="cmp_select",
    )(qnt, kca, vct, overlap_t)


def _selwin_body(bq, bks, q_ref, bias_ref, ks_ref, vs_ref, kw_ref, vw_ref, ocmp_ref, gate_ref,
                 y_ref, m_ref, l_ref, acc_ref, qa_ref, sa_ref, sb_ref):
    qi = pl.program_id(2)
    rows = NSA_HG * bq
    q2 = _heads_on_lanes(q_ref[0], bq)
    bias = bias_ref[0, 0]
    qa = jnp.concatenate([q2, jnp.concatenate([bias] * NSA_HG, axis=1)], axis=0)
    qw = jnp.concatenate([q2, jnp.zeros_like(q2)], axis=0)
    r = lax.broadcasted_iota(I32, (bq, rows), 0)
    c = lax.broadcasted_iota(I32, (bq, rows), 1) & (bq - 1)

    _softmax_reset(m_ref, l_ref, acc_ref)

    qa_ref[...] = qa
    last = (qi * bq) // bks

    def causal(s):
        kpos = last * bks + lax.broadcasted_iota(I32, (bks, rows), 0)
        qpos = qi * bq + (lax.broadcasted_iota(I32, (bks, rows), 1) & (bq - 1))
        return jnp.where(kpos <= qpos, s, NEG_BIG)

    _attend_tiles(last,
                  lambda kt: jnp.dot(_key_tile(ks_ref, kt, bks), qa_ref[...],
                                     preferred_element_type=F32),
                  lambda kt: _value_tile(vs_ref, kt, bks), causal,
                  sa_ref, sb_ref, m_ref, l_ref, acc_ref)
    o_sel = acc_ref[...] / l_ref[...]

    _softmax_reset(m_ref, l_ref, acc_ref)
    backs = list(range(WINDOW // bq, -1, -1))
    tiles = [jnp.maximum(qi - back, 0) for back in backs]
    raw = [jnp.dot(_key_tile(kw_ref, kt, bq), qw, preferred_element_type=F32) for kt in tiles]
    for back, kt, s in zip(backs, tiles, raw):
        if back >= 1 and (back + 1) * bq <= WINDOW:
            inside = qi >= back
        else:
            dist = c + back * bq - r
            inside = (dist >= 0) & (dist < WINDOW) & (qi >= back)
        _softmax_step(jnp.where(inside, s, NEG_BIG), _value_tile(vw_ref, kt, bq),
                      m_ref, l_ref, acc_ref)

    o_win = acc_ref[...] / l_ref[...]
    gate = gate_ref[0]
    for h in range(NSA_HG):
        sl = slice(h * bq, (h + 1) * bq)
        y = (gate[3 * h:3 * h + 1, :] * ocmp_ref[0, 0, h * HEAD:(h + 1) * HEAD, :]
             + gate[3 * h + 1:3 * h + 2, :] * o_sel[:, sl]
             + gate[3 * h + 2:3 * h + 3, :] * o_win[:, sl])
        y_ref[0, h * HEAD:(h + 1) * HEAD, :] = y.astype(BF16)


def _sel_win(qnt, bias_t, ksa, kwa, vswt, ocmp_t, gnt):
    b, s, _ = ksa.shape
    g = NSA_GROUPS
    bq, bks = 256, 512
    dq = NSA_HG * HEAD
    rows = NSA_HG * bq
    keys = pl.BlockSpec((1, s, LANES), lambda bi, gi, qi: (bi, 0, gi))
    vals = lambda first: pl.BlockSpec((1, HEAD, s), lambda bi, gi, qi: (bi, first + gi, 0))
    qtile = pl.BlockSpec((1, dq, bq), lambda bi, gi, qi: (bi, gi, qi))
    return pl.pallas_call(
        functools.partial(_selwin_body, bq, bks),
        grid=(b, g, s // bq),
        in_specs=[qtile,
                  pl.BlockSpec((1, 1, SEL_BLOCK, bq), lambda bi, gi, qi: (bi, gi, 0, qi)),
                  keys, vals(0), keys, vals(g),
                  pl.BlockSpec((1, 1, dq, bq), lambda bi, gi, qi: (bi, gi, 0, qi)),
                  pl.BlockSpec((1, LANES, bq), lambda bi, gi, qi: (bi, gi, qi))],
        out_specs=qtile,
        out_shape=jax.ShapeDtypeStruct((b, g * dq, s), BF16),
        scratch_shapes=[pltpu.VMEM((1, rows), F32), pltpu.VMEM((1, rows), F32),
                        pltpu.VMEM((HEAD, rows), F32), pltpu.VMEM((LANES, rows), BF16),
                        pltpu.VMEM((bks, rows), F32), pltpu.VMEM((bks, rows), F32)],
        compiler_params=_params("parallel", "parallel", "parallel"),
        name="sel_win",
    )(qnt, bias_t, ksa, vswt, kwa, vswt, ocmp_t, gnt)


def _outproj_body(yda_ref, yn_ref, gm_ref, x_ref, pda_ref, pnsa_ref, wo_ref, fg_ref, wq_ref,
                  k1_ref, k2_ref, h1_ref, c_ref, s1_ref, s2_ref):
    d = x_ref.shape[1]
    a = jnp.dot(yda_ref[...], pda_ref[...], preferred_element_type=F32)
    bn = jnp.dot(yn_ref[...], pnsa_ref[...], preferred_element_type=F32)
    merged = gm_ref[:, :d].astype(F32) * a + gm_ref[:, d:].astype(F32) * bn
    h1 = x_ref[...] + jnp.dot(merged.astype(BF16), wo_ref[...], preferred_element_type=F32)
    c = _rmsnorm(h1, fg_ref[...])
    _rows_to_tiles(h1, h1_ref)
    _rows_to_tiles(c, c_ref)
    cb = c.astype(BF16)
    for h in range(PEER_HEADS):
        qh = jnp.dot(cb, wq_ref[:, h * 256:(h + 1) * 256], preferred_element_type=F32).astype(BF16)
        s1_ref[h] = lax.dot_general(k1_ref[...], qh[:, :LANES], NT_DIMS, preferred_element_type=F32)
        s2_ref[h] = lax.dot_general(k2_ref[...], qh[:, LANES:], NT_DIMS, preferred_element_type=F32)


def _out_proj(yda, yn, gm, x2, pda, pnsa, wo, ffn_g, wq, k1, k2):
    t, d = x2.shape
    tm = 512
    row = lambda w: pl.BlockSpec((tm, w), lambda i: (i, 0))
    fixed = lambda a: pl.BlockSpec(a.shape, lambda i: (0, 0))
    sspec = pl.BlockSpec((PEER_HEADS, PEER_NKEYS, tm), lambda i: (0, 0, i))
    assert d == SUBLANES * LANES
    tiles = pl.BlockSpec((tm, SUBLANES, LANES), lambda i: (i, 0, 0))
    return pl.pallas_call(
        _outproj_body,
        grid=(t // tm,),
        in_specs=[row(512), row(512), row(2 * d), row(d), fixed(pda), fixed(pnsa), fixed(wo),
                  fixed(ffn_g), fixed(wq), fixed(k1), fixed(k2)],
        out_specs=[tiles, tiles, sspec, sspec],
        out_shape=[jax.ShapeDtypeStruct((t, SUBLANES, LANES), F32),
                   jax.ShapeDtypeStruct((t, SUBLANES, LANES), F32),
                   jax.ShapeDtypeStruct((PEER_HEADS, PEER_NKEYS, t), F32),
                   jax.ShapeDtypeStruct((PEER_HEADS, PEER_NKEYS, t), F32)],
        compiler_params=_params("parallel"),
        name="out_proj",
    )(yda, yn, gm, x2, pda, pnsa, wo, ffn_g, wq, k1, k2)


def _batcher_pairs(n):
    pairs = []

    def merge(lo, hi, r):
        step = r * 2
        if step < hi - lo:
            merge(lo, hi, step)
            merge(lo + r, hi, step)
            pairs.extend((i, i + r) for i in range(lo + r, hi - r, step))
        else:
            pairs.append((lo, lo + r))

    def sort(lo, hi):
        if hi - lo >= 1:
            mid = lo + (hi - lo) // 2
            sort(lo, mid)
            sort(mid + 1, hi)
            merge(lo, hi, 1)

    sort(0, n - 1)
    return pairs


_NET16 = _batcher_pairs(PEER_TOPK)


def _cmpx(a, b):
    c = (a[0] > b[0]) | ((a[0] == b[0]) & (a[1] < b[1]))
    return ((jnp.where(c, a[0], b[0]), jnp.where(c, a[1], b[1])),
            (jnp.where(c, b[0], a[0]), jnp.where(c, b[1], a[1])))


def _sort_lists(lists, n_real):
    lists = list(lists)
    for i, j in _NET16:
        if j < n_real:
            lists[i], lists[j] = _cmpx(lists[i], lists[j])
    return lists


def _merge_top(a, b):
    k = PEER_TOPK
    lists = [_cmpx(a[i], b[k - 1 - i])[0] for i in range(k)]
    step = k // 2
    while step >= 1:
        for i in range(k):
            if i & step == 0:
                lists[i], lists[i + step] = _cmpx(lists[i], lists[i + step])
        step //= 2
    return lists


def _top16(pairs):
    k = PEER_TOPK
    filler = (jnp.full(pairs[0][0].shape, -jnp.inf, F32), jnp.zeros(pairs[0][1].shape, I32))
    groups = []
    for g0 in range(0, len(pairs), k):
        chunk = list(pairs[g0:g0 + k])
        groups.append(_sort_lists(chunk + [filler] * (k - len(chunk)), len(chunk)))
    while len(groups) > 1:
        groups = [_merge_top(groups[i], groups[i + 1]) if i + 1 < len(groups) else groups[i]
                  for i in range(0, len(groups), 2)]
    return groups[0]


def _product_key_select(s1, s2):
    k = PEER_TOPK
    n_exp = PEER_NKEYS * PEER_NKEYS
    ids = lambda n: [jnp.full(s1[0].shape, i, I32) for i in range(n)]
    l1 = _top16(list(zip(s1, ids(len(s1)))))
    l2 = _top16(list(zip(s2, ids(len(s2)))))
    pair = lambda a, b: (l1[a][0] + l2[b][0],
                         (a * k + b) * n_exp + l1[a][1] * PEER_NKEYS + l2[b][1])
    first_row = [pair(0, b) for b in range(k)]
    rest = [pair(a, b) for a in range(1, k) for b in range(k // (a + 1))]
    best = _merge_top(first_row, _top16(rest))
    return [(v, p & (n_exp - 1)) for v, p in best]


def _peertopk_body(tt, s1_ref, s2_ref, idx_ref, gate_ref):
    nblk = tt // LANES
    assert nblk == SUBLANES

    def keys_major(ref, h):
        tiles = []
        for kg in range(PEER_NKEYS // SUBLANES):
            rows = slice(kg * SUBLANES, (kg + 1) * SUBLANES)
            tiles += _sublane_transpose([ref[h, rows, b * LANES:(b + 1) * LANES]
                                         for b in range(nblk)])
        return tiles

    def store(ref, h, tiles):
        for g in range(PEER_TOPK // SUBLANES):
            blocks = _sublane_transpose(tiles[g * SUBLANES:(g + 1) * SUBLANES])
            for b in range(nblk):
                ref[h, g * SUBLANES:(g + 1) * SUBLANES, b * LANES:(b + 1) * LANES] = blocks[b]

    def head(h, carry):
        best = _product_key_select(keys_major(s1_ref, h), keys_major(s2_ref, h))
        ex = [jnp.exp(v - best[0][0]) for v, _ in best]
        z = ex[0]
        for e in ex[1:]:
            z = z + e
        store(gate_ref, h, [e / z for e in ex])
        store(idx_ref, h, [ix for _, ix in best])
        return carry

    lax.fori_loop(0, PEER_HEADS, head, 0)


def _peer_topk(s1t, s2t):
    _, _, t = s1t.shape
    tt = SUBLANES * LANES
    spec_in = pl.BlockSpec((PEER_HEADS, PEER_NKEYS, tt), lambda i: (0, 0, i))
    spec_out = pl.BlockSpec((PEER_HEADS, PEER_TOPK, tt), lambda i: (0, 0, i))
    return pl.pallas_call(
        functools.partial(_peertopk_body, tt),
        grid=(t // tt,),
        in_specs=[spec_in, spec_in],
        out_specs=[spec_out, spec_out],
        out_shape=[jax.ShapeDtypeStruct((PEER_HEADS, PEER_TOPK, t), I32),
                   jax.ShapeDtypeStruct((PEER_HEADS, PEER_TOPK, t), F32)],
        compiler_params=_params("parallel"),
        name="peer_topk",
    )(s1t, s2t)


PEER_E = PEER_HEADS * PEER_TOPK
PEER_RING = 4
PEER_MID_ROWS = 48


def _pack_body(u_ref, v_ref, out_ref):
    bf16_bits = lambda a: lax.bitcast_convert_type(a.astype(BF16).astype(F32), I32)
    words = bf16_bits(u_ref[...]) | lax.shift_right_logical(bf16_bits(v_ref[...]), jnp.int32(16))
    _rows_to_tiles(words, out_ref)


def _pack_expert_rows(pu, pv):
    n, d = pu.shape
    assert d == SUBLANES * LANES
    te = 256
    blk = pl.BlockSpec((te, d), lambda i: (i, 0))
    return pl.pallas_call(
        _pack_body,
        grid=(n // te,),
        in_specs=[blk, blk],
        out_specs=pl.BlockSpec((te, SUBLANES, LANES), lambda i: (i, 0, 0)),
        out_shape=jax.ShapeDtypeStruct((n, SUBLANES, LANES), I32),
        compiler_params=_params("parallel"),
        name="pack_experts",
    )(pu, pv)


def _word_hi(w):
    return lax.bitcast_convert_type(w & jnp.int32(-65536), F32)


def _word_lo(w):
    return lax.bitcast_convert_type(lax.shift_left(w, jnp.int32(16)), F32)


def _sublane_sums(a, sub):
    for dist in (4, 2, 1):
        low = (sub & dist) == 0
        half = len(a) // 2
        a = [jnp.where(low, a[i], pltpu.roll(a[i + half], dist, 0))
             + jnp.where(low, pltpu.roll(a[i], SUBLANES - dist, 0), a[i + half])
             for i in range(half)]
    return a[0]


def _peer_pair_math(expert_u, expert_v, x8, gates, store, issue_some, wcol_ref):
    sub = lax.broadcasted_iota(I32, (SUBLANES, LANES), 0)
    eye = (lax.broadcasted_iota(I32, (PEER_E, LANES), 0)
           == lax.broadcasted_iota(I32, (PEER_E, LANES), 1))
    ones_rows = jnp.ones((SUBLANES, LANES), BF16)
    ones_sq = jnp.ones((LANES, LANES), BF16)
    ngroup = PEER_E // SUBLANES

    def hidden(a):
        groups = []
        for g in range(ngroup):
            prods = [expert_u(a, g * SUBLANES + r) * x8[a] for r in range(SUBLANES)]
            groups.append(_sublane_sums(prods, sub))
            issue_some(3 - g % 2)
        return jnp.concatenate(groups, axis=0)

    def expert_weights(a, q):
        q_hi = q.astype(BF16)
        q_lo = (q - q_hi.astype(F32)).astype(BF16)
        hid = (lax.dot_general(ones_rows, q_hi, NT_DIMS, preferred_element_type=F32)
               + lax.dot_general(ones_rows, q_lo, NT_DIMS, preferred_element_type=F32))
        issue_some(PEER_MID_ROWS // 2)
        w = _gelu(hid[0:1]) * gates[a]
        wd = jnp.where(eye, jnp.broadcast_to(w, (PEER_E, LANES)), 0.0).astype(BF16)
        wcol_ref[a] = jnp.dot(wd, ones_sq, preferred_element_type=F32)
        issue_some(PEER_MID_ROWS // 2)

    def combine(a):
        out = jnp.zeros((SUBLANES, LANES), F32)
        for g in range(ngroup):
            for r in range(SUBLANES):
                j = g * SUBLANES + r
                out = out + wcol_ref[a, j:j + 1, :] * expert_v(a, j)
            issue_some(3 - g % 2)
        store(a, out)

    expert_weights(0, hidden(0))
    expert_weights(1, hidden(1))
    combine(0)
    combine(1)


def _peer_finish(h1_ref, acc_ref, fg_ref, out_ref):
    hsum = h1_ref[...] + acc_ref[...]
    ms = jnp.mean(hsum * hsum, axis=(1, 2), keepdims=True)
    _tiles_to_rows(hsum * lax.rsqrt(ms + RMS_EPS) * fg_ref[...], out_ref)


def _peerffn_body(tb, idx_ref, gate_ref, c_ref, h1_ref, fg_ref, uv_ref, out_ref, buf_ref,
                  acc_ref, wcol_ref, sem_ref):
    npairs = tb // 2
    ahead = PEER_RING - 1

    def row_copy(t, j, slot):
        return pltpu.make_async_copy(uv_ref.at[idx_ref[t, j]], buf_ref.at[slot, j],
                                     sem_ref.at[slot])

    def wait_all(slot):
        pltpu.make_async_copy(uv_ref.at[pl.ds(0, PEER_E)], buf_ref.at[slot],
                              sem_ref.at[slot]).wait()

    def ring(phase):
        base = 2 * (phase % PEER_RING)
        return (base, base + 1)

    def pair(i, phase, prefetch):
        toks = (2 * i, 2 * i + 1)
        slots = ring(phase)
        nslots = ring(phase + ahead)
        todo = [(a, j) for a in range(2) for j in range(PEER_E)]

        def issue_some(n):
            for a, j in todo[:n]:
                if prefetch:
                    row_copy(toks[a] + 2 * ahead, j, nslots[a]).start(priority=j % 2)
            del todo[:n]

        wait_all(slots[0])
        wait_all(slots[1])

        def store(a, out):
            acc_ref[toks[a]] = out

        _peer_pair_math(
            lambda a, j: _word_hi(buf_ref[slots[a], j]),
            lambda a, j: _word_lo(buf_ref[slots[a], j]),
            [c_ref[t] for t in toks], [gate_ref[pl.ds(t, 1), :] for t in toks], store, issue_some,
            wcol_ref)
        assert not todo, "every prefetch row DMA must be issued exactly once"

    assert npairs % PEER_RING == 0
    for i in range(ahead):
        for a, slot in enumerate(ring(i)):
            for j in range(PEER_E):
                row_copy(2 * i + a, j, slot).start(priority=j % 2)

    def body(g, carry):
        for phase in range(PEER_RING):
            pair(PEER_RING * g + phase, phase, True)
        return carry

    ngroups = npairs // PEER_RING
    lax.fori_loop(0, ngroups - 1, body, 0)
    for phase in range(PEER_RING):
        i = PEER_RING * (ngroups - 1) + phase
        pair(i, phase, i + ahead < npairs)
    _peer_finish(h1_ref, acc_ref, fg_ref, out_ref)


def _peer_ffn(idx, gate, c3, h13, final_g3, uv_tiles):
    t = h13.shape[0]
    tb = 256
    row3 = pl.BlockSpec((tb, SUBLANES, LANES), lambda i: (i, 0, 0))
    return pl.pallas_call(
        functools.partial(_peerffn_body, tb),
        grid=(t // tb,),
        in_specs=[pl.BlockSpec((tb, PEER_E), lambda i: (i, 0), memory_space=pltpu.SMEM),
                  pl.BlockSpec((tb, PEER_E), lambda i: (i, 0)), row3, row3,
                  pl.BlockSpec((1, SUBLANES, LANES), lambda i: (0, 0, 0)),
                  pl.BlockSpec(memory_space=pl.ANY)],
        out_specs=pl.BlockSpec((tb, SUBLANES * LANES), lambda i: (i, 0)),
        out_shape=jax.ShapeDtypeStruct((t, SUBLANES * LANES), F32),
        scratch_shapes=[pltpu.VMEM((2 * PEER_RING, PEER_E, SUBLANES, LANES), I32),
                        pltpu.VMEM((tb, SUBLANES, LANES), F32),
                        pltpu.VMEM((2, PEER_E, LANES), F32),
                        pltpu.SemaphoreType.DMA((2 * PEER_RING,))],
        compiler_params=_params("arbitrary"),
        name="peer_ffn",
    )(idx, gate, c3, h13, final_g3, uv_tiles)


def _overlap_table(seq):
    ci = jnp.arange(seq // CMP_STRIDE)[None, :] * CMP_STRIDE
    sj = jnp.arange(SEL_BLOCK)[:, None] * SEL_BLOCK
    return ((ci < sj + SEL_BLOCK) & (ci + CMP_BLOCK > sj)).astype(F32)


def _cmp_blocks(kv):
    b, g, s, dh = kv.shape
    r = kv.reshape(b * g, s // CMP_STRIDE, CMP_STRIDE * dh)
    return jnp.concatenate([r, jnp.roll(r, -1, axis=1)], axis=-1)


def _pad_cmp_params(pe, w1):
    pe_p = jnp.pad(pe, ((0, 0), (0, LANES - HEAD))).reshape(1, -1)
    w1_p = jnp.pad(w1.reshape(CMP_BLOCK, HEAD, -1), ((0, 0), (0, LANES - HEAD), (0, 0)))
    return pe_p, w1_p.reshape(CMP_BLOCK * LANES, -1).astype(BF16)


def _layer(h, lidx, attn_norm, w_in, lq1, lk1, lq2, lk2, subln, pe_k, pe_v, w1k, w1v, w2k, w2v,
           p_da, p_nsa, w_o, ffn_norm, wq, k1, k2, pu, pv, out_norm):
    b, s, d = h.shape
    t = b * s
    g, hg = NSA_GROUPS, NSA_HG
    lambda_init = 0.8 - 0.6 * math.exp(-0.3 * lidx)
    x2 = h.reshape(t, d)

    qdat, kda, vdat, qnt, cmp_in, ksx, kwx, vswt, gm, gnt = _in_proj(
        x2, attn_norm.reshape(1, d), _pack_w_in(w_in), _rope_tables(s), s)
    ydat = _diff_attn(qdat, kda.reshape(b, s, -1), vdat,
                      lq1.reshape(1, -1), lk1.reshape(1, -1), lq2.reshape(1, -1),
                      lk2.reshape(1, -1), subln.reshape(-1, 1), lambda_init)
    yda = jnp.swapaxes(ydat, 1, 2).reshape(t, -1)

    assert s // SEL_BLOCK <= SEL_BLOCK, "selection bias rows hold at most 64 blocks"
    cmp4 = jnp.swapaxes(cmp_in.reshape(b, s, 2 * g, LANES), 1, 2)
    pe_kp, w1_kp = _pad_cmp_params(pe_k, w1k)
    pe_vp, w1_vp = _pad_cmp_params(pe_v, w1v)
    kca, vc = _compress(_cmp_blocks(cmp4[:, :g]), _cmp_blocks(cmp4[:, g:]), pe_kp, pe_vp,
                        w1_kp, w1_vp, w2k.astype(BF16), w2v.astype(BF16))
    ncp = s // CMP_STRIDE
    kca = kca.reshape(b, g, ncp, LANES)
    vct = jnp.swapaxes(vc.reshape(b, g, ncp, HEAD), 2, 3)

    ocmp_t, bias_t = _cmp_select(qnt, kca, vct, _overlap_table(s))
    onehot = (jnp.arange(s)[:, None] // SEL_BLOCK == jnp.arange(HEAD)[None, :]).astype(BF16)
    pad_hot = jnp.concatenate([jnp.zeros_like(onehot), onehot] * g, axis=1)
    ksa = ksx.reshape(b, s, -1) + pad_hot[None]
    ynt = _sel_win(qnt, bias_t, ksa, kwx.reshape(b, s, -1), vswt, ocmp_t, gnt)
    yn = jnp.swapaxes(ynt, 1, 2).reshape(t, -1)

    h1_tiles, c_tiles, s1t, s2t = _out_proj(
        yda, yn, gm, x2, p_da.astype(BF16), p_nsa.astype(BF16), w_o.astype(BF16),
        ffn_norm.reshape(1, d), wq.astype(BF16), k1.astype(BF16), k2.astype(BF16))
    idx_t, gate_t = _peer_topk(s1t, s2t)
    idx = idx_t.reshape(PEER_E, t).T
    gate = gate_t.reshape(PEER_E, t).T
    out = _peer_ffn(idx, gate, c_tiles, h1_tiles, out_norm.reshape(1, SUBLANES, LANES),
                    _pack_expert_rows(pu, pv))
    return out.reshape(b, s, d)


def kernel(x, attn_norm, w_in, da_lambda_q1, da_lambda_k1, da_lambda_q2, da_lambda_k2, da_subln,
           cmp_pe_k, cmp_pe_v, cmp_w1_k, cmp_w1_v, cmp_w2_k, cmp_w2_v, p_da, p_nsa, w_o,
           ffn_norm, peer_wq, peer_k1, peer_k2, peer_u, peer_v, final_norm):
    depth = attn_norm.shape[0]
    assert depth == 1, "the final norm is fused into the last layer's PEER kernel"
    h = x
    for l in range(depth):
        h = _layer(h, l, attn_norm[l], w_in[l], da_lambda_q1[l], da_lambda_k1[l], da_lambda_q2[l],
                   da_lambda_k2[l], da_subln[l], cmp_pe_k[l], cmp_pe_v[l], cmp_w1_k[l],
                   cmp_w1_v[l], cmp_w2_k[l], cmp_w2_v[l], p_da[l], p_nsa[l], w_o[l], ffn_norm[l],
                   peer_wq[l], peer_k1[l], peer_k2[l], peer_u[l], peer_v[l], final_norm)
    return h
```

```python
import functools
import math

import jax
import jax.numpy as jnp
from jax import lax
from jax.experimental import pallas as pl
from jax.experimental.pallas import tpu as pltpu

F32 = jnp.float32
BF16 = jnp.bfloat16
I32 = jnp.int32

RMS_EPS = 1e-6
ROPE_THETA = 500000.0
ROPE_HALF = 8
HEAD = 64
DA_HEADS = 4
NSA_GROUPS = 2
NSA_HG = 4
CMP_STRIDE = 16
CMP_BLOCK = 32
SEL_BLOCK = 64
SEL_TOPK = 16
WINDOW = 512
FORCE_BONUS = 1e4
NEG_BIG = -1e30
SEL_MASK_BIAS = -2.0 ** 100
ATTN_SCALE = HEAD ** -0.5 * math.log2(math.e)
PEER_HEADS = 8
PEER_NKEYS = 128
PEER_TOPK = 16
LANES = 128
SUBLANES = 8
VMEM_LIMIT = 56 * 1024 * 1024

NT_DIMS = (((1,), (1,)), ((), ()))


def _rmsnorm(x, g):
    return x * lax.rsqrt(jnp.mean(x * x, axis=-1, keepdims=True) + RMS_EPS) * g


def _sigmoid(z):
    return 1.0 / (1.0 + jnp.exp(-z))


def _gelu(z):
    return 0.5 * z * (1.0 + lax.erf(z * (2.0 ** -0.5)))


def _params(*sem):
    return pltpu.CompilerParams(dimension_semantics=sem, vmem_limit_bytes=VMEM_LIMIT)


def _sublane_transpose(v):
    sub = lax.broadcasted_iota(I32, (SUBLANES, LANES), 0)
    v = list(v)
    for dist in (4, 2, 1):
        low = (sub & dist) == 0
        nxt = list(v)
        for i in range(SUBLANES):
            if i & dist == 0:
                nxt[i] = jnp.where(low, v[i], pltpu.roll(v[i + dist], dist, 0))
                nxt[i + dist] = jnp.where(low, pltpu.roll(v[i], SUBLANES - dist, 0), v[i + dist])
        v = nxt
    return v


def _rows_to_tiles(x, tile_ref):
    for g in range(x.shape[0] // SUBLANES):
        rows = slice(g * SUBLANES, (g + 1) * SUBLANES)
        tiles = _sublane_transpose([x[rows, c * LANES:(c + 1) * LANES] for c in range(SUBLANES)])
        for e in range(SUBLANES):
            tile_ref[g * SUBLANES + e] = tiles[e]


def _tiles_to_rows(t, row_ref):
    for g in range(t.shape[0] // SUBLANES):
        chunks = _sublane_transpose([t[g * SUBLANES + e] for e in range(SUBLANES)])
        for c in range(SUBLANES):
            row_ref[g * SUBLANES:(g + 1) * SUBLANES, c * LANES:(c + 1) * LANES] = chunks[c]


_QDA0, _KDA0, _VDA0, _QN0 = 0, 512, 1024, 1536
_KC0, _VC0, _KS0, _KW0, _VSW0, _GM0, _GN0, _WCOLS = 2048, 2304, 2560, 2816, 3072, 3328, 5376, 5632


def _inproj_body(x_ref, g_ref, w_ref, rc_ref, rs1_ref, rs2_ref,
                 qda_ref, kda_ref, vda_ref, qn_ref, cmp_ref, ks_ref, kw_ref, vsw_ref, gm_ref, gn_ref):
    a = _rmsnorm(x_ref[...], g_ref[...]).astype(BF16)
    rc, rs1, rs2 = rc_ref[...], rs1_ref[...], rs2_ref[...]

    def rope(z):
        return (z * rc + pltpu.roll(z, ROPE_HALF, 1) * rs1
                + pltpu.roll(z, LANES - ROPE_HALF, 1) * rs2)

    def proj(c0):
        return jnp.dot(a, w_ref[:, c0:c0 + 256], preferred_element_type=F32)

    def rope2(z):
        return jnp.concatenate([rope(z[:, :LANES]), rope(z[:, LANES:])], axis=1)

    for c in range(2):
        qda_ref[:, c * 256:(c + 1) * 256] = (rope2(proj(_QDA0 + c * 256)) * ATTN_SCALE).astype(BF16)
        kda_ref[:, c * 256:(c + 1) * 256] = rope2(proj(_KDA0 + c * 256)).astype(BF16)
        vda_ref[:, c * 256:(c + 1) * 256] = proj(_VDA0 + c * 256).astype(BF16)
        qn_ref[:, c * 256:(c + 1) * 256] = (rope2(proj(_QN0 + c * 256)) * ATTN_SCALE).astype(BF16)
    cmp_ref[:, 0:256] = rope2(proj(_KC0)).astype(BF16)
    cmp_ref[:, 256:512] = proj(_VC0).astype(BF16)
    ks_ref[...] = rope2(proj(_KS0)).astype(BF16)
    kw_ref[...] = rope2(proj(_KW0)).astype(BF16)
    vsw_ref[...] = proj(_VSW0).astype(BF16)
    for c in range(8):
        gm_ref[:, c * 256:(c + 1) * 256] = _sigmoid(proj(_GM0 + c * 256)).astype(BF16)
    gn_ref[...] = _sigmoid(proj(_GN0))


def _pack_w_in(w):
    d = w.shape[0]
    zeros = lambda n: jnp.zeros((d, n), w.dtype)

    def spread_groups(c0):
        return [w[:, c0:c0 + HEAD], zeros(HEAD), w[:, c0 + HEAD:c0 + 2 * HEAD], zeros(HEAD)]

    kv_w = NSA_GROUPS * HEAD
    kv0 = 4 * 512
    kc, vc, ks, vs, kw, vw = (kv0 + kv_w * i for i in range(6))
    gn0 = kv0 + 6 * kv_w
    per_group = NSA_HG * 3
    gm0 = gn0 + NSA_GROUPS * per_group
    assert w.shape[1] == gm0 + 2 * d
    gn = w[:, gn0:gm0]
    cols = ([w[:, :kv0]] + spread_groups(kc) + spread_groups(vc) + spread_groups(ks)
            + spread_groups(kw) + [w[:, vs:vs + kv_w], w[:, vw:vw + kv_w], w[:, gm0:],
                                   gn[:, :per_group], zeros(LANES - per_group),
                                   gn[:, per_group:], zeros(LANES - per_group)])
    packed = jnp.concatenate(cols, axis=1).astype(BF16)
    assert packed.shape[1] == _WCOLS
    return packed


def _rope_tables(seq):
    inv = jnp.power(ROPE_THETA, -jnp.arange(ROPE_HALF, dtype=F32) * 2.0 / (2 * ROPE_HALF))
    ang = jnp.arange(seq, dtype=F32)[:, None] * inv[None, :]
    cos, sin = jnp.cos(ang), jnp.sin(ang)
    one = jnp.ones((seq, HEAD - 2 * ROPE_HALF), F32)
    zero8 = jnp.zeros((seq, ROPE_HALF), F32)
    zero48 = jnp.zeros_like(one)
    rc = jnp.concatenate([cos, cos, one], axis=1)
    rs1 = jnp.concatenate([zero8, sin, zero48], axis=1)
    rs2 = jnp.concatenate([-sin, zero8, zero48], axis=1)
    return tuple(jnp.concatenate([t, t], axis=1) for t in (rc, rs1, rs2))


def _in_proj(x2, norm_g, w_packed, rope_tabs, seq):
    t, d = x2.shape
    tm = 512
    nseq = seq // tm
    row = lambda i: (i, 0)
    fixed = lambda i: (0, 0)
    out_shapes = [
        jax.ShapeDtypeStruct((t, 512), BF16), jax.ShapeDtypeStruct((t, 512), BF16),
        jax.ShapeDtypeStruct((t, 512), BF16), jax.ShapeDtypeStruct((t, 512), BF16),
        jax.ShapeDtypeStruct((t, 512), BF16), jax.ShapeDtypeStruct((t, 256), BF16),
        jax.ShapeDtypeStruct((t, 256), BF16), jax.ShapeDtypeStruct((t, 256), BF16),
        jax.ShapeDtypeStruct((t, 2048), BF16), jax.ShapeDtypeStruct((t, 256), F32)]
    rope_spec = pl.BlockSpec((tm, LANES), lambda i: (i % nseq, 0))
    return pl.pallas_call(
        _inproj_body,
        grid=(t // tm,),
        in_specs=[pl.BlockSpec((tm, d), row), pl.BlockSpec((1, d), fixed),
                  pl.BlockSpec((d, _WCOLS), fixed), rope_spec, rope_spec, rope_spec],
        out_specs=[pl.BlockSpec((tm, s.shape[1]), row) for s in out_shapes],
        out_shape=out_shapes,
        compiler_params=_params("parallel"),
        name="in_proj",
    )(x2, norm_g, w_packed, *rope_tabs)


def _softmax_step(s, vt, m_ref, l_ref, acc_ref):
    m_prev = m_ref[...]
    m_new = jnp.maximum(m_prev, jnp.max(s, axis=0, keepdims=True))
    alpha = jnp.exp2(m_prev - m_new)
    p = jnp.exp2(s - m_new)
    l_ref[...] = alpha * l_ref[...] + jnp.sum(p, axis=0, keepdims=True)
    acc_ref[...] = alpha * acc_ref[...] + jnp.dot(vt, p.astype(BF16), preferred_element_type=F32)
    m_ref[...] = m_new


def _softmax_reset(m_ref, l_ref, acc_ref):
    m_ref[...] = jnp.full(m_ref.shape, NEG_BIG, F32)
    l_ref[...] = jnp.zeros(l_ref.shape, F32)
    acc_ref[...] = jnp.zeros(acc_ref.shape, F32)


def _attend_tiles(n_full, scores, values, mask_last, sa_ref, sb_ref, m_ref, l_ref, acc_ref):
    step = lambda s, t: _softmax_step(s, values(t), m_ref, l_ref, acc_ref)
    sa_ref[...] = scores(0)

    def two_tiles(i, carry):
        t = 2 * i
        sb_ref[...] = scores(t + 1)
        step(sa_ref[...], t)
        sa_ref[...] = scores(t + 2)
        step(sb_ref[...], t + 1)
        return carry

    lax.fori_loop(0, n_full // 2, two_tiles, 0)
    odd = (n_full & 1) == 1

    @pl.when(odd)
    def _():
        sb_ref[...] = scores(n_full)
        step(sa_ref[...], n_full - 1)
        step(mask_last(sb_ref[...]), n_full)

    @pl.when(jnp.logical_not(odd))
    def _():
        step(mask_last(sa_ref[...]), n_full)


def _key_tile(ref, kt, bk):
    return ref[(0,) * (len(ref.shape) - 2) + (pl.ds(pl.multiple_of(kt * bk, bk), bk), slice(None))]


def _value_tile(ref, kt, bk):
    return ref[(0,) * (len(ref.shape) - 2) + (slice(None), pl.ds(pl.multiple_of(kt * bk, bk), bk))]


def _diffattn_body(lambda_init, bq, q_ref, k_ref, v_ref, lq1_ref, lk1_ref, lq2_ref, lk2_ref,
                   sub_ref, y_ref, qbd_ref, m_ref, l_ref, acc_ref, sa_ref, sb_ref):
    qi = pl.program_id(2)
    bk = bq
    qt = q_ref[0]
    sub = lax.broadcasted_iota(I32, qt.shape, 0)
    zero = jnp.zeros_like(qt)
    qbd_ref[:, 0:bq] = jnp.where(sub < HEAD, qt, zero)
    qbd_ref[:, bq:2 * bq] = jnp.where(sub >= HEAD, qt, zero)
    _softmax_reset(m_ref, l_ref, acc_ref)

    def scores(kt):
        return jnp.dot(_key_tile(k_ref, kt, bk), qbd_ref[...], preferred_element_type=F32)

    def causal(s):
        r = lax.broadcasted_iota(I32, (bk, 2 * bq), 0)
        c = lax.broadcasted_iota(I32, (bk, 2 * bq), 1) & (bq - 1)
        return jnp.where(r <= c, s, NEG_BIG)

    _attend_tiles(qi, scores, lambda kt: _value_tile(v_ref, kt, bk), causal,
                  sa_ref, sb_ref, m_ref, l_ref, acc_ref)

    o = acc_ref[...] / l_ref[...]
    lam = (jnp.exp(jnp.sum(lq1_ref[...] * lk1_ref[...], axis=1, keepdims=True))
           - jnp.exp(jnp.sum(lq2_ref[...] * lk2_ref[...], axis=1, keepdims=True)) + lambda_init)
    d = o[:, 0:bq] - lam * o[:, bq:2 * bq]
    ms = jnp.mean(d * d, axis=0, keepdims=True)
    y = d * lax.rsqrt(ms + RMS_EPS) * sub_ref[...] * (1.0 - lambda_init)
    y_ref[0] = y.astype(BF16)


def _diff_attn(qdat, kda, vdat, lq1, lk1, lq2, lk2, subln_col, lambda_init):
    b, s, _ = kda.shape
    bq = 512
    vec = lambda n: pl.BlockSpec((1, n), lambda bi, h, qi: (0, 0))
    dv = 2 * HEAD
    qtile = pl.BlockSpec((1, dv, bq), lambda bi, h, qi: (bi, h, qi))
    return pl.pallas_call(
        functools.partial(_diffattn_body, lambda_init, bq),
        grid=(b, DA_HEADS, s // bq),
        in_specs=[qtile,
                  pl.BlockSpec((1, s, LANES), lambda bi, h, qi: (bi, 0, h)),
                  pl.BlockSpec((1, dv, s), lambda bi, h, qi: (bi, h, 0)),
                  vec(HEAD), vec(HEAD), vec(HEAD), vec(HEAD),
                  pl.BlockSpec((dv, 1), lambda bi, h, qi: (0, 0))],
        out_specs=qtile,
        out_shape=jax.ShapeDtypeStruct((b, DA_HEADS * dv, s), BF16),
        scratch_shapes=[pltpu.VMEM((LANES, 2 * bq), BF16), pltpu.VMEM((1, 2 * bq), F32),
                        pltpu.VMEM((1, 2 * bq), F32), pltpu.VMEM((dv, 2 * bq), F32),
                        pltpu.VMEM((bq, 2 * bq), F32), pltpu.VMEM((bq, 2 * bq), F32)],
        compiler_params=_params("parallel", "parallel", "parallel"),
        name="diff_attn",
    )(qdat, kda, vdat, lq1, lk1, lq2, lk2, subln_col)


def _compress_body(xk_ref, xv_ref, pek_ref, pev_ref, w1k_ref, w1v_ref, w2k_ref, w2v_ref,
                   kc_ref, vc_ref):
    def mlp(x_ref, pe_ref, w1_ref, w2_ref):
        blocks = (x_ref[0].astype(F32) + pe_ref[...]).astype(BF16)
        hid = _gelu(jnp.dot(blocks, w1_ref[...], preferred_element_type=F32))
        return jnp.dot(hid.astype(BF16), w2_ref[...], preferred_element_type=F32)

    kc = mlp(xk_ref, pek_ref, w1k_ref, w2k_ref)
    kc_ref[0] = jnp.concatenate([kc, jnp.zeros_like(kc)], axis=1).astype(BF16)
    vc_ref[0] = mlp(xv_ref, pev_ref, w1v_ref, w2v_ref).astype(BF16)


def _compress(xk, xv, pe_k, pe_v, w1k, w1v, w2k, w2v):
    n, ncp, width = xk.shape
    blk = pl.BlockSpec((1, ncp, width), lambda i: (i, 0, 0))
    fixed = lambda shape: pl.BlockSpec(shape, lambda i: (0, 0))
    return pl.pallas_call(
        _compress_body,
        grid=(n,),
        in_specs=[blk, blk, fixed((1, width)), fixed((1, width)), fixed((width, HEAD)),
                  fixed((width, HEAD)), fixed((HEAD, HEAD)), fixed((HEAD, HEAD))],
        out_specs=[pl.BlockSpec((1, ncp, LANES), lambda i: (i, 0, 0)),
                   pl.BlockSpec((1, ncp, HEAD), lambda i: (i, 0, 0))],
        out_shape=[jax.ShapeDtypeStruct((n, ncp, LANES), BF16),
                   jax.ShapeDtypeStruct((n, ncp, HEAD), BF16)],
        compiler_params=_params("parallel"),
        name="compress",
    )(xk, xv, pe_k, pe_v, w1k, w1v, w2k, w2v)


def _heads_on_lanes(qt, bq):
    return jnp.concatenate([qt[h * HEAD:(h + 1) * HEAD, :] for h in range(NSA_HG)], axis=1)


def _cmpsel_body(bq, q_ref, kc_ref, vc_ref, ovl_ref, ocmp_ref, bias_ref):
    qi = pl.program_id(2)
    ncp = kc_ref.shape[2]
    rows = NSA_HG * bq
    q2 = _heads_on_lanes(q_ref[0], bq)
    qz = jnp.concatenate([q2, jnp.zeros_like(q2)], axis=0)
    s = jnp.dot(kc_ref[0, 0], qz, preferred_element_type=F32)
    n = lax.broadcasted_iota(I32, (ncp, rows), 0)
    qpos = qi * bq + (lax.broadcasted_iota(I32, (ncp, rows), 1) & (bq - 1))
    cmask = n * CMP_STRIDE + (CMP_BLOCK - 1) <= qpos
    s = jnp.where(cmask, s, NEG_BIG)
    e = jnp.exp2(s - jnp.max(s, axis=0, keepdims=True))
    p = jnp.where(cmask, e / jnp.sum(e, axis=0, keepdims=True), 0.0)
    o = jnp.dot(vc_ref[0, 0], p.astype(BF16), preferred_element_type=F32)
    for h in range(NSA_HG):
        ocmp_ref[0, 0, h * HEAD:(h + 1) * HEAD, :] = o[:, h * bq:(h + 1) * bq]

    psum = p[:, 0:bq] + p[:, bq:2 * bq] + p[:, 2 * bq:3 * bq] + p[:, 3 * bq:4 * bq]
    imp = jnp.dot(ovl_ref[...], psum, preferred_element_type=F32)
    blk = lax.broadcasted_iota(I32, (SEL_BLOCK, bq), 0)
    pos = qi * bq + lax.broadcasted_iota(I32, (SEL_BLOCK, bq), 1)
    cur = lax.shift_right_logical(pos, SEL_BLOCK.bit_length() - 1)
    valid = blk <= cur
    forced = (blk == 0) | (blk == cur) | (blk == cur - 1)
    score = jnp.where(valid, imp + jnp.where(forced, FORCE_BONUS, 0.0), -jnp.inf)
    rank = jnp.zeros((SEL_BLOCK, bq), I32)
    for i in range(SEL_BLOCK):
        other = score[i:i + 1, :]
        beats = (other > score) | ((other == score) & (blk > i))
        rank = rank + beats.astype(I32)
    keep = valid & (rank < SEL_TOPK)
    bias_ref[0, 0] = jnp.where(keep, 0.0, SEL_MASK_BIAS).astype(BF16)


def _cmp_select(qnt, kca, vct, overlap_t):
    b, g, ncp, _ = kca.shape
    s = qnt.shape[2]
    bq = 128
    dq = NSA_HG * HEAD
    return pl.pallas_call(
        functools.partial(_cmpsel_body, bq),
        grid=(b, g, s // bq),
        in_specs=[pl.BlockSpec((1, dq, bq), lambda bi, gi, qi: (bi, gi, qi)),
                  pl.BlockSpec((1, 1, ncp, LANES), lambda bi, gi, qi: (bi, gi, 0, 0)),
                  pl.BlockSpec((1, 1, HEAD, ncp), lambda bi, gi, qi: (bi, gi, 0, 0)),
                  pl.BlockSpec((SEL_BLOCK, ncp), lambda bi, gi, qi: (0, 0))],
        out_specs=[pl.BlockSpec((1, 1, dq, bq), lambda bi, gi, qi: (bi, gi, 0, qi)),
                   pl.BlockSpec((1, 1, SEL_BLOCK, bq), lambda bi, gi, qi: (bi, gi, 0, qi))],
        out_shape=[jax.ShapeDtypeStruct((b, g, dq, s), F32),
                   jax.ShapeDtypeStruct((b, g, SEL_BLOCK, s), BF16)],
        compiler_params=_params("parallel", "parallel", "parallel"),
        name="cmp_select",
    )(qnt, kca, vct, overlap_t)


def _selwin_body(bq, bks, q_ref, bias_ref, ks_ref, vs_ref, kw_ref, vw_ref, ocmp_ref, gate_ref,
                 y_ref, m_ref, l_ref, acc_ref, qa_ref, sa_ref, sb_ref):
    qi = pl.program_id(2)
    rows = NSA_HG * bq
    q2 = _heads_on_lanes(q_ref[0], bq)
    bias = bias_ref[0, 0]
    qa = jnp.concatenate([q2, jnp.concatenate([bias] * NSA_HG, axis=1)], axis=0)
    qw = jnp.concatenate([q2, jnp.zeros_like(q2)], axis=0)
    r = lax.broadcasted_iota(I32, (bq, rows), 0)
    c = lax.broadcasted_iota(I32, (bq, rows), 1) & (bq - 1)

    _softmax_reset(m_ref, l_ref, acc_ref)

    qa_ref[...] = qa
    last = (qi * bq) // bks

    def causal(s):
        kpos = last * bks + lax.broadcasted_iota(I32, (bks, rows), 0)
        qpos = qi * bq + (lax.broadcasted_iota(I32, (bks, rows), 1) & (bq - 1))
        return jnp.where(kpos <= qpos, s, NEG_BIG)

    _attend_tiles(last,
                  lambda kt: jnp.dot(_key_tile(ks_ref, kt, bks), qa_ref[...],
                                     preferred_element_type=F32),
                  lambda kt: _value_tile(vs_ref, kt, bks), causal,
                  sa_ref, sb_ref, m_ref, l_ref, acc_ref)
    o_sel = acc_ref[...] / l_ref[...]

    _softmax_reset(m_ref, l_ref, acc_ref)
    backs = list(range(WINDOW // bq, -1, -1))
    tiles = [jnp.maximum(qi - back, 0) for back in backs]
    raw = [jnp.dot(_key_tile(kw_ref, kt, bq), qw, preferred_element_type=F32) for kt in tiles]
    for back, kt, s in zip(backs, tiles, raw):
        if back >= 1 and (back + 1) * bq <= WINDOW:
            inside = qi >= back
        else:
            dist = c + back * bq - r
            inside = (dist >= 0) & (dist < WINDOW) & (qi >= back)
        _softmax_step(jnp.where(inside, s, NEG_BIG), _value_tile(vw_ref, kt, bq),
                      m_ref, l_ref, acc_ref)

    o_win = acc_ref[...] / l_ref[...]
    gate = gate_ref[0]
    for h in range(NSA_HG):
        sl = slice(h * bq, (h + 1) * bq)
        y = (gate[3 * h:3 * h + 1, :] * ocmp_ref[0, 0, h * HEAD:(h + 1) * HEAD, :]
             + gate[3 * h + 1:3 * h + 2, :] * o_sel[:, sl]
             + gate[3 * h + 2:3 * h + 3, :] * o_win[:, sl])
        y_ref[0, h * HEAD:(h + 1) * HEAD, :] = y.astype(BF16)


def _sel_win(qnt, bias_t, ksa, kwa, vswt, ocmp_t, gnt):
    b, s, _ = ksa.shape
    g = NSA_GROUPS
    bq, bks = 256, 512
    dq = NSA_HG * HEAD
    rows = NSA_HG * bq
    keys = pl.BlockSpec((1, s, LANES), lambda bi, gi, qi: (bi, 0, gi))
    vals = lambda first: pl.BlockSpec((1, HEAD, s), lambda bi, gi, qi: (bi, first + gi, 0))
    qtile = pl.BlockSpec((1, dq, bq), lambda bi, gi, qi: (bi, gi, qi))
    return pl.pallas_call(
        functools.partial(_selwin_body, bq, bks),
        grid=(b, g, s // bq),
        in_specs=[qtile,
                  pl.BlockSpec((1, 1, SEL_BLOCK, bq), lambda bi, gi, qi: (bi, gi, 0, qi)),
                  keys, vals(0), keys, vals(g),
                  pl.BlockSpec((1, 1, dq, bq), lambda bi, gi, qi: (bi, gi, 0, qi)),
                  pl.BlockSpec((1, LANES, bq), lambda bi, gi, qi: (bi, gi, qi))],
        out_specs=qtile,
        out_shape=jax.ShapeDtypeStruct((b, g * dq, s), BF16),
        scratch_shapes=[pltpu.VMEM((1, rows), F32), pltpu.VMEM((1, rows), F32),
                        pltpu.VMEM((HEAD, rows), F32), pltpu.VMEM((LANES, rows), BF16),
                        pltpu.VMEM((bks, rows), F32), pltpu.VMEM((bks, rows), F32)],
        compiler_params=_params("parallel", "parallel", "parallel"),
        name="sel_win",
    )(qnt, bias_t, ksa, vswt, kwa, vswt, ocmp_t, gnt)


def _outproj_body(yda_ref, yn_ref, gm_ref, x_ref, pda_ref, pnsa_ref, wo_ref, fg_ref, wq_ref,
                  k1_ref, k2_ref, h1_ref, c_ref, s1_ref, s2_ref):
    d = x_ref.shape[1]
    a = jnp.dot(yda_ref[...], pda_ref[...], preferred_element_type=F32)
    bn = jnp.dot(yn_ref[...], pnsa_ref[...], preferred_element_type=F32)
    merged = gm_ref[:, :d].astype(F32) * a + gm_ref[:, d:].astype(F32) * bn
    h1 = x_ref[...] + jnp.dot(merged.astype(BF16), wo_ref[...], preferred_element_type=F32)
    c = _rmsnorm(h1, fg_ref[...])
    _rows_to_tiles(h1, h1_ref)
    _rows_to_tiles(c, c_ref)
    cb = c.astype(BF16)
    for h in range(PEER_HEADS):
        qh = jnp.dot(cb, wq_ref[:, h * 256:(h + 1) * 256], preferred_element_type=F32).astype(BF16)
        s1_ref[h] = lax.dot_general(k1_ref[...], qh[:, :LANES], NT_DIMS, preferred_element_type=F32)
        s2_ref[h] = lax.dot_general(k2_ref[...], qh[:, LANES:], NT_DIMS, preferred_element_type=F32)


def _out_proj(yda, yn, gm, x2, pda, pnsa, wo, ffn_g, wq, k1, k2):
    t, d = x2.shape
    tm = 512
    row = lambda w: pl.BlockSpec((tm, w), lambda i: (i, 0))
    fixed = lambda a: pl.BlockSpec(a.shape, lambda i: (0, 0))
    sspec = pl.BlockSpec((PEER_HEADS, PEER_NKEYS, tm), lambda i: (0, 0, i))
    assert d == SUBLANES * LANES
    tiles = pl.BlockSpec((tm, SUBLANES, LANES), lambda i: (i, 0, 0))
    return pl.pallas_call(
        _outproj_body,
        grid=(t // tm,),
        in_specs=[row(512), row(512), row(2 * d), row(d), fixed(pda), fixed(pnsa), fixed(wo),
                  fixed(ffn_g), fixed(wq), fixed(k1), fixed(k2)],
        out_specs=[tiles, tiles, sspec, sspec],
        out_shape=[jax.ShapeDtypeStruct((t, SUBLANES, LANES), F32),
                   jax.ShapeDtypeStruct((t, SUBLANES, LANES), F32),
                   jax.ShapeDtypeStruct((PEER_HEADS, PEER_NKEYS, t), F32),
                   jax.ShapeDtypeStruct((PEER_HEADS, PEER_NKEYS, t), F32)],
        compiler_params=_params("parallel"),
        name="out_proj",
    )(yda, yn, gm, x2, pda, pnsa, wo, ffn_g, wq, k1, k2)


def _batcher_pairs(n):
    pairs = []

    def merge(lo, hi, r):
        step = r * 2
        if step < hi - lo:
            merge(lo, hi, step)
            merge(lo + r, hi, step)
            pairs.extend((i, i + r) for i in range(lo + r, hi - r, step))
        else:
            pairs.append((lo, lo + r))

    def sort(lo, hi):
        if hi - lo >= 1:
            mid = lo + (hi - lo) // 2
            sort(lo, mid)
            sort(mid + 1, hi)
            merge(lo, hi, 1)

    sort(0, n - 1)
    return pairs


_NET16 = _batcher_pairs(PEER_TOPK)


def _cmpx(a, b):
    c = (a[0] > b[0]) | ((a[0] == b[0]) & (a[1] < b[1]))
    return ((jnp.where(c, a[0], b[0]), jnp.where(c, a[1], b[1])),
            (jnp.where(c, b[0], a[0]), jnp.where(c, b[1], a[1])))


def _sort_lists(lists, n_real):
    lists = list(lists)
    for i, j in _NET16:
        if j < n_real:
            lists[i], lists[j] = _cmpx(lists[i], lists[j])
    return lists


def _merge_top(a, b):
    k = PEER_TOPK
    lists = [_cmpx(a[i], b[k - 1 - i])[0] for i in range(k)]
    step = k // 2
    while step >= 1:
        for i in range(k):
            if i & step == 0:
                lists[i], lists[i + step] = _cmpx(lists[i], lists[i + step])
        step //= 2
    return lists


def _top16(pairs):
    k = PEER_TOPK
    filler = (jnp.full(pairs[0][0].shape, -jnp.inf, F32), jnp.zeros(pairs[0][1].shape, I32))
    groups = []
    for g0 in range(0, len(pairs), k):
        chunk = list(pairs[g0:g0 + k])
        groups.append(_sort_lists(chunk + [filler] * (k - len(chunk)), len(chunk)))
    while len(groups) > 1:
        groups = [_merge_top(groups[i], groups[i + 1]) if i + 1 < len(groups) else groups[i]
                  for i in range(0, len(groups), 2)]
    return groups[0]


def _product_key_select(s1, s2):
    k = PEER_TOPK
    n_exp = PEER_NKEYS * PEER_NKEYS
    ids = lambda n: [jnp.full(s1[0].shape, i, I32) for i in range(n)]
    l1 = _top16(list(zip(s1, ids(len(s1)))))
    l2 = _top16(list(zip(s2, ids(len(s2)))))
    pair = lambda a, b: (l1[a][0] + l2[b][0],
                         (a * k + b) * n_exp + l1[a][1] * PEER_NKEYS + l2[b][1])
    first_row = [pair(0, b) for b in range(k)]
    rest = [pair(a, b) for a in range(1, k) for b in range(k // (a + 1))]
    best = _merge_top(first_row, _top16(rest))
    return [(v, p & (n_exp - 1)) for v, p in best]


def _peertopk_body(tt, s1_ref, s2_ref, idx_ref, gate_ref):
    nblk = tt // LANES
    assert nblk == SUBLANES

    def keys_major(ref, h):
        tiles = []
        for kg in range(PEER_NKEYS // SUBLANES):
            rows = slice(kg * SUBLANES, (kg + 1) * SUBLANES)
            tiles += _sublane_transpose([ref[h, rows, b * LANES:(b + 1) * LANES]
                                         for b in range(nblk)])
        return tiles

    def store(ref, h, tiles):
        for g in range(PEER_TOPK // SUBLANES):
            blocks = _sublane_transpose(tiles[g * SUBLANES:(g + 1) * SUBLANES])
            for b in range(nblk):
                ref[h, g * SUBLANES:(g + 1) * SUBLANES, b * LANES:(b + 1) * LANES] = blocks[b]

    def head(h, carry):
        best = _product_key_select(keys_major(s1_ref, h), keys_major(s2_ref, h))
        ex = [jnp.exp(v - best[0][0]) for v, _ in best]
        z = ex[0]
        for e in ex[1:]:
            z = z + e
        store(gate_ref, h, [e / z for e in ex])
        store(idx_ref, h, [ix for _, ix in best])
        return carry

    lax.fori_loop(0, PEER_HEADS, head, 0)


def _peer_topk(s1t, s2t):
    _, _, t = s1t.shape
    tt = SUBLANES * LANES
    spec_in = pl.BlockSpec((PEER_HEADS, PEER_NKEYS, tt), lambda i: (0, 0, i))
    spec_out = pl.BlockSpec((PEER_HEADS, PEER_TOPK, tt), lambda i: (0, 0, i))
    return pl.pallas_call(
        functools.partial(_peertopk_body, tt),
        grid=(t // tt,),
        in_specs=[spec_in, spec_in],
        out_specs=[spec_out, spec_out],
        out_shape=[jax.ShapeDtypeStruct((PEER_HEADS, PEER_TOPK, t), I32),
                   jax.ShapeDtypeStruct((PEER_HEADS, PEER_TOPK, t), F32)],
        compiler_params=_params("parallel"),
        name="peer_topk",
    )(s1t, s2t)


PEER_E = PEER_HEADS * PEER_TOPK
PEER_RING = 4
PEER_MID_ROWS = 48


def _pack_body(u_ref, v_ref, out_ref):
    bf16_bits = lambda a: lax.bitcast_convert_type(a.astype(BF16).astype(F32), I32)
    words = bf16_bits(u_ref[...]) | lax.shift_right_logical(bf16_bits(v_ref[...]), jnp.int32(16))
    _rows_to_tiles(words, out_ref)


def _pack_expert_rows(pu, pv):
    n, d = pu.shape
    assert d == SUBLANES * LANES
    te = 256
    blk = pl.BlockSpec((te, d), lambda i: (i, 0))
    return pl.pallas_call(
        _pack_body,
        grid=(n // te,),
        in_specs=[blk, blk],
        out_specs=pl.BlockSpec((te, SUBLANES, LANES), lambda i: (i, 0, 0)),
        out_shape=jax.ShapeDtypeStruct((n, SUBLANES, LANES), I32),
        compiler_params=_params("parallel"),
        name="pack_experts",
    )(pu, pv)


def _word_hi(w):
    return lax.bitcast_convert_type(w & jnp.int32(-65536), F32)


def _word_lo(w):
    return lax.bitcast_convert_type(lax.shift_left(w, jnp.int32(16)), F32)


def _sublane_sums(a, sub):
    for dist in (4, 2, 1):
        low = (sub & dist) == 0
        half = len(a) // 2
        a = [jnp.where(low, a[i], pltpu.roll(a[i + half], dist, 0))
             + jnp.where(low, pltpu.roll(a[i], SUBLANES - dist, 0), a[i + half])
             for i in range(half)]
    return a[0]


def _peer_pair_math(expert_u, expert_v, x8, gates, store, issue_some, wcol_ref):
    sub = lax.broadcasted_iota(I32, (SUBLANES, LANES), 0)
    eye = (lax.broadcasted_iota(I32, (PEER_E, LANES), 0)
           == lax.broadcasted_iota(I32, (PEER_E, LANES), 1))
    ones_rows = jnp.ones((SUBLANES, LANES), BF16)
    ones_sq = jnp.ones((LANES, LANES), BF16)
    ngroup = PEER_E // SUBLANES

    def hidden(a):
        groups = []
        for g in range(ngroup):
            prods = [expert_u(a, g * SUBLANES + r) * x8[a] for r in range(SUBLANES)]
            groups.append(_sublane_sums(prods, sub))
            issue_some(3 - g % 2)
        return jnp.concatenate(groups, axis=0)

    def expert_weights(a, q):
        q_hi = q.astype(BF16)
        q_lo = (q - q_hi.astype(F32)).astype(BF16)
        hid = (lax.dot_general(ones_rows, q_hi, NT_DIMS, preferred_element_type=F32)
               + lax.dot_general(ones_rows, q_lo, NT_DIMS, preferred_element_type=F32))
        issue_some(PEER_MID_ROWS // 2)
        w = _gelu(hid[0:1]) * gates[a]
        wd = jnp.where(eye, jnp.broadcast_to(w, (PEER_E, LANES)), 0.0).astype(BF16)
        wcol_ref[a] = jnp.dot(wd, ones_sq, preferred_element_type=F32)
        issue_some(PEER_MID_ROWS // 2)

    def combine(a):
        out = jnp.zeros((SUBLANES, LANES), F32)
        for g in range(ngroup):
            for r in range(SUBLANES):
                j = g * SUBLANES + r
                out = out + wcol_ref[a, j:j + 1, :] * expert_v(a, j)
            issue_some(3 - g % 2)
        store(a, out)

    expert_weights(0, hidden(0))
    expert_weights(1, hidden(1))
    combine(0)
    combine(1)


def _peer_finish(h1_ref, acc_ref, fg_ref, out_ref):
    hsum = h1_ref[...] + acc_ref[...]
    ms = jnp.mean(hsum * hsum, axis=(1, 2), keepdims=True)
    _tiles_to_rows(hsum * lax.rsqrt(ms + RMS_EPS) * fg_ref[...], out_ref)


def _peerffn_body(tb, idx_ref, gate_ref, c_ref, h1_ref, fg_ref, uv_ref, out_ref, buf_ref,
                  acc_ref, wcol_ref, sem_ref):
    npairs = tb // 2
    ahead = PEER_RING - 1

    def row_copy(t, j, slot):
        return pltpu.make_async_copy(uv_ref.at[idx_ref[t, j]], buf_ref.at[slot, j],
                                     sem_ref.at[slot])

    def wait_all(slot):
        pltpu.make_async_copy(uv_ref.at[pl.ds(0, PEER_E)], buf_ref.at[slot],
                              sem_ref.at[slot]).wait()

    def ring(phase):
        base = 2 * (phase % PEER_RING)
        return (base, base + 1)

    def pair(i, phase, prefetch):
        toks = (2 * i, 2 * i + 1)
        slots = ring(phase)
        nslots = ring(phase + ahead)
        todo = [(a, j) for a in range(2) for j in range(PEER_E)]

        def issue_some(n):
            for a, j in todo[:n]:
                if prefetch:
                    row_copy(toks[a] + 2 * ahead, j, nslots[a]).start(priority=j % 2)
            del todo[:n]

        wait_all(slots[0])
        wait_all(slots[1])

        def store(a, out):
            acc_ref[toks[a]] = out

        _peer_pair_math(
            lambda a, j: _word_hi(buf_ref[slots[a], j]),
            lambda a, j: _word_lo(buf_ref[slots[a], j]),
            [c_ref[t] for t in toks], [gate_ref[pl.ds(t, 1), :] for t in toks], store, issue_some,
            wcol_ref)
        assert not todo, "every prefetch row DMA must be issued exactly once"

    assert npairs % PEER_RING == 0
    for i in range(ahead):
        for a, slot in enumerate(ring(i)):
            for j in range(PEER_E):
                row_copy(2 * i + a, j, slot).start(priority=j % 2)

    def body(g, carry):
        for phase in range(PEER_RING):
            pair(PEER_RING * g + phase, phase, True)
        return carry

    ngroups = npairs // PEER_RING
    lax.fori_loop(0, ngroups - 1, body, 0)
    for phase in range(PEER_RING):
        i = PEER_RING * (ngroups - 1) + phase
        pair(i, phase, i + ahead < npairs)
    _peer_finish(h1_ref, acc_ref, fg_ref, out_ref)


def _peer_ffn(idx, gate, c3, h13, final_g3, uv_tiles):
    t = h13.shape[0]
    tb = 256
    row3 = pl.BlockSpec((tb, SUBLANES, LANES), lambda i: (i, 0, 0))
    return pl.pallas_call(
        functools.partial(_peerffn_body, tb),
        grid=(t // tb,),
        in_specs=[pl.BlockSpec((tb, PEER_E), lambda i: (i, 0), memory_space=pltpu.SMEM),
                  pl.BlockSpec((tb, PEER_E), lambda i: (i, 0)), row3, row3,
                  pl.BlockSpec((1, SUBLANES, LANES), lambda i: (0, 0, 0)),
                  pl.BlockSpec(memory_space=pl.ANY)],
        out_specs=pl.BlockSpec((tb, SUBLANES * LANES), lambda i: (i, 0)),
        out_shape=jax.ShapeDtypeStruct((t, SUBLANES * LANES), F32),
        scratch_shapes=[pltpu.VMEM((2 * PEER_RING, PEER_E, SUBLANES, LANES), I32),
                        pltpu.VMEM((tb, SUBLANES, LANES), F32),
                        pltpu.VMEM((2, PEER_E, LANES), F32),
                        pltpu.SemaphoreType.DMA((2 * PEER_RING,))],
        compiler_params=pltpu.CompilerParams(
            dimension_semantics=("arbitrary",), vmem_limit_bytes=VMEM_LIMIT,
            disable_bounds_checks=True),
        name="peer_ffn",
    )(idx, gate, c3, h13, final_g3, uv_tiles)


def _overlap_table(seq):
    ci = jnp.arange(seq // CMP_STRIDE)[None, :] * CMP_STRIDE
    sj = jnp.arange(SEL_BLOCK)[:, None] * SEL_BLOCK
    return ((ci < sj + SEL_BLOCK) & (ci + CMP_BLOCK > sj)).astype(F32)


def _cmp_blocks(kv):
    b, g, s, dh = kv.shape
    r = kv.reshape(b * g, s // CMP_STRIDE, CMP_STRIDE * dh)
    return jnp.concatenate([r, jnp.roll(r, -1, axis=1)], axis=-1)


def _pad_cmp_params(pe, w1):
    pe_p = jnp.pad(pe, ((0, 0), (0, LANES - HEAD))).reshape(1, -1)
    w1_p = jnp.pad(w1.reshape(CMP_BLOCK, HEAD, -1), ((0, 0), (0, LANES - HEAD), (0, 0)))
    return pe_p, w1_p.reshape(CMP_BLOCK * LANES, -1).astype(BF16)


def _layer(h, lidx, attn_norm, w_in, lq1, lk1, lq2, lk2, subln, pe_k, pe_v, w1k, w1v, w2k, w2v,
           p_da, p_nsa, w_o, ffn_norm, wq, k1, k2, pu, pv, out_norm):
    b, s, d = h.shape
    t = b * s
    g, hg = NSA_GROUPS, NSA_HG
    lambda_init = 0.8 - 0.6 * math.exp(-0.3 * lidx)
    x2 = h.reshape(t, d)

    qda, kda, vda, qn, cmp_in, ksx, kwx, vsw, gm, gn = _in_proj(
        x2, attn_norm.reshape(1, d), _pack_w_in(w_in), _rope_tables(s), s)
    tr = lambda a2: jnp.swapaxes(a2.reshape(b, s, -1), 1, 2)
    ydat = _diff_attn(tr(qda), kda.reshape(b, s, -1), tr(vda),
                      lq1.reshape(1, -1), lk1.reshape(1, -1), lq2.reshape(1, -1),
                      lk2.reshape(1, -1), subln.reshape(-1, 1), lambda_init)
    yda = jnp.swapaxes(ydat, 1, 2).reshape(t, -1)

    assert s // SEL_BLOCK <= SEL_BLOCK, "selection bias rows hold at most 64 blocks"
    cmp4 = jnp.swapaxes(cmp_in.reshape(b, s, 2 * g, LANES), 1, 2)
    pe_kp, w1_kp = _pad_cmp_params(pe_k, w1k)
    pe_vp, w1_vp = _pad_cmp_params(pe_v, w1v)
    kca, vc = _compress(_cmp_blocks(cmp4[:, :g]), _cmp_blocks(cmp4[:, g:]), pe_kp, pe_vp,
                        w1_kp, w1_vp, w2k.astype(BF16), w2v.astype(BF16))
    ncp = s // CMP_STRIDE
    kca = kca.reshape(b, g, ncp, LANES)
    vct = jnp.swapaxes(vc.reshape(b, g, ncp, HEAD), 2, 3)

    qnt = tr(qn)
    ocmp_t, bias_t = _cmp_select(qnt, kca, vct, _overlap_table(s))
    onehot = (jnp.arange(s)[:, None] // SEL_BLOCK == jnp.arange(HEAD)[None, :]).astype(BF16)
    pad_hot = jnp.concatenate([jnp.zeros_like(onehot), onehot] * g, axis=1)
    ksa = ksx.reshape(b, s, -1) + pad_hot[None]
    ynt = _sel_win(qnt, bias_t, ksa, kwx.reshape(b, s, -1), tr(vsw), ocmp_t, tr(gn))
    yn = jnp.swapaxes(ynt, 1, 2).reshape(t, -1)

    h1_tiles, c_tiles, s1t, s2t = _out_proj(
        yda, yn, gm, x2, p_da.astype(BF16), p_nsa.astype(BF16), w_o.astype(BF16),
        ffn_norm.reshape(1, d), wq.astype(BF16), k1.astype(BF16), k2.astype(BF16))
    idx_t, gate_t = _peer_topk(s1t, s2t)
    idx = idx_t.reshape(PEER_E, t).T
    gate = gate_t.reshape(PEER_E, t).T
    out = _peer_ffn(idx, gate, c_tiles, h1_tiles, out_norm.reshape(1, SUBLANES, LANES),
                    _pack_expert_rows(pu, pv))
    return out.reshape(b, s, d)


def kernel(x, attn_norm, w_in, da_lambda_q1, da_lambda_k1, da_lambda_q2, da_lambda_k2, da_subln,
           cmp_pe_k, cmp_pe_v, cmp_w1_k, cmp_w1_v, cmp_w2_k, cmp_w2_v, p_da, p_nsa, w_o,
           ffn_norm, peer_wq, peer_k1, peer_k2, peer_u, peer_v, final_norm):
    depth = attn_norm.shape[0]
    assert depth == 1, "the final norm is fused into the last layer's PEER kernel"
    h = x
    for l in range(depth):
        h = _layer(h, l, attn_norm[l], w_in[l], da_lambda_q1[l], da_lambda_k1[l], da_lambda_q2[l],
                   da_lambda_k2[l], da_subln[l], cmp_pe_k[l], cmp_pe_v[l], cmp_w1_k[l],
                   cmp_w1_v[l], cmp_w2_k[l], cmp_w2_v[l], p_da[l], p_nsa[l], w_o[l], ffn_norm[l],
                   peer_wq[l], peer_k1[l], peer_k2[l], peer_u[l], peer_v[l], final_norm)
    return h
```
